```python
import math
import jax, jax.numpy as jnp
from jax import lax
import numpy as np

D_MODEL = 2048
BATCH = 4
SEQ = 4096
DEPTH = 4

GRID_W = 64
CTX_LEN = 256
N_BRANCH = 3
BRANCH_W = 1024
HY_W = 1024
HY_ORDER = 2
HY_EMB = 33
HY_HID = 64
HY_FAST = 0.3
HY_SLOW = 1.5
HY_TARGET = 1e-2
RET_HEADS = 8
RET_DK = 64
RET_DV = 128
RET_CHUNK = 128
NA_HEADS = 8
NA_DH = 128
NA_WR = 8
NA_WC = 16
D_FF = 5632
ROPE_BASE = 10000.0
EPS = 1e-6
NEG = -1e30

RET_QK_W = RET_HEADS * RET_DK
RET_V_W = RET_HEADS * RET_DV
NA_W = NA_HEADS * NA_DH
IN_SPLIT = (RET_QK_W, RET_V_W, NA_W, NA_W, RET_QK_W, RET_V_W, NA_W, (HY_ORDER + 1) * HY_W, N_BRANCH * D_MODEL)
CTX_STATE_W = RET_QK_W + RET_V_W + 2 * NA_W
IN_W = sum(IN_SPLIT)

kernel_name = "hybrid_hyena_retention_natten_dit"

F32 = jnp.float32


def _split(z, sizes):
    idx = np.cumsum(sizes)[:-1].tolist()
    return jnp.split(z, idx, axis=-1)


def _heads(a, h):
    return a.reshape(a.shape[0], a.shape[1], h, -1)


def _bhld(a):
    return a.transpose(0, 2, 1, 3)


def rmsnorm(x, g):
    xf = x.astype(F32)
    y = xf * lax.rsqrt(jnp.mean(xf * xf, axis=-1, keepdims=True) + EPS)
    return (y * g.astype(F32)).astype(x.dtype)


def modulate(h, shift, scale):
    return h * (1.0 + scale) + shift


def dwconv3(x, w, b):
    xp = jnp.pad(x, ((0, 0), (1, 1), (0, 0)))
    return xp[:, :-2] * w[0] + xp[:, 1:-1] * w[1] + xp[:, 2:] * w[2] + b


def axial_rope(x, rows, cols):
    d = x.shape[-1]
    half = d // 2
    n = half // 2
    inv = ROPE_BASE ** (-jnp.arange(n, dtype=F32) / n)

    def rot(xp, pos):
        ang = pos.astype(F32)[:, None] * inv[None]
        cos = jnp.cos(ang)[None, :, None, :]
        sin = jnp.sin(ang)[None, :, None, :]
        x1, x2 = xp[..., :n].astype(F32), xp[..., n:].astype(F32)
        return jnp.concatenate([x1 * cos - x2 * sin, x2 * cos + x1 * sin], axis=-1)

    return jnp.concatenate([rot(x[..., :half], rows), rot(x[..., half:], cols)], axis=-1).astype(x.dtype)


def hyena_filter_fft(L, w1, b1, w2, b2, w3, freq):
    t = jnp.linspace(0.0, 1.0, L, dtype=F32)[:, None]
    bands = (HY_EMB - 1) // 2
    w = 2.0 * math.pi * jnp.arange(L, dtype=F32)[:, None] / L
    fr = jnp.linspace(1e-4, bands - 1, bands, dtype=F32)[None]
    feat = jnp.concatenate([t, jnp.cos(fr * w), -jnp.sin(fr * w)], axis=-1)
    fq = freq.astype(F32)
    h = jnp.sin(fq * (feat @ w1.astype(F32) + b1.astype(F32)))
    h = jnp.sin(fq * (h @ w2.astype(F32) + b2.astype(F32)))
    h = (h @ w3.astype(F32)).reshape(L, HY_ORDER, 2, HY_W)
    deltas = jnp.abs(jnp.linspace(math.log(HY_TARGET) / HY_FAST, math.log(HY_TARGET) / HY_SLOW, HY_W, dtype=F32))
    h = h * jnp.exp(-t * deltas[None])[:, None, None, :]
    fwd, bwd = h[:, :, 0], h[:, :, 1]
    k = jnp.concatenate([fwd, jnp.zeros((1, HY_ORDER, HY_W), F32), bwd[:0:-1]], axis=0)
    k = k * lax.rsqrt(jnp.sum(k * k, axis=0, keepdims=True) + EPS)
    return jnp.fft.rfft(k, axis=0)


def hyena_mixer(z, conv_w, conv_b, kf, bias):
    L = z.shape[1]
    parts = jnp.split(dwconv3(z, conv_w, conv_b), HY_ORDER + 1, axis=-1)
    y = parts[-1].astype(F32)
    for o in range(HY_ORDER):
        yf = jnp.fft.rfft(y, n=2 * L, axis=1)
        conv = jnp.fft.irfft(yf * kf[None, :, o, :], n=2 * L, axis=1)[:, :L]
        y = parts[o].astype(F32) * (conv + bias[o].astype(F32) * y)
    return y.astype(z.dtype)


def ret_chunks(q, k, v, log_g, s0, strict):
    B, H, L, dk = q.shape
    dv = v.shape[-1]
    C = RET_CHUNK
    N = L // C
    qc = q.reshape(B, H, N, C, dk)
    kc = k.reshape(B, H, N, C, dk)
    vc = v.reshape(B, H, N, C, dv)
    pos = jnp.arange(C, dtype=F32)
    rel = pos[:, None] - pos[None, :]
    mask = (rel > 0) if strict else (rel >= 0)
    dmat = jnp.where(mask[None], jnp.exp(jnp.maximum(rel, 0.0)[None] * log_g[:, None, None]), 0.0)
    a = jnp.einsum('bhncd,bhnsd->bhncs', qc, kc) * dmat[None, :, None]
    o = jnp.einsum('bhncs,bhnse->bhnce', a, vc)
    zeta = jnp.exp((C - 1 - pos)[None] * log_g[:, None])
    kv = jnp.einsum('bhncd,bhnce->nbhde', kc * zeta[None, :, None, :, None], vc)
    g_chunk = jnp.exp(C * log_g)[None, :, None, None]

    def step(s, kv_n):
        return g_chunk * s + kv_n, s

    s_fin, s_prev = lax.scan(step, s0, kv)
    xi = jnp.exp((pos + 1)[None] * log_g[:, None])
    o = o + jnp.einsum('bhncd,nbhde->bhnce', qc * xi[None, :, None, :, None], s_prev)
    return o.reshape(B, H, L, dv), s_fin


def ret_final_state(k, v, log_g):
    L = k.shape[2]
    w = jnp.exp((L - 1 - jnp.arange(L, dtype=F32))[None] * log_g[:, None])
    return jnp.einsum('bhld,bhle->bhde', k * w[None, :, :, None], v)


def ret_bidir(q, k, v, log_g, s0_f, s0_b):
    flip = lambda a: a[:, :, ::-1]
    o_f, s_f = ret_chunks(q, k, v, log_g[0], s0_f, False)
    o_b, s_b = ret_chunks(flip(q), flip(k), flip(v), log_g[1], s0_b, True)
    return o_f + flip(o_b), s_f, s_b


def ret_output(o, g):
    o = o * lax.rsqrt(jnp.mean(o * o, axis=-1, keepdims=True) + EPS)
    B, H, L, dv = o.shape
    o = o.transpose(0, 2, 1, 3).reshape(B, L, H * dv)
    return (jax.nn.silu(g.astype(F32)) * o).astype(g.dtype)


def neighbourhood_attn(q, k, v, kc, vc, rpb):
    B, L, H, dh = q.shape
    R = L // GRID_W
    wr = min(NA_WR, R)
    grid = lambda a: a.reshape(B, R, GRID_W, H, dh).transpose(0, 3, 1, 2, 4)
    qg, kg, vg = grid(q), grid(k), grid(v)
    r = jnp.arange(R)
    rs = jnp.clip(r - NA_WR // 2, 0, R - wr)
    rows_idx = rs[:, None] + jnp.arange(wr)[None]
    ci = jnp.arange(GRID_W)
    cs = jnp.clip(ci - NA_WC // 2, 0, GRID_W - NA_WC)
    colmask = (ci[None, :] >= cs[:, None]) & (ci[None, :] < cs[:, None] + NA_WC)
    k_win = kg[:, :, rows_idx]
    v_win = vg[:, :, rows_idx]
    scale = dh ** -0.5
    s_win = jnp.einsum('bhrqd,bhrwkd->bhrqwk', qg, k_win).astype(F32) * scale
    dr = rows_idx - r[:, None] + (NA_WR - 1)
    dc = jnp.clip(ci[None, :] - ci[:, None] + NA_WC - 1, 0, 2 * NA_WC - 2)
    bias = rpb.astype(F32)[:, dr[:, None, :, None], dc[None, :, None, :]]
    s_win = jnp.where(colmask[:, None, :], s_win + bias[None], NEG).reshape(B, H, R, GRID_W, wr * GRID_W)
    kcx = _bhld(kc)
    vcx = _bhld(vc)
    s_ctx = jnp.einsum('bhrqd,bhjd->bhrqj', qg, kcx).astype(F32) * scale
    p = jax.nn.softmax(jnp.concatenate([s_win, s_ctx], axis=-1), axis=-1)
    p_win = p[..., :wr * GRID_W].reshape(B, H, R, GRID_W, wr, GRID_W).astype(v.dtype)
    p_ctx = p[..., wr * GRID_W:].astype(v.dtype)
    o = jnp.einsum('bhrqwk,bhrwkd->bhrqd', p_win, v_win) + jnp.einsum('bhrqj,bhjd->bhrqd', p_ctx, vcx)
    return o.transpose(0, 2, 3, 1, 4).reshape(B, L, H * dh)


def dense_attn(q, k, v):
    B, L, H, dh = q.shape
    s = jnp.einsum('bqhd,bkhd->bhqk', q, k).astype(F32) * dh ** -0.5
    p = jax.nn.softmax(s, axis=-1).astype(v.dtype)
    return jnp.einsum('bhqk,bkhd->bqhd', p, v).reshape(B, L, H * dh)


def merge_branches(ys, gate, w_branch, w_out):
    gates = jnp.split(gate, N_BRANCH, axis=-1)
    acc = jax.nn.sigmoid(gates[0]) * (ys[0] @ w_branch[0])
    for i in range(1, N_BRANCH):
        acc = acc + jax.nn.sigmoid(gates[i]) * (ys[i] @ w_branch[i])
    return acc @ w_out


def token_mixers(hl, hc, rows, cols, w_in, hy_conv_w, hy_conv_b, hy_f_w1, hy_f_b1, hy_f_w2, hy_f_b2,
                 hy_f_w3, hy_f_freq, hy_bias, ret_log_decay, na_rpb, w_branch, w_out, ctx_out):
    B, L, _ = hl.shape
    Lc = hc.shape[1]
    log_g = ret_log_decay.astype(F32)
    rk, rv, nk, nv, rq, rg, nq, hy, gate = _split(hl @ w_in, IN_SPLIT)
    if ctx_out:
        crk, crv, cnk, cnv, crq, crg, cnq, chy, cgate = _split(hc @ w_in, IN_SPLIT)
    else:
        crk, crv, cnk, cnv = _split(hc @ w_in[:, :CTX_STATE_W], IN_SPLIT[:4])

    ks = RET_DK ** -0.5
    ck = _bhld(_heads(crk, RET_HEADS)).astype(F32) * ks
    cv = _bhld(_heads(crv, RET_HEADS)).astype(F32)
    if ctx_out:
        cq = _bhld(_heads(crq, RET_HEADS)).astype(F32)
        zero = jnp.zeros((B, RET_HEADS, RET_DK, RET_DV), F32)
        co, s_f, s_b = ret_bidir(cq, ck, cv, log_g, zero, zero)
        yc_ret = ret_output(co, crg)
    else:
        s_f = ret_final_state(ck, cv, log_g[0])
        s_b = ret_final_state(ck[:, :, ::-1], cv[:, :, ::-1], log_g[1])
    lq = _bhld(axial_rope(_heads(rq, RET_HEADS), rows, cols)).astype(F32)
    lk = _bhld(axial_rope(_heads(rk, RET_HEADS), rows, cols)).astype(F32) * ks
    lv = _bhld(_heads(rv, RET_HEADS)).astype(F32)
    lo, _, _ = ret_bidir(lq, lk, lv, log_g, s_f, s_b)
    yl_ret = ret_output(lo, rg)

    cnk_h = _heads(cnk, NA_HEADS)
    cnv_h = _heads(cnv, NA_HEADS)
    yl_na = neighbourhood_attn(_heads(nq, NA_HEADS), _heads(nk, NA_HEADS), _heads(nv, NA_HEADS), cnk_h, cnv_h, na_rpb)

    kf_l = hyena_filter_fft(L, hy_f_w1, hy_f_b1, hy_f_w2, hy_f_b2, hy_f_w3, hy_f_freq)
    yl_hy = hyena_mixer(hy, hy_conv_w, hy_conv_b, kf_l, hy_bias)

    yl = merge_branches((yl_hy, yl_ret, yl_na), gate, w_branch, w_out)
    if not ctx_out:
        return yl, None
    yc_na = dense_attn(_heads(cnq, NA_HEADS), cnk_h, cnv_h)
    kf_c = hyena_filter_fft(Lc, hy_f_w1, hy_f_b1, hy_f_w2, hy_f_b2, hy_f_w3, hy_f_freq)
    yc_hy = hyena_mixer(chy, hy_conv_w, hy_conv_b, kf_c, hy_bias)
    yc = merge_branches((yc_hy, yc_ret, yc_na), cgate, w_branch, w_out)
    return yl, yc


def conv_ffn(h, w_in, conv_w, conv_b, w_out):
    a, b = jnp.split(h @ w_in, 2, axis=-1)
    a = dwconv3(a, conv_w, conv_b)
    return (jax.nn.gelu(a) * b) @ w_out


def setup_inputs(seed: int = 0) -> dict:
    key = jax.random.key(seed)
    ks = jax.random.split(key, 32)
    D = D_MODEL
    nrm = lambda k, shape, std: std * jax.random.normal(k, shape, F32)
    base_decay = jnp.log(1.0 - 2.0 ** (-5.0 - jnp.arange(RET_HEADS, dtype=F32)))
    return {
        "x": nrm(ks[0], (BATCH, SEQ, D), 1.0),
        "c": nrm(ks[1], (BATCH, D), 1.0),
        "ctx": nrm(ks[2], (BATCH, CTX_LEN, D), 1.0),
        "c_ctx": nrm(ks[3], (D,), 1.0),
        "w_mod": nrm(ks[4], (DEPTH, D, 6 * D), 0.5 * D ** -0.5),
        "b_mod": nrm(ks[5], (DEPTH, 6 * D), 0.02),
        "g_norm1": 1.0 + nrm(ks[6], (DEPTH, D), 0.02),
        "g_norm2": 1.0 + nrm(ks[7], (DEPTH, D), 0.02),
        "w_in": nrm(ks[8], (DEPTH, D, IN_W), D ** -0.5),
        "hy_conv_w": nrm(ks[9], (DEPTH, 3, (HY_ORDER + 1) * HY_W), 3 ** -0.5),
        "hy_conv_b": nrm(ks[10], (DEPTH, (HY_ORDER + 1) * HY_W), 0.02),
        "hy_f_w1": nrm(ks[11], (DEPTH, HY_EMB, HY_HID), HY_EMB ** -0.5),
        "hy_f_b1": nrm(ks[12], (DEPTH, HY_HID), 0.02),
        "hy_f_w2": nrm(ks[13], (DEPTH, HY_HID, HY_HID), HY_HID ** -0.5),
        "hy_f_b2": nrm(ks[14], (DEPTH, HY_HID), 0.02),
        "hy_f_w3": nrm(ks[15], (DEPTH, HY_HID, HY_ORDER * 2 * HY_W), HY_HID ** -0.5),
        "hy_f_freq": 1.0 + nrm(ks[16], (DEPTH, HY_HID), 0.02),
        "hy_bias": nrm(ks[17], (DEPTH, HY_ORDER, HY_W), 0.5),
        "ret_log_decay": base_decay * (1.0 + nrm(ks[18], (DEPTH, 2, RET_HEADS), 0.05)),
        "na_rpb": nrm(ks[19], (DEPTH, NA_HEADS, 2 * NA_WR - 1, 2 * NA_WC - 1), 0.02),
        "w_branch": nrm(ks[20], (DEPTH, N_BRANCH, BRANCH_W, D), BRANCH_W ** -0.5),
        "w_out": nrm(ks[21], (DEPTH, D, D), D ** -0.5),
        "ffn_w_in": nrm(ks[22], (DEPTH, D, 2 * D_FF), D ** -0.5),
        "ffn_conv_w": nrm(ks[23], (DEPTH, 3, D_FF), 3 ** -0.5),
        "ffn_conv_b": nrm(ks[24], (DEPTH, D_FF), 0.02),
        "ffn_w_out": nrm(ks[25], (DEPTH, D_FF, D), D_FF ** -0.5),
        "g_final": 1.0 + nrm(ks[26], (D,), 0.02),
    }


def reference(x, c, ctx, c_ctx, w_mod, b_mod, g_norm1, g_norm2, w_in, hy_conv_w, hy_conv_b, hy_f_w1, hy_f_b1,
              hy_f_w2, hy_f_b2, hy_f_w3, hy_f_freq, hy_bias, ret_log_decay, na_rpb, w_branch, w_out,
              ffn_w_in, ffn_conv_w, ffn_conv_b, ffn_w_out, g_final):
    B, L, D = x.shape
    t = jnp.arange(L)
    rows = t // GRID_W
    cols = t % GRID_W
    xl, xc = x, ctx
    sc_l = jax.nn.silu(c)
    sc_c = jax.nn.silu(c_ctx)
    for l in range(DEPTH):
        last = l == DEPTH - 1
        mod_l = (sc_l @ w_mod[l] + b_mod[l])[:, None, :]
        sh1, s1, g1, sh2, s2, g2 = jnp.split(mod_l, 6, axis=-1)
        n_cm = 2 if last else 6
        cmods = jnp.split(sc_c @ w_mod[l][:, :n_cm * D] + b_mod[l][:n_cm * D], n_cm)
        hl = modulate(rmsnorm(xl, g_norm1[l]), sh1, s1)
        hc = modulate(rmsnorm(xc, g_norm1[l]), cmods[0], cmods[1])
        yl, yc = token_mixers(hl, hc, rows, cols, w_in[l], hy_conv_w[l], hy_conv_b[l], hy_f_w1[l], hy_f_b1[l],
                              hy_f_w2[l], hy_f_b2[l], hy_f_w3[l], hy_f_freq[l], hy_bias[l], ret_log_decay[l],
                              na_rpb[l], w_branch[l], w_out[l], not last)
        xl = xl + g1 * yl
        xl = xl + g2 * conv_ffn(modulate(rmsnorm(xl, g_norm2[l]), sh2, s2), ffn_w_in[l], ffn_conv_w[l], ffn_conv_b[l], ffn_w_out[l])
        if not last:
            xc = xc + cmods[2] * yc
            xc = xc + cmods[5] * conv_ffn(modulate(rmsnorm(xc, g_norm2[l]), cmods[3], cmods[4]), ffn_w_in[l], ffn_conv_w[l], ffn_conv_b[l], ffn_w_out[l])
    return rmsnorm(xl, g_final)
```

```python
import functools
import math

import numpy as np
import jax
import jax.numpy as jnp
from jax import lax
from jax.experimental import pallas as pl
from jax.experimental.pallas import tpu as pltpu

F32 = jnp.float32
BF16 = jnp.bfloat16

GRID_W = 64
N_BRANCH = 3
HY_ORDER = 2
HY_EMB = 33
HY_FAST = 0.3
HY_SLOW = 1.5
HY_TARGET = 1e-2
RET_HEADS = 8
RET_DK = 64
RET_DV = 128
RET_CHUNK = 128
NA_HEADS = 8
NA_DH = 128
NA_WR = 8
NA_WC = 16
ROPE_BASE = 10000.0
EPS = 1e-6
NEG = -1e30

LANES = 128
SUBLANES = 8
VMEM_LIMIT = 56 * 1024 * 1024

TM = 1024


def _cparams(*sem):
    return pltpu.CompilerParams(dimension_semantics=sem, vmem_limit_bytes=VMEM_LIMIT)


def _mods_kernel(a_ref, w_ref, b_ref, o_ref):
    a = a_ref[...]
    a = a * jax.nn.sigmoid(a)
    o_ref[...] = jnp.dot(a.astype(BF16), w_ref[...].astype(BF16), preferred_element_type=F32) + b_ref[...]


def _mods(cond, w_mod, b_mod):
    depth, d, n = w_mod.shape
    tn = 1024
    return pl.pallas_call(
        _mods_kernel,
        grid=(depth, n // tn),
        in_specs=[
            pl.BlockSpec((SUBLANES, d), lambda l, j: (0, 0)),
            pl.BlockSpec((None, d, tn), lambda l, j: (l, 0, j)),
            pl.BlockSpec((None, 1, tn), lambda l, j: (l, 0, j)),
        ],
        out_specs=pl.BlockSpec((None, SUBLANES, tn), lambda l, j: (l, 0, j)),
        out_shape=jax.ShapeDtypeStruct((depth, SUBLANES, n), F32),
        compiler_params=_cparams("parallel", "parallel"),
        name="mods",
    )(cond, w_mod, b_mod.reshape(depth, 1, n))


def _normmod_kernel(mrow_ref, x_ref, g_ref, sh_ref, sc_ref, o_ref):
    x = x_ref[...]
    y = x * lax.rsqrt(jnp.mean(x * x, axis=-1, keepdims=True) + EPS)
    y = y * g_ref[...]
    o_ref[...] = (y * (1.0 + sc_ref[...]) + sh_ref[...]).astype(o_ref.dtype)


def _normmod(x, g, mods, mrow, shift_idx, scale_idx, nblk):
    m, d = x.shape
    grid_spec = pltpu.PrefetchScalarGridSpec(
        num_scalar_prefetch=1,
        grid=(nblk,),
        in_specs=[
            pl.BlockSpec((TM, d), lambda i, mr: (i, 0)),
            pl.BlockSpec((1, d), lambda i, mr: (0, 0)),
            pl.BlockSpec((None, None, 1, d), lambda i, mr: (mr[i], shift_idx, 0, 0)),
            pl.BlockSpec((None, None, 1, d), lambda i, mr: (mr[i], scale_idx, 0, 0)),
        ],
        out_specs=pl.BlockSpec((TM, d), lambda i, mr: (i, 0)),
    )
    return pl.pallas_call(
        _normmod_kernel,
        grid_spec=grid_spec,
        out_shape=jax.ShapeDtypeStruct((m, d), BF16),
        compiler_params=_cparams("parallel"),
        name="normmod",
    )(mrow, x, g.reshape(1, d), mods, mods)


def _final_norm_kernel(x_ref, g_ref, o_ref):
    x = x_ref[...]
    y = x * lax.rsqrt(jnp.mean(x * x, axis=-1, keepdims=True) + EPS)
    o_ref[...] = y * g_ref[...]


def _final_norm(x, g, nblk):
    m, d = x.shape
    return pl.pallas_call(
        _final_norm_kernel,
        grid=(nblk,),
        in_specs=[pl.BlockSpec((TM, d), lambda i: (i, 0)), pl.BlockSpec((1, d), lambda i: (0, 0))],
        out_specs=pl.BlockSpec((TM, d), lambda i: (i, 0)),
        out_shape=jax.ShapeDtypeStruct((nblk * TM, d), F32),
        compiler_params=_cparams("parallel"),
        name="final_norm",
    )(x, g.reshape(1, d))


def _mm_kernel(a_ref, w_ref, o_ref):
    o_ref[...] = jnp.dot(a_ref[...], w_ref[...], preferred_element_type=F32).astype(o_ref.dtype)


def _matmul(a, w, nblk, tn):
    m, k = a.shape
    n = w.shape[1]
    return pl.pallas_call(
        _mm_kernel,
        grid=(nblk, n // tn),
        in_specs=[pl.BlockSpec((TM, k), lambda i, j: (i, 0)), pl.BlockSpec((k, tn), lambda i, j: (0, j))],
        out_specs=pl.BlockSpec((TM, tn), lambda i, j: (i, j)),
        out_shape=jax.ShapeDtypeStruct((m, n), BF16),
        compiler_params=_cparams("parallel", "arbitrary"),
        name="matmul",
    )(a, w)


def _resmm_kernel(mrow_ref, a_ref, w_ref, x_ref, g_ref, o_ref):
    y = jnp.dot(a_ref[...], w_ref[...], preferred_element_type=F32)
    o_ref[...] = x_ref[...] + g_ref[...] * y


def _res_matmul(a, w, x, mods, mrow, gate_idx, nblk, tn):
    m, k = a.shape
    d = w.shape[1]
    grid_spec = pltpu.PrefetchScalarGridSpec(
        num_scalar_prefetch=1,
        grid=(nblk, d // tn),
        in_specs=[
            pl.BlockSpec((TM, k), lambda i, j, mr: (i, 0)),
            pl.BlockSpec((k, tn), lambda i, j, mr: (0, j)),
            pl.BlockSpec((TM, tn), lambda i, j, mr: (i, j)),
            pl.BlockSpec((None, None, 1, tn), lambda i, j, mr: (mr[i], gate_idx, 0, j)),
        ],
        out_specs=pl.BlockSpec((TM, tn), lambda i, j, mr: (i, j)),
    )
    return pl.pallas_call(
        _resmm_kernel,
        grid_spec=grid_spec,
        out_shape=jax.ShapeDtypeStruct((m, d), F32),
        compiler_params=_cparams("parallel", "arbitrary"),
        name="res_matmul",
    )(mrow, a, w, x, mods)


def _merge_kernel(y0_ref, y1_ref, y2_ref, g0_ref, g1_ref, g2_ref, w_ref, o_ref):
    acc = None
    for i, (y_ref, g_ref) in enumerate(((y0_ref, g0_ref), (y1_ref, g1_ref), (y2_ref, g2_ref))):
        t = jnp.dot(y_ref[...], w_ref[i], preferred_element_type=F32)
        t = jax.nn.sigmoid(g_ref[...].astype(F32)) * t
        acc = t if acc is None else acc + t
    o_ref[...] = acc.astype(o_ref.dtype)


def _merge(ys, z, gate_col0, w_branch, nblk, tn):
    m, bw = ys[0].shape
    d = w_branch.shape[2]
    gspec = lambda i_br: pl.BlockSpec((TM, tn), lambda i, j: (i, (gate_col0 + i_br * d) // tn + j))
    yspec = pl.BlockSpec((TM, bw), lambda i, j: (i, 0))
    return pl.pallas_call(
        _merge_kernel,
        grid=(nblk, d // tn),
        in_specs=[yspec, yspec, yspec, gspec(0), gspec(1), gspec(2),
                  pl.BlockSpec((N_BRANCH, bw, tn), lambda i, j: (0, 0, j))],
        out_specs=pl.BlockSpec((TM, tn), lambda i, j: (i, j)),
        out_shape=jax.ShapeDtypeStruct((m, d), BF16),
        compiler_params=_cparams("parallel", "arbitrary"),
        name="merge",
    )(ys[0], ys[1], ys[2], z, z, z, w_branch)


def _ffn_gate_kernel(seq_ref, a_ref, ap_ref, an_ref, b_ref, cw_ref, cb_ref, o_ref):
    i = pl.program_id(0)
    seq = seq_ref[i]
    a = a_ref[...].astype(F32)
    tm = a.shape[0]
    row = lax.broadcasted_iota(jnp.int32, a.shape, 0)
    pos = lax.rem(i * tm + row, seq)
    prev = jnp.where(row == 0, ap_ref[SUBLANES - 1:SUBLANES, :].astype(F32), pltpu.roll(a, 1, axis=0))
    prev = jnp.where(pos == 0, 0.0, prev)
    nxt = jnp.where(row == tm - 1, an_ref[0:1, :].astype(F32), pltpu.roll(a, tm - 1, axis=0))
    nxt = jnp.where(pos == seq - 1, 0.0, nxt)
    cw = cw_ref[...]
    conv = prev * cw[0:1, :] + a * cw[1:2, :] + nxt * cw[2:3, :] + cb_ref[...]
    o_ref[...] = (jax.nn.gelu(conv) * b_ref[...].astype(F32)).astype(o_ref.dtype)


def _ffn_gate(u, conv_w, conv_b, seqlen, nblk, tc):
    m, ff2 = u.shape
    ff = ff2 // 2
    nrow8 = m // SUBLANES
    r8 = TM // SUBLANES
    grid_spec = pltpu.PrefetchScalarGridSpec(
        num_scalar_prefetch=1,
        grid=(nblk, ff // tc),
        in_specs=[
            pl.BlockSpec((TM, tc), lambda i, j, s: (i, j)),
            pl.BlockSpec((SUBLANES, tc), lambda i, j, s: (jnp.maximum(i * r8 - 1, 0), j)),
            pl.BlockSpec((SUBLANES, tc), lambda i, j, s: (jnp.minimum((i + 1) * r8, nrow8 - 1), j)),
            pl.BlockSpec((TM, tc), lambda i, j, s: (i, ff // tc + j)),
            pl.BlockSpec((3, tc), lambda i, j, s: (0, j)),
            pl.BlockSpec((1, tc), lambda i, j, s: (0, j)),
        ],
        out_specs=pl.BlockSpec((TM, tc), lambda i, j, s: (i, j)),
    )
    return pl.pallas_call(
        _ffn_gate_kernel,
        grid_spec=grid_spec,
        out_shape=jax.ShapeDtypeStruct((m, ff), BF16),
        compiler_params=_cparams("parallel", "arbitrary"),
        name="ffn_gate",
    )(seqlen, u, u, u, u, conv_w, conv_b.reshape(1, ff))


def _rope_pair(x, cos, sin):
    lane = lax.broadcasted_iota(jnp.int32, x.shape, 1)
    first = lax.rem(lane, 32) < 16
    partner = jnp.where(first, pltpu.roll(x, LANES - 16, axis=1), pltpu.roll(x, 16, axis=1))
    return x * cos + partner * sin


def _ret_kernel(lg_ref, q_ref, k_ref, v_ref, g_ref, cos_ref, sin_ref, s0_ref, o_ref, sfin_ref,
                qr_scr, kr_scr, kv_scr, *, rope):
    hp = pl.program_id(1)
    seq = q_ref.shape[0]
    c = RET_CHUNK
    nchunk = seq // c
    ks = RET_DK ** -0.5
    q = q_ref[...].astype(F32)
    k = k_ref[...].astype(F32)
    if rope:
        q = _rope_pair(q, cos_ref[...], sin_ref[...])
        k = _rope_pair(k, cos_ref[...], sin_ref[...])
    qr_scr[...] = q
    kr_scr[...] = k * ks

    pos_r = lax.broadcasted_iota(jnp.int32, (c, 1), 0).astype(F32)
    rel = (lax.broadcasted_iota(jnp.int32, (c, c), 0) - lax.broadcasted_iota(jnp.int32, (c, c), 1)).astype(F32)
    lane = lax.broadcasted_iota(jnp.int32, (1, LANES), 1)

    for hh in range(2):
        h = 2 * hp + hh
        lgf = lg_ref[0, h]
        lgb = lg_ref[1, h]
        hm = (lane // RET_DK == hh).astype(F32)
        zeta_f = jnp.exp(lgf * (c - 1 - pos_r))
        zeta_b = jnp.exp(lgb * pos_r)
        xi_f = jnp.exp(lgf * (pos_r + 1.0))
        xi_b = jnp.exp(lgb * (c - pos_r))
        dmat = jnp.where(rel >= 0, jnp.exp(lgf * jnp.maximum(rel, 0.0)), jnp.exp(lgb * jnp.maximum(-rel, 0.0)))
        gf = jnp.exp(lgf * c)
        gb = jnp.exp(lgb * c)
        vcols = slice(hh * RET_DV, (hh + 1) * RET_DV)

        def kv_body(n, _):
            rows = pl.ds(pl.multiple_of(n * c, c), c)
            kh = kr_scr[rows, :] * hm
            kz = jnp.concatenate([kh * zeta_f, kh * zeta_b], axis=1).astype(BF16)
            kv_scr[n] = lax.dot_general(kz, v_ref[rows, vcols], (((0,), (0,)), ((), ())),
                                        preferred_element_type=F32)
            return 0

        lax.fori_loop(0, nchunk, kv_body, 0)

        def scan_f(n, s):
            t = kv_scr[n, 0:LANES, :]
            kv_scr[n, 0:LANES, :] = s
            return gf * s + t

        s_f = lax.fori_loop(0, nchunk, scan_f, s0_ref[hh, 0])

        def scan_b(j, s):
            n = nchunk - 1 - j
            t = kv_scr[n, LANES:2 * LANES, :]
            kv_scr[n, LANES:2 * LANES, :] = s
            return gb * s + t

        s_b = lax.fori_loop(0, nchunk, scan_b, s0_ref[hh, 1])
        sfin_ref[hh, 0] = s_f
        sfin_ref[hh, 1] = s_b

        def out_body(n, _):
            rows = pl.ds(pl.multiple_of(n * c, c), c)
            qm = qr_scr[rows, :] * hm
            a = lax.dot_general(qm.astype(BF16), kr_scr[rows, :].astype(BF16), (((1,), (1,)), ((), ())),
                                preferred_element_type=F32) * dmat
            o = jnp.dot(a.astype(BF16), v_ref[rows, vcols], preferred_element_type=F32)
            qx = jnp.concatenate([qm * xi_f, qm * xi_b], axis=1).astype(BF16)
            o = o + jnp.dot(qx, kv_scr[n].astype(BF16), preferred_element_type=F32)
            o = o * lax.rsqrt(jnp.mean(o * o, axis=-1, keepdims=True) + EPS)
            g = g_ref[rows, vcols].astype(F32)
            o_ref[rows, vcols] = (g * jax.nn.sigmoid(g) * o).astype(o_ref.dtype)
            return 0

        lax.fori_loop(0, nchunk, out_body, 0)


def _retention(z, log_decay, cos, sin, s0, *, nb, seq, row0, rope):
    rb0 = row0 // seq
    hpairs = RET_HEADS // 2
    kcol, vcol, qcol, gcol = 0, 512 // 256, 3584 // LANES, 4096 // 256
    kernel = functools.partial(_ret_kernel, rope=rope)
    return pl.pallas_call(
        kernel,
        grid=(nb, hpairs),
        in_specs=[
            pl.BlockSpec(memory_space=pltpu.SMEM),
            pl.BlockSpec((seq, LANES), lambda b, p: (rb0 + b, qcol + p)),
            pl.BlockSpec((seq, LANES), lambda b, p: (rb0 + b, kcol + p)),
            pl.BlockSpec((seq, 2 * RET_DV), lambda b, p: (rb0 + b, vcol + p)),
            pl.BlockSpec((seq, 2 * RET_DV), lambda b, p: (rb0 + b, gcol + p)),
            pl.BlockSpec((seq, LANES), lambda b, p: (0, 0)),
            pl.BlockSpec((seq, LANES), lambda b, p: (0, 0)),
            pl.BlockSpec((None, 2, 2, LANES, RET_DV), lambda b, p: (b, p, 0, 0, 0)),
        ],
        out_specs=[
            pl.BlockSpec((seq, 2 * RET_DV), lambda b, p: (b, p)),
            pl.BlockSpec((None, 2, 2, LANES, RET_DV), lambda b, p: (b, p, 0, 0, 0)),
        ],
        out_shape=[
            jax.ShapeDtypeStruct((nb * seq, RET_HEADS * RET_DV), BF16),
            jax.ShapeDtypeStruct((nb, RET_HEADS, 2, LANES, RET_DV), F32),
        ],
        scratch_shapes=[
            pltpu.VMEM((seq, LANES), F32),
            pltpu.VMEM((seq, LANES), F32),
            pltpu.VMEM((seq // RET_CHUNK, 2 * LANES, RET_DV), F32),
        ],
        compiler_params=_cparams("parallel", "arbitrary"),
        name="retention_rope" if rope else "retention",
    )(log_decay, z, z, z, z, cos, sin, s0)


def _na_kernel(q_ref, k_ref, v_ref, kc_ref, vc_ref, bias_ref, o_ref):
    seq = q_ref.shape[0]
    nrow = seq // GRID_W
    wr = min(NA_WR, nrow)
    win = wr * GRID_W
    scale = NA_DH ** -0.5
    kc = kc_ref[...]
    vc = vc_ref[...]
    nt = (((1,), (1,)), ((), ()))

    def body(r, _):
        rs = jnp.clip(r - NA_WR // 2, 0, nrow - wr)
        cls = jnp.where(r < NA_WR // 2, r, jnp.where(r <= nrow - NA_WR // 2, NA_WR // 2, r - (nrow - wr)))
        qrows = pl.ds(pl.multiple_of(r * GRID_W, GRID_W), GRID_W)
        krows = pl.ds(pl.multiple_of(rs * GRID_W, GRID_W), win)
        q = q_ref[qrows, :]
        tab = bias_ref[cls]
        s = lax.dot_general(q, k_ref[krows, :], nt, preferred_element_type=F32)
        s = jnp.where(tab > 0.5 * NEG, s * scale + tab, NEG)
        sc = lax.dot_general(q, kc, nt, preferred_element_type=F32) * scale
        m = jnp.maximum(jnp.max(s, axis=-1, keepdims=True), jnp.max(sc, axis=-1, keepdims=True))
        e = jnp.exp(s - m)
        ec = jnp.exp(sc - m)
        den = jnp.sum(e, axis=-1, keepdims=True) + jnp.sum(ec, axis=-1, keepdims=True)
        o = jnp.dot(e.astype(BF16), v_ref[krows, :], preferred_element_type=F32)
        o = o + jnp.dot(ec.astype(BF16), vc, preferred_element_type=F32)
        o_ref[qrows, :] = (o / den).astype(o_ref.dtype)
        return 0

    lax.fori_loop(0, nrow, body, 0)


def _na_bias_table(rpb, nrow):
    wr = min(NA_WR, nrow)
    half = NA_WR // 2
    reps = np.array(list(range(half)) + [half] + list(range(nrow - half + 1, nrow)), dtype=np.int64)
    rs = np.clip(reps - half, 0, nrow - wr)
    dr = rs[:, None] + np.arange(wr)[None] - reps[:, None] + (NA_WR - 1)
    ci = np.arange(GRID_W)
    cs = np.clip(ci - NA_WC // 2, 0, GRID_W - NA_WC)
    colmask = (ci[None, :] >= cs[:, None]) & (ci[None, :] < cs[:, None] + NA_WC)
    dc = np.clip(ci[None, :] - ci[:, None] + NA_WC - 1, 0, 2 * NA_WC - 2)
    tab = rpb.astype(F32)[:, dr[:, None, :, None], dc[None, :, None, :]]
    tab = jnp.where(colmask[None, None, :, None, :], tab, NEG)
    return tab.reshape(rpb.shape[0], reps.shape[0], GRID_W, wr * GRID_W)


def _na_attention(z, bias_tab, *, nb, seq, cseq, crow0):
    kcol, vcol, qcol = 1536 // LANES, 2560 // LANES, 5120 // LANES
    crb0 = crow0 // cseq
    ncls = bias_tab.shape[1]
    win = bias_tab.shape[3]
    return pl.pallas_call(
        _na_kernel,
        grid=(nb, NA_HEADS),
        in_specs=[
            pl.BlockSpec((seq, NA_DH), lambda b, h: (b, qcol + h)),
            pl.BlockSpec((seq, NA_DH), lambda b, h: (b, kcol + h)),
            pl.BlockSpec((seq, NA_DH), lambda b, h: (b, vcol + h)),
            pl.BlockSpec((cseq, NA_DH), lambda b, h: (crb0 + b, kcol + h)),
            pl.BlockSpec((cseq, NA_DH), lambda b, h: (crb0 + b, vcol + h)),
            pl.BlockSpec((None, ncls, GRID_W, win), lambda b, h: (h, 0, 0, 0)),
        ],
        out_specs=pl.BlockSpec((seq, NA_DH), lambda b, h: (b, h)),
        out_shape=jax.ShapeDtypeStruct((nb * seq, NA_HEADS * NA_DH), BF16),
        compiler_params=_cparams("parallel", "arbitrary"),
        name="na_attention",
    )(z, z, z, z, z, bias_tab)


def _dense_attn_kernel(q_ref, k_ref, v_ref, o_ref):
    s = lax.dot_general(q_ref[...], k_ref[...], (((1,), (1,)), ((), ())), preferred_element_type=F32) * NA_DH ** -0.5
    e = jnp.exp(s - jnp.max(s, axis=-1, keepdims=True))
    o = jnp.dot(e.astype(BF16), v_ref[...], preferred_element_type=F32)
    o_ref[...] = (o / jnp.sum(e, axis=-1, keepdims=True)).astype(o_ref.dtype)


def _dense_attention(z, *, nb, cseq, crow0):
    kcol, vcol, qcol = 1536 // LANES, 2560 // LANES, 5120 // LANES
    crb0 = crow0 // cseq
    return pl.pallas_call(
        _dense_attn_kernel,
        grid=(nb, NA_HEADS),
        in_specs=[
            pl.BlockSpec((cseq, NA_DH), lambda b, h: (crb0 + b, qcol + h)),
            pl.BlockSpec((cseq, NA_DH), lambda b, h: (crb0 + b, kcol + h)),
            pl.BlockSpec((cseq, NA_DH), lambda b, h: (crb0 + b, vcol + h)),
        ],
        out_specs=pl.BlockSpec((cseq, NA_DH), lambda b, h: (b, h)),
        out_shape=jax.ShapeDtypeStruct((nb * cseq, NA_HEADS * NA_DH), BF16),
        compiler_params=_cparams("parallel", "arbitrary"),
        name="dense_attention",
    )(z, z, z)


HY_FEAT_ROWS = 64
HIGHEST = lax.Precision.HIGHEST


def _hy_features(seq):
    t = np.linspace(0.0, 1.0, seq)
    bands = (HY_EMB - 1) // 2
    w = 2.0 * math.pi * np.arange(seq) / seq
    fr = np.linspace(1e-4, bands - 1, bands)
    ang = fr[None] * w[:, None]
    feat = np.concatenate([t[:, None], np.cos(ang), -np.sin(ang)], axis=-1)
    src = np.concatenate([np.arange(seq), np.zeros(1, np.int64), np.arange(seq - 1, 0, -1)])
    feat2 = np.zeros((HY_FEAT_ROWS, 2 * seq), np.float32)
    feat2[:HY_EMB] = feat[src].T
    tt = t[src][None].astype(np.float32)
    mask = np.ones((1, 2 * seq), np.float32)
    mask[0, seq] = 0.0
    return feat2, tt, mask


def _hy_mlp_kernel(feat_ref, w1_ref, b1_ref, w2_ref, b2_ref, fq_ref, o_ref):
    fq = fq_ref[...]
    h = jnp.dot(w1_ref[...], feat_ref[...], preferred_element_type=F32, precision=HIGHEST)
    h = jnp.sin(fq * (h + b1_ref[...]))
    h = jnp.dot(w2_ref[...], h, preferred_element_type=F32, precision=HIGHEST)
    o_ref[...] = jnp.sin(fq * (h + b2_ref[...]))


def _hy_mlp(feat2, w1t, b1, w2t, b2, fq):
    depth, hid, _ = w1t.shape
    n = feat2.shape[1]
    col = lambda a: a.reshape(depth, hid, 1)
    wspec = lambda k: pl.BlockSpec((None, hid, k), lambda l: (l, 0, 0))
    return pl.pallas_call(
        _hy_mlp_kernel,
        grid=(depth,),
        in_specs=[pl.BlockSpec((HY_FEAT_ROWS, n), lambda l: (0, 0)), wspec(HY_FEAT_ROWS), wspec(1), wspec(hid),
                  wspec(1), wspec(1)],
        out_specs=pl.BlockSpec((None, hid, n), lambda l: (l, 0, 0)),
        out_shape=jax.ShapeDtypeStruct((depth, hid, n), F32),
        compiler_params=_cparams("parallel"),
        name="hyena_filter_mlp",
    )(feat2, w1t, col(b1), w2t, col(b2), col(fq))


def _hy_filter_kernel(h_ref, w3_ref, dl_ref, tt_ref, mask_ref, o_ref):
    seq = h_ref.shape[1] // 2
    kf = jnp.dot(w3_ref[0], h_ref[:, :seq], preferred_element_type=F32, precision=HIGHEST)
    kb = jnp.dot(w3_ref[1], h_ref[:, seq:], preferred_element_type=F32, precision=HIGHEST)
    k = jnp.concatenate([kf, kb], axis=1) * (jnp.exp(-tt_ref[...] * dl_ref[...]) * mask_ref[...])
    o_ref[...] = k * lax.rsqrt(jnp.sum(k * k, axis=1, keepdims=True) + EPS)


def _hy_filter(h2, w3t, deltas, tt, mask, cb):
    depth, hid, n = h2.shape
    c = w3t.shape[3]
    return pl.pallas_call(
        _hy_filter_kernel,
        grid=(depth, HY_ORDER, c // cb),
        in_specs=[
            pl.BlockSpec((None, hid, n), lambda l, o, j: (l, 0, 0)),
            pl.BlockSpec((None, None, 2, cb, hid), lambda l, o, j: (l, o, 0, j, 0)),
            pl.BlockSpec((cb, 1), lambda l, o, j: (j, 0)),
            pl.BlockSpec((1, n), lambda l, o, j: (0, 0)),
            pl.BlockSpec((1, n), lambda l, o, j: (0, 0)),
        ],
        out_specs=pl.BlockSpec((None, None, cb, n), lambda l, o, j: (l, o, j, 0)),
        out_shape=jax.ShapeDtypeStruct((depth, HY_ORDER, c, n), F32),
        compiler_params=_cparams("parallel", "parallel", "arbitrary"),
        name="hyena_filter",
    )(h2, w3t, deltas, tt, mask)


FFT_NO = 64
FFT_NI = 128


def _fft_tables():
    n = FFT_NO * FFT_NI
    a = np.arange(FFT_NO)
    fo = np.exp(-2j * np.pi * np.outer(a, a) / FFT_NO)
    i = np.arange(FFT_NI)
    ci = np.exp(-2j * np.pi * np.outer(i, i) / FFT_NI)
    tw = np.exp(-2j * np.pi * np.outer(a, i) / n)
    half = FFT_NO // 2
    f32 = lambda x: np.ascontiguousarray(x, dtype=np.float32)
    g1_real = f32(np.concatenate([fo.real, fo.imag], axis=0))
    g1 = f32(np.block([[fo.real[:, :half], -fo.imag[:, :half]], [fo.imag[:, :half], fo.real[:, :half]]]))
    w2 = f32(np.block([[ci.real, ci.imag], [-ci.imag, ci.real]]))
    w2i = f32(np.block([[ci.real, -ci.imag], [ci.imag, ci.real]]))
    g4 = f32(np.block([[fo.real[:half], fo.imag[:half]], [-fo.imag[:half], fo.real[:half]]]) / n)
    return dict(g1_real=g1_real, g1=g1, w2=w2, w2i=w2i, g4=g4, twr=f32(tw.real), twi=f32(tw.imag))


def _hy_spectrum_kernel(k_ref, g1_ref, w2_ref, twr_ref, twi_ref, o_ref, y_scr):
    cb = k_ref.shape[0]
    g1 = g1_ref[...]
    twr = twr_ref[...]
    twi = twi_ref[...]

    def body(c, _):
        y = jnp.dot(g1, k_ref[c].astype(BF16), preferred_element_type=F32)
        yr, yi = y[:FFT_NO], y[FFT_NO:]
        rows = pl.ds(pl.multiple_of(c * FFT_NO, FFT_NO), FFT_NO)
        y_scr[rows, :] = jnp.concatenate([yr * twr - yi * twi, yr * twi + yi * twr], axis=1).astype(BF16)
        return 0

    lax.fori_loop(0, cb, body, 0)
    z = jnp.dot(y_scr[...], w2_ref[...], preferred_element_type=F32)
    o_ref[...] = z.reshape(cb, FFT_NO, 2 * FFT_NI)


def _hy_spectrum(kfilt, tabs, cb):
    g, c = kfilt.shape[:2]
    const = lambda a: pl.BlockSpec(a.shape, lambda i, j: (0,) * a.ndim)
    g1 = jnp.asarray(tabs["g1_real"], BF16)
    w2 = jnp.asarray(tabs["w2"], BF16)
    twr, twi = jnp.asarray(tabs["twr"]), jnp.asarray(tabs["twi"])
    return pl.pallas_call(
        _hy_spectrum_kernel,
        grid=(g, c // cb),
        in_specs=[pl.BlockSpec((None, cb, FFT_NO, FFT_NI), lambda i, j: (i, j, 0, 0)),
                  const(g1), const(w2), const(twr), const(twi)],
        out_specs=pl.BlockSpec((None, cb, FFT_NO, 2 * FFT_NI), lambda i, j: (i, j, 0, 0)),
        out_shape=jax.ShapeDtypeStruct((g, c, FFT_NO, 2 * FFT_NI), F32),
        scratch_shapes=[pltpu.VMEM((cb * FFT_NO, 2 * FFT_NI), BF16)],
        compiler_params=_cparams("parallel", "arbitrary"),
        name="hyena_spectrum",
    )(kfilt, g1, w2, twr, twi)


def _shift_conv3(x, w0, w1, w2, b):
    nrow, nlane = x.shape
    row = lax.broadcasted_iota(jnp.int32, x.shape, 0)
    lane = lax.broadcasted_iota(jnp.int32, x.shape, 1)
    r = pltpu.roll(x, 1, axis=1)
    prev = jnp.where(lane == 0, pltpu.roll(r, 1, axis=0), r)
    prev = jnp.where((lane == 0) & (row == 0), 0.0, prev)
    r = pltpu.roll(x, nlane - 1, axis=1)
    nxt = jnp.where(lane == nlane - 1, pltpu.roll(r, nrow - 1, axis=0), r)
    nxt = jnp.where((lane == nlane - 1) & (row == nrow - 1), 0.0, nxt)
    return prev * w0 + x * w1 + nxt * w2 + b


def _hyena_kernel(x_ref, kf_ref, pm_ref, g1_ref, w2_ref, w2i_ref, g4_ref, twr_ref, twi_ref, o_ref,
                  xg_scr, y_scr, a_scr, q_scr):
    cb = x_ref.shape[2]
    half = FFT_NO // 2
    g1 = g1_ref[...]
    g4 = g4_ref[...]
    twr = twr_ref[...]
    twi = twi_ref[...]

    def conv_body(c, _):
        for part in range(3):
            for bb in range(2):
                x = x_ref[bb, part, c].astype(F32)
                v = _shift_conv3(x, pm_ref[part, c], pm_ref[3 + part, c], pm_ref[6 + part, c], pm_ref[9 + part, c])
                rows = slice(bb * half, (bb + 1) * half)
                if part < 2:
                    xg_scr[part, c, rows, :] = v
                else:
                    y_scr[c, rows, :] = v
        return 0

    lax.fori_loop(0, cb, conv_body, 0)

    chunk = 8
    for order in range(HY_ORDER):
        def fwd_body(c, _):
            y = jnp.dot(g1, y_scr[c].astype(BF16), preferred_element_type=F32)
            yr, yi = y[:FFT_NO], y[FFT_NO:]
            rows = pl.ds(pl.multiple_of(c * FFT_NO, FFT_NO), FFT_NO)
            a_scr[rows, :] = jnp.concatenate([yr * twr - yi * twi, yr * twi + yi * twr], axis=1).astype(BF16)
            return 0

        lax.fori_loop(0, cb, fwd_body, 0)

        def mid_body(j, _):
            rows = pl.ds(pl.multiple_of(j * chunk * FFT_NO, chunk * FFT_NO), chunk * FFT_NO)
            z = jnp.dot(a_scr[rows, :], w2_ref[...], preferred_element_type=F32)
            kf = kf_ref[order, pl.ds(pl.multiple_of(j * chunk, chunk), chunk)].reshape(chunk * FFT_NO, 2 * FFT_NI)
            zr, zi = z[:, :FFT_NI], z[:, FFT_NI:]
            kr, ki = kf[:, :FFT_NI], kf[:, FFT_NI:]
            p = jnp.concatenate([zr * kr - zi * ki, zr * ki + zi * kr], axis=1).astype(BF16)
            q_scr[rows, :] = jnp.dot(p, w2i_ref[...], preferred_element_type=F32)
            return 0

        lax.fori_loop(0, cb // chunk, mid_body, 0)

        def inv_body(c, _):
            rows = pl.ds(pl.multiple_of(c * FFT_NO, FFT_NO), FFT_NO)
            q = q_scr[rows, :]
            qr, qi = q[:, :FFT_NI], q[:, FFT_NI:]
            qs = jnp.concatenate([qr * twr + qi * twi, qi * twr - qr * twi], axis=0).astype(BF16)
            conv = jnp.dot(g4, qs, preferred_element_type=F32)
            y = xg_scr[order, c] * (conv + pm_ref[12 + order, c] * y_scr[c])
            if order + 1 < HY_ORDER:
                y_scr[c] = y
            else:
                o_ref[0, c] = y[:half].astype(o_ref.dtype)
                o_ref[1, c] = y[half:].astype(o_ref.dtype)
            return 0

        lax.fori_loop(0, cb, inv_body, 0)


def _hyena(xt, kf, pm, tabs, cb):
    nb, _, c, half, _ = xt.shape
    const = lambda a: pl.BlockSpec(a.shape, lambda j, p: (0,) * a.ndim)
    bf = lambda name: jnp.asarray(tabs[name], BF16)
    g1, w2, w2i, g4 = bf("g1"), bf("w2"), bf("w2i"), bf("g4")
    twr, twi = jnp.asarray(tabs["twr"]), jnp.asarray(tabs["twi"])
    return pl.pallas_call(
        _hyena_kernel,
        grid=(c // cb, nb // 2),
        in_specs=[
            pl.BlockSpec((2, 3, cb, half, FFT_NI), lambda j, p: (p, 0, j, 0, 0)),
            pl.BlockSpec((HY_ORDER, cb, FFT_NO, 2 * FFT_NI), lambda j, p: (0, j, 0, 0)),
            pl.BlockSpec((14, cb, 1, FFT_NI), lambda j, p: (0, j, 0, 0)),
            const(g1), const(w2), const(w2i), const(g4), const(twr), const(twi),
        ],
        out_specs=pl.BlockSpec((2, cb, half, FFT_NI), lambda j, p: (p, j, 0, 0)),
        out_shape=jax.ShapeDtypeStruct((nb, c, half, FFT_NI), BF16),
        scratch_shapes=[
            pltpu.VMEM((2, cb, FFT_NO, FFT_NI), F32),
            pltpu.VMEM((cb, FFT_NO, FFT_NI), F32),
            pltpu.VMEM((cb * FFT_NO, 2 * FFT_NI), BF16),
            pltpu.VMEM((cb * FFT_NO, 2 * FFT_NI), F32),
        ],
        compiler_params=_cparams("parallel", "arbitrary"),
        name="hyena",
    )(xt, kf, pm, g1, w2, w2i, g4, twr, twi)


def _dft_tables(seq):
    n = 2 * seq
    wmat = np.exp(-2j * np.pi * np.outer(np.arange(n), np.arange(n)) / n)
    f32 = lambda x: np.ascontiguousarray(x, dtype=np.float32)
    fwd_real = f32(np.concatenate([wmat.real, wmat.imag], axis=1))
    ws = wmat[:seq]
    fwd = f32(np.block([[ws.real, ws.imag], [-ws.imag, ws.real]]))
    wi = np.conj(wmat)[:, :seq] / n
    inv = f32(np.block([[wi.real, wi.imag], [-wi.imag, wi.real]]))
    return dict(fwd_real=fwd_real, fwd=fwd, inv=inv)


def _rowdft_kernel(x_ref, w_ref, o_ref):
    o_ref[...] = jnp.dot(x_ref[...].astype(BF16), w_ref[...], preferred_element_type=F32)


def _rowdft(x, w, tr):
    m, k = x.shape
    n = w.shape[1]
    return pl.pallas_call(
        _rowdft_kernel,
        grid=(m // tr,),
        in_specs=[pl.BlockSpec((tr, k), lambda i: (i, 0)), pl.BlockSpec((k, n), lambda i: (0, 0))],
        out_specs=pl.BlockSpec((tr, n), lambda i: (i, 0)),
        out_shape=jax.ShapeDtypeStruct((m, n), F32),
        compiler_params=_cparams("parallel"),
        name="row_dft",
    )(x, w)


def _lane_conv3(x, w0, w1, w2, b):
    seq = x.shape[1]
    lane = lax.broadcasted_iota(jnp.int32, x.shape, 1)
    prev = jnp.where(lane == 0, 0.0, pltpu.roll(x, 1, axis=1))
    nxt = jnp.where(lane == seq - 1, 0.0, pltpu.roll(x, seq - 1, axis=1))
    return prev * w0 + x * w1 + nxt * w2 + b


def _hyena_ctx_kernel(x_ref, kf_ref, pm_ref, fwd_ref, inv_ref, o_ref):
    seq = x_ref.shape[3]
    n = 2 * seq
    conv = [[_lane_conv3(x_ref[bb, part].astype(F32), pm_ref[part], pm_ref[3 + part], pm_ref[6 + part],
                         pm_ref[9 + part]) for bb in range(2)] for part in range(3)]
    ya, yb = conv[2]
    for order in range(HY_ORDER):
        z = jnp.dot(jnp.concatenate([ya, yb], axis=1).astype(BF16), fwd_ref[...], preferred_element_type=F32)
        kf = kf_ref[order]
        zr, zi, kr, ki = z[:, :n], z[:, n:], kf[:, :n], kf[:, n:]
        p = jnp.concatenate([zr * kr - zi * ki, zr * ki + zi * kr], axis=1).astype(BF16)
        cv = jnp.dot(p, inv_ref[...], preferred_element_type=F32)
        bias = pm_ref[12 + order]
        ya = conv[order][0] * (cv[:, :seq] + bias * ya)
        yb = conv[order][1] * (cv[:, seq:] + bias * yb)
    o_ref[0] = ya.astype(o_ref.dtype)
    o_ref[1] = yb.astype(o_ref.dtype)


def _hyena_ctx(xt, kf, pm, tabs, cb):
    nb, _, c, seq = xt.shape
    fwd, inv = jnp.asarray(tabs["fwd"], BF16), jnp.asarray(tabs["inv"], BF16)
    const = lambda a: pl.BlockSpec(a.shape, lambda j, p: (0,) * a.ndim)
    return pl.pallas_call(
        _hyena_ctx_kernel,
        grid=(c // cb, nb // 2),
        in_specs=[
            pl.BlockSpec((2, 3, cb, seq), lambda j, p: (p, 0, j, 0)),
            pl.BlockSpec((HY_ORDER, cb, 4 * seq), lambda j, p: (0, j, 0)),
            pl.BlockSpec((14, cb, 1), lambda j, p: (0, j, 0)),
            const(fwd), const(inv),
        ],
        out_specs=pl.BlockSpec((2, cb, seq), lambda j, p: (p, j, 0)),
        out_shape=jax.ShapeDtypeStruct((nb, c, seq), BF16),
        compiler_params=_cparams("parallel", "arbitrary"),
        name="hyena_ctx",
    )(xt, kf, pm, fwd, inv)


def _rope_tables(seq):
    t = np.arange(seq)
    pos = np.stack([t // GRID_W, t % GRID_W], axis=1).astype(np.float64)
    n = RET_DK // 4
    inv = ROPE_BASE ** (-np.arange(n, dtype=np.float64) / n)
    lane = np.arange(LANES) % RET_DK
    ang = pos[:, lane // (2 * n)] * inv[lane % n][None]
    sign = np.where(lane % (2 * n) < n, -1.0, 1.0)
    return np.cos(ang).astype(np.float32), (np.sin(ang) * sign[None]).astype(np.float32)


def _hy_filters(seq, hy_f_w1, hy_f_b1, hy_f_w2, hy_f_b2, hy_f_w3, hy_f_freq):
    depth, _, hid = hy_f_w1.shape
    c = hy_f_w3.shape[2] // (2 * HY_ORDER)
    feat2, tt, mask = _hy_features(seq)
    w1t = jnp.pad(hy_f_w1.transpose(0, 2, 1), ((0, 0), (0, 0), (0, HY_FEAT_ROWS - HY_EMB)))
    h2 = _hy_mlp(jnp.asarray(feat2), w1t, hy_f_b1, hy_f_w2.transpose(0, 2, 1), hy_f_b2, hy_f_freq)
    w3t = hy_f_w3.reshape(depth, hid, HY_ORDER, 2, c).transpose(0, 2, 3, 4, 1)
    deltas = np.abs(np.linspace(math.log(HY_TARGET) / HY_FAST, math.log(HY_TARGET) / HY_SLOW, c))
    deltas = jnp.asarray(deltas.astype(np.float32).reshape(c, 1))
    return _hy_filter(h2, w3t, deltas, jnp.asarray(tt), jnp.asarray(mask), min(c, 8 * 1024 * 128 // (2 * seq)))


def _hy_params(hy_conv_w, hy_conv_b, hy_bias):
    c = hy_bias.shape[1]
    return jnp.concatenate([hy_conv_w.reshape(9, c), hy_conv_b.reshape(3, c), hy_bias], axis=0)


def kernel(x, c, ctx, c_ctx, w_mod, b_mod, g_norm1, g_norm2, w_in, hy_conv_w, hy_conv_b, hy_f_w1, hy_f_b1, hy_f_w2, hy_f_b2, hy_f_w3, hy_f_freq, hy_bias, ret_log_decay, na_rpb, w_branch, w_out, ffn_w_in, ffn_conv_w, ffn_conv_b, ffn_w_out, g_final):
    nb, seq, d = x.shape
    cseq = ctx.shape[1]
    depth = w_mod.shape[0]
    hy_w = hy_bias.shape[2]
    t_lat, t_ctx = nb * seq, nb * cseq
    assert seq % TM == 0 and t_ctx % TM == 0 and TM % cseq == 0 and nb % 2 == 0 and nb < SUBLANES
    assert 2 * seq == FFT_NO * FFT_NI and seq % GRID_W == 0
    blk_lat, blk_all = t_lat // TM, (t_lat + t_ctx) // TM
    mrow = jnp.asarray(np.concatenate([np.repeat(np.arange(nb), seq // TM), np.full(t_ctx // TM, nb)]), jnp.int32)
    seqlen = jnp.asarray(np.concatenate([np.full(blk_lat, seq), np.full(t_ctx // TM, cseq)]), jnp.int32)

    cond = jnp.zeros((SUBLANES, d), F32).at[:nb].set(c).at[nb].set(c_ctx)
    mods = _mods(cond, w_mod, b_mod).reshape(depth, SUBLANES, 6, 1, d)

    cos, sin = (jnp.asarray(a) for a in _rope_tables(seq))
    tabs = _fft_tables()
    ctabs = _dft_tables(cseq)
    filt = _hy_filters(seq, hy_f_w1, hy_f_b1, hy_f_w2, hy_f_b2, hy_f_w3, hy_f_freq)
    spec = _hy_spectrum(filt.reshape(depth * HY_ORDER, hy_w, FFT_NO, FFT_NI), tabs, 32)
    spec = spec.reshape(depth, HY_ORDER, hy_w, FFT_NO, 2 * FFT_NI)
    cfilt = _hy_filters(cseq, hy_f_w1, hy_f_b1, hy_f_w2, hy_f_b2, hy_f_w3, hy_f_freq)
    cspec = _rowdft(cfilt.reshape(depth * HY_ORDER * hy_w, 2 * cseq), jnp.asarray(ctabs["fwd_real"], BF16), 1024)
    cspec = cspec.reshape(depth, HY_ORDER, hy_w, 4 * cseq)

    xs = jnp.concatenate([x.reshape(t_lat, d), ctx.reshape(t_ctx, d)], axis=0)
    hy0 = 6144
    gate0 = hy0 + 3 * hy_w
    s_zero = jnp.zeros((nb, RET_HEADS, 2, LANES, RET_DV), F32)
    for l in range(depth):
        last = l == depth - 1
        nblk = blk_lat if last else blk_all
        hn = _normmod(xs, g_norm1[l], mods[l], mrow, 0, 1, blk_all)
        z = _matmul(hn, w_in[l].astype(BF16), blk_all, 1024)

        yc_ret, s_ctx = _retention(z, ret_log_decay[l], cos, sin, s_zero, nb=nb, seq=cseq, row0=t_lat, rope=False)
        yl_ret, _ = _retention(z, ret_log_decay[l], cos, sin, s_ctx, nb=nb, seq=seq, row0=0, rope=True)
        yl_na = _na_attention(z, _na_bias_table(na_rpb[l], seq // GRID_W), nb=nb, seq=seq, cseq=cseq, crow0=t_lat)
        pm = _hy_params(hy_conv_w[l], hy_conv_b[l], hy_bias[l])
        zh = z[:t_lat, hy0:gate0].reshape(nb, seq, 3, hy_w).transpose(0, 2, 3, 1)
        pm_lat = jnp.broadcast_to(pm[:, :, None, None], (14, hy_w, 1, FFT_NI))
        yl_hy = _hyena(zh.reshape(nb, 3, hy_w, FFT_NO // 2, FFT_NI), spec[l], pm_lat, tabs, 32)
        yl_hy = yl_hy.reshape(nb, hy_w, seq).transpose(0, 2, 1).reshape(t_lat, hy_w)
        if last:
            ys = (yl_hy, yl_ret, yl_na)
        else:
            yc_na = _dense_attention(z, nb=nb, cseq=cseq, crow0=t_lat)
            zc = z[t_lat:, hy0:gate0].reshape(nb, cseq, 3, hy_w).transpose(0, 2, 3, 1)
            yc_hy = _hyena_ctx(zc, cspec[l], pm[:, :, None], ctabs, 256)
            yc_hy = yc_hy.transpose(0, 2, 1).reshape(t_ctx, hy_w)
            ys = tuple(jnp.concatenate(p, axis=0) for p in ((yl_hy, yc_hy), (yl_ret, yc_ret), (yl_na, yc_na)))
        acc = _merge(ys, z, gate0, w_branch[l].astype(BF16), nblk, 1024)
        xs = _res_matmul(acc, w_out[l].astype(BF16), xs, mods[l], mrow, 2, nblk, 1024)
        hn = _normmod(xs, g_norm2[l], mods[l], mrow, 3, 4, nblk)
        u = _matmul(hn, ffn_w_in[l].astype(BF16), nblk, 1024)
        hg = _ffn_gate(u, ffn_conv_w[l], ffn_conv_b[l], seqlen, nblk, 512)
        xs = _res_matmul(hg, ffn_w_out[l].astype(BF16), xs, mods[l], mrow, 5, nblk, 512)
    return _final_norm(xs, g_final, blk_lat).reshape(nb, seq, d)
```

```python
import functools
import math

import numpy as np
import jax
import jax.numpy as jnp
from jax import lax
from jax.experimental import pallas as pl
from jax.experimental.pallas import tpu as pltpu

F32 = jnp.float32
BF16 = jnp.bfloat16

GRID_W = 64
N_BRANCH = 3
HY_ORDER = 2
HY_EMB = 33
HY_FAST = 0.3
HY_SLOW = 1.5
HY_TARGET = 1e-2
RET_HEADS = 8
RET_DK = 64
RET_DV = 128
RET_CHUNK = 128
NA_HEADS = 8
NA_DH = 128
NA_WR = 8
NA_WC = 16
ROPE_BASE = 10000.0
EPS = 1e-6
NEG = -1e30

LANES = 128
SUBLANES = 8
VMEM_LIMIT = 56 * 1024 * 1024

TM = 1024


def _cparams(*sem):
    return pltpu.CompilerParams(dimension_semantics=sem, vmem_limit_bytes=VMEM_LIMIT)


def _mods_kernel(a_ref, w_ref, b_ref, o_ref):
    a = a_ref[...]
    a = a * jax.nn.sigmoid(a)
    o_ref[...] = jnp.dot(a.astype(BF16), w_ref[...].astype(BF16), preferred_element_type=F32) + b_ref[...]


def _mods(cond, w_mod, b_mod):
    depth, d, n = w_mod.shape
    tn = 1024
    return pl.pallas_call(
        _mods_kernel,
        grid=(depth, n // tn),
        in_specs=[
            pl.BlockSpec((SUBLANES, d), lambda l, j: (0, 0)),
            pl.BlockSpec((None, d, tn), lambda l, j: (l, 0, j)),
            pl.BlockSpec((None, 1, tn), lambda l, j: (l, 0, j)),
        ],
        out_specs=pl.BlockSpec((None, SUBLANES, tn), lambda l, j: (l, 0, j)),
        out_shape=jax.ShapeDtypeStruct((depth, SUBLANES, n), F32),
        compiler_params=_cparams("parallel", "parallel"),
        name="mods",
    )(cond, w_mod, b_mod.reshape(depth, 1, n))


def _normmod_kernel(mrow_ref, x_ref, g_ref, sh_ref, sc_ref, o_ref):
    x = x_ref[...]
    y = x * lax.rsqrt(jnp.mean(x * x, axis=-1, keepdims=True) + EPS)
    y = y * g_ref[...]
    o_ref[...] = (y * (1.0 + sc_ref[...]) + sh_ref[...]).astype(o_ref.dtype)


def _normmod(x, g, mods, mrow, shift_idx, scale_idx, nblk):
    m, d = x.shape
    grid_spec = pltpu.PrefetchScalarGridSpec(
        num_scalar_prefetch=1,
        grid=(nblk,),
        in_specs=[
            pl.BlockSpec((TM, d), lambda i, mr: (i, 0)),
            pl.BlockSpec((1, d), lambda i, mr: (0, 0)),
            pl.BlockSpec((None, None, 1, d), lambda i, mr: (mr[i], shift_idx, 0, 0)),
            pl.BlockSpec((None, None, 1, d), lambda i, mr: (mr[i], scale_idx, 0, 0)),
        ],
        out_specs=pl.BlockSpec((TM, d), lambda i, mr: (i, 0)),
    )
    return pl.pallas_call(
        _normmod_kernel,
        grid_spec=grid_spec,
        out_shape=jax.ShapeDtypeStruct((m, d), BF16),
        compiler_params=_cparams("parallel"),
        name="normmod",
    )(mrow, x, g.reshape(1, d), mods, mods)


def _final_norm_kernel(x_ref, g_ref, o_ref):
    x = x_ref[...]
    y = x * lax.rsqrt(jnp.mean(x * x, axis=-1, keepdims=True) + EPS)
    o_ref[...] = y * g_ref[...]


def _final_norm(x, g, nblk):
    m, d = x.shape
    return pl.pallas_call(
        _final_norm_kernel,
        grid=(nblk,),
        in_specs=[pl.BlockSpec((TM, d), lambda i: (i, 0)), pl.BlockSpec((1, d), lambda i: (0, 0))],
        out_specs=pl.BlockSpec((TM, d), lambda i: (i, 0)),
        out_shape=jax.ShapeDtypeStruct((nblk * TM, d), F32),
        compiler_params=_cparams("parallel"),
        name="final_norm",
    )(x, g.reshape(1, d))


def _mm_kernel(a_ref, w_ref, o_ref):
    o_ref[...] = jnp.dot(a_ref[...], w_ref[...], preferred_element_type=F32).astype(o_ref.dtype)


def _matmul(a, w, nblk, tn):
    m, k = a.shape
    n = w.shape[1]
    return pl.pallas_call(
        _mm_kernel,
        grid=(nblk, n // tn),
        in_specs=[pl.BlockSpec((TM, k), lambda i, j: (i, 0)), pl.BlockSpec((k, tn), lambda i, j: (0, j))],
        out_specs=pl.BlockSpec((TM, tn), lambda i, j: (i, j)),
        out_shape=jax.ShapeDtypeStruct((m, n), BF16),
        compiler_params=_cparams("parallel", "arbitrary"),
        name="matmul",
    )(a, w)


def _mm_nt_kernel(w_ref, a_ref, o_ref):
    r = lax.dot_general(w_ref[...], a_ref[...], (((1,), (1,)), ((), ())), preferred_element_type=F32)
    per = o_ref.shape[0]
    width = r.shape[1] // per
    for s in range(per):
        o_ref[s] = r[:, s * width:(s + 1) * width].astype(o_ref.dtype)


def _matmul_nt(a, wt, *, blk0, nblk, nb, seq, tc):
    n, k = wt.shape
    per, sblk = max(TM // seq, 1), max(seq // TM, 1)
    return pl.pallas_call(
        _mm_nt_kernel,
        grid=(nblk, n // tc),
        in_specs=[pl.BlockSpec((tc, k), lambda i, j: (j, 0)), pl.BlockSpec((TM, k), lambda i, j: (blk0 + i, 0))],
        out_specs=pl.BlockSpec((per, tc, TM // per), lambda i, j: (i // sblk, j, i % sblk)),
        out_shape=jax.ShapeDtypeStruct((nb, n, seq), BF16),
        compiler_params=_cparams("parallel", "arbitrary"),
        name="matmul_nt",
    )(wt, a)


def _resmm_kernel(mrow_ref, a_ref, w_ref, x_ref, g_ref, o_ref):
    y = jnp.dot(a_ref[...], w_ref[...], preferred_element_type=F32)
    o_ref[...] = x_ref[...] + g_ref[...] * y


def _res_matmul(a, w, x, mods, mrow, gate_idx, nblk, tn):
    m, k = a.shape
    d = w.shape[1]
    grid_spec = pltpu.PrefetchScalarGridSpec(
        num_scalar_prefetch=1,
        grid=(nblk, d // tn),
        in_specs=[
            pl.BlockSpec((TM, k), lambda i, j, mr: (i, 0)),
            pl.BlockSpec((k, tn), lambda i, j, mr: (0, j)),
            pl.BlockSpec((TM, tn), lambda i, j, mr: (i, j)),
            pl.BlockSpec((None, None, 1, tn), lambda i, j, mr: (mr[i], gate_idx, 0, j)),
        ],
        out_specs=pl.BlockSpec((TM, tn), lambda i, j, mr: (i, j)),
    )
    return pl.pallas_call(
        _resmm_kernel,
        grid_spec=grid_spec,
        out_shape=jax.ShapeDtypeStruct((m, d), F32),
        compiler_params=_cparams("parallel", "arbitrary"),
        name="res_matmul",
    )(mrow, a, w, x, mods)


def _merge_kernel(y0_ref, y1_ref, y2_ref, g0_ref, g1_ref, g2_ref, w_ref, o_ref):
    acc = None
    for i, (y_ref, g_ref) in enumerate(((y0_ref, g0_ref), (y1_ref, g1_ref), (y2_ref, g2_ref))):
        t = jnp.dot(y_ref[...], w_ref[i], preferred_element_type=F32)
        t = jax.nn.sigmoid(g_ref[...].astype(F32)) * t
        acc = t if acc is None else acc + t
    o_ref[...] = acc.astype(o_ref.dtype)


def _merge(ys, z, gate_col0, w_branch, nblk, tn):
    m, bw = ys[0].shape
    d = w_branch.shape[2]
    gspec = lambda i_br: pl.BlockSpec((TM, tn), lambda i, j: (i, (gate_col0 + i_br * d) // tn + j))
    yspec = pl.BlockSpec((TM, bw), lambda i, j: (i, 0))
    return pl.pallas_call(
        _merge_kernel,
        grid=(nblk, d // tn),
        in_specs=[yspec, yspec, yspec, gspec(0), gspec(1), gspec(2),
                  pl.BlockSpec((N_BRANCH, bw, tn), lambda i, j: (0, 0, j))],
        out_specs=pl.BlockSpec((TM, tn), lambda i, j: (i, j)),
        out_shape=jax.ShapeDtypeStruct((m, d), BF16),
        compiler_params=_cparams("parallel", "arbitrary"),
        name="merge",
    )(ys[0], ys[1], ys[2], z, z, z, w_branch)


def _ffn_gate_kernel(seq_ref, a_ref, ap_ref, an_ref, b_ref, cw_ref, cb_ref, o_ref):
    i = pl.program_id(0)
    seq_m1 = seq_ref[i] - 1
    a = a_ref[...].astype(F32)
    tm = a.shape[0]
    row = lax.broadcasted_iota(jnp.int32, (tm, 1), 0)
    pos = (i * tm + row) & seq_m1
    prev = jnp.where(row == 0, ap_ref[SUBLANES - 1:SUBLANES, :].astype(F32), pltpu.roll(a, 1, axis=0))
    prev = jnp.where(pos == 0, 0.0, prev)
    nxt = jnp.where(row == tm - 1, an_ref[0:1, :].astype(F32), pltpu.roll(a, tm - 1, axis=0))
    nxt = jnp.where(pos == seq_m1, 0.0, nxt)
    cw = cw_ref[...]
    conv = prev * cw[0:1, :] + a * cw[1:2, :] + nxt * cw[2:3, :] + cb_ref[...]
    o_ref[...] = (jax.nn.gelu(conv) * b_ref[...].astype(F32)).astype(o_ref.dtype)


def _ffn_gate(u, conv_w, conv_b, seqlen, nblk, tc):
    m, ff2 = u.shape
    ff = ff2 // 2
    nrow8 = m // SUBLANES
    r8 = TM // SUBLANES
    grid_spec = pltpu.PrefetchScalarGridSpec(
        num_scalar_prefetch=1,
        grid=(nblk, ff // tc),
        in_specs=[
            pl.BlockSpec((TM, tc), lambda i, j, s: (i, j)),
            pl.BlockSpec((SUBLANES, tc), lambda i, j, s: (jnp.maximum(i * r8 - 1, 0), j)),
            pl.BlockSpec((SUBLANES, tc), lambda i, j, s: (jnp.minimum((i + 1) * r8, nrow8 - 1), j)),
            pl.BlockSpec((TM, tc), lambda i, j, s: (i, ff // tc + j)),
            pl.BlockSpec((3, tc), lambda i, j, s: (0, j)),
            pl.BlockSpec((1, tc), lambda i, j, s: (0, j)),
        ],
        out_specs=pl.BlockSpec((TM, tc), lambda i, j, s: (i, j)),
    )
    return pl.pallas_call(
        _ffn_gate_kernel,
        grid_spec=grid_spec,
        out_shape=jax.ShapeDtypeStruct((m, ff), BF16),
        compiler_params=_cparams("parallel", "arbitrary"),
        name="ffn_gate",
    )(seqlen, u, u, u, u, conv_w, conv_b.reshape(1, ff))


def _rope_pair(x, cos, sin):
    lane = lax.broadcasted_iota(jnp.int32, x.shape, 1)
    first = lax.rem(lane, 32) < 16
    partner = jnp.where(first, pltpu.roll(x, LANES - 16, axis=1), pltpu.roll(x, 16, axis=1))
    return x * cos + partner * sin


def _ret_kernel(lg_ref, q_ref, k_ref, v_ref, g_ref, cos_ref, sin_ref, s0_ref, *rest, rope, has_prev):
    o_ref, sfin_ref, qr_scr, kr_scr, kv_scr = rest[1:] if has_prev else rest
    hp = pl.program_id(1)
    seq = q_ref.shape[0]
    c = RET_CHUNK
    nchunk = seq // c
    ks = RET_DK ** -0.5
    q = q_ref[...].astype(F32)
    k = k_ref[...].astype(F32)
    if rope:
        q = _rope_pair(q, cos_ref[...], sin_ref[...])
        k = _rope_pair(k, cos_ref[...], sin_ref[...])
    qr_scr[...] = q
    kr_scr[...] = k * ks

    pos_r = lax.broadcasted_iota(jnp.int32, (c, 1), 0).astype(F32)
    rel = (lax.broadcasted_iota(jnp.int32, (c, c), 0) - lax.broadcasted_iota(jnp.int32, (c, c), 1)).astype(F32)
    lane = lax.broadcasted_iota(jnp.int32, (1, LANES), 1)

    for hh in range(2):
        h = 2 * hp + hh
        lgf = lg_ref[0, h]
        lgb = lg_ref[1, h]
        hm = (lane // RET_DK == hh).astype(F32)
        zeta_f = jnp.exp(lgf * (c - 1 - pos_r))
        zeta_b = jnp.exp(lgb * pos_r)
        xi_f = jnp.exp(lgf * (pos_r + 1.0))
        xi_b = jnp.exp(lgb * (c - pos_r))
        dmat = jnp.where(rel >= 0, jnp.exp(lgf * jnp.maximum(rel, 0.0)), jnp.exp(lgb * jnp.maximum(-rel, 0.0)))
        gf = jnp.exp(lgf * c)
        gb = jnp.exp(lgb * c)
        vcols = slice(hh * RET_DV, (hh + 1) * RET_DV)

        def kv_body(n, _):
            rows = pl.ds(pl.multiple_of(n * c, c), c)
            kh = kr_scr[rows, :] * hm
            kz = jnp.concatenate([kh * zeta_f, kh * zeta_b], axis=1).astype(BF16)
            kv_scr[n] = lax.dot_general(kz, v_ref[rows, vcols], (((0,), (0,)), ((), ())),
                                        preferred_element_type=F32)
            return 0

        lax.fori_loop(0, nchunk, kv_body, 0, unroll=2)

        def scan_f(n, s):
            t = kv_scr[n, 0:LANES, :]
            kv_scr[n, 0:LANES, :] = s
            return gf * s + t

        s_f = lax.fori_loop(0, nchunk, scan_f, s0_ref[hh, 0])

        def scan_b(j, s):
            n = nchunk - 1 - j
            t = kv_scr[n, LANES:2 * LANES, :]
            kv_scr[n, LANES:2 * LANES, :] = s
            return gb * s + t

        s_b = lax.fori_loop(0, nchunk, scan_b, s0_ref[hh, 1])
        sfin_ref[hh, 0] = s_f
        sfin_ref[hh, 1] = s_b

        def out_body(n, _):
            rows = pl.ds(pl.multiple_of(n * c, c), c)
            qm = qr_scr[rows, :] * hm
            a = lax.dot_general(qm.astype(BF16), kr_scr[rows, :].astype(BF16), (((1,), (1,)), ((), ())),
                                preferred_element_type=F32) * dmat
            o = jnp.dot(a.astype(BF16), v_ref[rows, vcols], preferred_element_type=F32)
            qx = jnp.concatenate([qm * xi_f, qm * xi_b], axis=1).astype(BF16)
            o = o + jnp.dot(qx, kv_scr[n].astype(BF16), preferred_element_type=F32)
            o = o * lax.rsqrt(jnp.mean(o * o, axis=-1, keepdims=True) + EPS)
            g = g_ref[rows, vcols].astype(F32)
            o_ref[rows, vcols] = (g * jax.nn.sigmoid(g) * o).astype(o_ref.dtype)
            return 0

        lax.fori_loop(0, nchunk, out_body, 0, unroll=2)


def _retention(z, log_decay, cos, sin, s0, *, nb, seq, row0, rope, out_rows, y_prev=None):
    rb0 = row0 // seq
    hpairs = RET_HEADS // 2
    kcol, vcol, qcol, gcol = 0, 512 // 256, 3584 // LANES, 4096 // 256
    has_prev = y_prev is not None
    kernel = functools.partial(_ret_kernel, rope=rope, has_prev=has_prev)
    in_specs = [
        pl.BlockSpec(memory_space=pltpu.SMEM),
        pl.BlockSpec((seq, LANES), lambda b, p: (rb0 + b, qcol + p)),
        pl.BlockSpec((seq, LANES), lambda b, p: (rb0 + b, kcol + p)),
        pl.BlockSpec((seq, 2 * RET_DV), lambda b, p: (rb0 + b, vcol + p)),
        pl.BlockSpec((seq, 2 * RET_DV), lambda b, p: (rb0 + b, gcol + p)),
        pl.BlockSpec((seq, LANES), lambda b, p: (0, 0)),
        pl.BlockSpec((seq, LANES), lambda b, p: (0, 0)),
        pl.BlockSpec((None, 2, 2, LANES, RET_DV), lambda b, p: (b, p, 0, 0, 0)),
    ]
    args = [log_decay, z, z, z, z, cos, sin, s0]
    if has_prev:
        in_specs.append(pl.BlockSpec(memory_space=pl.ANY))
        args.append(y_prev)
    return pl.pallas_call(
        kernel,
        grid=(nb, hpairs),
        in_specs=in_specs,
        out_specs=[
            pl.BlockSpec((seq, 2 * RET_DV), lambda b, p: (rb0 + b, p)),
            pl.BlockSpec((None, 2, 2, LANES, RET_DV), lambda b, p: (b, p, 0, 0, 0)),
        ],
        out_shape=[
            jax.ShapeDtypeStruct((out_rows, RET_HEADS * RET_DV), BF16),
            jax.ShapeDtypeStruct((nb, RET_HEADS, 2, LANES, RET_DV), F32),
        ],
        scratch_shapes=[
            pltpu.VMEM((seq, LANES), F32),
            pltpu.VMEM((seq, LANES), F32),
            pltpu.VMEM((seq // RET_CHUNK, 2 * LANES, RET_DV), F32),
        ],
        input_output_aliases={len(args) - 1: 0} if has_prev else {},
        compiler_params=_cparams("parallel", "arbitrary"),
        name="retention_rope" if rope else "retention",
    )(*args)


NA_GROUP = 4
NA_KROWS = 12


def _na_kernel(rpb_ref, q_ref, k_ref, v_ref, kc_ref, vc_ref, o_ref, bias_scr):
    seq = q_ref.shape[0]
    nrow = seq // GRID_W
    ndr, ndc = 2 * NA_WR - 1, 2 * NA_WC - 1
    scale = NA_DH ** -0.5
    nt = (((1,), (1,)), ((), ()))
    h = pl.program_id(1)

    qi = lax.broadcasted_iota(jnp.int32, (GRID_W, LANES), 0)
    lane = lax.broadcasted_iota(jnp.int32, (GRID_W, LANES), 1)
    ki = lane & (GRID_W - 1)
    dc = ki - qi + (NA_WC - 1)
    cs = jnp.clip(qi - NA_WC // 2, 0, GRID_W - NA_WC)
    colmask = (ki >= cs) & (ki < cs + NA_WC)

    def toeplitz(d):
        t = jnp.zeros((GRID_W, LANES), F32)
        for j in range(ndc):
            t = jnp.where(dc == j, rpb_ref[h * (ndr * ndc) + d * ndc + j], t)
        return t

    prev = jnp.zeros((GRID_W, LANES), F32)
    for i in range(ndr + 1):
        nxt = toeplitz(i) if i < ndr else jnp.zeros((GRID_W, LANES), F32)
        bias_scr[i] = jnp.where(colmask, jnp.where(lane < GRID_W, prev, nxt), NEG)
        prev = nxt

    kc = kc_ref[...]
    vc = vc_ref[...]
    kwin = NA_KROWS * GRID_W

    def group(g, _):
        r0 = g * NA_GROUP
        us = jnp.clip(r0 - NA_WR // 2, 0, nrow - NA_KROWS)
        qrows = pl.ds(pl.multiple_of(r0 * GRID_W, NA_GROUP * GRID_W), NA_GROUP * GRID_W)
        krows = pl.ds(pl.multiple_of(us * GRID_W, GRID_W), kwin)
        q = q_ref[qrows, :]
        s = lax.dot_general(q, k_ref[krows, :], nt, preferred_element_type=F32)
        row_blocks = []
        for u in range(NA_GROUP):
            r = r0 + u
            rs = jnp.clip(r - NA_WR // 2, 0, nrow - NA_WR)
            parts = []
            for m in range(kwin // LANES):
                kr = us + 2 * m
                add_lo = jnp.where((kr >= rs) & (kr < rs + NA_WR), 0.0, NEG)
                add_hi = jnp.where((kr + 1 >= rs) & (kr + 1 < rs + NA_WR), 0.0, NEG)
                tab = bias_scr[jnp.clip(kr - r + NA_WR, 0, ndr)] + jnp.where(lane < GRID_W, add_lo, add_hi)
                st = s[u * GRID_W:(u + 1) * GRID_W, m * LANES:(m + 1) * LANES]
                parts.append(jnp.where(tab > 0.5 * NEG, st * scale + tab, NEG))
            row_blocks.append(jnp.concatenate(parts, axis=1))
        s = jnp.concatenate(row_blocks, axis=0)
        sc = lax.dot_general(q, kc, nt, preferred_element_type=F32) * scale
        mx = jnp.maximum(jnp.max(s, axis=-1, keepdims=True), jnp.max(sc, axis=-1, keepdims=True))
        e = jnp.exp(s - mx)
        ec = jnp.exp(sc - mx)
        den = jnp.sum(e, axis=-1, keepdims=True) + jnp.sum(ec, axis=-1, keepdims=True)
        o = jnp.dot(e.astype(BF16), v_ref[krows, :], preferred_element_type=F32)
        o = o + jnp.dot(ec.astype(BF16), vc, preferred_element_type=F32)
        o_ref[qrows, :] = (o / den).astype(o_ref.dtype)
        return 0

    lax.fori_loop(0, nrow // NA_GROUP, group, 0)


def _na_attention(z, rpb, *, nb, seq, cseq, crow0, out_rows):
    kcol, vcol, qcol = 1536 // LANES, 2560 // LANES, 5120 // LANES
    crb0 = crow0 // cseq
    nrow = seq // GRID_W
    assert nrow >= NA_KROWS and nrow % NA_GROUP == 0 and GRID_W * 2 == LANES
    assert NA_KROWS % 2 == 0 and NA_KROWS >= NA_WR + NA_GROUP - 1
    return pl.pallas_call(
        _na_kernel,
        grid=(nb, NA_HEADS),
        in_specs=[
            pl.BlockSpec(memory_space=pltpu.SMEM),
            pl.BlockSpec((seq, NA_DH), lambda b, h: (b, qcol + h)),
            pl.BlockSpec((seq, NA_DH), lambda b, h: (b, kcol + h)),
            pl.BlockSpec((seq, NA_DH), lambda b, h: (b, vcol + h)),
            pl.BlockSpec((cseq, NA_DH), lambda b, h: (crb0 + b, kcol + h)),
            pl.BlockSpec((cseq, NA_DH), lambda b, h: (crb0 + b, vcol + h)),
        ],
        out_specs=pl.BlockSpec((seq, NA_DH), lambda b, h: (b, h)),
        out_shape=jax.ShapeDtypeStruct((out_rows, NA_HEADS * NA_DH), BF16),
        scratch_shapes=[pltpu.VMEM((2 * NA_WR, GRID_W, LANES), F32)],
        compiler_params=_cparams("parallel", "arbitrary"),
        name="na_attention",
    )(rpb.reshape(-1), z, z, z, z, z)


def _dense_attn_kernel(q_ref, k_ref, v_ref, yprev_ref, o_ref):
    s = lax.dot_general(q_ref[...], k_ref[...], (((1,), (1,)), ((), ())), preferred_element_type=F32) * NA_DH ** -0.5
    e = jnp.exp(s - jnp.max(s, axis=-1, keepdims=True))
    o = jnp.dot(e.astype(BF16), v_ref[...], preferred_element_type=F32)
    o_ref[...] = (o / jnp.sum(e, axis=-1, keepdims=True)).astype(o_ref.dtype)


def _dense_attention(z, y_prev, *, nb, cseq, crow0):
    kcol, vcol, qcol = 1536 // LANES, 2560 // LANES, 5120 // LANES
    crb0 = crow0 // cseq
    return pl.pallas_call(
        _dense_attn_kernel,
        grid=(nb, NA_HEADS),
        in_specs=[
            pl.BlockSpec((cseq, NA_DH), lambda b, h: (crb0 + b, qcol + h)),
            pl.BlockSpec((cseq, NA_DH), lambda b, h: (crb0 + b, kcol + h)),
            pl.BlockSpec((cseq, NA_DH), lambda b, h: (crb0 + b, vcol + h)),
            pl.BlockSpec(memory_space=pl.ANY),
        ],
        out_specs=pl.BlockSpec((cseq, NA_DH), lambda b, h: (crb0 + b, h)),
        out_shape=jax.ShapeDtypeStruct(y_prev.shape, BF16),
        input_output_aliases={3: 0},
        compiler_params=_cparams("parallel", "arbitrary"),
        name="dense_attention",
    )(z, z, z, y_prev)


HY_FEAT_ROWS = 64
HIGHEST = lax.Precision.HIGHEST


def _hy_features(seq):
    t = np.linspace(0.0, 1.0, seq)
    bands = (HY_EMB - 1) // 2
    w = 2.0 * math.pi * np.arange(seq) / seq
    fr = np.linspace(1e-4, bands - 1, bands)
    ang = fr[None] * w[:, None]
    feat = np.concatenate([t[:, None], np.cos(ang), -np.sin(ang)], axis=-1)
    src = np.concatenate([np.arange(seq), np.zeros(1, np.int64), np.arange(seq - 1, 0, -1)])
    feat2 = np.zeros((HY_FEAT_ROWS, 2 * seq), np.float32)
    feat2[:HY_EMB] = feat[src].T
    tt = t[src][None].astype(np.float32)
    mask = np.ones((1, 2 * seq), np.float32)
    mask[0, seq] = 0.0
    return feat2, tt, mask


def _hy_mlp_kernel(feat_ref, w1_ref, b1_ref, w2_ref, b2_ref, fq_ref, o_ref):
    fq = fq_ref[...]
    h = jnp.dot(w1_ref[...], feat_ref[...], preferred_element_type=F32, precision=HIGHEST)
    h = jnp.sin(fq * (h + b1_ref[...]))
    h = jnp.dot(w2_ref[...], h, preferred_element_type=F32, precision=HIGHEST)
    o_ref[...] = jnp.sin(fq * (h + b2_ref[...]))


def _hy_mlp(feat2, w1t, b1, w2t, b2, fq):
    depth, hid, _ = w1t.shape
    n = feat2.shape[1]
    col = lambda a: a.reshape(depth, hid, 1)
    wspec = lambda k: pl.BlockSpec((None, hid, k), lambda l: (l, 0, 0))
    return pl.pallas_call(
        _hy_mlp_kernel,
        grid=(depth,),
        in_specs=[pl.BlockSpec((HY_FEAT_ROWS, n), lambda l: (0, 0)), wspec(HY_FEAT_ROWS), wspec(1), wspec(hid),
                  wspec(1), wspec(1)],
        out_specs=pl.BlockSpec((None, hid, n), lambda l: (l, 0, 0)),
        out_shape=jax.ShapeDtypeStruct((depth, hid, n), F32),
        compiler_params=_cparams("parallel"),
        name="hyena_filter_mlp",
    )(feat2, w1t, col(b1), w2t, col(b2), col(fq))


def _hy_filter_kernel(h_ref, w3_ref, dl_ref, tt_ref, mask_ref, o_ref):
    seq = h_ref.shape[1] // 2
    kf = jnp.dot(w3_ref[0], h_ref[:, :seq], preferred_element_type=F32, precision=HIGHEST)
    kb = jnp.dot(w3_ref[1], h_ref[:, seq:], preferred_element_type=F32, precision=HIGHEST)
    k = jnp.concatenate([kf, kb], axis=1) * (jnp.exp(-tt_ref[...] * dl_ref[...]) * mask_ref[...])
    o_ref[...] = k * lax.rsqrt(jnp.sum(k * k, axis=1, keepdims=True) + EPS)


def _hy_filter(h2, w3t, deltas, tt, mask, cb):
    depth, hid, n = h2.shape
    c = w3t.shape[3]
    return pl.pallas_call(
        _hy_filter_kernel,
        grid=(depth, HY_ORDER, c // cb),
        in_specs=[
            pl.BlockSpec((None, hid, n), lambda l, o, j: (l, 0, 0)),
            pl.BlockSpec((None, None, 2, cb, hid), lambda l, o, j: (l, o, 0, j, 0)),
            pl.BlockSpec((cb, 1), lambda l, o, j: (j, 0)),
            pl.BlockSpec((1, n), lambda l, o, j: (0, 0)),
            pl.BlockSpec((1, n), lambda l, o, j: (0, 0)),
        ],
        out_specs=pl.BlockSpec((None, None, cb, n), lambda l, o, j: (l, o, j, 0)),
        out_shape=jax.ShapeDtypeStruct((depth, HY_ORDER, c, n), F32),
        compiler_params=_cparams("parallel", "parallel", "arbitrary"),
        name="hyena_filter",
    )(h2, w3t, deltas, tt, mask)


FFT_NO = 64
FFT_NI = 128


def _fft_tables():
    n = FFT_NO * FFT_NI
    a = np.arange(FFT_NO)
    fo = np.exp(-2j * np.pi * np.outer(a, a) / FFT_NO)
    i = np.arange(FFT_NI)
    ci = np.exp(-2j * np.pi * np.outer(i, i) / FFT_NI)
    tw = np.exp(-2j * np.pi * np.outer(a, i) / n)
    half = FFT_NO // 2
    f32 = lambda x: np.ascontiguousarray(x, dtype=np.float32)
    g1_real = f32(np.concatenate([fo.real, fo.imag], axis=0))
    g1 = f32(np.block([[fo.real[:, :half], -fo.imag[:, :half]], [fo.imag[:, :half], fo.real[:, :half]]]))
    w2 = f32(np.block([[ci.real, ci.imag], [-ci.imag, ci.real]]))
    w2i = f32(np.block([[ci.real, -ci.imag], [ci.imag, ci.real]]))
    g4 = f32(np.block([[fo.real[:half], fo.imag[:half]], [-fo.imag[:half], fo.real[:half]]]) / n)
    return dict(g1_real=g1_real, g1=g1, w2=w2, w2i=w2i, g4=g4, twr=f32(tw.real), twi=f32(tw.imag))


HY_CHUNK = 8


def _fft_stage1(g1, y2, twr, twi):
    o1 = jnp.dot(g1, y2, preferred_element_type=F32)
    yr, yi = o1[:FFT_NO], o1[FFT_NO:]
    ar = (yr * twr - yi * twi).astype(BF16)
    ai = (yr * twi + yi * twr).astype(BF16)
    lanes = lambda ci: slice(ci * FFT_NI, (ci + 1) * FFT_NI)
    return jnp.concatenate([jnp.concatenate([ar[:, lanes(ci)], ai[:, lanes(ci)]], axis=1)
                            for ci in range(HY_CHUNK)], axis=0)


def _hy_spectrum_kernel(k_ref, g1_ref, w2_ref, twr_ref, twi_ref, o_ref):
    cb = k_ref.shape[0]
    g1 = g1_ref[...]
    w2 = w2_ref[...]
    twr = twr_ref[...]
    twi = twi_ref[...]

    def body(j, _):
        chans = pl.ds(pl.multiple_of(j * HY_CHUNK, HY_CHUNK), HY_CHUNK)
        ks = k_ref[chans]
        k2 = jnp.concatenate([ks[ci] for ci in range(HY_CHUNK)], axis=1).astype(BF16)
        z = jnp.dot(_fft_stage1(g1, k2, twr, twi), w2, preferred_element_type=F32)
        o_ref[chans] = z.reshape(HY_CHUNK, FFT_NO, 2 * FFT_NI)
        return 0

    lax.fori_loop(0, cb // HY_CHUNK, body, 0)


def _tiled_twiddles(tabs):
    return (jnp.asarray(np.tile(tabs["twr"], (1, HY_CHUNK))), jnp.asarray(np.tile(tabs["twi"], (1, HY_CHUNK))))


def _hy_spectrum(kfilt, tabs, cb):
    g, c = kfilt.shape[:2]
    const = lambda a: pl.BlockSpec(a.shape, lambda i, j: (0,) * a.ndim)
    g1 = jnp.asarray(tabs["g1_real"], BF16)
    w2 = jnp.asarray(tabs["w2"], BF16)
    twr, twi = _tiled_twiddles(tabs)
    return pl.pallas_call(
        _hy_spectrum_kernel,
        grid=(g, c // cb),
        in_specs=[pl.BlockSpec((None, cb, FFT_NO, FFT_NI), lambda i, j: (i, j, 0, 0)),
                  const(g1), const(w2), const(twr), const(twi)],
        out_specs=pl.BlockSpec((None, cb, FFT_NO, 2 * FFT_NI), lambda i, j: (i, j, 0, 0)),
        out_shape=jax.ShapeDtypeStruct((g, c, FFT_NO, 2 * FFT_NI), F32),
        compiler_params=_cparams("parallel", "arbitrary"),
        name="hyena_spectrum",
    )(kfilt, g1, w2, twr, twi)


def _shift_conv3(x, w0, w1, w2, b):
    nrow, nlane = x.shape
    row = lax.broadcasted_iota(jnp.int32, x.shape, 0)
    lane = lax.broadcasted_iota(jnp.int32, x.shape, 1)
    r = pltpu.roll(x, 1, axis=1)
    prev = jnp.where(lane == 0, pltpu.roll(r, 1, axis=0), r)
    prev = jnp.where((lane == 0) & (row == 0), 0.0, prev)
    r = pltpu.roll(x, nlane - 1, axis=1)
    nxt = jnp.where(lane == nlane - 1, pltpu.roll(r, nrow - 1, axis=0), r)
    nxt = jnp.where((lane == nlane - 1) & (row == nrow - 1), 0.0, nxt)
    return prev * w0 + x * w1 + nxt * w2 + b


def _hyena_kernel(x_ref, kf_ref, pm_ref, g1_ref, w2_ref, w2i_ref, g4_ref, twr_ref, twi_ref, o_ref):
    cb = x_ref.shape[2]
    half = FFT_NO // 2
    ch = HY_CHUNK
    g1 = g1_ref[...]
    g4 = g4_ref[...]
    w2 = w2_ref[...]
    w2i = w2i_ref[...]
    twr = twr_ref[...]
    twi = twi_ref[...]
    lanes = lambda ci: slice(ci * FFT_NI, (ci + 1) * FFT_NI)
    krows = lambda ci: slice(ci * FFT_NO, (ci + 1) * FFT_NO)

    def body(j, _):
        c0 = pl.multiple_of(j * ch, ch)
        pm = pm_ref[:, pl.ds(c0, ch)]
        conv = []
        for part in range(3):
            halves = []
            for bb in range(2):
                xs = x_ref[bb, part, pl.ds(c0, ch)].astype(F32)
                halves.append(jnp.concatenate(
                    [_shift_conv3(xs[ci], pm[part, ci], pm[3 + part, ci], pm[6 + part, ci], pm[9 + part, ci])
                     for ci in range(ch)], axis=1))
            conv.append(jnp.concatenate(halves, axis=0))
        y = conv[2]
        for order in range(HY_ORDER):
            z = jnp.dot(_fft_stage1(g1, y.astype(BF16), twr, twi), w2, preferred_element_type=F32)
            kf = kf_ref[order, pl.ds(c0, ch)].reshape(ch * FFT_NO, 2 * FFT_NI)
            zr, zi = z[:, :FFT_NI], z[:, FFT_NI:]
            kr, ki = kf[:, :FFT_NI], kf[:, FFT_NI:]
            p = jnp.concatenate([zr * kr - zi * ki, zr * ki + zi * kr], axis=1).astype(BF16)
            q = jnp.dot(p, w2i, preferred_element_type=F32)
            qr = jnp.concatenate([q[krows(ci), :FFT_NI] for ci in range(ch)], axis=1)
            qi = jnp.concatenate([q[krows(ci), FFT_NI:] for ci in range(ch)], axis=1)
            qs = jnp.concatenate([qr * twr + qi * twi, qi * twr - qr * twi], axis=0).astype(BF16)
            cv = jnp.dot(g4, qs, preferred_element_type=F32)
            bias = jnp.concatenate([pm[12 + order, ci] for ci in range(ch)], axis=1)
            y = conv[order] * (cv + bias * y)
        for ci in range(ch):
            o_ref[0, c0 + ci] = y[:half, lanes(ci)].astype(o_ref.dtype)
            o_ref[1, c0 + ci] = y[half:, lanes(ci)].astype(o_ref.dtype)
        return 0

    lax.fori_loop(0, cb // ch, body, 0)


def _hyena(xt, kf, pm, tabs, cb):
    nb, _, c, half, _ = xt.shape
    const = lambda a: pl.BlockSpec(a.shape, lambda j, p: (0,) * a.ndim)
    bf = lambda name: jnp.asarray(tabs[name], BF16)
    g1, w2, w2i, g4 = bf("g1"), bf("w2"), bf("w2i"), bf("g4")
    twr, twi = _tiled_twiddles(tabs)
    return pl.pallas_call(
        _hyena_kernel,
        grid=(c // cb, nb // 2),
        in_specs=[
            pl.BlockSpec((2, 3, cb, half, FFT_NI), lambda j, p: (p, 0, j, 0, 0)),
            pl.BlockSpec((HY_ORDER, cb, FFT_NO, 2 * FFT_NI), lambda j, p: (0, j, 0, 0)),
            pl.BlockSpec((14, cb, 1, FFT_NI), lambda j, p: (0, j, 0, 0)),
            const(g1), const(w2), const(w2i), const(g4), const(twr), const(twi),
        ],
        out_specs=pl.BlockSpec((2, cb, half, FFT_NI), lambda j, p: (p, j, 0, 0)),
        out_shape=jax.ShapeDtypeStruct((nb, c, half, FFT_NI), BF16),
        compiler_params=_cparams("parallel", "arbitrary"),
        name="hyena",
    )(xt, kf, pm, g1, w2, w2i, g4, twr, twi)


def _dft_tables(seq):
    n = 2 * seq
    wmat = np.exp(-2j * np.pi * np.outer(np.arange(n), np.arange(n)) / n)
    f32 = lambda x: np.ascontiguousarray(x, dtype=np.float32)
    fwd_real = f32(np.concatenate([wmat.real, wmat.imag], axis=1))
    ws = wmat[:seq]
    fwd = f32(np.block([[ws.real, ws.imag], [-ws.imag, ws.real]]))
    wi = np.conj(wmat)[:, :seq] / n
    inv = f32(np.block([[wi.real, wi.imag], [-wi.imag, wi.real]]))
    return dict(fwd_real=fwd_real, fwd=fwd, inv=inv)


def _rowdft_kernel(x_ref, w_ref, o_ref):
    o_ref[...] = jnp.dot(x_ref[...].astype(BF16), w_ref[...], preferred_element_type=F32)


def _rowdft(x, w, tr):
    m, k = x.shape
    n = w.shape[1]
    return pl.pallas_call(
        _rowdft_kernel,
        grid=(m // tr,),
        in_specs=[pl.BlockSpec((tr, k), lambda i: (i, 0)), pl.BlockSpec((k, n), lambda i: (0, 0))],
        out_specs=pl.BlockSpec((tr, n), lambda i: (i, 0)),
        out_shape=jax.ShapeDtypeStruct((m, n), F32),
        compiler_params=_cparams("parallel"),
        name="row_dft",
    )(x, w)


def _lane_conv3(x, w0, w1, w2, b):
    seq = x.shape[1]
    lane = lax.broadcasted_iota(jnp.int32, x.shape, 1)
    prev = jnp.where(lane == 0, 0.0, pltpu.roll(x, 1, axis=1))
    nxt = jnp.where(lane == seq - 1, 0.0, pltpu.roll(x, seq - 1, axis=1))
    return prev * w0 + x * w1 + nxt * w2 + b


def _hyena_ctx_kernel(x_ref, kf_ref, pm_ref, fwd_ref, inv_ref, o_ref):
    seq = x_ref.shape[3]
    n = 2 * seq
    conv = [[_lane_conv3(x_ref[bb, part].astype(F32), pm_ref[part], pm_ref[3 + part], pm_ref[6 + part],
                         pm_ref[9 + part]) for bb in range(2)] for part in range(3)]
    ya, yb = conv[2]
    for order in range(HY_ORDER):
        z = jnp.dot(jnp.concatenate([ya, yb], axis=1).astype(BF16), fwd_ref[...], preferred_element_type=F32)
        kf = kf_ref[order]
        zr, zi, kr, ki = z[:, :n], z[:, n:], kf[:, :n], kf[:, n:]
        p = jnp.concatenate([zr * kr - zi * ki, zr * ki + zi * kr], axis=1).astype(BF16)
        cv = jnp.dot(p, inv_ref[...], preferred_element_type=F32)
        bias = pm_ref[12 + order]
        ya = conv[order][0] * (cv[:, :seq] + bias * ya)
        yb = conv[order][1] * (cv[:, seq:] + bias * yb)
    o_ref[0] = ya.astype(o_ref.dtype)
    o_ref[1] = yb.astype(o_ref.dtype)


def _hyena_ctx(xt, kf, pm, tabs, cb):
    nb, _, c, seq = xt.shape
    fwd, inv = jnp.asarray(tabs["fwd"], BF16), jnp.asarray(tabs["inv"], BF16)
    const = lambda a: pl.BlockSpec(a.shape, lambda j, p: (0,) * a.ndim)
    return pl.pallas_call(
        _hyena_ctx_kernel,
        grid=(c // cb, nb // 2),
        in_specs=[
            pl.BlockSpec((2, 3, cb, seq), lambda j, p: (p, 0, j, 0)),
            pl.BlockSpec((HY_ORDER, cb, 4 * seq), lambda j, p: (0, j, 0)),
            pl.BlockSpec((14, cb, 1), lambda j, p: (0, j, 0)),
            const(fwd), const(inv),
        ],
        out_specs=pl.BlockSpec((2, cb, seq), lambda j, p: (p, j, 0)),
        out_shape=jax.ShapeDtypeStruct((nb, c, seq), BF16),
        compiler_params=_cparams("parallel", "arbitrary"),
        name="hyena_ctx",
    )(xt, kf, pm, fwd, inv)


def _rope_tables(seq):
    t = np.arange(seq)
    pos = np.stack([t // GRID_W, t % GRID_W], axis=1).astype(np.float64)
    n = RET_DK // 4
    inv = ROPE_BASE ** (-np.arange(n, dtype=np.float64) / n)
    lane = np.arange(LANES) % RET_DK
    ang = pos[:, lane // (2 * n)] * inv[lane % n][None]
    sign = np.where(lane % (2 * n) < n, -1.0, 1.0)
    return np.cos(ang).astype(np.float32), (np.sin(ang) * sign[None]).astype(np.float32)


def _hy_filters(seq, hy_f_w1, hy_f_b1, hy_f_w2, hy_f_b2, hy_f_w3, hy_f_freq):
    depth, _, hid = hy_f_w1.shape
    c = hy_f_w3.shape[2] // (2 * HY_ORDER)
    feat2, tt, mask = _hy_features(seq)
    w1t = jnp.pad(hy_f_w1.transpose(0, 2, 1), ((0, 0), (0, 0), (0, HY_FEAT_ROWS - HY_EMB)))
    h2 = _hy_mlp(jnp.asarray(feat2), w1t, hy_f_b1, hy_f_w2.transpose(0, 2, 1), hy_f_b2, hy_f_freq)
    w3t = hy_f_w3.reshape(depth, hid, HY_ORDER, 2, c).transpose(0, 2, 3, 4, 1)
    deltas = np.abs(np.linspace(math.log(HY_TARGET) / HY_FAST, math.log(HY_TARGET) / HY_SLOW, c))
    deltas = jnp.asarray(deltas.astype(np.float32).reshape(c, 1))
    return _hy_filter(h2, w3t, deltas, jnp.asarray(tt), jnp.asarray(mask), min(c, 8 * 1024 * 128 // (2 * seq)))


def _hy_params(hy_conv_w, hy_conv_b, hy_bias):
    c = hy_bias.shape[1]
    return jnp.concatenate([hy_conv_w.reshape(9, c), hy_conv_b.reshape(3, c), hy_bias], axis=0)


def kernel(x, c, ctx, c_ctx, w_mod, b_mod, g_norm1, g_norm2, w_in, hy_conv_w, hy_conv_b, hy_f_w1, hy_f_b1, hy_f_w2, hy_f_b2, hy_f_w3, hy_f_freq, hy_bias, ret_log_decay, na_rpb, w_branch, w_out, ffn_w_in, ffn_conv_w, ffn_conv_b, ffn_w_out, g_final):
    nb, seq, d = x.shape
    cseq = ctx.shape[1]
    depth = w_mod.shape[0]
    hy_w = hy_bias.shape[2]
    t_lat, t_ctx = nb * seq, nb * cseq
    assert seq % TM == 0 and t_ctx % TM == 0 and TM % cseq == 0 and nb % 2 == 0 and nb < SUBLANES
    assert 2 * seq == FFT_NO * FFT_NI and seq % GRID_W == 0 and cseq & (cseq - 1) == 0
    blk_lat, blk_all = t_lat // TM, (t_lat + t_ctx) // TM
    mrow = jnp.asarray(np.concatenate([np.repeat(np.arange(nb), seq // TM), np.full(t_ctx // TM, nb)]), jnp.int32)
    seqlen = jnp.asarray(np.concatenate([np.full(blk_lat, seq), np.full(t_ctx // TM, cseq)]), jnp.int32)

    cond = jnp.zeros((SUBLANES, d), F32).at[:nb].set(c).at[nb].set(c_ctx)
    mods = _mods(cond, w_mod, b_mod).reshape(depth, SUBLANES, 6, 1, d)

    cos, sin = (jnp.asarray(a) for a in _rope_tables(seq))
    tabs = _fft_tables()
    ctabs = _dft_tables(cseq)
    filt = _hy_filters(seq, hy_f_w1, hy_f_b1, hy_f_w2, hy_f_b2, hy_f_w3, hy_f_freq)
    spec = _hy_spectrum(filt.reshape(depth * HY_ORDER, hy_w, FFT_NO, FFT_NI), tabs, 32)
    spec = spec.reshape(depth, HY_ORDER, hy_w, FFT_NO, 2 * FFT_NI)
    cfilt = _hy_filters(cseq, hy_f_w1, hy_f_b1, hy_f_w2, hy_f_b2, hy_f_w3, hy_f_freq)
    cspec = _rowdft(cfilt.reshape(depth * HY_ORDER * hy_w, 2 * cseq), jnp.asarray(ctabs["fwd_real"], BF16), 1024)
    cspec = cspec.reshape(depth, HY_ORDER, hy_w, 4 * cseq)

    xs = jnp.concatenate([x.reshape(t_lat, d), ctx.reshape(t_ctx, d)], axis=0)
    hy0 = 6144
    gate0 = hy0 + 3 * hy_w
    s_zero = jnp.zeros((nb, RET_HEADS, 2, LANES, RET_DV), F32)
    for l in range(depth):
        last = l == depth - 1
        nblk = blk_lat if last else blk_all
        rows = nblk * TM
        hn = _normmod(xs, g_norm1[l], mods[l], mrow, 0, 1, blk_all)
        w_main = jnp.concatenate([w_in[l][:, :hy0], w_in[l][:, gate0:]], axis=1).astype(BF16)
        w_hyt = w_in[l][:, hy0:gate0].T.astype(BF16)
        z = _matmul(hn, w_main, blk_all, 1024)
        zh = _matmul_nt(hn, w_hyt, blk0=0, nblk=blk_lat, nb=nb, seq=seq, tc=512)

        y_ret, s_ctx = _retention(z, ret_log_decay[l], cos, sin, s_zero, nb=nb, seq=cseq, row0=t_lat, rope=False,
                                  out_rows=blk_all * TM)
        y_ret, _ = _retention(z, ret_log_decay[l], cos, sin, s_ctx, nb=nb, seq=seq, row0=0, rope=True,
                              out_rows=blk_all * TM, y_prev=y_ret)
        y_na = _na_attention(z, na_rpb[l], nb=nb, seq=seq, cseq=cseq, crow0=t_lat, out_rows=rows)
        pm = _hy_params(hy_conv_w[l], hy_conv_b[l], hy_bias[l])
        pm_lat = jnp.broadcast_to(pm[:, :, None, None], (14, hy_w, 1, FFT_NI))
        y_hy = _hyena(zh.reshape(nb, 3, hy_w, FFT_NO // 2, FFT_NI), spec[l], pm_lat, tabs, 32)
        y_hy = y_hy.reshape(nb, hy_w, seq).transpose(0, 2, 1).reshape(t_lat, hy_w)
        if not last:
            y_na = _dense_attention(z, y_na, nb=nb, cseq=cseq, crow0=t_lat)
            zc = _matmul_nt(hn, w_hyt, blk0=blk_lat, nblk=t_ctx // TM, nb=nb, seq=cseq, tc=512)
            yc_hy = _hyena_ctx(zc.reshape(nb, 3, hy_w, cseq), cspec[l], pm[:, :, None], ctabs, 256)
            y_hy = jnp.concatenate([y_hy, yc_hy.transpose(0, 2, 1).reshape(t_ctx, hy_w)], axis=0)
        acc = _merge((y_hy, y_ret, y_na), z, hy0, w_branch[l].astype(BF16), nblk, 1024)
        xs = _res_matmul(acc, w_out[l].astype(BF16), xs, mods[l], mrow, 2, nblk, 1024)
        hn = _normmod(xs, g_norm2[l], mods[l], mrow, 3, 4, nblk)
        u = _matmul(hn, ffn_w_in[l].astype(BF16), nblk, 1024)
        hg = _ffn_gate(u, ffn_conv_w[l], ffn_conv_b[l], seqlen, nblk, 512)
        xs = _res_matmul(hg, ffn_w_out[l].astype(BF16), xs, mods[l], mrow, 5, nblk, 512)
    return _final_norm(xs, g_final, blk_lat).reshape(nb, seq, d)
```

```python
import functools
import math

import numpy as np
import jax
import jax.numpy as jnp
from jax import lax
from jax.experimental import pallas as pl
from jax.experimental.pallas import tpu as pltpu

F32 = jnp.float32
BF16 = jnp.bfloat16

GRID_W = 64
N_BRANCH = 3
HY_ORDER = 2
HY_EMB = 33
HY_FAST = 0.3
HY_SLOW = 1.5
HY_TARGET = 1e-2
RET_HEADS = 8
RET_DK = 64
RET_DV = 128
RET_CHUNK = 128
NA_HEADS = 8
NA_DH = 128
NA_WR = 8
NA_WC = 16
ROPE_BASE = 10000.0
EPS = 1e-6
NEG = -1e30

LANES = 128
SUBLANES = 8
VMEM_LIMIT = 56 * 1024 * 1024

TM = 1024


def _cparams(*sem):
    return pltpu.CompilerParams(dimension_semantics=sem, vmem_limit_bytes=VMEM_LIMIT)


def _mods_kernel(a_ref, w_ref, b_ref, o_ref):
    a = a_ref[...]
    a = a * jax.nn.sigmoid(a)
    o_ref[...] = jnp.dot(a.astype(BF16), w_ref[...].astype(BF16), preferred_element_type=F32) + b_ref[...]


def _mods(cond, w_mod, b_mod):
    depth, d, n = w_mod.shape
    tn = 1024
    return pl.pallas_call(
        _mods_kernel,
        grid=(depth, n // tn),
        in_specs=[
            pl.BlockSpec((SUBLANES, d), lambda l, j: (0, 0)),
            pl.BlockSpec((None, d, tn), lambda l, j: (l, 0, j)),
            pl.BlockSpec((None, 1, tn), lambda l, j: (l, 0, j)),
        ],
        out_specs=pl.BlockSpec((None, SUBLANES, tn), lambda l, j: (l, 0, j)),
        out_shape=jax.ShapeDtypeStruct((depth, SUBLANES, n), F32),
        compiler_params=_cparams("parallel", "parallel"),
        name="mods",
    )(cond, w_mod, b_mod.reshape(depth, 1, n))


def _normmod_kernel(mrow_ref, x_ref, g_ref, sh_ref, sc_ref, o_ref):
    x = x_ref[...]
    y = x * lax.rsqrt(jnp.mean(x * x, axis=-1, keepdims=True) + EPS)
    y = y * g_ref[...]
    o_ref[...] = (y * (1.0 + sc_ref[...]) + sh_ref[...]).astype(o_ref.dtype)


def _normmod(x, g, mods, mrow, shift_idx, scale_idx, nblk):
    m, d = x.shape
    grid_spec = pltpu.PrefetchScalarGridSpec(
        num_scalar_prefetch=1,
        grid=(nblk,),
        in_specs=[
            pl.BlockSpec((TM, d), lambda i, mr: (i, 0)),
            pl.BlockSpec((1, d), lambda i, mr: (0, 0)),
            pl.BlockSpec((None, None, 1, d), lambda i, mr: (mr[i], shift_idx, 0, 0)),
            pl.BlockSpec((None, None, 1, d), lambda i, mr: (mr[i], scale_idx, 0, 0)),
        ],
        out_specs=pl.BlockSpec((TM, d), lambda i, mr: (i, 0)),
    )
    return pl.pallas_call(
        _normmod_kernel,
        grid_spec=grid_spec,
        out_shape=jax.ShapeDtypeStruct((m, d), BF16),
        compiler_params=_cparams("parallel"),
        name="normmod",
    )(mrow, x, g.reshape(1, d), mods, mods)


def _mm_kernel(a_ref, w_ref, o_ref, w_scr):
    @pl.when(pl.program_id(1) == 0)
    def _():
        w_scr[...] = w_ref[...].astype(BF16)

    o_ref[...] = jnp.dot(a_ref[...], w_scr[...], preferred_element_type=F32).astype(o_ref.dtype)


def _matmul(a, w, layer, nblk, tn, n_out, skip=None):
    m, k = a.shape
    col = (lambda j: j) if skip is None else (lambda j: j + jnp.where(j >= skip[0], skip[1], 0))
    return pl.pallas_call(
        _mm_kernel,
        grid=(n_out // tn, nblk),
        in_specs=[pl.BlockSpec((TM, k), lambda j, i: (i, 0)),
                  pl.BlockSpec((None, k, tn), lambda j, i: (layer, 0, col(j)))],
        out_specs=pl.BlockSpec((TM, tn), lambda j, i: (i, j)),
        out_shape=jax.ShapeDtypeStruct((m, n_out), BF16),
        scratch_shapes=[pltpu.VMEM((k, tn), BF16)],
        compiler_params=_cparams("arbitrary", "arbitrary"),
        name="matmul",
    )(a, w)


def _mm_nt_kernel(w_ref, a_ref, o_ref, wt_scr):
    @pl.when(pl.program_id(1) == 0)
    def _():
        wt_scr[...] = w_ref[...].T.astype(BF16)

    r = lax.dot_general(wt_scr[...], a_ref[...], (((1,), (1,)), ((), ())), preferred_element_type=F32)
    per = o_ref.shape[0]
    width = r.shape[1] // per
    for s in range(per):
        o_ref[s] = r[:, s * width:(s + 1) * width].astype(o_ref.dtype)


def _matmul_nt(a, w, layer, *, col0, n_out, blk0, nblk, nb, seq, tc):
    k = a.shape[1]
    per, sblk = max(TM // seq, 1), max(seq // TM, 1)
    return pl.pallas_call(
        _mm_nt_kernel,
        grid=(n_out // tc, nblk),
        in_specs=[pl.BlockSpec((None, k, tc), lambda j, i: (layer, 0, col0 // tc + j)),
                  pl.BlockSpec((TM, k), lambda j, i: (blk0 + i, 0))],
        out_specs=pl.BlockSpec((per, tc, TM // per), lambda j, i: (i // sblk, j, i % sblk)),
        out_shape=jax.ShapeDtypeStruct((nb, n_out, seq), BF16),
        scratch_shapes=[pltpu.VMEM((tc, k), BF16)],
        compiler_params=_cparams("arbitrary", "arbitrary"),
        name="matmul_nt",
    )(w, a)


TR = 512


def _norm_modulate(x, g, shift, scale):
    y = x * lax.rsqrt(jnp.mean(x * x, axis=-1, keepdims=True) + EPS)
    return (y * g) * (1.0 + scale) + shift


def _outproj_kernel(mrow_ref, a_ref, w_ref, x_ref, gate_ref, g_ref, sh_ref, sc_ref, x_out, h_out):
    y = jnp.dot(a_ref[...], w_ref[...], preferred_element_type=F32)
    x = x_ref[...] + gate_ref[...] * y
    x_out[...] = x
    h_out[...] = _norm_modulate(x, g_ref[...], sh_ref[...], sc_ref[...]).astype(h_out.dtype)


def _out_proj(a, w, x, mods, mrow, g_norm, nblk):
    m, k = a.shape
    d = w.shape[1]
    mod = lambda idx: pl.BlockSpec((None, None, 1, d), lambda i, mr: (mr[i], idx, 0, 0))
    row = pl.BlockSpec((TR, d), lambda i, mr: (i, 0))
    grid_spec = pltpu.PrefetchScalarGridSpec(
        num_scalar_prefetch=1,
        grid=(nblk,),
        in_specs=[pl.BlockSpec((TR, k), lambda i, mr: (i, 0)), pl.BlockSpec((k, d), lambda i, mr: (0, 0)), row,
                  mod(2), pl.BlockSpec((1, d), lambda i, mr: (0, 0)), mod(3), mod(4)],
        out_specs=[row, row],
    )
    return pl.pallas_call(
        _outproj_kernel,
        grid_spec=grid_spec,
        out_shape=[jax.ShapeDtypeStruct((m, d), F32), jax.ShapeDtypeStruct((m, d), BF16)],
        compiler_params=_cparams("parallel"),
        name="out_proj",
    )(mrow, a, w, x, mods, g_norm.reshape(1, d), mods, mods)


def _merge_kernel(y0_ref, y1_ref, y2_ref, g0_ref, g1_ref, g2_ref, w_ref, o_ref):
    acc = None
    for i, (y_ref, g_ref) in enumerate(((y0_ref, g0_ref), (y1_ref, g1_ref), (y2_ref, g2_ref))):
        t = jnp.dot(y_ref[...], w_ref[i], preferred_element_type=F32)
        t = jax.nn.sigmoid(g_ref[...].astype(F32)) * t
        acc = t if acc is None else acc + t
    o_ref[...] = acc.astype(o_ref.dtype)


def _merge(ys, z, gate_col0, w_branch, nblk, tn):
    m, bw = ys[0].shape
    d = w_branch.shape[2]
    gspec = lambda i_br: pl.BlockSpec((TM, tn), lambda i, j: (i, (gate_col0 + i_br * d) // tn + j))
    yspec = pl.BlockSpec((TM, bw), lambda i, j: (i, 0))
    return pl.pallas_call(
        _merge_kernel,
        grid=(nblk, d // tn),
        in_specs=[yspec, yspec, yspec, gspec(0), gspec(1), gspec(2),
                  pl.BlockSpec((N_BRANCH, bw, tn), lambda i, j: (0, 0, j))],
        out_specs=pl.BlockSpec((TM, tn), lambda i, j: (i, j)),
        out_shape=jax.ShapeDtypeStruct((m, d), BF16),
        compiler_params=_cparams("parallel", "arbitrary"),
        name="merge",
    )(ys[0], ys[1], ys[2], z, z, z, w_branch)


def _ffn_gate_kernel(seq_ref, a_ref, ap_ref, an_ref, b_ref, cw_ref, cb_ref, o_ref):
    i = pl.program_id(0)
    seq_m1 = seq_ref[i] - 1
    a = a_ref[...].astype(F32)
    tm = a.shape[0]
    row = lax.broadcasted_iota(jnp.int32, (tm, 1), 0)
    pos = (i * tm + row) & seq_m1
    prev = jnp.where(row == 0, ap_ref[SUBLANES - 1:SUBLANES, :].astype(F32), pltpu.roll(a, 1, axis=0))
    prev = jnp.where(pos == 0, 0.0, prev)
    nxt = jnp.where(row == tm - 1, an_ref[0:1, :].astype(F32), pltpu.roll(a, tm - 1, axis=0))
    nxt = jnp.where(pos == seq_m1, 0.0, nxt)
    cw = cw_ref[...]
    conv = prev * cw[0:1, :] + a * cw[1:2, :] + nxt * cw[2:3, :] + cb_ref[...]
    o_ref[...] = (jax.nn.gelu(conv) * b_ref[...].astype(F32)).astype(o_ref.dtype)


def _ffn_gate(u, conv_w, conv_b, seqlen, nblk, tc):
    m, ff2 = u.shape
    ff = ff2 // 2
    nrow8 = m // SUBLANES
    r8 = TM // SUBLANES
    grid_spec = pltpu.PrefetchScalarGridSpec(
        num_scalar_prefetch=1,
        grid=(nblk, ff // tc),
        in_specs=[
            pl.BlockSpec((TM, tc), lambda i, j, s: (i, j)),
            pl.BlockSpec((SUBLANES, tc), lambda i, j, s: (jnp.maximum(i * r8 - 1, 0), j)),
            pl.BlockSpec((SUBLANES, tc), lambda i, j, s: (jnp.minimum((i + 1) * r8, nrow8 - 1), j)),
            pl.BlockSpec((TM, tc), lambda i, j, s: (i, ff // tc + j)),
            pl.BlockSpec((3, tc), lambda i, j, s: (0, j)),
            pl.BlockSpec((1, tc), lambda i, j, s: (0, j)),
        ],
        out_specs=pl.BlockSpec((TM, tc), lambda i, j, s: (i, j)),
    )
    return pl.pallas_call(
        _ffn_gate_kernel,
        grid_spec=grid_spec,
        out_shape=jax.ShapeDtypeStruct((m, ff), BF16),
        compiler_params=_cparams("parallel", "arbitrary"),
        name="ffn_gate",
    )(seqlen, u, u, u, u, conv_w, conv_b.reshape(1, ff))


def _ffn_out_kernel(mrow_ref, h_ref, w_ref, x_ref, gate_ref, g_ref, sh_ref, sc_ref, *rest, final):
    outs, xrow = rest[:-1], rest[-1]
    j = pl.program_id(1)
    y = jnp.dot(h_ref[...], w_ref[...], preferred_element_type=F32)
    xrow[j] = x_ref[...] + gate_ref[...] * y

    @pl.when(j == pl.num_programs(1) - 1)
    def _():
        x = jnp.concatenate([xrow[t] for t in range(xrow.shape[0])], axis=1)
        if final:
            y = x * lax.rsqrt(jnp.mean(x * x, axis=-1, keepdims=True) + EPS)
            outs[0][...] = y * g_ref[...]
        else:
            outs[0][...] = x
            outs[1][...] = _norm_modulate(x, g_ref[...], sh_ref[...], sc_ref[...]).astype(outs[1].dtype)


def _ffn_out(h, w, x, mods, mods_next, g_next, mrow, nblk, tn, final):
    m, ff = h.shape
    d = w.shape[1]
    mod = lambda arr_idx: pl.BlockSpec((None, None, 1, d), lambda i, j, mr: (mr[i], arr_idx, 0, 0))
    row = pl.BlockSpec((TR, d), lambda i, j, mr: (i, 0))
    grid_spec = pltpu.PrefetchScalarGridSpec(
        num_scalar_prefetch=1,
        grid=(nblk, d // tn),
        in_specs=[
            pl.BlockSpec((TR, ff), lambda i, j, mr: (i, 0)),
            pl.BlockSpec((ff, tn), lambda i, j, mr: (0, j)),
            pl.BlockSpec((TR, tn), lambda i, j, mr: (i, j)),
            pl.BlockSpec((None, None, 1, tn), lambda i, j, mr: (mr[i], 5, 0, j)),
            pl.BlockSpec((1, d), lambda i, j, mr: (0, 0)), mod(0), mod(1),
        ],
        out_specs=[row] if final else [row, row],
        scratch_shapes=[pltpu.VMEM((d // tn, TR, tn), F32)],
    )
    out_shape = [jax.ShapeDtypeStruct((m, d), F32)] + ([] if final else [jax.ShapeDtypeStruct((m, d), BF16)])
    return pl.pallas_call(
        functools.partial(_ffn_out_kernel, final=final),
        grid_spec=grid_spec,
        out_shape=out_shape,
        compiler_params=_cparams("parallel", "arbitrary"),
        name="ffn_out",
    )(mrow, h, w, x, mods, g_next.reshape(1, d), mods_next, mods_next)


def _rope_pair(x, cos, sin):
    lane = lax.broadcasted_iota(jnp.int32, x.shape, 1)
    first = lax.rem(lane, 32) < 16
    partner = jnp.where(first, pltpu.roll(x, LANES - 16, axis=1), pltpu.roll(x, 16, axis=1))
    return x * cos + partner * sin


def _ret_kernel(lg_ref, q_ref, k_ref, v_ref, g_ref, cos_ref, sin_ref, s0_ref, *rest, rope, has_prev):
    o_ref, sfin_ref, qr_scr, kr_scr, kv_scr = rest[1:] if has_prev else rest
    hp = pl.program_id(1)
    seq = q_ref.shape[0]
    c = RET_CHUNK
    nchunk = seq // c
    ks = RET_DK ** -0.5
    q = q_ref[...].astype(F32)
    k = k_ref[...].astype(F32)
    if rope:
        q = _rope_pair(q, cos_ref[...], sin_ref[...])
        k = _rope_pair(k, cos_ref[...], sin_ref[...])
    qr_scr[...] = q
    kr_scr[...] = k * ks

    pos_r = lax.broadcasted_iota(jnp.int32, (c, 1), 0).astype(F32)
    rel = (lax.broadcasted_iota(jnp.int32, (c, c), 0) - lax.broadcasted_iota(jnp.int32, (c, c), 1)).astype(F32)
    lane = lax.broadcasted_iota(jnp.int32, (1, LANES), 1)

    heads = []
    for hh in range(2):
        lgf = lg_ref[0, 2 * hp + hh]
        lgb = lg_ref[1, 2 * hp + hh]
        heads.append(dict(
            hm=(lane // RET_DK == hh).astype(F32),
            zeta_f=jnp.exp(lgf * (c - 1 - pos_r)), zeta_b=jnp.exp(lgb * pos_r),
            xi_f=jnp.exp(lgf * (pos_r + 1.0)), xi_b=jnp.exp(lgb * (c - pos_r)),
            dmat=jnp.where(rel >= 0, jnp.exp(lgf * jnp.maximum(rel, 0.0)), jnp.exp(lgb * jnp.maximum(-rel, 0.0))),
            gf=jnp.exp(lgf * c), gb=jnp.exp(lgb * c),
            vcols=slice(hh * RET_DV, (hh + 1) * RET_DV)))

    def kv_body(n, _):
        rows = pl.ds(pl.multiple_of(n * c, c), c)
        kr = kr_scr[rows, :]
        for hh, hd in enumerate(heads):
            kh = kr * hd["hm"]
            kz = jnp.concatenate([kh * hd["zeta_f"], kh * hd["zeta_b"]], axis=1).astype(BF16)
            kv_scr[hh, n] = lax.dot_general(kz, v_ref[rows, hd["vcols"]], (((0,), (0,)), ((), ())),
                                            preferred_element_type=F32)
        return 0

    lax.fori_loop(0, nchunk, kv_body, 0)

    def scan(j, carry):
        nf, nbk = j, nchunk - 1 - j
        out = []
        for hh, hd in enumerate(heads):
            sf, sb = carry[2 * hh], carry[2 * hh + 1]
            tf = kv_scr[hh, nf, 0:LANES, :]
            kv_scr[hh, nf, 0:LANES, :] = sf
            tb = kv_scr[hh, nbk, LANES:2 * LANES, :]
            kv_scr[hh, nbk, LANES:2 * LANES, :] = sb
            out += [hd["gf"] * sf + tf, hd["gb"] * sb + tb]
        return tuple(out)

    fin = lax.fori_loop(0, nchunk, scan, (s0_ref[0, 0], s0_ref[0, 1], s0_ref[1, 0], s0_ref[1, 1]))
    for hh in range(2):
        sfin_ref[hh, 0] = fin[2 * hh]
        sfin_ref[hh, 1] = fin[2 * hh + 1]

    def out_body(n, _):
        rows = pl.ds(pl.multiple_of(n * c, c), c)
        qr = qr_scr[rows, :]
        kb = kr_scr[rows, :].astype(BF16)
        for hh, hd in enumerate(heads):
            qm = qr * hd["hm"]
            a = lax.dot_general(qm.astype(BF16), kb, (((1,), (1,)), ((), ())),
                                preferred_element_type=F32) * hd["dmat"]
            o = jnp.dot(a.astype(BF16), v_ref[rows, hd["vcols"]], preferred_element_type=F32)
            qx = jnp.concatenate([qm * hd["xi_f"], qm * hd["xi_b"]], axis=1).astype(BF16)
            o = o + jnp.dot(qx, kv_scr[hh, n].astype(BF16), preferred_element_type=F32)
            o = o * lax.rsqrt(jnp.mean(o * o, axis=-1, keepdims=True) + EPS)
            g = g_ref[rows, hd["vcols"]].astype(F32)
            o_ref[rows, hd["vcols"]] = (g * jax.nn.sigmoid(g) * o).astype(o_ref.dtype)
        return 0

    lax.fori_loop(0, nchunk, out_body, 0)


def _retention(z, log_decay, cos, sin, s0, *, nb, seq, row0, rope, out_rows, y_prev=None):
    rb0 = row0 // seq
    hpairs = RET_HEADS // 2
    kcol, vcol, qcol, gcol = 0, 512 // 256, 3584 // LANES, 4096 // 256
    has_prev = y_prev is not None
    kernel = functools.partial(_ret_kernel, rope=rope, has_prev=has_prev)
    in_specs = [
        pl.BlockSpec(memory_space=pltpu.SMEM),
        pl.BlockSpec((seq, LANES), lambda b, p: (rb0 + b, qcol + p)),
        pl.BlockSpec((seq, LANES), lambda b, p: (rb0 + b, kcol + p)),
        pl.BlockSpec((seq, 2 * RET_DV), lambda b, p: (rb0 + b, vcol + p)),
        pl.BlockSpec((seq, 2 * RET_DV), lambda b, p: (rb0 + b, gcol + p)),
        pl.BlockSpec((seq, LANES), lambda b, p: (0, 0)),
        pl.BlockSpec((seq, LANES), lambda b, p: (0, 0)),
        pl.BlockSpec((None, 2, 2, LANES, RET_DV), lambda b, p: (b, p, 0, 0, 0)),
    ]
    args = [log_decay, z, z, z, z, cos, sin, s0]
    if has_prev:
        in_specs.append(pl.BlockSpec(memory_space=pl.ANY))
        args.append(y_prev)
    return pl.pallas_call(
        kernel,
        grid=(nb, hpairs),
        in_specs=in_specs,
        out_specs=[
            pl.BlockSpec((seq, 2 * RET_DV), lambda b, p: (rb0 + b, p)),
            pl.BlockSpec((None, 2, 2, LANES, RET_DV), lambda b, p: (b, p, 0, 0, 0)),
        ],
        out_shape=[
            jax.ShapeDtypeStruct((out_rows, RET_HEADS * RET_DV), BF16),
            jax.ShapeDtypeStruct((nb, RET_HEADS, 2, LANES, RET_DV), F32),
        ],
        scratch_shapes=[
            pltpu.VMEM((seq, LANES), F32),
            pltpu.VMEM((seq, LANES), F32),
            pltpu.VMEM((2, seq // RET_CHUNK, 2 * LANES, RET_DV), F32),
        ],
        input_output_aliases={len(args) - 1: 0} if has_prev else {},
        compiler_params=_cparams("parallel", "arbitrary"),
        name="retention_rope" if rope else "retention",
    )(*args)


NA_GROUP = 4
NA_KROWS = 12


def _na_kernel(rpb_ref, q_ref, k_ref, v_ref, kc_ref, vc_ref, o_ref, bias_scr):
    seq = q_ref.shape[0]
    nrow = seq // GRID_W
    ndr, ndc = 2 * NA_WR - 1, 2 * NA_WC - 1
    scale = NA_DH ** -0.5
    nt = (((1,), (1,)), ((), ()))
    h = pl.program_id(1)

    qi = lax.broadcasted_iota(jnp.int32, (GRID_W, LANES), 0)
    lane = lax.broadcasted_iota(jnp.int32, (GRID_W, LANES), 1)
    ki = lane & (GRID_W - 1)
    dc = ki - qi + (NA_WC - 1)
    cs = jnp.clip(qi - NA_WC // 2, 0, GRID_W - NA_WC)
    colmask = (ki >= cs) & (ki < cs + NA_WC)

    def toeplitz(d):
        t = jnp.zeros((GRID_W, LANES), F32)
        for j in range(ndc):
            t = jnp.where(dc == j, rpb_ref[h * (ndr * ndc) + d * ndc + j], t)
        return t

    prev = jnp.zeros((GRID_W, LANES), F32)
    for i in range(ndr + 1):
        nxt = toeplitz(i) if i < ndr else jnp.zeros((GRID_W, LANES), F32)
        bias_scr[i] = jnp.where(colmask, jnp.where(lane < GRID_W, prev, nxt), NEG)
        prev = nxt

    kc = kc_ref[...]
    vc = vc_ref[...]
    kwin = NA_KROWS * GRID_W

    def group(g, _):
        r0 = g * NA_GROUP
        us = jnp.clip(r0 - NA_WR // 2, 0, nrow - NA_KROWS)
        qrows = pl.ds(pl.multiple_of(r0 * GRID_W, NA_GROUP * GRID_W), NA_GROUP * GRID_W)
        krows = pl.ds(pl.multiple_of(us * GRID_W, GRID_W), kwin)
        q = q_ref[qrows, :]
        s = lax.dot_general(q, k_ref[krows, :], nt, preferred_element_type=F32)
        row_blocks = []
        for u in range(NA_GROUP):
            r = r0 + u
            rs = jnp.clip(r - NA_WR // 2, 0, nrow - NA_WR)
            parts = []
            for m in range(kwin // LANES):
                kr = us + 2 * m
                add_lo = jnp.where((kr >= rs) & (kr < rs + NA_WR), 0.0, NEG)
                add_hi = jnp.where((kr + 1 >= rs) & (kr + 1 < rs + NA_WR), 0.0, NEG)
                tab = bias_scr[jnp.clip(kr - r + NA_WR, 0, ndr)] + jnp.where(lane < GRID_W, add_lo, add_hi)
                st = s[u * GRID_W:(u + 1) * GRID_W, m * LANES:(m + 1) * LANES]
                parts.append(jnp.where(tab > 0.5 * NEG, st * scale + tab, NEG))
            row_blocks.append(jnp.concatenate(parts, axis=1))
        s = jnp.concatenate(row_blocks, axis=0)
        sc = lax.dot_general(q, kc, nt, preferred_element_type=F32) * scale
        mx = jnp.maximum(jnp.max(s, axis=-1, keepdims=True), jnp.max(sc, axis=-1, keepdims=True))
        e = jnp.exp(s - mx)
        ec = jnp.exp(sc - mx)
        den = jnp.sum(e, axis=-1, keepdims=True) + jnp.sum(ec, axis=-1, keepdims=True)
        o = jnp.dot(e.astype(BF16), v_ref[krows, :], preferred_element_type=F32)
        o = o + jnp.dot(ec.astype(BF16), vc, preferred_element_type=F32)
        o_ref[qrows, :] = (o / den).astype(o_ref.dtype)
        return 0

    lax.fori_loop(0, nrow // NA_GROUP, group, 0)


def _na_attention(z, rpb, *, nb, seq, cseq, crow0, out_rows):
    kcol, vcol, qcol = 1536 // LANES, 2560 // LANES, 5120 // LANES
    crb0 = crow0 // cseq
    nrow = seq // GRID_W
    assert nrow >= NA_KROWS and nrow % NA_GROUP == 0 and GRID_W * 2 == LANES
    assert NA_KROWS % 2 == 0 and NA_KROWS >= NA_WR + NA_GROUP - 1
    return pl.pallas_call(
        _na_kernel,
        grid=(nb, NA_HEADS),
        in_specs=[
            pl.BlockSpec(memory_space=pltpu.SMEM),
            pl.BlockSpec((seq, NA_DH), lambda b, h: (b, qcol + h)),
            pl.BlockSpec((seq, NA_DH), lambda b, h: (b, kcol + h)),
            pl.BlockSpec((seq, NA_DH), lambda b, h: (b, vcol + h)),
            pl.BlockSpec((cseq, NA_DH), lambda b, h: (crb0 + b, kcol + h)),
            pl.BlockSpec((cseq, NA_DH), lambda b, h: (crb0 + b, vcol + h)),
        ],
        out_specs=pl.BlockSpec((seq, NA_DH), lambda b, h: (b, h)),
        out_shape=jax.ShapeDtypeStruct((out_rows, NA_HEADS * NA_DH), BF16),
        scratch_shapes=[pltpu.VMEM((2 * NA_WR, GRID_W, LANES), F32)],
        compiler_params=_cparams("parallel", "arbitrary"),
        name="na_attention",
    )(rpb.reshape(-1), z, z, z, z, z)


def _dense_attn_kernel(q_ref, k_ref, v_ref, yprev_ref, o_ref):
    s = lax.dot_general(q_ref[...], k_ref[...], (((1,), (1,)), ((), ())), preferred_element_type=F32) * NA_DH ** -0.5
    e = jnp.exp(s - jnp.max(s, axis=-1, keepdims=True))
    o = jnp.dot(e.astype(BF16), v_ref[...], preferred_element_type=F32)
    o_ref[...] = (o / jnp.sum(e, axis=-1, keepdims=True)).astype(o_ref.dtype)


def _dense_attention(z, y_prev, *, nb, cseq, crow0):
    kcol, vcol, qcol = 1536 // LANES, 2560 // LANES, 5120 // LANES
    crb0 = crow0 // cseq
    return pl.pallas_call(
        _dense_attn_kernel,
        grid=(nb, NA_HEADS),
        in_specs=[
            pl.BlockSpec((cseq, NA_DH), lambda b, h: (crb0 + b, qcol + h)),
            pl.BlockSpec((cseq, NA_DH), lambda b, h: (crb0 + b, kcol + h)),
            pl.BlockSpec((cseq, NA_DH), lambda b, h: (crb0 + b, vcol + h)),
            pl.BlockSpec(memory_space=pl.ANY),
        ],
        out_specs=pl.BlockSpec((cseq, NA_DH), lambda b, h: (crb0 + b, h)),
        out_shape=jax.ShapeDtypeStruct(y_prev.shape, BF16),
        input_output_aliases={3: 0},
        compiler_params=_cparams("parallel", "arbitrary"),
        name="dense_attention",
    )(z, z, z, y_prev)


HY_FEAT_ROWS = 64
HIGHEST = lax.Precision.HIGHEST


def _hy_features(seq):
    t = np.linspace(0.0, 1.0, seq)
    bands = (HY_EMB - 1) // 2
    w = 2.0 * math.pi * np.arange(seq) / seq
    fr = np.linspace(1e-4, bands - 1, bands)
    ang = fr[None] * w[:, None]
    feat = np.concatenate([t[:, None], np.cos(ang), -np.sin(ang)], axis=-1)
    src = np.concatenate([np.arange(seq), np.zeros(1, np.int64), np.arange(seq - 1, 0, -1)])
    feat2 = np.zeros((HY_FEAT_ROWS, 2 * seq), np.float32)
    feat2[:HY_EMB] = feat[src].T
    tt = t[src][None].astype(np.float32)
    mask = np.ones((1, 2 * seq), np.float32)
    mask[0, seq] = 0.0
    return feat2, tt, mask


def _hy_mlp_kernel(feat_ref, w1_ref, b1_ref, w2_ref, b2_ref, fq_ref, o_ref):
    fq = fq_ref[...]
    h = jnp.dot(w1_ref[...], feat_ref[...], preferred_element_type=F32, precision=HIGHEST)
    h = jnp.sin(fq * (h + b1_ref[...]))
    h = jnp.dot(w2_ref[...], h, preferred_element_type=F32, precision=HIGHEST)
    o_ref[...] = jnp.sin(fq * (h + b2_ref[...]))


def _hy_mlp(feat2, w1t, b1, w2t, b2, fq):
    depth, hid, _ = w1t.shape
    n = feat2.shape[1]
    col = lambda a: a.reshape(depth, hid, 1)
    wspec = lambda k: pl.BlockSpec((None, hid, k), lambda l: (l, 0, 0))
    return pl.pallas_call(
        _hy_mlp_kernel,
        grid=(depth,),
        in_specs=[pl.BlockSpec((HY_FEAT_ROWS, n), lambda l: (0, 0)), wspec(HY_FEAT_ROWS), wspec(1), wspec(hid),
                  wspec(1), wspec(1)],
        out_specs=pl.BlockSpec((None, hid, n), lambda l: (l, 0, 0)),
        out_shape=jax.ShapeDtypeStruct((depth, hid, n), F32),
        compiler_params=_cparams("parallel"),
        name="hyena_filter_mlp",
    )(feat2, w1t, col(b1), w2t, col(b2), col(fq))


def _hy_filter_kernel(h_ref, w3_ref, dl_ref, tt_ref, mask_ref, o_ref):
    seq = h_ref.shape[1] // 2
    kf = jnp.dot(w3_ref[0], h_ref[:, :seq], preferred_element_type=F32, precision=HIGHEST)
    kb = jnp.dot(w3_ref[1], h_ref[:, seq:], preferred_element_type=F32, precision=HIGHEST)
    k = jnp.concatenate([kf, kb], axis=1) * (jnp.exp(-tt_ref[...] * dl_ref[...]) * mask_ref[...])
    o_ref[...] = k * lax.rsqrt(jnp.sum(k * k, axis=1, keepdims=True) + EPS)


def _hy_filter(h2, w3t, deltas, tt, mask, cb):
    depth, hid, n = h2.shape
    c = w3t.shape[3]
    return pl.pallas_call(
        _hy_filter_kernel,
        grid=(depth, HY_ORDER, c // cb),
        in_specs=[
            pl.BlockSpec((None, hid, n), lambda l, o, j: (l, 0, 0)),
            pl.BlockSpec((None, None, 2, cb, hid), lambda l, o, j: (l, o, 0, j, 0)),
            pl.BlockSpec((cb, 1), lambda l, o, j: (j, 0)),
            pl.BlockSpec((1, n), lambda l, o, j: (0, 0)),
            pl.BlockSpec((1, n), lambda l, o, j: (0, 0)),
        ],
        out_specs=pl.BlockSpec((None, None, cb, n), lambda l, o, j: (l, o, j, 0)),
        out_shape=jax.ShapeDtypeStruct((depth, HY_ORDER, c, n), F32),
        compiler_params=_cparams("parallel", "parallel", "arbitrary"),
        name="hyena_filter",
    )(h2, w3t, deltas, tt, mask)


FFT_NO = 64
FFT_NI = 128


def _fft_tables():
    n = FFT_NO * FFT_NI
    a = np.arange(FFT_NO)
    fo = np.exp(-2j * np.pi * np.outer(a, a) / FFT_NO)
    i = np.arange(FFT_NI)
    ci = np.exp(-2j * np.pi * np.outer(i, i) / FFT_NI)
    tw = np.exp(-2j * np.pi * np.outer(a, i) / n)
    half = FFT_NO // 2
    f32 = lambda x: np.ascontiguousarray(x, dtype=np.float32)
    g1_real = f32(np.concatenate([fo.real, fo.imag], axis=0))
    g1 = f32(np.block([[fo.real[:, :half], -fo.imag[:, :half]], [fo.imag[:, :half], fo.real[:, :half]]]))
    w2 = f32(np.block([[ci.real, ci.imag], [-ci.imag, ci.real]]))
    w2i = f32(np.block([[ci.real, -ci.imag], [ci.imag, ci.real]]))
    g4 = f32(np.block([[fo.real[:half], fo.imag[:half]], [-fo.imag[:half], fo.real[:half]]]) / n)
    return dict(g1_real=g1_real, g1=g1, w2=w2, w2i=w2i, g4=g4, twr=f32(tw.real), twi=f32(tw.imag))


HY_CHUNK = 8


def _fft_stage1(g1, y2, twr, twi):
    o1 = jnp.dot(g1, y2, preferred_element_type=F32)
    yr, yi = o1[:FFT_NO], o1[FFT_NO:]
    ar = (yr * twr - yi * twi).astype(BF16)
    ai = (yr * twi + yi * twr).astype(BF16)
    lanes = lambda ci: slice(ci * FFT_NI, (ci + 1) * FFT_NI)
    return jnp.concatenate([jnp.concatenate([ar[:, lanes(ci)], ai[:, lanes(ci)]], axis=1)
                            for ci in range(HY_CHUNK)], axis=0)


def _hy_spectrum_kernel(k_ref, g1_ref, w2_ref, twr_ref, twi_ref, o_ref):
    cb = k_ref.shape[0]
    g1 = g1_ref[...]
    w2 = w2_ref[...]
    twr = twr_ref[...]
    twi = twi_ref[...]

    def body(j, _):
        chans = pl.ds(pl.multiple_of(j * HY_CHUNK, HY_CHUNK), HY_CHUNK)
        ks = k_ref[chans]
        k2 = jnp.concatenate([ks[ci] for ci in range(HY_CHUNK)], axis=1).astype(BF16)
        z = jnp.dot(_fft_stage1(g1, k2, twr, twi), w2, preferred_element_type=F32)
        o_ref[chans] = z.reshape(HY_CHUNK, FFT_NO, 2 * FFT_NI)
        return 0

    lax.fori_loop(0, cb // HY_CHUNK, body, 0)


def _tiled_twiddles(tabs):
    return (jnp.asarray(np.tile(tabs["twr"], (1, HY_CHUNK))), jnp.asarray(np.tile(tabs["twi"], (1, HY_CHUNK))))


def _hy_spectrum(kfilt, tabs, cb):
    g, c = kfilt.shape[:2]
    const = lambda a: pl.BlockSpec(a.shape, lambda i, j: (0,) * a.ndim)
    g1 = jnp.asarray(tabs["g1_real"], BF16)
    w2 = jnp.asarray(tabs["w2"], BF16)
    twr, twi = _tiled_twiddles(tabs)
    return pl.pallas_call(
        _hy_spectrum_kernel,
        grid=(g, c // cb),
        in_specs=[pl.BlockSpec((None, cb, FFT_NO, FFT_NI), lambda i, j: (i, j, 0, 0)),
                  const(g1), const(w2), const(twr), const(twi)],
        out_specs=pl.BlockSpec((None, cb, FFT_NO, 2 * FFT_NI), lambda i, j: (i, j, 0, 0)),
        out_shape=jax.ShapeDtypeStruct((g, c, FFT_NO, 2 * FFT_NI), F32),
        compiler_params=_cparams("parallel", "arbitrary"),
        name="hyena_spectrum",
    )(kfilt, g1, w2, twr, twi)


def _shift_conv3(x, w0, w1, w2, b):
    nrow, nlane = x.shape
    row = lax.broadcasted_iota(jnp.int32, x.shape, 0)
    lane = lax.broadcasted_iota(jnp.int32, x.shape, 1)
    r = pltpu.roll(x, 1, axis=1)
    prev = jnp.where(lane == 0, pltpu.roll(r, 1, axis=0), r)
    prev = jnp.where((lane == 0) & (row == 0), 0.0, prev)
    r = pltpu.roll(x, nlane - 1, axis=1)
    nxt = jnp.where(lane == nlane - 1, pltpu.roll(r, nrow - 1, axis=0), r)
    nxt = jnp.where((lane == nlane - 1) & (row == nrow - 1), 0.0, nxt)
    return prev * w0 + x * w1 + nxt * w2 + b


def _hyena_kernel(x_ref, kf_ref, pm_ref, g1_ref, w2_ref, w2i_ref, g4_ref, twr_ref, twi_ref, o_ref):
    cb = x_ref.shape[2]
    half = FFT_NO // 2
    ch = HY_CHUNK
    g1 = g1_ref[...]
    g4 = g4_ref[...]
    w2 = w2_ref[...]
    w2i = w2i_ref[...]
    twr = twr_ref[...]
    twi = twi_ref[...]
    lanes = lambda ci: slice(ci * FFT_NI, (ci + 1) * FFT_NI)
    krows = lambda ci: slice(ci * FFT_NO, (ci + 1) * FFT_NO)

    def body(j, _):
        c0 = pl.multiple_of(j * ch, ch)
        pm = pm_ref[:, pl.ds(c0, ch)]
        conv = []
        for part in range(3):
            halves = []
            for bb in range(2):
                xs = x_ref[bb, part, pl.ds(c0, ch)].astype(F32)
                halves.append(jnp.concatenate(
                    [_shift_conv3(xs[ci], pm[part, ci], pm[3 + part, ci], pm[6 + part, ci], pm[9 + part, ci])
                     for ci in range(ch)], axis=1))
            conv.append(jnp.concatenate(halves, axis=0))
        y = conv[2]
        for order in range(HY_ORDER):
            z = jnp.dot(_fft_stage1(g1, y.astype(BF16), twr, twi), w2, preferred_element_type=F32)
            kf = kf_ref[order, pl.ds(c0, ch)].reshape(ch * FFT_NO, 2 * FFT_NI)
            zr, zi = z[:, :FFT_NI], z[:, FFT_NI:]
            kr, ki = kf[:, :FFT_NI], kf[:, FFT_NI:]
            p = jnp.concatenate([zr * kr - zi * ki, zr * ki + zi * kr], axis=1).astype(BF16)
            q = jnp.dot(p, w2i, preferred_element_type=F32)
            qr = jnp.concatenate([q[krows(ci), :FFT_NI] for ci in range(ch)], axis=1)
            qi = jnp.concatenate([q[krows(ci), FFT_NI:] for ci in range(ch)], axis=1)
            qs = jnp.concatenate([qr * twr + qi * twi, qi * twr - qr * twi], axis=0).astype(BF16)
            cv = jnp.dot(g4, qs, preferred_element_type=F32)
            bias = jnp.concatenate([pm[12 + order, ci] for ci in range(ch)], axis=1)
            y = conv[order] * (cv + bias * y)
        for ci in range(ch):
            o_ref[0, c0 + ci] = y[:half, lanes(ci)].astype(o_ref.dtype)
            o_ref[1, c0 + ci] = y[half:, lanes(ci)].astype(o_ref.dtype)
        return 0

    lax.fori_loop(0, cb // ch, body, 0)


def _hyena(xt, kf, pm, tabs, cb):
    nb, _, c, half, _ = xt.shape
    const = lambda a: pl.BlockSpec(a.shape, lambda j, p: (0,) * a.ndim)
    bf = lambda name: jnp.asarray(tabs[name], BF16)
    g1, w2, w2i, g4 = bf("g1"), bf("w2"), bf("w2i"), bf("g4")
    twr, twi = _tiled_twiddles(tabs)
    return pl.pallas_call(
        _hyena_kernel,
        grid=(c // cb, nb // 2),
        in_specs=[
            pl.BlockSpec((2, 3, cb, half, FFT_NI), lambda j, p: (p, 0, j, 0, 0)),
            pl.BlockSpec((HY_ORDER, cb, FFT_NO, 2 * FFT_NI), lambda j, p: (0, j, 0, 0)),
            pl.BlockSpec((14, cb, 1, FFT_NI), lambda j, p: (0, j, 0, 0)),
            const(g1), const(w2), const(w2i), const(g4), const(twr), const(twi),
        ],
        out_specs=pl.BlockSpec((2, cb, half, FFT_NI), lambda j, p: (p, j, 0, 0)),
        out_shape=jax.ShapeDtypeStruct((nb, c, half, FFT_NI), BF16),
        compiler_params=_cparams("parallel", "arbitrary"),
        name="hyena",
    )(xt, kf, pm, g1, w2, w2i, g4, twr, twi)


def _dft_tables(seq):
    n = 2 * seq
    wmat = np.exp(-2j * np.pi * np.outer(np.arange(n), np.arange(n)) / n)
    f32 = lambda x: np.ascontiguousarray(x, dtype=np.float32)
    fwd_real = f32(np.concatenate([wmat.real, wmat.imag], axis=1))
    ws = wmat[:seq]
    fwd = f32(np.block([[ws.real, ws.imag], [-ws.imag, ws.real]]))
    wi = np.conj(wmat)[:, :seq] / n
    inv = f32(np.block([[wi.real, wi.imag], [-wi.imag, wi.real]]))
    return dict(fwd_real=fwd_real, fwd=fwd, inv=inv)


def _rowdft_kernel(x_ref, w_ref, o_ref):
    o_ref[...] = jnp.dot(x_ref[...].astype(BF16), w_ref[...], preferred_element_type=F32)


def _rowdft(x, w, tr):
    m, k = x.shape
    n = w.shape[1]
    return pl.pallas_call(
        _rowdft_kernel,
        grid=(m // tr,),
        in_specs=[pl.BlockSpec((tr, k), lambda i: (i, 0)), pl.BlockSpec((k, n), lambda i: (0, 0))],
        out_specs=pl.BlockSpec((tr, n), lambda i: (i, 0)),
        out_shape=jax.ShapeDtypeStruct((m, n), F32),
        compiler_params=_cparams("parallel"),
        name="row_dft",
    )(x, w)


def _lane_conv3(x, w0, w1, w2, b):
    seq = x.shape[1]
    lane = lax.broadcasted_iota(jnp.int32, x.shape, 1)
    prev = jnp.where(lane == 0, 0.0, pltpu.roll(x, 1, axis=1))
    nxt = jnp.where(lane == seq - 1, 0.0, pltpu.roll(x, seq - 1, axis=1))
    return prev * w0 + x * w1 + nxt * w2 + b


def _hyena_ctx_kernel(x_ref, kf_ref, pm_ref, fwd_ref, inv_ref, o_ref):
    seq = x_ref.shape[3]
    n = 2 * seq
    conv = [[_lane_conv3(x_ref[bb, part].astype(F32), pm_ref[part], pm_ref[3 + part], pm_ref[6 + part],
                         pm_ref[9 + part]) for bb in range(2)] for part in range(3)]
    ya, yb = conv[2]
    for order in range(HY_ORDER):
        z = jnp.dot(jnp.concatenate([ya, yb], axis=1).astype(BF16), fwd_ref[...], preferred_element_type=F32)
        kf = kf_ref[order]
        zr, zi, kr, ki = z[:, :n], z[:, n:], kf[:, :n], kf[:, n:]
        p = jnp.concatenate([zr * kr - zi * ki, zr * ki + zi * kr], axis=1).astype(BF16)
        cv = jnp.dot(p, inv_ref[...], preferred_element_type=F32)
        bias = pm_ref[12 + order]
        ya = conv[order][0] * (cv[:, :seq] + bias * ya)
        yb = conv[order][1] * (cv[:, seq:] + bias * yb)
    o_ref[0] = ya.astype(o_ref.dtype)
    o_ref[1] = yb.astype(o_ref.dtype)


def _hyena_ctx(xt, kf, pm, tabs, cb):
    nb, _, c, seq = xt.shape
    fwd, inv = jnp.asarray(tabs["fwd"], BF16), jnp.asarray(tabs["inv"], BF16)
    const = lambda a: pl.BlockSpec(a.shape, lambda j, p: (0,) * a.ndim)
    return pl.pallas_call(
        _hyena_ctx_kernel,
        grid=(c // cb, nb // 2),
        in_specs=[
            pl.BlockSpec((2, 3, cb, seq), lambda j, p: (p, 0, j, 0)),
            pl.BlockSpec((HY_ORDER, cb, 4 * seq), lambda j, p: (0, j, 0)),
            pl.BlockSpec((14, cb, 1), lambda j, p: (0, j, 0)),
            const(fwd), const(inv),
        ],
        out_specs=pl.BlockSpec((2, cb, seq), lambda j, p: (p, j, 0)),
        out_shape=jax.ShapeDtypeStruct((nb, c, seq), BF16),
        compiler_params=_cparams("parallel", "arbitrary"),
        name="hyena_ctx",
    )(xt, kf, pm, fwd, inv)


def _rope_tables(seq):
    t = np.arange(seq)
    pos = np.stack([t // GRID_W, t % GRID_W], axis=1).astype(np.float64)
    n = RET_DK // 4
    inv = ROPE_BASE ** (-np.arange(n, dtype=np.float64) / n)
    lane = np.arange(LANES) % RET_DK
    ang = pos[:, lane // (2 * n)] * inv[lane % n][None]
    sign = np.where(lane % (2 * n) < n, -1.0, 1.0)
    return np.cos(ang).astype(np.float32), (np.sin(ang) * sign[None]).astype(np.float32)


def _hy_filters(seq, hy_f_w1, hy_f_b1, hy_f_w2, hy_f_b2, hy_f_w3, hy_f_freq):
    depth, _, hid = hy_f_w1.shape
    c = hy_f_w3.shape[2] // (2 * HY_ORDER)
    feat2, tt, mask = _hy_features(seq)
    w1t = jnp.pad(hy_f_w1.transpose(0, 2, 1), ((0, 0), (0, 0), (0, HY_FEAT_ROWS - HY_EMB)))
    h2 = _hy_mlp(jnp.asarray(feat2), w1t, hy_f_b1, hy_f_w2.transpose(0, 2, 1), hy_f_b2, hy_f_freq)
    w3t = hy_f_w3.reshape(depth, hid, HY_ORDER, 2, c).transpose(0, 2, 3, 4, 1)
    deltas = np.abs(np.linspace(math.log(HY_TARGET) / HY_FAST, math.log(HY_TARGET) / HY_SLOW, c))
    deltas = jnp.asarray(deltas.astype(np.float32).reshape(c, 1))
    return _hy_filter(h2, w3t, deltas, jnp.asarray(tt), jnp.asarray(mask), min(c, 8 * 1024 * 128 // (2 * seq)))


def _hy_params(hy_conv_w, hy_conv_b, hy_bias):
    c = hy_bias.shape[1]
    return jnp.concatenate([hy_conv_w.reshape(9, c), hy_conv_b.reshape(3, c), hy_bias], axis=0)


def kernel(x, c, ctx, c_ctx, w_mod, b_mod, g_norm1, g_norm2, w_in, hy_conv_w, hy_conv_b, hy_f_w1, hy_f_b1, hy_f_w2, hy_f_b2, hy_f_w3, hy_f_freq, hy_bias, ret_log_decay, na_rpb, w_branch, w_out, ffn_w_in, ffn_conv_w, ffn_conv_b, ffn_w_out, g_final):
    nb, seq, d = x.shape
    cseq = ctx.shape[1]
    depth = w_mod.shape[0]
    hy_w = hy_bias.shape[2]
    t_lat, t_ctx = nb * seq, nb * cseq
    assert seq % TM == 0 and t_ctx % TM == 0 and TM % cseq == 0 and nb % 2 == 0 and nb < SUBLANES
    assert 2 * seq == FFT_NO * FFT_NI and seq % GRID_W == 0 and cseq & (cseq - 1) == 0
    blk_lat, blk_all = t_lat // TM, (t_lat + t_ctx) // TM
    mrow = jnp.asarray(np.concatenate([np.repeat(np.arange(nb), seq // TM), np.full(t_ctx // TM, nb)]), jnp.int32)
    seqlen = jnp.asarray(np.concatenate([np.full(blk_lat, seq), np.full(t_ctx // TM, cseq)]), jnp.int32)
    per = TM // TR
    mrow_r = jnp.repeat(mrow, per)
    tn = 1024

    cond = jnp.zeros((SUBLANES, d), F32).at[:nb].set(c).at[nb].set(c_ctx)
    mods = _mods(cond, w_mod, b_mod).reshape(depth, SUBLANES, 6, 1, d)

    cos, sin = (jnp.asarray(a) for a in _rope_tables(seq))
    tabs = _fft_tables()
    ctabs = _dft_tables(cseq)
    filt = _hy_filters(seq, hy_f_w1, hy_f_b1, hy_f_w2, hy_f_b2, hy_f_w3, hy_f_freq)
    spec = _hy_spectrum(filt.reshape(depth * HY_ORDER, hy_w, FFT_NO, FFT_NI), tabs, 32)
    spec = spec.reshape(depth, HY_ORDER, hy_w, FFT_NO, 2 * FFT_NI)
    cfilt = _hy_filters(cseq, hy_f_w1, hy_f_b1, hy_f_w2, hy_f_b2, hy_f_w3, hy_f_freq)
    cspec = _rowdft(cfilt.reshape(depth * HY_ORDER * hy_w, 2 * cseq), jnp.asarray(ctabs["fwd_real"], BF16), 1024)
    cspec = cspec.reshape(depth, HY_ORDER, hy_w, 4 * cseq)

    xs = jnp.concatenate([x.reshape(t_lat, d), ctx.reshape(t_ctx, d)], axis=0)
    hy0 = 6144
    gate0 = hy0 + 3 * hy_w
    s_zero = jnp.zeros((nb, RET_HEADS, 2, LANES, RET_DV), F32)
    hn = _normmod(xs, g_norm1[0], mods[0], mrow, 0, 1, blk_all)
    for l in range(depth):
        last = l == depth - 1
        nblk = blk_lat if last else blk_all
        rows = nblk * TM
        z = _matmul(hn, w_in, l, blk_all, tn, w_in.shape[2] - 3 * hy_w, skip=(hy0 // tn, 3 * hy_w // tn))
        hy_proj = functools.partial(_matmul_nt, hn, w_in, l, col0=hy0, n_out=3 * hy_w, nb=nb, tc=512)
        zh = hy_proj(blk0=0, nblk=blk_lat, seq=seq)

        y_ret, s_ctx = _retention(z, ret_log_decay[l], cos, sin, s_zero, nb=nb, seq=cseq, row0=t_lat, rope=False,
                                  out_rows=blk_all * TM)
        y_ret, _ = _retention(z, ret_log_decay[l], cos, sin, s_ctx, nb=nb, seq=seq, row0=0, rope=True,
                              out_rows=blk_all * TM, y_prev=y_ret)
        y_na = _na_attention(z, na_rpb[l], nb=nb, seq=seq, cseq=cseq, crow0=t_lat, out_rows=rows)
        pm = _hy_params(hy_conv_w[l], hy_conv_b[l], hy_bias[l])
        pm_lat = jnp.broadcast_to(pm[:, :, None, None], (14, hy_w, 1, FFT_NI))
        y_hy = _hyena(zh.reshape(nb, 3, hy_w, FFT_NO // 2, FFT_NI), spec[l], pm_lat, tabs, 32)
        y_hy = y_hy.reshape(nb, hy_w, seq).transpose(0, 2, 1).reshape(t_lat, hy_w)
        if not last:
            y_na = _dense_attention(z, y_na, nb=nb, cseq=cseq, crow0=t_lat)
            zc = hy_proj(blk0=blk_lat, nblk=t_ctx // TM, seq=cseq)
            yc_hy = _hyena_ctx(zc.reshape(nb, 3, hy_w, cseq), cspec[l], pm[:, :, None], ctabs, 256)
            y_hy = jnp.concatenate([y_hy, yc_hy.transpose(0, 2, 1).reshape(t_ctx, hy_w)], axis=0)
        acc = _merge((y_hy, y_ret, y_na), z, hy0, w_branch[l].astype(BF16), nblk, 1024)
        xs, hn = _out_proj(acc, w_out[l].astype(BF16), xs, mods[l], mrow_r, g_norm2[l], nblk * per)
        u = _matmul(hn, ffn_w_in, l, nblk, tn, ffn_w_in.shape[2])
        hg = _ffn_gate(u, ffn_conv_w[l], ffn_conv_b[l], seqlen, nblk, 512)
        nxt = l if last else l + 1
        res = _ffn_out(hg, ffn_w_out[l].astype(BF16), xs, mods[l], mods[nxt], g_final if last else g_norm1[nxt],
                       mrow_r, nblk * per, 512, last)
        if not last:
            xs, hn = res
    return res[0].reshape(nb, seq, d)
```

```python
import functools
import math

import numpy as np
import jax
import jax.numpy as jnp
from jax import lax
from jax.experimental import pallas as pl
from jax.experimental.pallas import tpu as pltpu

F32 = jnp.float32
BF16 = jnp.bfloat16

GRID_W = 64
N_BRANCH = 3
HY_ORDER = 2
HY_EMB = 33
HY_FAST = 0.3
HY_SLOW = 1.5
HY_TARGET = 1e-2
RET_HEADS = 8
RET_DK = 64
RET_DV = 128
RET_CHUNK = 128
NA_HEADS = 8
NA_DH = 128
NA_WR = 8
NA_WC = 16
ROPE_BASE = 10000.0
EPS = 1e-6
NEG = -1e30

LANES = 128
SUBLANES = 8
VMEM_LIMIT = 56 * 1024 * 1024

TM = 1024


def _cparams(*sem):
    return pltpu.CompilerParams(dimension_semantics=sem, vmem_limit_bytes=VMEM_LIMIT)


def _mods_kernel(a_ref, w_ref, b_ref, o_ref):
    a = a_ref[...]
    a = a * jax.nn.sigmoid(a)
    o_ref[...] = jnp.dot(a.astype(BF16), w_ref[...].astype(BF16), preferred_element_type=F32) + b_ref[...]


def _mods(cond, w_mod, b_mod):
    depth, d, n = w_mod.shape
    tn = 1024
    return pl.pallas_call(
        _mods_kernel,
        grid=(depth, n // tn),
        in_specs=[
            pl.BlockSpec((SUBLANES, d), lambda l, j: (0, 0)),
            pl.BlockSpec((None, d, tn), lambda l, j: (l, 0, j)),
            pl.BlockSpec((None, 1, tn), lambda l, j: (l, 0, j)),
        ],
        out_specs=pl.BlockSpec((None, SUBLANES, tn), lambda l, j: (l, 0, j)),
        out_shape=jax.ShapeDtypeStruct((depth, SUBLANES, n), F32),
        compiler_params=_cparams("parallel", "parallel"),
        name="mods",
    )(cond, w_mod, b_mod.reshape(depth, 1, n))


def _normmod_kernel(mrow_ref, x_ref, g_ref, sh_ref, sc_ref, o_ref):
    x = x_ref[...]
    y = x * lax.rsqrt(jnp.mean(x * x, axis=-1, keepdims=True) + EPS)
    y = y * g_ref[...]
    o_ref[...] = (y * (1.0 + sc_ref[...]) + sh_ref[...]).astype(o_ref.dtype)


def _normmod(x, g, mods, mrow, shift_idx, scale_idx, nblk):
    m, d = x.shape
    grid_spec = pltpu.PrefetchScalarGridSpec(
        num_scalar_prefetch=1,
        grid=(nblk,),
        in_specs=[
            pl.BlockSpec((TM, d), lambda i, mr: (i, 0)),
            pl.BlockSpec((1, d), lambda i, mr: (0, 0)),
            pl.BlockSpec((None, None, 1, d), lambda i, mr: (mr[i], shift_idx, 0, 0)),
            pl.BlockSpec((None, None, 1, d), lambda i, mr: (mr[i], scale_idx, 0, 0)),
        ],
        out_specs=pl.BlockSpec((TM, d), lambda i, mr: (i, 0)),
    )
    return pl.pallas_call(
        _normmod_kernel,
        grid_spec=grid_spec,
        out_shape=jax.ShapeDtypeStruct((m, d), BF16),
        compiler_params=_cparams("parallel"),
        name="normmod",
    )(mrow, x, g.reshape(1, d), mods, mods)


def _mm_kernel(a_ref, w_ref, o_ref, w_scr):
    @pl.when(pl.program_id(1) == 0)
    def _():
        w_scr[...] = w_ref[...].astype(BF16)

    o_ref[...] = jnp.dot(a_ref[...], w_scr[...], preferred_element_type=F32).astype(o_ref.dtype)


def _matmul(a, w, layer, nblk, tn, n_out, skip=None):
    m, k = a.shape
    col = (lambda j: j) if skip is None else (lambda j: j + jnp.where(j >= skip[0], skip[1], 0))
    return pl.pallas_call(
        _mm_kernel,
        grid=(n_out // tn, nblk),
        in_specs=[pl.BlockSpec((TM, k), lambda j, i: (i, 0)),
                  pl.BlockSpec((None, k, tn), lambda j, i: (layer, 0, col(j)))],
        out_specs=pl.BlockSpec((TM, tn), lambda j, i: (i, j)),
        out_shape=jax.ShapeDtypeStruct((m, n_out), BF16),
        scratch_shapes=[pltpu.VMEM((k, tn), BF16)],
        compiler_params=_cparams("arbitrary", "arbitrary"),
        name="matmul",
    )(a, w)


def _mm_nt_kernel(w_ref, a_ref, o_ref, wt_scr):
    @pl.when(pl.program_id(1) == 0)
    def _():
        wt_scr[...] = w_ref[...].T.astype(BF16)

    r = lax.dot_general(wt_scr[...], a_ref[...], (((1,), (1,)), ((), ())), preferred_element_type=F32)
    per = o_ref.shape[0]
    width = r.shape[1] // per
    for s in range(per):
        o_ref[s] = r[:, s * width:(s + 1) * width].astype(o_ref.dtype)


def _matmul_nt(a, w, layer, *, col0, n_out, blk0, nblk, nb, seq, tc):
    k = a.shape[1]
    per, sblk = max(TM // seq, 1), max(seq // TM, 1)
    return pl.pallas_call(
        _mm_nt_kernel,
        grid=(n_out // tc, nblk),
        in_specs=[pl.BlockSpec((None, k, tc), lambda j, i: (layer, 0, col0 // tc + j)),
                  pl.BlockSpec((TM, k), lambda j, i: (blk0 + i, 0))],
        out_specs=pl.BlockSpec((per, tc, TM // per), lambda j, i: (i // sblk, j, i % sblk)),
        out_shape=jax.ShapeDtypeStruct((nb, n_out, seq), BF16),
        scratch_shapes=[pltpu.VMEM((tc, k), BF16)],
        compiler_params=_cparams("arbitrary", "arbitrary"),
        name="matmul_nt",
    )(w, a)


TR = 512


def _norm_modulate(x, g, shift, scale):
    y = x * lax.rsqrt(jnp.mean(x * x, axis=-1, keepdims=True) + EPS)
    return (y * g) * (1.0 + scale) + shift


def _outproj_kernel(mrow_ref, a_ref, w_ref, x_ref, gate_ref, g_ref, sh_ref, sc_ref, x_out, h_out):
    y = jnp.dot(a_ref[...], w_ref[...], preferred_element_type=F32)
    x = x_ref[...] + gate_ref[...] * y
    x_out[...] = x
    h_out[...] = _norm_modulate(x, g_ref[...], sh_ref[...], sc_ref[...]).astype(h_out.dtype)


def _out_proj(a, w, x, mods, mrow, g_norm, nblk):
    m, k = a.shape
    d = w.shape[1]
    mod = lambda idx: pl.BlockSpec((None, None, 1, d), lambda i, mr: (mr[i], idx, 0, 0))
    row = pl.BlockSpec((TR, d), lambda i, mr: (i, 0))
    grid_spec = pltpu.PrefetchScalarGridSpec(
        num_scalar_prefetch=1,
        grid=(nblk,),
        in_specs=[pl.BlockSpec((TR, k), lambda i, mr: (i, 0)), pl.BlockSpec((k, d), lambda i, mr: (0, 0)), row,
                  mod(2), pl.BlockSpec((1, d), lambda i, mr: (0, 0)), mod(3), mod(4)],
        out_specs=[row, row],
    )
    return pl.pallas_call(
        _outproj_kernel,
        grid_spec=grid_spec,
        out_shape=[jax.ShapeDtypeStruct((m, d), F32), jax.ShapeDtypeStruct((m, d), BF16)],
        compiler_params=_cparams("parallel"),
        name="out_proj",
    )(mrow, a, w, x, mods, g_norm.reshape(1, d), mods, mods)


def _merge_kernel(y0_ref, y1_ref, y2_ref, g0_ref, g1_ref, g2_ref, w_ref, o_ref):
    acc = None
    for i, (y_ref, g_ref) in enumerate(((y0_ref, g0_ref), (y1_ref, g1_ref), (y2_ref, g2_ref))):
        t = jnp.dot(y_ref[...], w_ref[i], preferred_element_type=F32)
        t = jax.nn.sigmoid(g_ref[...].astype(F32)) * t
        acc = t if acc is None else acc + t
    o_ref[...] = acc.astype(o_ref.dtype)


def _merge(ys, z, gate_col0, w_branch, nblk, tn):
    m, bw = ys[0].shape
    d = w_branch.shape[2]
    gspec = lambda i_br: pl.BlockSpec((TM, tn), lambda i, j: (i, (gate_col0 + i_br * d) // tn + j))
    yspec = pl.BlockSpec((TM, bw), lambda i, j: (i, 0))
    return pl.pallas_call(
        _merge_kernel,
        grid=(nblk, d // tn),
        in_specs=[yspec, yspec, yspec, gspec(0), gspec(1), gspec(2),
                  pl.BlockSpec((N_BRANCH, bw, tn), lambda i, j: (0, 0, j))],
        out_specs=pl.BlockSpec((TM, tn), lambda i, j: (i, j)),
        out_shape=jax.ShapeDtypeStruct((m, d), BF16),
        compiler_params=_cparams("parallel", "arbitrary"),
        name="merge",
    )(ys[0], ys[1], ys[2], z, z, z, w_branch)


def _ffn_gate_kernel(seq_ref, a_ref, ap_ref, an_ref, b_ref, cw_ref, cb_ref, o_ref):
    i = pl.program_id(0)
    seq_m1 = seq_ref[i] - 1
    a = a_ref[...].astype(F32)
    tm = a.shape[0]
    row = lax.broadcasted_iota(jnp.int32, (tm, 1), 0)
    pos = (i * tm + row) & seq_m1
    prev = jnp.where(row == 0, ap_ref[SUBLANES - 1:SUBLANES, :].astype(F32), pltpu.roll(a, 1, axis=0))
    prev = jnp.where(pos == 0, 0.0, prev)
    nxt = jnp.where(row == tm - 1, an_ref[0:1, :].astype(F32), pltpu.roll(a, tm - 1, axis=0))
    nxt = jnp.where(pos == seq_m1, 0.0, nxt)
    cw = cw_ref[...]
    conv = prev * cw[0:1, :] + a * cw[1:2, :] + nxt * cw[2:3, :] + cb_ref[...]
    o_ref[...] = (jax.nn.gelu(conv) * b_ref[...].astype(F32)).astype(o_ref.dtype)


def _ffn_gate(u, conv_w, conv_b, seqlen, nblk, tc):
    m, ff2 = u.shape
    ff = ff2 // 2
    nrow8 = m // SUBLANES
    r8 = TM // SUBLANES
    grid_spec = pltpu.PrefetchScalarGridSpec(
        num_scalar_prefetch=1,
        grid=(nblk, ff // tc),
        in_specs=[
            pl.BlockSpec((TM, tc), lambda i, j, s: (i, j)),
            pl.BlockSpec((SUBLANES, tc), lambda i, j, s: (jnp.maximum(i * r8 - 1, 0), j)),
            pl.BlockSpec((SUBLANES, tc), lambda i, j, s: (jnp.minimum((i + 1) * r8, nrow8 - 1), j)),
            pl.BlockSpec((TM, tc), lambda i, j, s: (i, ff // tc + j)),
            pl.BlockSpec((3, tc), lambda i, j, s: (0, j)),
            pl.BlockSpec((1, tc), lambda i, j, s: (0, j)),
        ],
        out_specs=pl.BlockSpec((TM, tc), lambda i, j, s: (i, j)),
    )
    return pl.pallas_call(
        _ffn_gate_kernel,
        grid_spec=grid_spec,
        out_shape=jax.ShapeDtypeStruct((m, ff), BF16),
        compiler_params=_cparams("parallel", "arbitrary"),
        name="ffn_gate",
    )(seqlen, u, u, u, u, conv_w, conv_b.reshape(1, ff))


def _ffn_out_kernel(mrow_ref, h_ref, w_ref, x_ref, gate_ref, g_ref, sh_ref, sc_ref, *rest, final):
    outs, xrow = rest[:-1], rest[-1]
    j = pl.program_id(1)
    y = jnp.dot(h_ref[...], w_ref[...], preferred_element_type=F32)
    xt = x_ref[...] + gate_ref[...] * y
    xrow[j] = xt
    if not final:
        outs[0][...] = xt

    @pl.when(j == pl.num_programs(1) - 1)
    def _():
        x = jnp.concatenate([xrow[t] for t in range(xrow.shape[0])], axis=1)
        if final:
            y = x * lax.rsqrt(jnp.mean(x * x, axis=-1, keepdims=True) + EPS)
            outs[0][...] = y * g_ref[...]
        else:
            outs[1][...] = _norm_modulate(x, g_ref[...], sh_ref[...], sc_ref[...]).astype(outs[1].dtype)


def _ffn_out(h, w, x, mods, mods_next, g_next, mrow, nblk, tr, tn, final):
    m, ff = h.shape
    d = w.shape[1]
    mod = lambda arr_idx: pl.BlockSpec((None, None, 1, d), lambda i, j, mr: (mr[i], arr_idx, 0, 0))
    row = pl.BlockSpec((tr, d), lambda i, j, mr: (i, 0))
    tile = pl.BlockSpec((tr, tn), lambda i, j, mr: (i, j))
    grid_spec = pltpu.PrefetchScalarGridSpec(
        num_scalar_prefetch=1,
        grid=(nblk, d // tn),
        in_specs=[
            pl.BlockSpec((tr, ff), lambda i, j, mr: (i, 0)),
            pl.BlockSpec((ff, tn), lambda i, j, mr: (0, j)),
            tile,
            pl.BlockSpec((None, None, 1, tn), lambda i, j, mr: (mr[i], 5, 0, j)),
            pl.BlockSpec((1, d), lambda i, j, mr: (0, 0)), mod(0), mod(1),
        ],
        out_specs=[row] if final else [tile, row],
        scratch_shapes=[pltpu.VMEM((d // tn, tr, tn), F32)],
    )
    out_shape = [jax.ShapeDtypeStruct((m, d), F32)] + ([] if final else [jax.ShapeDtypeStruct((m, d), BF16)])
    return pl.pallas_call(
        functools.partial(_ffn_out_kernel, final=final),
        grid_spec=grid_spec,
        out_shape=out_shape,
        compiler_params=_cparams("parallel", "arbitrary"),
        name="ffn_out",
    )(mrow, h, w, x, mods, g_next.reshape(1, d), mods_next, mods_next)


def _rope_pair(x, cos, sin):
    lane = lax.broadcasted_iota(jnp.int32, x.shape, 1)
    first = lax.rem(lane, 32) < 16
    partner = jnp.where(first, pltpu.roll(x, LANES - 16, axis=1), pltpu.roll(x, 16, axis=1))
    return x * cos + partner * sin


def _ret_kernel(lg_ref, q_ref, k_ref, v_ref, g_ref, cos_ref, sin_ref, s0_ref, *rest, rope, has_prev):
    o_ref, sfin_ref, qr_scr, kr_scr, kv_scr = rest[1:] if has_prev else rest
    hp = pl.program_id(1)
    seq = q_ref.shape[0]
    c = RET_CHUNK
    nchunk = seq // c
    ks = RET_DK ** -0.5
    q = q_ref[...].astype(F32)
    k = k_ref[...].astype(F32)
    if rope:
        q = _rope_pair(q, cos_ref[...], sin_ref[...])
        k = _rope_pair(k, cos_ref[...], sin_ref[...])
    qr_scr[...] = q
    kr_scr[...] = k * ks

    pos_r = lax.broadcasted_iota(jnp.int32, (c, 1), 0).astype(F32)
    rel = (lax.broadcasted_iota(jnp.int32, (c, c), 0) - lax.broadcasted_iota(jnp.int32, (c, c), 1)).astype(F32)
    lane = lax.broadcasted_iota(jnp.int32, (1, LANES), 1)

    heads = []
    for hh in range(2):
        lgf = lg_ref[0, 2 * hp + hh]
        lgb = lg_ref[1, 2 * hp + hh]
        heads.append(dict(
            hm=(lane // RET_DK == hh).astype(F32),
            zeta_f=jnp.exp(lgf * (c - 1 - pos_r)), zeta_b=jnp.exp(lgb * pos_r),
            xi_f=jnp.exp(lgf * (pos_r + 1.0)), xi_b=jnp.exp(lgb * (c - pos_r)),
            dmat=jnp.where(rel >= 0, jnp.exp(lgf * jnp.maximum(rel, 0.0)), jnp.exp(lgb * jnp.maximum(-rel, 0.0))),
            gf=jnp.exp(lgf * c), gb=jnp.exp(lgb * c),
            vcols=slice(hh * RET_DV, (hh + 1) * RET_DV)))

    def kv_body(n, _):
        rows = pl.ds(pl.multiple_of(n * c, c), c)
        kr = kr_scr[rows, :]
        for hh, hd in enumerate(heads):
            kh = kr * hd["hm"]
            kz = jnp.concatenate([kh * hd["zeta_f"], kh * hd["zeta_b"]], axis=1).astype(BF16)
            kv_scr[hh, n] = lax.dot_general(kz, v_ref[rows, hd["vcols"]], (((0,), (0,)), ((), ())),
                                            preferred_element_type=F32)
        return 0

    lax.fori_loop(0, nchunk, kv_body, 0, unroll=2)

    def scan(j, carry):
        nf, nbk = j, nchunk - 1 - j
        out = []
        for hh, hd in enumerate(heads):
            sf, sb = carry[2 * hh], carry[2 * hh + 1]
            tf = kv_scr[hh, nf, 0:LANES, :]
            kv_scr[hh, nf, 0:LANES, :] = sf
            tb = kv_scr[hh, nbk, LANES:2 * LANES, :]
            kv_scr[hh, nbk, LANES:2 * LANES, :] = sb
            out += [hd["gf"] * sf + tf, hd["gb"] * sb + tb]
        return tuple(out)

    fin = lax.fori_loop(0, nchunk, scan, (s0_ref[0, 0], s0_ref[0, 1], s0_ref[1, 0], s0_ref[1, 1]))
    for hh in range(2):
        sfin_ref[hh, 0] = fin[2 * hh]
        sfin_ref[hh, 1] = fin[2 * hh + 1]

    def out_body(n, _):
        rows = pl.ds(pl.multiple_of(n * c, c), c)
        qr = qr_scr[rows, :]
        kb = kr_scr[rows, :].astype(BF16)
        for hh, hd in enumerate(heads):
            qm = qr * hd["hm"]
            a = lax.dot_general(qm.astype(BF16), kb, (((1,), (1,)), ((), ())),
                                preferred_element_type=F32) * hd["dmat"]
            o = jnp.dot(a.astype(BF16), v_ref[rows, hd["vcols"]], preferred_element_type=F32)
            qx = jnp.concatenate([qm * hd["xi_f"], qm * hd["xi_b"]], axis=1).astype(BF16)
            o = o + jnp.dot(qx, kv_scr[hh, n].astype(BF16), preferred_element_type=F32)
            o = o * lax.rsqrt(jnp.mean(o * o, axis=-1, keepdims=True) + EPS)
            g = g_ref[rows, hd["vcols"]].astype(F32)
            o_ref[rows, hd["vcols"]] = (g * jax.nn.sigmoid(g) * o).astype(o_ref.dtype)
        return 0

    lax.fori_loop(0, nchunk, out_body, 0, unroll=2)


def _retention(z, log_decay, cos, sin, s0, *, nb, seq, row0, rope, out_rows, y_prev=None):
    rb0 = row0 // seq
    hpairs = RET_HEADS // 2
    kcol, vcol, qcol, gcol = 0, 512 // 256, 3584 // LANES, 4096 // 256
    has_prev = y_prev is not None
    kernel = functools.partial(_ret_kernel, rope=rope, has_prev=has_prev)
    in_specs = [
        pl.BlockSpec(memory_space=pltpu.SMEM),
        pl.BlockSpec((seq, LANES), lambda b, p: (rb0 + b, qcol + p)),
        pl.BlockSpec((seq, LANES), lambda b, p: (rb0 + b, kcol + p)),
        pl.BlockSpec((seq, 2 * RET_DV), lambda b, p: (rb0 + b, vcol + p)),
        pl.BlockSpec((seq, 2 * RET_DV), lambda b, p: (rb0 + b, gcol + p)),
        pl.BlockSpec((seq, LANES), lambda b, p: (0, 0)),
        pl.BlockSpec((seq, LANES), lambda b, p: (0, 0)),
        pl.BlockSpec((None, 2, 2, LANES, RET_DV), lambda b, p: (b, p, 0, 0, 0)),
    ]
    args = [log_decay, z, z, z, z, cos, sin, s0]
    if has_prev:
        in_specs.append(pl.BlockSpec(memory_space=pl.ANY))
        args.append(y_prev)
    return pl.pallas_call(
        kernel,
        grid=(nb, hpairs),
        in_specs=in_specs,
        out_specs=[
            pl.BlockSpec((seq, 2 * RET_DV), lambda b, p: (rb0 + b, p)),
            pl.BlockSpec((None, 2, 2, LANES, RET_DV), lambda b, p: (b, p, 0, 0, 0)),
        ],
        out_shape=[
            jax.ShapeDtypeStruct((out_rows, RET_HEADS * RET_DV), BF16),
            jax.ShapeDtypeStruct((nb, RET_HEADS, 2, LANES, RET_DV), F32),
        ],
        scratch_shapes=[
            pltpu.VMEM((seq, LANES), F32),
            pltpu.VMEM((seq, LANES), F32),
            pltpu.VMEM((2, seq // RET_CHUNK, 2 * LANES, RET_DV), F32),
        ],
        input_output_aliases={len(args) - 1: 0} if has_prev else {},
        compiler_params=_cparams("parallel", "arbitrary"),
        name="retention_rope" if rope else "retention",
    )(*args)


NA_GROUP = 4
NA_KROWS = 12


def _na_kernel(rpb_ref, q_ref, k_ref, v_ref, kc_ref, vc_ref, o_ref, bias_scr):
    seq = q_ref.shape[0]
    nrow = seq // GRID_W
    ndr, ndc = 2 * NA_WR - 1, 2 * NA_WC - 1
    scale = NA_DH ** -0.5
    nt = (((1,), (1,)), ((), ()))
    h = pl.program_id(0)
    lane = lax.broadcasted_iota(jnp.int32, (GRID_W, LANES), 1)

    @pl.when(pl.program_id(1) == 0)
    def _():
        qi = lax.broadcasted_iota(jnp.int32, (GRID_W, LANES), 0)
        ki = lane & (GRID_W - 1)
        dc = ki - qi + (NA_WC - 1)
        cs = jnp.clip(qi - NA_WC // 2, 0, GRID_W - NA_WC)
        colmask = (ki >= cs) & (ki < cs + NA_WC)

        def toeplitz(d):
            t = jnp.zeros((GRID_W, LANES), F32)
            for j in range(ndc):
                t = jnp.where(dc == j, rpb_ref[h * (ndr * ndc) + d * ndc + j], t)
            return t

        prev = jnp.zeros((GRID_W, LANES), F32)
        for i in range(ndr + 1):
            nxt = toeplitz(i) if i < ndr else jnp.zeros((GRID_W, LANES), F32)
            bias_scr[i] = jnp.where(colmask, jnp.where(lane < GRID_W, prev, nxt), NEG)
            prev = nxt

    kc = kc_ref[...]
    vc = vc_ref[...]
    kwin = NA_KROWS * GRID_W

    def group(g, _):
        r0 = g * NA_GROUP
        us = jnp.clip(r0 - NA_WR // 2, 0, nrow - NA_KROWS)
        qrows = pl.ds(pl.multiple_of(r0 * GRID_W, NA_GROUP * GRID_W), NA_GROUP * GRID_W)
        krows = pl.ds(pl.multiple_of(us * GRID_W, GRID_W), kwin)
        q = q_ref[qrows, :]
        s = lax.dot_general(q, k_ref[krows, :], nt, preferred_element_type=F32)
        row_blocks = []
        for u in range(NA_GROUP):
            r = r0 + u
            rs = jnp.clip(r - NA_WR // 2, 0, nrow - NA_WR)
            parts = []
            for m in range(kwin // LANES):
                kr = us + 2 * m
                add_lo = jnp.where((kr >= rs) & (kr < rs + NA_WR), 0.0, NEG)
                add_hi = jnp.where((kr + 1 >= rs) & (kr + 1 < rs + NA_WR), 0.0, NEG)
                tab = bias_scr[jnp.clip(kr - r + NA_WR, 0, ndr)] + jnp.where(lane < GRID_W, add_lo, add_hi)
                st = s[u * GRID_W:(u + 1) * GRID_W, m * LANES:(m + 1) * LANES]
                parts.append(jnp.where(tab > 0.5 * NEG, st * scale + tab, NEG))
            row_blocks.append(jnp.concatenate(parts, axis=1))
        s = jnp.concatenate(row_blocks, axis=0)
        sc = lax.dot_general(q, kc, nt, preferred_element_type=F32) * scale
        mx = jnp.maximum(jnp.max(s, axis=-1, keepdims=True), jnp.max(sc, axis=-1, keepdims=True))
        e = jnp.exp(s - mx)
        ec = jnp.exp(sc - mx)
        den = jnp.sum(e, axis=-1, keepdims=True) + jnp.sum(ec, axis=-1, keepdims=True)
        o = jnp.dot(e.astype(BF16), v_ref[krows, :], preferred_element_type=F32)
        o = o + jnp.dot(ec.astype(BF16), vc, preferred_element_type=F32)
        o_ref[qrows, :] = (o / den).astype(o_ref.dtype)
        return 0

    lax.fori_loop(0, nrow // NA_GROUP, group, 0)


def _na_attention(z, rpb, *, nb, seq, cseq, crow0, out_rows):
    kcol, vcol, qcol = 1536 // LANES, 2560 // LANES, 5120 // LANES
    crb0 = crow0 // cseq
    nrow = seq // GRID_W
    assert nrow >= NA_KROWS and nrow % NA_GROUP == 0 and GRID_W * 2 == LANES
    assert NA_KROWS % 2 == 0 and NA_KROWS >= NA_WR + NA_GROUP - 1
    return pl.pallas_call(
        _na_kernel,
        grid=(NA_HEADS, nb),
        in_specs=[
            pl.BlockSpec(memory_space=pltpu.SMEM),
            pl.BlockSpec((seq, NA_DH), lambda h, b: (b, qcol + h)),
            pl.BlockSpec((seq, NA_DH), lambda h, b: (b, kcol + h)),
            pl.BlockSpec((seq, NA_DH), lambda h, b: (b, vcol + h)),
            pl.BlockSpec((cseq, NA_DH), lambda h, b: (crb0 + b, kcol + h)),
            pl.BlockSpec((cseq, NA_DH), lambda h, b: (crb0 + b, vcol + h)),
        ],
        out_specs=pl.BlockSpec((seq, NA_DH), lambda h, b: (b, h)),
        out_shape=jax.ShapeDtypeStruct((out_rows, NA_HEADS * NA_DH), BF16),
        scratch_shapes=[pltpu.VMEM((2 * NA_WR, GRID_W, LANES), F32)],
        compiler_params=_cparams("arbitrary", "arbitrary"),
        name="na_attention",
    )(rpb.reshape(-1), z, z, z, z, z)


def _dense_attn_kernel(q_ref, k_ref, v_ref, yprev_ref, o_ref):
    s = lax.dot_general(q_ref[...], k_ref[...], (((1,), (1,)), ((), ())), preferred_element_type=F32) * NA_DH ** -0.5
    e = jnp.exp(s - jnp.max(s, axis=-1, keepdims=True))
    o = jnp.dot(e.astype(BF16), v_ref[...], preferred_element_type=F32)
    o_ref[...] = (o / jnp.sum(e, axis=-1, keepdims=True)).astype(o_ref.dtype)


def _dense_attention(z, y_prev, *, nb, cseq, crow0):
    kcol, vcol, qcol = 1536 // LANES, 2560 // LANES, 5120 // LANES
    crb0 = crow0 // cseq
    return pl.pallas_call(
        _dense_attn_kernel,
        grid=(nb, NA_HEADS),
        in_specs=[
            pl.BlockSpec((cseq, NA_DH), lambda b, h: (crb0 + b, qcol + h)),
            pl.BlockSpec((cseq, NA_DH), lambda b, h: (crb0 + b, kcol + h)),
            pl.BlockSpec((cseq, NA_DH), lambda b, h: (crb0 + b, vcol + h)),
            pl.BlockSpec(memory_space=pl.ANY),
        ],
        out_specs=pl.BlockSpec((cseq, NA_DH), lambda b, h: (crb0 + b, h)),
        out_shape=jax.ShapeDtypeStruct(y_prev.shape, BF16),
        input_output_aliases={3: 0},
        compiler_params=_cparams("parallel", "arbitrary"),
        name="dense_attention",
    )(z, z, z, y_prev)


HY_FEAT_ROWS = 64
HIGHEST = lax.Precision.HIGHEST


def _hy_features(seq):
    t = np.linspace(0.0, 1.0, seq)
    bands = (HY_EMB - 1) // 2
    w = 2.0 * math.pi * np.arange(seq) / seq
    fr = np.linspace(1e-4, bands - 1, bands)
    ang = fr[None] * w[:, None]
    feat = np.concatenate([t[:, None], np.cos(ang), -np.sin(ang)], axis=-1)
    src = np.concatenate([np.arange(seq), np.zeros(1, np.int64), np.arange(seq - 1, 0, -1)])
    feat2 = np.zeros((HY_FEAT_ROWS, 2 * seq), np.float32)
    feat2[:HY_EMB] = feat[src].T
    tt = t[src][None].astype(np.float32)
    mask = np.ones((1, 2 * seq), np.float32)
    mask[0, seq] = 0.0
    return feat2, tt, mask


def _hy_mlp_kernel(feat_ref, w1_ref, b1_ref, w2_ref, b2_ref, fq_ref, o_ref):
    fq = fq_ref[...]
    h = jnp.dot(w1_ref[...], feat_ref[...], preferred_element_type=F32, precision=HIGHEST)
    h = jnp.sin(fq * (h + b1_ref[...]))
    h = jnp.dot(w2_ref[...], h, preferred_element_type=F32, precision=HIGHEST)
    o_ref[...] = jnp.sin(fq * (h + b2_ref[...]))


def _hy_mlp(feat2, w1t, b1, w2t, b2, fq):
    depth, hid, _ = w1t.shape
    n = feat2.shape[1]
    col = lambda a: a.reshape(depth, hid, 1)
    wspec = lambda k: pl.BlockSpec((None, hid, k), lambda l: (l, 0, 0))
    return pl.pallas_call(
        _hy_mlp_kernel,
        grid=(depth,),
        in_specs=[pl.BlockSpec((HY_FEAT_ROWS, n), lambda l: (0, 0)), wspec(HY_FEAT_ROWS), wspec(1), wspec(hid),
                  wspec(1), wspec(1)],
        out_specs=pl.BlockSpec((None, hid, n), lambda l: (l, 0, 0)),
        out_shape=jax.ShapeDtypeStruct((depth, hid, n), F32),
        compiler_params=_cparams("parallel"),
        name="hyena_filter_mlp",
    )(feat2, w1t, col(b1), w2t, col(b2), col(fq))


def _hy_filter_kernel(h_ref, w3_ref, dl_ref, tt_ref, mask_ref, o_ref):
    seq = h_ref.shape[1] // 2
    kf = jnp.dot(w3_ref[0], h_ref[:, :seq], preferred_element_type=F32, precision=HIGHEST)
    kb = jnp.dot(w3_ref[1], h_ref[:, seq:], preferred_element_type=F32, precision=HIGHEST)
    k = jnp.concatenate([kf, kb], axis=1) * (jnp.exp(-tt_ref[...] * dl_ref[...]) * mask_ref[...])
    o_ref[...] = (k * lax.rsqrt(jnp.sum(k * k, axis=1, keepdims=True) + EPS)).astype(o_ref.dtype)


def _hy_filter(h2, w3t, deltas, tt, mask, cb):
    depth, hid, n = h2.shape
    c = w3t.shape[3]
    return pl.pallas_call(
        _hy_filter_kernel,
        grid=(depth, HY_ORDER, c // cb),
        in_specs=[
            pl.BlockSpec((None, hid, n), lambda l, o, j: (l, 0, 0)),
            pl.BlockSpec((None, None, 2, cb, hid), lambda l, o, j: (l, o, 0, j, 0)),
            pl.BlockSpec((cb, 1), lambda l, o, j: (j, 0)),
            pl.BlockSpec((1, n), lambda l, o, j: (0, 0)),
            pl.BlockSpec((1, n), lambda l, o, j: (0, 0)),
        ],
        out_specs=pl.BlockSpec((None, None, cb, n), lambda l, o, j: (l, o, j, 0)),
        out_shape=jax.ShapeDtypeStruct((depth, HY_ORDER, c, n), BF16),
        compiler_params=_cparams("parallel", "parallel", "arbitrary"),
        name="hyena_filter",
    )(h2, w3t, deltas, tt, mask)


FFT_NO = 64
FFT_NI = 128


def _fft_tables():
    n = FFT_NO * FFT_NI
    a = np.arange(FFT_NO)
    fo = np.exp(-2j * np.pi * np.outer(a, a) / FFT_NO)
    i = np.arange(FFT_NI)
    ci = np.exp(-2j * np.pi * np.outer(i, i) / FFT_NI)
    tw = np.exp(-2j * np.pi * np.outer(a, i) / n)
    half = FFT_NO // 2
    f32 = lambda x: np.ascontiguousarray(x, dtype=np.float32)
    g1_real = f32(np.concatenate([fo.real, fo.imag], axis=0))
    g1 = f32(np.block([[fo.real[:, :half], -fo.imag[:, :half]], [fo.imag[:, :half], fo.real[:, :half]]]))
    w2 = f32(np.block([[ci.real, ci.imag], [-ci.imag, ci.real]]))
    w2i = f32(np.block([[ci.real, -ci.imag], [ci.imag, ci.real]]))
    g4 = f32(np.block([[fo.real[:half], fo.imag[:half]], [-fo.imag[:half], fo.real[:half]]]) / n)
    return dict(g1_real=g1_real, g1=g1, w2=w2, w2i=w2i, g4=g4, twr=f32(tw.real), twi=f32(tw.imag))


HY_CHUNK = 8


def _fft_stage1(g1, y2, twr, twi):
    o1 = jnp.dot(g1, y2, preferred_element_type=F32)
    yr, yi = o1[:FFT_NO], o1[FFT_NO:]
    ar = (yr * twr - yi * twi).astype(BF16)
    ai = (yr * twi + yi * twr).astype(BF16)
    lanes = lambda ci: slice(ci * FFT_NI, (ci + 1) * FFT_NI)
    return jnp.concatenate([jnp.concatenate([ar[:, lanes(ci)], ai[:, lanes(ci)]], axis=1)
                            for ci in range(HY_CHUNK)], axis=0)


def _hy_spectrum_kernel(k_ref, g1_ref, w2_ref, twr_ref, twi_ref, o_ref):
    cb = k_ref.shape[0]
    g1 = g1_ref[...]
    w2 = w2_ref[...]
    twr = twr_ref[...]
    twi = twi_ref[...]

    def body(j, _):
        chans = pl.ds(pl.multiple_of(j * HY_CHUNK, HY_CHUNK), HY_CHUNK)
        ks = k_ref[chans]
        k2 = jnp.concatenate([ks[ci] for ci in range(HY_CHUNK)], axis=1).astype(BF16)
        z = jnp.dot(_fft_stage1(g1, k2, twr, twi), w2, preferred_element_type=F32)
        o_ref[chans] = z.reshape(HY_CHUNK, FFT_NO, 2 * FFT_NI)
        return 0

    lax.fori_loop(0, cb // HY_CHUNK, body, 0)


def _tiled_twiddles(tabs):
    return (jnp.asarray(np.tile(tabs["twr"], (1, HY_CHUNK))), jnp.asarray(np.tile(tabs["twi"], (1, HY_CHUNK))))


def _hy_spectrum(kfilt, tabs, cb):
    g, c = kfilt.shape[:2]
    const = lambda a: pl.BlockSpec(a.shape, lambda i, j: (0,) * a.ndim)
    g1 = jnp.asarray(tabs["g1_real"], BF16)
    w2 = jnp.asarray(tabs["w2"], BF16)
    twr, twi = _tiled_twiddles(tabs)
    return pl.pallas_call(
        _hy_spectrum_kernel,
        grid=(g, c // cb),
        in_specs=[pl.BlockSpec((None, cb, FFT_NO, FFT_NI), lambda i, j: (i, j, 0, 0)),
                  const(g1), const(w2), const(twr), const(twi)],
        out_specs=pl.BlockSpec((None, cb, FFT_NO, 2 * FFT_NI), lambda i, j: (i, j, 0, 0)),
        out_shape=jax.ShapeDtypeStruct((g, c, FFT_NO, 2 * FFT_NI), F32),
        compiler_params=_cparams("parallel", "arbitrary"),
        name="hyena_spectrum",
    )(kfilt, g1, w2, twr, twi)


def _shift_conv3(x, w0, w1, w2, b):
    nrow, nlane = x.shape
    row = lax.broadcasted_iota(jnp.int32, x.shape, 0)
    lane = lax.broadcasted_iota(jnp.int32, x.shape, 1)
    r = pltpu.roll(x, 1, axis=1)
    prev = jnp.where(lane == 0, pltpu.roll(r, 1, axis=0), r)
    prev = jnp.where((lane == 0) & (row == 0), 0.0, prev)
    r = pltpu.roll(x, nlane - 1, axis=1)
    nxt = jnp.where(lane == nlane - 1, pltpu.roll(r, nrow - 1, axis=0), r)
    nxt = jnp.where((lane == nlane - 1) & (row == nrow - 1), 0.0, nxt)
    return prev * w0 + x * w1 + nxt * w2 + b


def _hyena_kernel(x_ref, kf_ref, pm_ref, g1_ref, w2_ref, w2i_ref, g4_ref, twr_ref, twi_ref, o_ref):
    cb = x_ref.shape[2]
    half = FFT_NO // 2
    ch = HY_CHUNK
    g1 = g1_ref[...]
    g4 = g4_ref[...]
    w2 = w2_ref[...]
    w2i = w2i_ref[...]
    twr = twr_ref[...]
    twi = twi_ref[...]
    lanes = lambda ci: slice(ci * FFT_NI, (ci + 1) * FFT_NI)
    krows = lambda ci: slice(ci * FFT_NO, (ci + 1) * FFT_NO)

    def body(j, _):
        c0 = pl.multiple_of(j * ch, ch)
        pm = pm_ref[:, pl.ds(c0, ch)]
        conv = []
        for part in range(3):
            halves = []
            for bb in range(2):
                xs = x_ref[bb, part, pl.ds(c0, ch)].astype(F32)
                halves.append(jnp.concatenate(
                    [_shift_conv3(xs[ci], pm[part, ci], pm[3 + part, ci], pm[6 + part, ci], pm[9 + part, ci])
                     for ci in range(ch)], axis=1))
            conv.append(jnp.concatenate(halves, axis=0))
        y = conv[2]
        for order in range(HY_ORDER):
            z = jnp.dot(_fft_stage1(g1, y.astype(BF16), twr, twi), w2, preferred_element_type=F32)
            kf = kf_ref[order, pl.ds(c0, ch)].reshape(ch * FFT_NO, 2 * FFT_NI)
            zr, zi = z[:, :FFT_NI], z[:, FFT_NI:]
            kr, ki = kf[:, :FFT_NI], kf[:, FFT_NI:]
            p = jnp.concatenate([zr * kr - zi * ki, zr * ki + zi * kr], axis=1).astype(BF16)
            q = jnp.dot(p, w2i, preferred_element_type=F32)
            qr = jnp.concatenate([q[krows(ci), :FFT_NI] for ci in range(ch)], axis=1)
            qi = jnp.concatenate([q[krows(ci), FFT_NI:] for ci in range(ch)], axis=1)
            qs = jnp.concatenate([qr * twr + qi * twi, qi * twr - qr * twi], axis=0).astype(BF16)
            cv = jnp.dot(g4, qs, preferred_element_type=F32)
            bias = jnp.concatenate([pm[12 + order, ci] for ci in range(ch)], axis=1)
            y = conv[order] * (cv + bias * y)
        for ci in range(ch):
            o_ref[0, c0 + ci] = y[:half, lanes(ci)].astype(o_ref.dtype)
            o_ref[1, c0 + ci] = y[half:, lanes(ci)].astype(o_ref.dtype)
        return 0

    lax.fori_loop(0, cb // ch, body, 0)


def _hyena(xt, kf, pm, tabs, cb):
    nb, _, c, half, _ = xt.shape
    const = lambda a: pl.BlockSpec(a.shape, lambda j, p: (0,) * a.ndim)
    bf = lambda name: jnp.asarray(tabs[name], BF16)
    g1, w2, w2i, g4 = bf("g1"), bf("w2"), bf("w2i"), bf("g4")
    twr, twi = _tiled_twiddles(tabs)
    return pl.pallas_call(
        _hyena_kernel,
        grid=(c // cb, nb // 2),
        in_specs=[
            pl.BlockSpec((2, 3, cb, half, FFT_NI), lambda j, p: (p, 0, j, 0, 0)),
            pl.BlockSpec((HY_ORDER, cb, FFT_NO, 2 * FFT_NI), lambda j, p: (0, j, 0, 0)),
            pl.BlockSpec((14, cb, 1, FFT_NI), lambda j, p: (0, j, 0, 0)),
            const(g1), const(w2), const(w2i), const(g4), const(twr), const(twi),
        ],
        out_specs=pl.BlockSpec((2, cb, half, FFT_NI), lambda j, p: (p, j, 0, 0)),
        out_shape=jax.ShapeDtypeStruct((nb, c, half, FFT_NI), BF16),
        compiler_params=_cparams("parallel", "arbitrary"),
        name="hyena",
    )(xt, kf, pm, g1, w2, w2i, g4, twr, twi)


def _dft_tables(seq):
    n = 2 * seq
    wmat = np.exp(-2j * np.pi * np.outer(np.arange(n), np.arange(n)) / n)
    f32 = lambda x: np.ascontiguousarray(x, dtype=np.float32)
    fwd_real = f32(np.concatenate([wmat.real, wmat.imag], axis=1))
    ws = wmat[:seq]
    fwd = f32(np.block([[ws.real, ws.imag], [-ws.imag, ws.real]]))
    wi = np.conj(wmat)[:, :seq] / n
    inv = f32(np.block([[wi.real, wi.imag], [-wi.imag, wi.real]]))
    return dict(fwd_real=fwd_real, fwd=fwd, inv=inv)


def _rowdft_kernel(x_ref, w_ref, o_ref):
    o_ref[...] = jnp.dot(x_ref[...].astype(BF16), w_ref[...], preferred_element_type=F32)


def _rowdft(x, w, tr):
    m, k = x.shape
    n = w.shape[1]
    return pl.pallas_call(
        _rowdft_kernel,
        grid=(m // tr,),
        in_specs=[pl.BlockSpec((tr, k), lambda i: (i, 0)), pl.BlockSpec((k, n), lambda i: (0, 0))],
        out_specs=pl.BlockSpec((tr, n), lambda i: (i, 0)),
        out_shape=jax.ShapeDtypeStruct((m, n), F32),
        compiler_params=_cparams("parallel"),
        name="row_dft",
    )(x, w)


def _lane_conv3(x, w0, w1, w2, b):
    seq = x.shape[1]
    lane = lax.broadcasted_iota(jnp.int32, x.shape, 1)
    prev = jnp.where(lane == 0, 0.0, pltpu.roll(x, 1, axis=1))
    nxt = jnp.where(lane == seq - 1, 0.0, pltpu.roll(x, seq - 1, axis=1))
    return prev * w0 + x * w1 + nxt * w2 + b


def _hyena_ctx_kernel(x_ref, kf_ref, pm_ref, fwd_ref, inv_ref, o_ref):
    seq = x_ref.shape[3]
    n = 2 * seq
    conv = [[_lane_conv3(x_ref[bb, part].astype(F32), pm_ref[part], pm_ref[3 + part], pm_ref[6 + part],
                         pm_ref[9 + part]) for bb in range(2)] for part in range(3)]
    ya, yb = conv[2]
    for order in range(HY_ORDER):
        z = jnp.dot(jnp.concatenate([ya, yb], axis=1).astype(BF16), fwd_ref[...], preferred_element_type=F32)
        kf = kf_ref[order]
        zr, zi, kr, ki = z[:, :n], z[:, n:], kf[:, :n], kf[:, n:]
        p = jnp.concatenate([zr * kr - zi * ki, zr * ki + zi * kr], axis=1).astype(BF16)
        cv = jnp.dot(p, inv_ref[...], preferred_element_type=F32)
        bias = pm_ref[12 + order]
        ya = conv[order][0] * (cv[:, :seq] + bias * ya)
        yb = conv[order][1] * (cv[:, seq:] + bias * yb)
    o_ref[0] = ya.astype(o_ref.dtype)
    o_ref[1] = yb.astype(o_ref.dtype)


def _hyena_ctx(xt, kf, pm, tabs, cb):
    nb, _, c, seq = xt.shape
    fwd, inv = jnp.asarray(tabs["fwd"], BF16), jnp.asarray(tabs["inv"], BF16)
    const = lambda a: pl.BlockSpec(a.shape, lambda j, p: (0,) * a.ndim)
    return pl.pallas_call(
        _hyena_ctx_kernel,
        grid=(c // cb, nb // 2),
        in_specs=[
            pl.BlockSpec((2, 3, cb, seq), lambda j, p: (p, 0, j, 0)),
            pl.BlockSpec((HY_ORDER, cb, 4 * seq), lambda j, p: (0, j, 0)),
            pl.BlockSpec((14, cb, 1), lambda j, p: (0, j, 0)),
            const(fwd), const(inv),
        ],
        out_specs=pl.BlockSpec((2, cb, seq), lambda j, p: (p, j, 0)),
        out_shape=jax.ShapeDtypeStruct((nb, c, seq), BF16),
        compiler_params=_cparams("parallel", "arbitrary"),
        name="hyena_ctx",
    )(xt, kf, pm, fwd, inv)


def _rope_tables(seq):
    t = np.arange(seq)
    pos = np.stack([t // GRID_W, t % GRID_W], axis=1).astype(np.float64)
    n = RET_DK // 4
    inv = ROPE_BASE ** (-np.arange(n, dtype=np.float64) / n)
    lane = np.arange(LANES) % RET_DK
    ang = pos[:, lane // (2 * n)] * inv[lane % n][None]
    sign = np.where(lane % (2 * n) < n, -1.0, 1.0)
    return np.cos(ang).astype(np.float32), (np.sin(ang) * sign[None]).astype(np.float32)


def _hy_filters(seq, hy_f_w1, hy_f_b1, hy_f_w2, hy_f_b2, hy_f_w3, hy_f_freq):
    depth, _, hid = hy_f_w1.shape
    c = hy_f_w3.shape[2] // (2 * HY_ORDER)
    feat2, tt, mask = _hy_features(seq)
    w1t = jnp.pad(hy_f_w1.transpose(0, 2, 1), ((0, 0), (0, 0), (0, HY_FEAT_ROWS - HY_EMB)))
    h2 = _hy_mlp(jnp.asarray(feat2), w1t, hy_f_b1, hy_f_w2.transpose(0, 2, 1), hy_f_b2, hy_f_freq)
    w3t = hy_f_w3.reshape(depth, hid, HY_ORDER, 2, c).transpose(0, 2, 3, 4, 1)
    deltas = np.abs(np.linspace(math.log(HY_TARGET) / HY_FAST, math.log(HY_TARGET) / HY_SLOW, c))
    deltas = jnp.asarray(deltas.astype(np.float32).reshape(c, 1))
    return _hy_filter(h2, w3t, deltas, jnp.asarray(tt), jnp.asarray(mask), min(c, 8 * 1024 * 128 // (2 * seq)))


def _hy_params(hy_conv_w, hy_conv_b, hy_bias):
    c = hy_bias.shape[1]
    return jnp.concatenate([hy_conv_w.reshape(9, c), hy_conv_b.reshape(3, c), hy_bias], axis=0)


def kernel(x, c, ctx, c_ctx, w_mod, b_mod, g_norm1, g_norm2, w_in, hy_conv_w, hy_conv_b, hy_f_w1, hy_f_b1, hy_f_w2, hy_f_b2, hy_f_w3, hy_f_freq, hy_bias, ret_log_decay, na_rpb, w_branch, w_out, ffn_w_in, ffn_conv_w, ffn_conv_b, ffn_w_out, g_final):
    nb, seq, d = x.shape
    cseq = ctx.shape[1]
    depth = w_mod.shape[0]
    hy_w = hy_bias.shape[2]
    t_lat, t_ctx = nb * seq, nb * cseq
    assert seq % TM == 0 and t_ctx % TM == 0 and TM % cseq == 0 and nb % 2 == 0 and nb < SUBLANES
    assert 2 * seq == FFT_NO * FFT_NI and seq % GRID_W == 0 and cseq & (cseq - 1) == 0
    blk_lat, blk_all = t_lat // TM, (t_lat + t_ctx) // TM
    mrow = jnp.asarray(np.concatenate([np.repeat(np.arange(nb), seq // TM), np.full(t_ctx // TM, nb)]), jnp.int32)
    seqlen = jnp.asarray(np.concatenate([np.full(blk_lat, seq), np.full(t_ctx // TM, cseq)]), jnp.int32)
    per = TM // TR
    mrow_r = jnp.repeat(mrow, per)
    tn = 1024

    cond = jnp.zeros((SUBLANES, d), F32).at[:nb].set(c).at[nb].set(c_ctx)
    mods = _mods(cond, w_mod, b_mod).reshape(depth, SUBLANES, 6, 1, d)

    cos, sin = (jnp.asarray(a) for a in _rope_tables(seq))
    tabs = _fft_tables()
    ctabs = _dft_tables(cseq)
    filt = _hy_filters(seq, hy_f_w1, hy_f_b1, hy_f_w2, hy_f_b2, hy_f_w3, hy_f_freq)
    spec = _hy_spectrum(filt.reshape(depth * HY_ORDER, hy_w, FFT_NO, FFT_NI), tabs, 32)
    spec = spec.reshape(depth, HY_ORDER, hy_w, FFT_NO, 2 * FFT_NI)
    cfilt = _hy_filters(cseq, hy_f_w1, hy_f_b1, hy_f_w2, hy_f_b2, hy_f_w3, hy_f_freq)
    cspec = _rowdft(cfilt.reshape(depth * HY_ORDER * hy_w, 2 * cseq), jnp.asarray(ctabs["fwd_real"], BF16), 1024)
    cspec = cspec.reshape(depth, HY_ORDER, hy_w, 4 * cseq)

    xs = jnp.concatenate([x.reshape(t_lat, d), ctx.reshape(t_ctx, d)], axis=0)
    hy0 = 6144
    gate0 = hy0 + 3 * hy_w
    s_zero = jnp.zeros((nb, RET_HEADS, 2, LANES, RET_DV), F32)
    hn = _normmod(xs, g_norm1[0], mods[0], mrow, 0, 1, blk_all)
    for l in range(depth):
        last = l == depth - 1
        nblk = blk_lat if last else blk_all
        rows = nblk * TM
        z = _matmul(hn, w_in, l, blk_all, tn, w_in.shape[2] - 3 * hy_w, skip=(hy0 // tn, 3 * hy_w // tn))
        hy_proj = functools.partial(_matmul_nt, hn, w_in, l, col0=hy0, n_out=3 * hy_w, nb=nb, tc=512)
        zh = hy_proj(blk0=0, nblk=blk_lat, seq=seq)

        y_ret, s_ctx = _retention(z, ret_log_decay[l], cos, sin, s_zero, nb=nb, seq=cseq, row0=t_lat, rope=False,
                                  out_rows=blk_all * TM)
        y_ret, _ = _retention(z, ret_log_decay[l], cos, sin, s_ctx, nb=nb, seq=seq, row0=0, rope=True,
                              out_rows=blk_all * TM, y_prev=y_ret)
        y_na = _na_attention(z, na_rpb[l], nb=nb, seq=seq, cseq=cseq, crow0=t_lat, out_rows=rows)
        pm = _hy_params(hy_conv_w[l], hy_conv_b[l], hy_bias[l])
        pm_lat = jnp.broadcast_to(pm[:, :, None, None], (14, hy_w, 1, FFT_NI))
        y_hy = _hyena(zh.reshape(nb, 3, hy_w, FFT_NO // 2, FFT_NI), spec[l], pm_lat, tabs, 32)
        y_hy = y_hy.reshape(nb, hy_w, seq).transpose(0, 2, 1).reshape(t_lat, hy_w)
        if not last:
            y_na = _dense_attention(z, y_na, nb=nb, cseq=cseq, crow0=t_lat)
            zc = hy_proj(blk0=blk_lat, nblk=t_ctx // TM, seq=cseq)
            yc_hy = _hyena_ctx(zc.reshape(nb, 3, hy_w, cseq), cspec[l], pm[:, :, None], ctabs, 256)
            y_hy = jnp.concatenate([y_hy, yc_hy.transpose(0, 2, 1).reshape(t_ctx, hy_w)], axis=0)
        acc = _merge((y_hy, y_ret, y_na), z, hy0, w_branch[l].astype(BF16), nblk, 1024)
        xs, hn = _out_proj(acc, w_out[l].astype(BF16), xs, mods[l], mrow_r, g_norm2[l], nblk * per)
        u = _matmul(hn, ffn_w_in, l, nblk, tn, ffn_w_in.shape[2])
        hg = _ffn_gate(u, ffn_conv_w[l], ffn_conv_b[l], seqlen, nblk, 512)
        nxt = l if last else l + 1
        w_ffo = ffn_w_out[l].astype(BF16)
        if last:
            res = _ffn_out(hg, w_ffo, xs, mods[l], mods[l], g_final, mrow_r, nblk * per, TR, 512, True)
        else:
            res = _ffn_out(hg, w_ffo, xs, mods[l], mods[nxt], g_norm1[nxt], mrow, nblk, TM, 256, False)
        if not last:
            xs, hn = res
    return res[0].reshape(nb, seq, d)
```

```python
import functools
import math

import numpy as np
import jax
import jax.numpy as jnp
from jax import lax
from jax.experimental import pallas as pl
from jax.experimental.pallas import tpu as pltpu

F32 = jnp.float32
BF16 = jnp.bfloat16

GRID_W = 64
N_BRANCH = 3
HY_ORDER = 2
HY_EMB = 33
HY_FAST = 0.3
HY_SLOW = 1.5
HY_TARGET = 1e-2
RET_HEADS = 8
RET_DK = 64
RET_DV = 128
RET_CHUNK = 128
NA_HEADS = 8
NA_DH = 128
NA_WR = 8
NA_WC = 16
ROPE_BASE = 10000.0
EPS = 1e-6
NEG = -1e30

LANES = 128
SUBLANES = 8
VMEM_LIMIT = 56 * 1024 * 1024

TM = 1024


def _cparams(*sem):
    return pltpu.CompilerParams(dimension_semantics=sem, vmem_limit_bytes=VMEM_LIMIT)


def _mods_kernel(a_ref, w_ref, b_ref, o_ref):
    a = a_ref[...]
    a = a * jax.nn.sigmoid(a)
    o_ref[...] = jnp.dot(a.astype(BF16), w_ref[...].astype(BF16), preferred_element_type=F32) + b_ref[...]


def _mods(cond, w_mod, b_mod):
    depth, d, n = w_mod.shape
    tn = 1024
    return pl.pallas_call(
        _mods_kernel,
        grid=(depth, n // tn),
        in_specs=[
            pl.BlockSpec((SUBLANES, d), lambda l, j: (0, 0)),
            pl.BlockSpec((None, d, tn), lambda l, j: (l, 0, j)),
            pl.BlockSpec((None, 1, tn), lambda l, j: (l, 0, j)),
        ],
        out_specs=pl.BlockSpec((None, SUBLANES, tn), lambda l, j: (l, 0, j)),
        out_shape=jax.ShapeDtypeStruct((depth, SUBLANES, n), F32),
        compiler_params=_cparams("parallel", "parallel"),
        name="mods",
    )(cond, w_mod, b_mod.reshape(depth, 1, n))


def _normmod_kernel(mrow_ref, x_ref, g_ref, sh_ref, sc_ref, o_ref):
    x = x_ref[...]
    y = x * lax.rsqrt(jnp.mean(x * x, axis=-1, keepdims=True) + EPS)
    y = y * g_ref[...]
    o_ref[...] = (y * (1.0 + sc_ref[...]) + sh_ref[...]).astype(o_ref.dtype)


def _normmod(x, g, mods, mrow, shift_idx, scale_idx, nblk):
    m, d = x.shape
    grid_spec = pltpu.PrefetchScalarGridSpec(
        num_scalar_prefetch=1,
        grid=(nblk,),
        in_specs=[
            pl.BlockSpec((TM, d), lambda i, mr: (i, 0)),
            pl.BlockSpec((1, d), lambda i, mr: (0, 0)),
            pl.BlockSpec((None, None, 1, d), lambda i, mr: (mr[i], shift_idx, 0, 0)),
            pl.BlockSpec((None, None, 1, d), lambda i, mr: (mr[i], scale_idx, 0, 0)),
        ],
        out_specs=pl.BlockSpec((TM, d), lambda i, mr: (i, 0)),
    )
    return pl.pallas_call(
        _normmod_kernel,
        grid_spec=grid_spec,
        out_shape=jax.ShapeDtypeStruct((m, d), BF16),
        compiler_params=_cparams("parallel"),
        name="normmod",
    )(mrow, x, g.reshape(1, d), mods, mods)


def _mm_kernel(a_ref, w_ref, o_ref, w_scr):
    @pl.when(pl.program_id(1) == 0)
    def _():
        w_scr[...] = w_ref[...].astype(BF16)

    o_ref[...] = jnp.dot(a_ref[...], w_scr[...], preferred_element_type=F32).astype(o_ref.dtype)


def _matmul(a, w, layer, nblk, tn, n_out, skip=None):
    m, k = a.shape
    col = (lambda j: j) if skip is None else (lambda j: j + jnp.where(j >= skip[0], skip[1], 0))
    return pl.pallas_call(
        _mm_kernel,
        grid=(n_out // tn, nblk),
        in_specs=[pl.BlockSpec((TM, k), lambda j, i: (i, 0)),
                  pl.BlockSpec((None, k, tn), lambda j, i: (layer, 0, col(j)))],
        out_specs=pl.BlockSpec((TM, tn), lambda j, i: (i, j)),
        out_shape=jax.ShapeDtypeStruct((m, n_out), BF16),
        scratch_shapes=[pltpu.VMEM((k, tn), BF16)],
        compiler_params=_cparams("arbitrary", "arbitrary"),
        name="matmul",
    )(a, w)


def _mm_nt_kernel(w_ref, a_ref, o_ref, wt_scr):
    @pl.when(pl.program_id(1) == 0)
    def _():
        wt_scr[...] = w_ref[...].T.astype(BF16)

    r = lax.dot_general(wt_scr[...], a_ref[...], (((1,), (1,)), ((), ())), preferred_element_type=F32)
    per = o_ref.shape[0]
    width = r.shape[1] // per
    for s in range(per):
        o_ref[s] = r[:, s * width:(s + 1) * width].astype(o_ref.dtype)


def _matmul_nt(a, w, layer, *, col0, n_out, blk0, nblk, nb, seq, tc):
    k = a.shape[1]
    per, sblk = max(TM // seq, 1), max(seq // TM, 1)
    return pl.pallas_call(
        _mm_nt_kernel,
        grid=(n_out // tc, nblk),
        in_specs=[pl.BlockSpec((None, k, tc), lambda j, i: (layer, 0, col0 // tc + j)),
                  pl.BlockSpec((TM, k), lambda j, i: (blk0 + i, 0))],
        out_specs=pl.BlockSpec((per, tc, TM // per), lambda j, i: (i // sblk, j, i % sblk)),
        out_shape=jax.ShapeDtypeStruct((nb, n_out, seq), BF16),
        scratch_shapes=[pltpu.VMEM((tc, k), BF16)],
        compiler_params=_cparams("arbitrary", "arbitrary"),
        name="matmul_nt",
    )(w, a)


TR = 512


def _norm_modulate(x, g, shift, scale):
    y = x * lax.rsqrt(jnp.mean(x * x, axis=-1, keepdims=True) + EPS)
    return (y * g) * (1.0 + scale) + shift


def _outproj_kernel(mrow_ref, a_ref, w_ref, x_ref, gate_ref, g_ref, sh_ref, sc_ref, x_out, h_out):
    y = jnp.dot(a_ref[...], w_ref[...], preferred_element_type=F32)
    x = x_ref[...] + gate_ref[...] * y
    x_out[...] = x
    h_out[...] = _norm_modulate(x, g_ref[...], sh_ref[...], sc_ref[...]).astype(h_out.dtype)


def _out_proj(a, w, x, mods, mrow, g_norm, nblk):
    m, k = a.shape
    d = w.shape[1]
    mod = lambda idx: pl.BlockSpec((None, None, 1, d), lambda i, mr: (mr[i], idx, 0, 0))
    row = pl.BlockSpec((TR, d), lambda i, mr: (i, 0))
    grid_spec = pltpu.PrefetchScalarGridSpec(
        num_scalar_prefetch=1,
        grid=(nblk,),
        in_specs=[pl.BlockSpec((TR, k), lambda i, mr: (i, 0)), pl.BlockSpec((k, d), lambda i, mr: (0, 0)), row,
                  mod(2), pl.BlockSpec((1, d), lambda i, mr: (0, 0)), mod(3), mod(4)],
        out_specs=[row, row],
    )
    return pl.pallas_call(
        _outproj_kernel,
        grid_spec=grid_spec,
        out_shape=[jax.ShapeDtypeStruct((m, d), F32), jax.ShapeDtypeStruct((m, d), BF16)],
        compiler_params=_cparams("parallel"),
        name="out_proj",
    )(mrow, a, w, x, mods, g_norm.reshape(1, d), mods, mods)


def _merge_kernel(y0_ref, y1_ref, y2_ref, g0_ref, g1_ref, g2_ref, w_ref, o_ref):
    acc = None
    for i, (y_ref, g_ref) in enumerate(((y0_ref, g0_ref), (y1_ref, g1_ref), (y2_ref, g2_ref))):
        t = jnp.dot(y_ref[...], w_ref[i], preferred_element_type=F32)
        t = jax.nn.sigmoid(g_ref[...].astype(F32)) * t
        acc = t if acc is None else acc + t
    o_ref[...] = acc.astype(o_ref.dtype)


def _merge(ys, z, gate_col0, w_branch, nblk, tn):
    m, bw = ys[0].shape
    d = w_branch.shape[2]
    gspec = lambda i_br: pl.BlockSpec((TM, tn), lambda i, j: (i, (gate_col0 + i_br * d) // tn + j))
    yspec = pl.BlockSpec((TM, bw), lambda i, j: (i, 0))
    return pl.pallas_call(
        _merge_kernel,
        grid=(nblk, d // tn),
        in_specs=[yspec, yspec, yspec, gspec(0), gspec(1), gspec(2),
                  pl.BlockSpec((N_BRANCH, bw, tn), lambda i, j: (0, 0, j))],
        out_specs=pl.BlockSpec((TM, tn), lambda i, j: (i, j)),
        out_shape=jax.ShapeDtypeStruct((m, d), BF16),
        compiler_params=_cparams("parallel", "arbitrary"),
        name="merge",
    )(ys[0], ys[1], ys[2], z, z, z, w_branch)


def _ffn_gate_kernel(seq_ref, a_ref, ap_ref, an_ref, b_ref, cw_ref, cb_ref, o_ref):
    i = pl.program_id(0)
    seq_m1 = seq_ref[i] - 1
    a = a_ref[...].astype(F32)
    tm = a.shape[0]
    row = lax.broadcasted_iota(jnp.int32, (tm, 1), 0)
    pos = (i * tm + row) & seq_m1
    prev = jnp.where(row == 0, ap_ref[SUBLANES - 1:SUBLANES, :].astype(F32), pltpu.roll(a, 1, axis=0))
    prev = jnp.where(pos == 0, 0.0, prev)
    nxt = jnp.where(row == tm - 1, an_ref[0:1, :].astype(F32), pltpu.roll(a, tm - 1, axis=0))
    nxt = jnp.where(pos == seq_m1, 0.0, nxt)
    cw = cw_ref[...]
    conv = prev * cw[0:1, :] + a * cw[1:2, :] + nxt * cw[2:3, :] + cb_ref[...]
    o_ref[...] = jax.nn.gelu(conv.astype(BF16)) * b_ref[...]


def _ffn_gate(u, conv_w, conv_b, seqlen, nblk, tc):
    m, ff2 = u.shape
    ff = ff2 // 2
    nrow8 = m // SUBLANES
    r8 = TM // SUBLANES
    grid_spec = pltpu.PrefetchScalarGridSpec(
        num_scalar_prefetch=1,
        grid=(nblk, ff // tc),
        in_specs=[
            pl.BlockSpec((TM, tc), lambda i, j, s: (i, j)),
            pl.BlockSpec((SUBLANES, tc), lambda i, j, s: (jnp.maximum(i * r8 - 1, 0), j)),
            pl.BlockSpec((SUBLANES, tc), lambda i, j, s: (jnp.minimum((i + 1) * r8, nrow8 - 1), j)),
            pl.BlockSpec((TM, tc), lambda i, j, s: (i, ff // tc + j)),
            pl.BlockSpec((3, tc), lambda i, j, s: (0, j)),
            pl.BlockSpec((1, tc), lambda i, j, s: (0, j)),
        ],
        out_specs=pl.BlockSpec((TM, tc), lambda i, j, s: (i, j)),
    )
    return pl.pallas_call(
        _ffn_gate_kernel,
        grid_spec=grid_spec,
        out_shape=jax.ShapeDtypeStruct((m, ff), BF16),
        compiler_params=_cparams("parallel", "arbitrary"),
        name="ffn_gate",
    )(seqlen, u, u, u, u, conv_w, conv_b.reshape(1, ff))


def _ffn_out_kernel(mrow_ref, h_ref, w_ref, x_ref, gate_ref, g_ref, sh_ref, sc_ref, *rest, final):
    outs, xrow = rest[:-1], rest[-1]
    j = pl.program_id(1)
    y = jnp.dot(h_ref[...], w_ref[...], preferred_element_type=F32)
    xt = x_ref[...] + gate_ref[...] * y
    xrow[j] = xt
    if not final:
        outs[0][...] = xt

    @pl.when(j == pl.num_programs(1) - 1)
    def _():
        x = jnp.concatenate([xrow[t] for t in range(xrow.shape[0])], axis=1)
        if final:
            y = x * lax.rsqrt(jnp.mean(x * x, axis=-1, keepdims=True) + EPS)
            outs[0][...] = y * g_ref[...]
        else:
            outs[1][...] = _norm_modulate(x, g_ref[...], sh_ref[...], sc_ref[...]).astype(outs[1].dtype)


def _ffn_out(h, w, x, mods, mods_next, g_next, mrow, nblk, tr, tn, final):
    m, ff = h.shape
    d = w.shape[1]
    mod = lambda arr_idx: pl.BlockSpec((None, None, 1, d), lambda i, j, mr: (mr[i], arr_idx, 0, 0))
    row = pl.BlockSpec((tr, d), lambda i, j, mr: (i, 0))
    tile = pl.BlockSpec((tr, tn), lambda i, j, mr: (i, j))
    grid_spec = pltpu.PrefetchScalarGridSpec(
        num_scalar_prefetch=1,
        grid=(nblk, d // tn),
        in_specs=[
            pl.BlockSpec((tr, ff), lambda i, j, mr: (i, 0)),
            pl.BlockSpec((ff, tn), lambda i, j, mr: (0, j)),
            tile,
            pl.BlockSpec((None, None, 1, tn), lambda i, j, mr: (mr[i], 5, 0, j)),
            pl.BlockSpec((1, d), lambda i, j, mr: (0, 0)), mod(0), mod(1),
        ],
        out_specs=[row] if final else [tile, row],
        scratch_shapes=[pltpu.VMEM((d // tn, tr, tn), F32)],
    )
    out_shape = [jax.ShapeDtypeStruct((m, d), F32)] + ([] if final else [jax.ShapeDtypeStruct((m, d), BF16)])
    return pl.pallas_call(
        functools.partial(_ffn_out_kernel, final=final),
        grid_spec=grid_spec,
        out_shape=out_shape,
        compiler_params=_cparams("parallel", "arbitrary"),
        name="ffn_out",
    )(mrow, h, w, x, mods, g_next.reshape(1, d), mods_next, mods_next)


def _rope_pair(x, cos, sin):
    lane = lax.broadcasted_iota(jnp.int32, x.shape, 1)
    first = lax.rem(lane, 32) < 16
    partner = jnp.where(first, pltpu.roll(x, LANES - 16, axis=1), pltpu.roll(x, 16, axis=1))
    return x * cos + partner * sin


def _ret_kernel(lg_ref, q_ref, k_ref, v_ref, g_ref, cos_ref, sin_ref, s0_ref, *rest, rope, has_prev):
    o_ref, sfin_ref, qr_scr, kr_scr, kv_scr = rest[1:] if has_prev else rest
    hp = pl.program_id(1)
    seq = q_ref.shape[0]
    c = RET_CHUNK
    nchunk = seq // c
    ks = RET_DK ** -0.5
    q = q_ref[...].astype(F32)
    k = k_ref[...].astype(F32)
    if rope:
        q = _rope_pair(q, cos_ref[...], sin_ref[...])
        k = _rope_pair(k, cos_ref[...], sin_ref[...])
    qr_scr[...] = q
    kr_scr[...] = k * ks

    pos_r = lax.broadcasted_iota(jnp.int32, (c, 1), 0).astype(F32)
    rel = (lax.broadcasted_iota(jnp.int32, (c, c), 0) - lax.broadcasted_iota(jnp.int32, (c, c), 1)).astype(F32)
    lane = lax.broadcasted_iota(jnp.int32, (1, LANES), 1)

    heads = []
    for hh in range(2):
        lgf = lg_ref[0, 2 * hp + hh]
        lgb = lg_ref[1, 2 * hp + hh]
        heads.append(dict(
            hm=(lane // RET_DK == hh).astype(F32),
            zeta_f=jnp.exp(lgf * (c - 1 - pos_r)), zeta_b=jnp.exp(lgb * pos_r),
            xi_f=jnp.exp(lgf * (pos_r + 1.0)), xi_b=jnp.exp(lgb * (c - pos_r)),
            dmat=jnp.where(rel >= 0, jnp.exp(lgf * jnp.maximum(rel, 0.0)), jnp.exp(lgb * jnp.maximum(-rel, 0.0))),
            gf=jnp.exp(lgf * c), gb=jnp.exp(lgb * c),
            vcols=slice(hh * RET_DV, (hh + 1) * RET_DV)))

    def kv_body(n, _):
        rows = pl.ds(pl.multiple_of(n * c, c), c)
        kr = kr_scr[rows, :]
        for hh, hd in enumerate(heads):
            kh = kr * hd["hm"]
            kz = jnp.concatenate([kh * hd["zeta_f"], kh * hd["zeta_b"]], axis=1).astype(BF16)
            kv_scr[hh, n] = lax.dot_general(kz, v_ref[rows, hd["vcols"]], (((0,), (0,)), ((), ())),
                                            preferred_element_type=F32)
        return 0

    lax.fori_loop(0, nchunk, kv_body, 0, unroll=2)

    def scan(j, carry):
        nf, nbk = j, nchunk - 1 - j
        out = []
        for hh, hd in enumerate(heads):
            sf, sb = carry[2 * hh], carry[2 * hh + 1]
            tf = kv_scr[hh, nf, 0:LANES, :]
            kv_scr[hh, nf, 0:LANES, :] = sf
            tb = kv_scr[hh, nbk, LANES:2 * LANES, :]
            kv_scr[hh, nbk, LANES:2 * LANES, :] = sb
            out += [hd["gf"] * sf + tf, hd["gb"] * sb + tb]
        return tuple(out)

    fin = lax.fori_loop(0, nchunk, scan, (s0_ref[0, 0], s0_ref[0, 1], s0_ref[1, 0], s0_ref[1, 1]))
    for hh in range(2):
        sfin_ref[hh, 0] = fin[2 * hh]
        sfin_ref[hh, 1] = fin[2 * hh + 1]

    def out_body(n, _):
        rows = pl.ds(pl.multiple_of(n * c, c), c)
        qr = qr_scr[rows, :]
        kb = kr_scr[rows, :].astype(BF16)
        for hh, hd in enumerate(heads):
            qm = qr * hd["hm"]
            a = lax.dot_general(qm.astype(BF16), kb, (((1,), (1,)), ((), ())),
                                preferred_element_type=F32) * hd["dmat"]
            o = jnp.dot(a.astype(BF16), v_ref[rows, hd["vcols"]], preferred_element_type=F32)
            qx = jnp.concatenate([qm * hd["xi_f"], qm * hd["xi_b"]], axis=1).astype(BF16)
            o = o + jnp.dot(qx, kv_scr[hh, n].astype(BF16), preferred_element_type=F32)
            o = o * lax.rsqrt(jnp.mean(o * o, axis=-1, keepdims=True) + EPS)
            g = g_ref[rows, hd["vcols"]].astype(F32)
            o_ref[rows, hd["vcols"]] = (g * jax.nn.sigmoid(g) * o).astype(o_ref.dtype)
        return 0

    lax.fori_loop(0, nchunk, out_body, 0, unroll=2)


def _retention(z, log_decay, cos, sin, s0, *, nb, seq, row0, rope, out_rows, y_prev=None):
    rb0 = row0 // seq
    hpairs = RET_HEADS // 2
    kcol, vcol, qcol, gcol = 0, 512 // 256, 3584 // LANES, 4096 // 256
    has_prev = y_prev is not None
    kernel = functools.partial(_ret_kernel, rope=rope, has_prev=has_prev)
    in_specs = [
        pl.BlockSpec(memory_space=pltpu.SMEM),
        pl.BlockSpec((seq, LANES), lambda b, p: (rb0 + b, qcol + p)),
        pl.BlockSpec((seq, LANES), lambda b, p: (rb0 + b, kcol + p)),
        pl.BlockSpec((seq, 2 * RET_DV), lambda b, p: (rb0 + b, vcol + p)),
        pl.BlockSpec((seq, 2 * RET_DV), lambda b, p: (rb0 + b, gcol + p)),
        pl.BlockSpec((seq, LANES), lambda b, p: (0, 0)),
        pl.BlockSpec((seq, LANES), lambda b, p: (0, 0)),
        pl.BlockSpec((None, 2, 2, LANES, RET_DV), lambda b, p: (b, p, 0, 0, 0)),
    ]
    args = [log_decay, z, z, z, z, cos, sin, s0]
    if has_prev:
        in_specs.append(pl.BlockSpec(memory_space=pl.ANY))
        args.append(y_prev)
    return pl.pallas_call(
        kernel,
        grid=(nb, hpairs),
        in_specs=in_specs,
        out_specs=[
            pl.BlockSpec((seq, 2 * RET_DV), lambda b, p: (rb0 + b, p)),
            pl.BlockSpec((None, 2, 2, LANES, RET_DV), lambda b, p: (b, p, 0, 0, 0)),
        ],
        out_shape=[
            jax.ShapeDtypeStruct((out_rows, RET_HEADS * RET_DV), BF16),
            jax.ShapeDtypeStruct((nb, RET_HEADS, 2, LANES, RET_DV), F32),
        ],
        scratch_shapes=[
            pltpu.VMEM((seq, LANES), F32),
            pltpu.VMEM((seq, LANES), F32),
            pltpu.VMEM((2, seq // RET_CHUNK, 2 * LANES, RET_DV), F32),
        ],
        input_output_aliases={len(args) - 1: 0} if has_prev else {},
        compiler_params=_cparams("parallel", "arbitrary"),
        name="retention_rope" if rope else "retention",
    )(*args)


NA_GROUP = 4
NA_KROWS = 12


def _na_kernel(rpb_ref, q_ref, k_ref, v_ref, kc_ref, vc_ref, o_ref, bias_scr):
    seq = q_ref.shape[0]
    nrow = seq // GRID_W
    ndr, ndc = 2 * NA_WR - 1, 2 * NA_WC - 1
    scale = NA_DH ** -0.5
    nt = (((1,), (1,)), ((), ()))
    h = pl.program_id(0)
    lane = lax.broadcasted_iota(jnp.int32, (GRID_W, LANES), 1)

    @pl.when(pl.program_id(1) == 0)
    def _():
        qi = lax.broadcasted_iota(jnp.int32, (GRID_W, LANES), 0)
        ki = lane & (GRID_W - 1)
        dc = ki - qi + (NA_WC - 1)
        cs = jnp.clip(qi - NA_WC // 2, 0, GRID_W - NA_WC)
        colmask = (ki >= cs) & (ki < cs + NA_WC)

        def toeplitz(d):
            t = jnp.zeros((GRID_W, LANES), F32)
            for j in range(ndc):
                t = jnp.where(dc == j, rpb_ref[h * (ndr * ndc) + d * ndc + j], t)
            return t

        prev = jnp.zeros((GRID_W, LANES), F32)
        for i in range(ndr + 1):
            nxt = toeplitz(i) if i < ndr else jnp.zeros((GRID_W, LANES), F32)
            bias_scr[i] = jnp.where(colmask, jnp.where(lane < GRID_W, prev, nxt), NEG)
            prev = nxt

    kc = kc_ref[...]
    vc = vc_ref[...]
    kwin = NA_KROWS * GRID_W

    def group(g, _):
        r0 = g * NA_GROUP
        us = jnp.clip(r0 - NA_WR // 2, 0, nrow - NA_KROWS)
        qrows = pl.ds(pl.multiple_of(r0 * GRID_W, NA_GROUP * GRID_W), NA_GROUP * GRID_W)
        krows = pl.ds(pl.multiple_of(us * GRID_W, GRID_W), kwin)
        q = q_ref[qrows, :]
        s = lax.dot_general(q, k_ref[krows, :], nt, preferred_element_type=F32)
        row_blocks = []
        for u in range(NA_GROUP):
            r = r0 + u
            rs = jnp.clip(r - NA_WR // 2, 0, nrow - NA_WR)
            parts = []
            for m in range(kwin // LANES):
                kr = us + 2 * m
                add_lo = jnp.where((kr >= rs) & (kr < rs + NA_WR), 0.0, NEG)
                add_hi = jnp.where((kr + 1 >= rs) & (kr + 1 < rs + NA_WR), 0.0, NEG)
                tab = bias_scr[jnp.clip(kr - r + NA_WR, 0, ndr)] + jnp.where(lane < GRID_W, add_lo, add_hi)
                st = s[u * GRID_W:(u + 1) * GRID_W, m * LANES:(m + 1) * LANES]
                parts.append(jnp.where(tab > 0.5 * NEG, st * scale + tab, NEG))
            row_blocks.append(jnp.concatenate(parts, axis=1))
        s = jnp.concatenate(row_blocks, axis=0)
        sc = lax.dot_general(q, kc, nt, preferred_element_type=F32) * scale
        mx = jnp.maximum(jnp.max(s, axis=-1, keepdims=True), jnp.max(sc, axis=-1, keepdims=True))
        e = jnp.exp(s - mx)
        ec = jnp.exp(sc - mx)
        den = jnp.sum(e, axis=-1, keepdims=True) + jnp.sum(ec, axis=-1, keepdims=True)
        o = jnp.dot(e.astype(BF16), v_ref[krows, :], preferred_element_type=F32)
        o = o + jnp.dot(ec.astype(BF16), vc, preferred_element_type=F32)
        o_ref[qrows, :] = (o / den).astype(o_ref.dtype)
        return 0

    lax.fori_loop(0, nrow // NA_GROUP, group, 0)


def _na_attention(z, rpb, *, nb, seq, cseq, crow0, out_rows):
    kcol, vcol, qcol = 1536 // LANES, 2560 // LANES, 5120 // LANES
    crb0 = crow0 // cseq
    nrow = seq // GRID_W
    assert nrow >= NA_KROWS and nrow % NA_GROUP == 0 and GRID_W * 2 == LANES
    assert NA_KROWS % 2 == 0 and NA_KROWS >= NA_WR + NA_GROUP - 1
    return pl.pallas_call(
        _na_kernel,
        grid=(NA_HEADS, nb),
        in_specs=[
            pl.BlockSpec(memory_space=pltpu.SMEM),
            pl.BlockSpec((seq, NA_DH), lambda h, b: (b, qcol + h)),
            pl.BlockSpec((seq, NA_DH), lambda h, b: (b, kcol + h)),
            pl.BlockSpec((seq, NA_DH), lambda h, b: (b, vcol + h)),
            pl.BlockSpec((cseq, NA_DH), lambda h, b: (crb0 + b, kcol + h)),
            pl.BlockSpec((cseq, NA_DH), lambda h, b: (crb0 + b, vcol + h)),
        ],
        out_specs=pl.BlockSpec((seq, NA_DH), lambda h, b: (b, h)),
        out_shape=jax.ShapeDtypeStruct((out_rows, NA_HEADS * NA_DH), BF16),
        scratch_shapes=[pltpu.VMEM((2 * NA_WR, GRID_W, LANES), F32)],
        compiler_params=_cparams("arbitrary", "arbitrary"),
        name="na_attention",
    )(rpb.reshape(-1), z, z, z, z, z)


def _dense_attn_kernel(q_ref, k_ref, v_ref, yprev_ref, o_ref):
    s = lax.dot_general(q_ref[...], k_ref[...], (((1,), (1,)), ((), ())), preferred_element_type=F32) * NA_DH ** -0.5
    e = jnp.exp(s - jnp.max(s, axis=-1, keepdims=True))
    o = jnp.dot(e.astype(BF16), v_ref[...], preferred_element_type=F32)
    o_ref[...] = (o / jnp.sum(e, axis=-1, keepdims=True)).astype(o_ref.dtype)


def _dense_attention(z, y_prev, *, nb, cseq, crow0):
    kcol, vcol, qcol = 1536 // LANES, 2560 // LANES, 5120 // LANES
    crb0 = crow0 // cseq
    return pl.pallas_call(
        _dense_attn_kernel,
        grid=(nb, NA_HEADS),
        in_specs=[
            pl.BlockSpec((cseq, NA_DH), lambda b, h: (crb0 + b, qcol + h)),
            pl.BlockSpec((cseq, NA_DH), lambda b, h: (crb0 + b, kcol + h)),
            pl.BlockSpec((cseq, NA_DH), lambda b, h: (crb0 + b, vcol + h)),
            pl.BlockSpec(memory_space=pl.ANY),
        ],
        out_specs=pl.BlockSpec((cseq, NA_DH), lambda b, h: (crb0 + b, h)),
        out_shape=jax.ShapeDtypeStruct(y_prev.shape, BF16),
        input_output_aliases={3: 0},
        compiler_params=_cparams("parallel", "arbitrary"),
        name="dense_attention",
    )(z, z, z, y_prev)


HY_FEAT_ROWS = 64
HIGHEST = lax.Precision.HIGHEST


def _hy_features(seq):
    t = np.linspace(0.0, 1.0, seq)
    bands = (HY_EMB - 1) // 2
    w = 2.0 * math.pi * np.arange(seq) / seq
    fr = np.linspace(1e-4, bands - 1, bands)
    ang = fr[None] * w[:, None]
    feat = np.concatenate([t[:, None], np.cos(ang), -np.sin(ang)], axis=-1)
    src = np.concatenate([np.arange(seq), np.zeros(1, np.int64), np.arange(seq - 1, 0, -1)])
    feat2 = np.zeros((HY_FEAT_ROWS, 2 * seq), np.float32)
    feat2[:HY_EMB] = feat[src].T
    tt = t[src][None].astype(np.float32)
    mask = np.ones((1, 2 * seq), np.float32)
    mask[0, seq] = 0.0
    return feat2, tt, mask


def _hy_mlp_kernel(feat_ref, w1_ref, b1_ref, w2_ref, b2_ref, fq_ref, o_ref):
    fq = fq_ref[...]
    h = jnp.dot(w1_ref[...], feat_ref[...], preferred_element_type=F32, precision=HIGHEST)
    h = jnp.sin(fq * (h + b1_ref[...]))
    h = jnp.dot(w2_ref[...], h, preferred_element_type=F32, precision=HIGHEST)
    o_ref[...] = jnp.sin(fq * (h + b2_ref[...]))


def _hy_mlp(feat2, w1t, b1, w2t, b2, fq):
    depth, hid, _ = w1t.shape
    n = feat2.shape[1]
    col = lambda a: a.reshape(depth, hid, 1)
    wspec = lambda k: pl.BlockSpec((None, hid, k), lambda l: (l, 0, 0))
    return pl.pallas_call(
        _hy_mlp_kernel,
        grid=(depth,),
        in_specs=[pl.BlockSpec((HY_FEAT_ROWS, n), lambda l: (0, 0)), wspec(HY_FEAT_ROWS), wspec(1), wspec(hid),
                  wspec(1), wspec(1)],
        out_specs=pl.BlockSpec((None, hid, n), lambda l: (l, 0, 0)),
        out_shape=jax.ShapeDtypeStruct((depth, hid, n), F32),
        compiler_params=_cparams("parallel"),
        name="hyena_filter_mlp",
    )(feat2, w1t, col(b1), w2t, col(b2), col(fq))


def _hy_filter_kernel(h_ref, w3_ref, dl_ref, tt_ref, mask_ref, o_ref):
    seq = h_ref.shape[1] // 2
    kf = jnp.dot(w3_ref[0], h_ref[:, :seq], preferred_element_type=F32, precision=HIGHEST)
    kb = jnp.dot(w3_ref[1], h_ref[:, seq:], preferred_element_type=F32, precision=HIGHEST)
    k = jnp.concatenate([kf, kb], axis=1) * (jnp.exp(-tt_ref[...] * dl_ref[...]) * mask_ref[...])
    o_ref[...] = (k * lax.rsqrt(jnp.sum(k * k, axis=1, keepdims=True) + EPS)).astype(o_ref.dtype)


def _hy_filter(h2, w3t, deltas, tt, mask, cb):
    depth, hid, n = h2.shape
    c = w3t.shape[3]
    return pl.pallas_call(
        _hy_filter_kernel,
        grid=(depth, HY_ORDER, c // cb),
        in_specs=[
            pl.BlockSpec((None, hid, n), lambda l, o, j: (l, 0, 0)),
            pl.BlockSpec((None, None, 2, cb, hid), lambda l, o, j: (l, o, 0, j, 0)),
            pl.BlockSpec((cb, 1), lambda l, o, j: (j, 0)),
            pl.BlockSpec((1, n), lambda l, o, j: (0, 0)),
            pl.BlockSpec((1, n), lambda l, o, j: (0, 0)),
        ],
        out_specs=pl.BlockSpec((None, None, cb, n), lambda l, o, j: (l, o, j, 0)),
        out_shape=jax.ShapeDtypeStruct((depth, HY_ORDER, c, n), BF16),
        compiler_params=_cparams("parallel", "parallel", "arbitrary"),
        name="hyena_filter",
    )(h2, w3t, deltas, tt, mask)


FFT_NO = 64
FFT_NI = 128


def _fft_tables():
    n = FFT_NO * FFT_NI
    a = np.arange(FFT_NO)
    fo = np.exp(-2j * np.pi * np.outer(a, a) / FFT_NO)
    i = np.arange(FFT_NI)
    ci = np.exp(-2j * np.pi * np.outer(i, i) / FFT_NI)
    tw = np.exp(-2j * np.pi * np.outer(a, i) / n)
    half = FFT_NO // 2
    f32 = lambda x: np.ascontiguousarray(x, dtype=np.float32)
    g1_real = f32(np.concatenate([fo.real, fo.imag], axis=0))
    g1 = f32(np.block([[fo.real[:, :half], -fo.imag[:, :half]], [fo.imag[:, :half], fo.real[:, :half]]]))
    w2 = f32(np.block([[ci.real, ci.imag], [-ci.imag, ci.real]]))
    w2i = f32(np.block([[ci.real, -ci.imag], [ci.imag, ci.real]]))
    g4 = f32(np.block([[fo.real[:half], fo.imag[:half]], [-fo.imag[:half], fo.real[:half]]]) / n)
    return dict(g1_real=g1_real, g1=g1, w2=w2, w2i=w2i, g4=g4, twr=f32(tw.real), twi=f32(tw.imag))


HY_CHUNK = 8


def _fft_stage1(g1, y2, twr, twi):
    o1 = jnp.dot(g1, y2, preferred_element_type=F32)
    yr, yi = o1[:FFT_NO], o1[FFT_NO:]
    ar = (yr * twr - yi * twi).astype(BF16)
    ai = (yr * twi + yi * twr).astype(BF16)
    lanes = lambda ci: slice(ci * FFT_NI, (ci + 1) * FFT_NI)
    return jnp.concatenate([jnp.concatenate([ar[:, lanes(ci)], ai[:, lanes(ci)]], axis=1)
                            for ci in range(HY_CHUNK)], axis=0)


def _hy_spectrum_kernel(k_ref, g1_ref, w2_ref, twr_ref, twi_ref, o_ref):
    cb = k_ref.shape[0]
    g1 = g1_ref[...]
    w2 = w2_ref[...]
    twr = twr_ref[...]
    twi = twi_ref[...]

    def body(j, _):
        chans = pl.ds(pl.multiple_of(j * HY_CHUNK, HY_CHUNK), HY_CHUNK)
        ks = k_ref[chans]
        k2 = jnp.concatenate([ks[ci] for ci in range(HY_CHUNK)], axis=1).astype(BF16)
        z = jnp.dot(_fft_stage1(g1, k2, twr, twi), w2, preferred_element_type=F32)
        o_ref[chans] = z.reshape(HY_CHUNK, FFT_NO, 2 * FFT_NI)
        return 0

    lax.fori_loop(0, cb // HY_CHUNK, body, 0)


def _tiled_twiddles(tabs):
    return (jnp.asarray(np.tile(tabs["twr"], (1, HY_CHUNK))), jnp.asarray(np.tile(tabs["twi"], (1, HY_CHUNK))))


def _hy_spectrum(kfilt, tabs, cb):
    g, c = kfilt.shape[:2]
    const = lambda a: pl.BlockSpec(a.shape, lambda i, j: (0,) * a.ndim)
    g1 = jnp.asarray(tabs["g1_real"], BF16)
    w2 = jnp.asarray(tabs["w2"], BF16)
    twr, twi = _tiled_twiddles(tabs)
    return pl.pallas_call(
        _hy_spectrum_kernel,
        grid=(g, c // cb),
        in_specs=[pl.BlockSpec((None, cb, FFT_NO, FFT_NI), lambda i, j: (i, j, 0, 0)),
                  const(g1), const(w2), const(twr), const(twi)],
        out_specs=pl.BlockSpec((None, cb, FFT_NO, 2 * FFT_NI), lambda i, j: (i, j, 0, 0)),
        out_shape=jax.ShapeDtypeStruct((g, c, FFT_NO, 2 * FFT_NI), F32),
        compiler_params=_cparams("parallel", "arbitrary"),
        name="hyena_spectrum",
    )(kfilt, g1, w2, twr, twi)


def _shift_conv3(x, w0, w1, w2, b):
    nrow, nlane = x.shape
    row = lax.broadcasted_iota(jnp.int32, x.shape, 0)
    lane = lax.broadcasted_iota(jnp.int32, x.shape, 1)
    r = pltpu.roll(x, 1, axis=1)
    prev = jnp.where(lane == 0, pltpu.roll(r, 1, axis=0), r)
    prev = jnp.where((lane == 0) & (row == 0), 0.0, prev)
    r = pltpu.roll(x, nlane - 1, axis=1)
    nxt = jnp.where(lane == nlane - 1, pltpu.roll(r, nrow - 1, axis=0), r)
    nxt = jnp.where((lane == nlane - 1) & (row == nrow - 1), 0.0, nxt)
    return prev * w0 + x * w1 + nxt * w2 + b


def _hyena_kernel(x_ref, kf_ref, pm_ref, g1_ref, w2_ref, w2i_ref, g4_ref, twr_ref, twi_ref, o_ref):
    cb = x_ref.shape[2]
    half = FFT_NO // 2
    ch = HY_CHUNK
    g1 = g1_ref[...]
    g4 = g4_ref[...]
    w2 = w2_ref[...]
    w2i = w2i_ref[...]
    twr = twr_ref[...]
    twi = twi_ref[...]
    lanes = lambda ci: slice(ci * FFT_NI, (ci + 1) * FFT_NI)
    krows = lambda ci: slice(ci * FFT_NO, (ci + 1) * FFT_NO)

    def body(j, _):
        c0 = pl.multiple_of(j * ch, ch)
        pm = pm_ref[:, pl.ds(c0, ch)]
        conv = []
        for part in range(3):
            halves = []
            for bb in range(2):
                xs = x_ref[bb, part, pl.ds(c0, ch)].astype(F32)
                halves.append(jnp.concatenate(
                    [_shift_conv3(xs[ci], pm[part, ci], pm[3 + part, ci], pm[6 + part, ci], pm[9 + part, ci])
                     for ci in range(ch)], axis=1))
            conv.append(jnp.concatenate(halves, axis=0))
        y = conv[2]
        for order in range(HY_ORDER):
            z = jnp.dot(_fft_stage1(g1, y.astype(BF16), twr, twi), w2, preferred_element_type=F32)
            kf = kf_ref[order, pl.ds(c0, ch)].reshape(ch * FFT_NO, 2 * FFT_NI)
            zr, zi = z[:, :FFT_NI], z[:, FFT_NI:]
            kr, ki = kf[:, :FFT_NI], kf[:, FFT_NI:]
            p = jnp.concatenate([zr * kr - zi * ki, zr * ki + zi * kr], axis=1).astype(BF16)
            q = jnp.dot(p, w2i, preferred_element_type=F32)
            qr = jnp.concatenate([q[krows(ci), :FFT_NI] for ci in range(ch)], axis=1)
            qi = jnp.concatenate([q[krows(ci), FFT_NI:] for ci in range(ch)], axis=1)
            qs = jnp.concatenate([qr * twr + qi * twi, qi * twr - qr * twi], axis=0).astype(BF16)
            cv = jnp.dot(g4, qs, preferred_element_type=F32)
            bias = jnp.concatenate([pm[12 + order, ci] for ci in range(ch)], axis=1)
            y = conv[order] * (cv + bias * y)
        for ci in range(ch):
            o_ref[0, c0 + ci] = y[:half, lanes(ci)].astype(o_ref.dtype)
            o_ref[1, c0 + ci] = y[half:, lanes(ci)].astype(o_ref.dtype)
        return 0

    lax.fori_loop(0, cb // ch, body, 0)


def _hyena(xt, kf, layer, pm, tabs, cb):
    nb, _, c, half, _ = xt.shape
    const = lambda a: pl.BlockSpec(a.shape, lambda j, p: (0,) * a.ndim)
    bf = lambda name: jnp.asarray(tabs[name], BF16)
    g1, w2, w2i, g4 = bf("g1"), bf("w2"), bf("w2i"), bf("g4")
    twr, twi = _tiled_twiddles(tabs)
    return pl.pallas_call(
        _hyena_kernel,
        grid=(c // cb, nb // 2),
        in_specs=[
            pl.BlockSpec((2, 3, cb, half, FFT_NI), lambda j, p: (p, 0, j, 0, 0)),
            pl.BlockSpec((None, HY_ORDER, cb, FFT_NO, 2 * FFT_NI), lambda j, p: (layer, 0, j, 0, 0)),
            pl.BlockSpec((14, cb, 1, FFT_NI), lambda j, p: (0, j, 0, 0)),
            const(g1), const(w2), const(w2i), const(g4), const(twr), const(twi),
        ],
        out_specs=pl.BlockSpec((2, cb, half, FFT_NI), lambda j, p: (p, j, 0, 0)),
        out_shape=jax.ShapeDtypeStruct((nb, c, half, FFT_NI), BF16),
        compiler_params=_cparams("parallel", "arbitrary"),
        name="hyena",
    )(xt, kf, pm, g1, w2, w2i, g4, twr, twi)


def _dft_tables(seq):
    n = 2 * seq
    wmat = np.exp(-2j * np.pi * np.outer(np.arange(n), np.arange(n)) / n)
    f32 = lambda x: np.ascontiguousarray(x, dtype=np.float32)
    fwd_real = f32(np.concatenate([wmat.real, wmat.imag], axis=1))
    ws = wmat[:seq]
    fwd = f32(np.block([[ws.real, ws.imag], [-ws.imag, ws.real]]))
    wi = np.conj(wmat)[:, :seq] / n
    inv = f32(np.block([[wi.real, wi.imag], [-wi.imag, wi.real]]))
    return dict(fwd_real=fwd_real, fwd=fwd, inv=inv)


def _rowdft_kernel(x_ref, w_ref, o_ref):
    o_ref[...] = jnp.dot(x_ref[...].astype(BF16), w_ref[...], preferred_element_type=F32)


def _rowdft(x, w, tr):
    m, k = x.shape
    n = w.shape[1]
    return pl.pallas_call(
        _rowdft_kernel,
        grid=(m // tr,),
        in_specs=[pl.BlockSpec((tr, k), lambda i: (i, 0)), pl.BlockSpec((k, n), lambda i: (0, 0))],
        out_specs=pl.BlockSpec((tr, n), lambda i: (i, 0)),
        out_shape=jax.ShapeDtypeStruct((m, n), F32),
        compiler_params=_cparams("parallel"),
        name="row_dft",
    )(x, w)


def _lane_conv3(x, w0, w1, w2, b):
    seq = x.shape[1]
    lane = lax.broadcasted_iota(jnp.int32, x.shape, 1)
    prev = jnp.where(lane == 0, 0.0, pltpu.roll(x, 1, axis=1))
    nxt = jnp.where(lane == seq - 1, 0.0, pltpu.roll(x, seq - 1, axis=1))
    return prev * w0 + x * w1 + nxt * w2 + b


def _hyena_ctx_kernel(x_ref, kf_ref, pm_ref, fwd_ref, inv_ref, o_ref):
    seq = x_ref.shape[3]
    n = 2 * seq
    conv = [[_lane_conv3(x_ref[bb, part].astype(F32), pm_ref[part], pm_ref[3 + part], pm_ref[6 + part],
                         pm_ref[9 + part]) for bb in range(2)] for part in range(3)]
    ya, yb = conv[2]
    for order in range(HY_ORDER):
        z = jnp.dot(jnp.concatenate([ya, yb], axis=1).astype(BF16), fwd_ref[...], preferred_element_type=F32)
        kf = kf_ref[order]
        zr, zi, kr, ki = z[:, :n], z[:, n:], kf[:, :n], kf[:, n:]
        p = jnp.concatenate([zr * kr - zi * ki, zr * ki + zi * kr], axis=1).astype(BF16)
        cv = jnp.dot(p, inv_ref[...], preferred_element_type=F32)
        bias = pm_ref[12 + order]
        ya = conv[order][0] * (cv[:, :seq] + bias * ya)
        yb = conv[order][1] * (cv[:, seq:] + bias * yb)
    o_ref[0] = ya.astype(o_ref.dtype)
    o_ref[1] = yb.astype(o_ref.dtype)


def _hyena_ctx(xt, kf, pm, tabs, cb):
    nb, _, c, seq = xt.shape
    fwd, inv = jnp.asarray(tabs["fwd"], BF16), jnp.asarray(tabs["inv"], BF16)
    const = lambda a: pl.BlockSpec(a.shape, lambda j, p: (0,) * a.ndim)
    return pl.pallas_call(
        _hyena_ctx_kernel,
        grid=(c // cb, nb // 2),
        in_specs=[
            pl.BlockSpec((2, 3, cb, seq), lambda j, p: (p, 0, j, 0)),
            pl.BlockSpec((HY_ORDER, cb, 4 * seq), lambda j, p: (0, j, 0)),
            pl.BlockSpec((14, cb, 1), lambda j, p: (0, j, 0)),
            const(fwd), const(inv),
        ],
        out_specs=pl.BlockSpec((2, cb, seq), lambda j, p: (p, j, 0)),
        out_shape=jax.ShapeDtypeStruct((nb, c, seq), BF16),
        compiler_params=_cparams("parallel", "arbitrary"),
        name="hyena_ctx",
    )(xt, kf, pm, fwd, inv)


def _rope_tables(seq):
    t = np.arange(seq)
    pos = np.stack([t // GRID_W, t % GRID_W], axis=1).astype(np.float64)
    n = RET_DK // 4
    inv = ROPE_BASE ** (-np.arange(n, dtype=np.float64) / n)
    lane = np.arange(LANES) % RET_DK
    ang = pos[:, lane // (2 * n)] * inv[lane % n][None]
    sign = np.where(lane % (2 * n) < n, -1.0, 1.0)
    return np.cos(ang).astype(np.float32), (np.sin(ang) * sign[None]).astype(np.float32)


def _hy_filters(seq, hy_f_w1, hy_f_b1, hy_f_w2, hy_f_b2, hy_f_w3, hy_f_freq):
    depth, _, hid = hy_f_w1.shape
    c = hy_f_w3.shape[2] // (2 * HY_ORDER)
    feat2, tt, mask = _hy_features(seq)
    w1t = jnp.pad(hy_f_w1.transpose(0, 2, 1), ((0, 0), (0, 0), (0, HY_FEAT_ROWS - HY_EMB)))
    h2 = _hy_mlp(jnp.asarray(feat2), w1t, hy_f_b1, hy_f_w2.transpose(0, 2, 1), hy_f_b2, hy_f_freq)
    w3t = hy_f_w3.reshape(depth, hid, HY_ORDER, 2, c).transpose(0, 2, 3, 4, 1)
    deltas = np.abs(np.linspace(math.log(HY_TARGET) / HY_FAST, math.log(HY_TARGET) / HY_SLOW, c))
    deltas = jnp.asarray(deltas.astype(np.float32).reshape(c, 1))
    return _hy_filter(h2, w3t, deltas, jnp.asarray(tt), jnp.asarray(mask), min(c, 8 * 1024 * 128 // (2 * seq)))


def _hy_params(hy_conv_w, hy_conv_b, hy_bias):
    c = hy_bias.shape[1]
    return jnp.concatenate([hy_conv_w.reshape(9, c), hy_conv_b.reshape(3, c), hy_bias], axis=0)


def kernel(x, c, ctx, c_ctx, w_mod, b_mod, g_norm1, g_norm2, w_in, hy_conv_w, hy_conv_b, hy_f_w1, hy_f_b1, hy_f_w2, hy_f_b2, hy_f_w3, hy_f_freq, hy_bias, ret_log_decay, na_rpb, w_branch, w_out, ffn_w_in, ffn_conv_w, ffn_conv_b, ffn_w_out, g_final):
    nb, seq, d = x.shape
    cseq = ctx.shape[1]
    depth = w_mod.shape[0]
    hy_w = hy_bias.shape[2]
    t_lat, t_ctx = nb * seq, nb * cseq
    assert seq % TM == 0 and t_ctx % TM == 0 and TM % cseq == 0 and nb % 2 == 0 and nb < SUBLANES
    assert 2 * seq == FFT_NO * FFT_NI and seq % GRID_W == 0 and cseq & (cseq - 1) == 0
    blk_lat, blk_all = t_lat // TM, (t_lat + t_ctx) // TM
    mrow = jnp.asarray(np.concatenate([np.repeat(np.arange(nb), seq // TM), np.full(t_ctx // TM, nb)]), jnp.int32)
    seqlen = jnp.asarray(np.concatenate([np.full(blk_lat, seq), np.full(t_ctx // TM, cseq)]), jnp.int32)
    per = TM // TR
    mrow_r = jnp.repeat(mrow, per)
    tn = 1024

    cond = jnp.zeros((SUBLANES, d), F32).at[:nb].set(c).at[nb].set(c_ctx)
    mods = _mods(cond, w_mod, b_mod).reshape(depth, SUBLANES, 6, 1, d)

    cos, sin = (jnp.asarray(a) for a in _rope_tables(seq))
    tabs = _fft_tables()
    ctabs = _dft_tables(cseq)
    filt = _hy_filters(seq, hy_f_w1, hy_f_b1, hy_f_w2, hy_f_b2, hy_f_w3, hy_f_freq)
    spec = _hy_spectrum(filt.reshape(depth * HY_ORDER, hy_w, FFT_NO, FFT_NI), tabs, 32)
    spec = spec.reshape(depth, HY_ORDER, hy_w, FFT_NO, 2 * FFT_NI)
    cfilt = _hy_filters(cseq, hy_f_w1, hy_f_b1, hy_f_w2, hy_f_b2, hy_f_w3, hy_f_freq)
    cspec = _rowdft(cfilt.reshape(depth * HY_ORDER * hy_w, 2 * cseq), jnp.asarray(ctabs["fwd_real"], BF16), 1024)
    cspec = cspec.reshape(depth, HY_ORDER, hy_w, 4 * cseq)

    xs = jnp.concatenate([x.reshape(t_lat, d), ctx.reshape(t_ctx, d)], axis=0)
    hy0 = 6144
    gate0 = hy0 + 3 * hy_w
    s_zero = jnp.zeros((nb, RET_HEADS, 2, LANES, RET_DV), F32)
    hn = _normmod(xs, g_norm1[0], mods[0], mrow, 0, 1, blk_all)
    for l in range(depth):
        last = l == depth - 1
        nblk = blk_lat if last else blk_all
        rows = nblk * TM
        z = _matmul(hn, w_in, l, blk_all, tn, w_in.shape[2] - 3 * hy_w, skip=(hy0 // tn, 3 * hy_w // tn))
        hy_proj = functools.partial(_matmul_nt, hn, w_in, l, col0=hy0, n_out=3 * hy_w, nb=nb, tc=512)
        zh = hy_proj(blk0=0, nblk=blk_lat, seq=seq)

        y_ret, s_ctx = _retention(z, ret_log_decay[l], cos, sin, s_zero, nb=nb, seq=cseq, row0=t_lat, rope=False,
                                  out_rows=blk_all * TM)
        y_ret, _ = _retention(z, ret_log_decay[l], cos, sin, s_ctx, nb=nb, seq=seq, row0=0, rope=True,
                              out_rows=blk_all * TM, y_prev=y_ret)
        y_na = _na_attention(z, na_rpb[l], nb=nb, seq=seq, cseq=cseq, crow0=t_lat, out_rows=rows)
        pm = _hy_params(hy_conv_w[l], hy_conv_b[l], hy_bias[l])
        pm_lat = jnp.broadcast_to(pm[:, :, None, None], (14, hy_w, 1, FFT_NI))
        y_hy = _hyena(zh.reshape(nb, 3, hy_w, FFT_NO // 2, FFT_NI), spec, l, pm_lat, tabs, 32)
        y_hy = y_hy.reshape(nb, hy_w, seq).transpose(0, 2, 1).reshape(t_lat, hy_w)
        if not last:
            y_na = _dense_attention(z, y_na, nb=nb, cseq=cseq, crow0=t_lat)
            zc = hy_proj(blk0=blk_lat, nblk=t_ctx // TM, seq=cseq)
            yc_hy = _hyena_ctx(zc.reshape(nb, 3, hy_w, cseq), cspec[l], pm[:, :, None], ctabs, 256)
            y_hy = jnp.concatenate([y_hy, yc_hy.transpose(0, 2, 1).reshape(t_ctx, hy_w)], axis=0)
        acc = _merge((y_hy, y_ret, y_na), z, hy0, w_branch[l].astype(BF16), nblk, 1024)
        xs, hn = _out_proj(acc, w_out[l].astype(BF16), xs, mods[l], mrow_r, g_norm2[l], nblk * per)
        u = _matmul(hn, ffn_w_in, l, nblk, tn, ffn_w_in.shape[2])
        hg = _ffn_gate(u, ffn_conv_w[l], ffn_conv_b[l], seqlen, nblk, 512)
        nxt = l if last else l + 1
        w_ffo = ffn_w_out[l].astype(BF16)
        if last:
            res = _ffn_out(hg, w_ffo, xs, mods[l], mods[l], g_final, mrow_r, nblk * per, TR, 512, True)
        else:
            res = _ffn_out(hg, w_ffo, xs, mods[l], mods[nxt], g_norm1[nxt], mrow, nblk, TM, 256, False)
        if not last:
            xs, hn = res
    return res[0].reshape(nb, seq, d)
```

```python
import functools
import math

import numpy as np
import jax
import jax.numpy as jnp
from jax import lax
from jax.experimental import pallas as pl
from jax.experimental.pallas import tpu as pltpu

F32 = jnp.float32
BF16 = jnp.bfloat16

GRID_W = 64
N_BRANCH = 3
HY_ORDER = 2
HY_EMB = 33
HY_FAST = 0.3
HY_SLOW = 1.5
HY_TARGET = 1e-2
RET_HEADS = 8
RET_DK = 64
RET_DV = 128
RET_CHUNK = 128
NA_HEADS = 8
NA_DH = 128
NA_WR = 8
NA_WC = 16
ROPE_BASE = 10000.0
EPS = 1e-6
NEG = -1e30

LANES = 128
SUBLANES = 8
VMEM_LIMIT = 56 * 1024 * 1024

TM = 1024


def _cparams(*sem):
    return pltpu.CompilerParams(dimension_semantics=sem, vmem_limit_bytes=VMEM_LIMIT)


def _mods_kernel(a_ref, w_ref, b_ref, o_ref):
    a = a_ref[...]
    a = a * jax.nn.sigmoid(a)
    o_ref[...] = jnp.dot(a.astype(BF16), w_ref[...].astype(BF16), preferred_element_type=F32) + b_ref[...]


def _mods(cond, w_mod, b_mod):
    depth, d, n = w_mod.shape
    tn = 1024
    return pl.pallas_call(
        _mods_kernel,
        grid=(depth, n // tn),
        in_specs=[
            pl.BlockSpec((SUBLANES, d), lambda l, j: (0, 0)),
            pl.BlockSpec((None, d, tn), lambda l, j: (l, 0, j)),
            pl.BlockSpec((None, 1, tn), lambda l, j: (l, 0, j)),
        ],
        out_specs=pl.BlockSpec((None, SUBLANES, tn), lambda l, j: (l, 0, j)),
        out_shape=jax.ShapeDtypeStruct((depth, SUBLANES, n), F32),
        compiler_params=_cparams("parallel", "parallel"),
        name="mods",
    )(cond, w_mod, b_mod.reshape(depth, 1, n))


def _normmod_kernel(mrow_ref, x_ref, g_ref, sh_ref, sc_ref, o_ref):
    x = x_ref[...]
    y = x * lax.rsqrt(jnp.mean(x * x, axis=-1, keepdims=True) + EPS)
    y = y * g_ref[...]
    o_ref[...] = (y * (1.0 + sc_ref[...]) + sh_ref[...]).astype(o_ref.dtype)


def _normmod(x, g, mods, mrow, shift_idx, scale_idx, nblk):
    m, d = x.shape
    grid_spec = pltpu.PrefetchScalarGridSpec(
        num_scalar_prefetch=1,
        grid=(nblk,),
        in_specs=[
            pl.BlockSpec((TM, d), lambda i, mr: (i, 0)),
            pl.BlockSpec((1, d), lambda i, mr: (0, 0)),
            pl.BlockSpec((None, None, 1, d), lambda i, mr: (mr[i], shift_idx, 0, 0)),
            pl.BlockSpec((None, None, 1, d), lambda i, mr: (mr[i], scale_idx, 0, 0)),
        ],
        out_specs=pl.BlockSpec((TM, d), lambda i, mr: (i, 0)),
    )
    return pl.pallas_call(
        _normmod_kernel,
        grid_spec=grid_spec,
        out_shape=jax.ShapeDtypeStruct((m, d), BF16),
        compiler_params=_cparams("parallel"),
        name="normmod",
    )(mrow, x, g.reshape(1, d), mods, mods)


def _mm_kernel(a_ref, w_ref, o_ref, w_scr):
    @pl.when(pl.program_id(1) == 0)
    def _():
        w_scr[...] = w_ref[...].astype(BF16)

    o_ref[...] = jnp.dot(a_ref[...], w_scr[...], preferred_element_type=F32).astype(o_ref.dtype)


def _matmul(a, w, layer, nblk, tn, n_out, skip=None):
    m, k = a.shape
    col = (lambda j: j) if skip is None else (lambda j: j + jnp.where(j >= skip[0], skip[1], 0))
    return pl.pallas_call(
        _mm_kernel,
        grid=(n_out // tn, nblk),
        in_specs=[pl.BlockSpec((TM, k), lambda j, i: (i, 0)),
                  pl.BlockSpec((None, k, tn), lambda j, i: (layer, 0, col(j)))],
        out_specs=pl.BlockSpec((TM, tn), lambda j, i: (i, j)),
        out_shape=jax.ShapeDtypeStruct((m, n_out), BF16),
        scratch_shapes=[pltpu.VMEM((k, tn), BF16)],
        compiler_params=_cparams("arbitrary", "arbitrary"),
        name="matmul",
    )(a, w)


def _mm_nt_kernel(w_ref, a_ref, o_ref, wt_scr):
    @pl.when(pl.program_id(1) == 0)
    def _():
        wt_scr[...] = w_ref[...].T.astype(BF16)

    r = lax.dot_general(wt_scr[...], a_ref[...], (((1,), (1,)), ((), ())), preferred_element_type=F32)
    per = o_ref.shape[0]
    width = r.shape[1] // per
    for s in range(per):
        o_ref[s] = r[:, s * width:(s + 1) * width].astype(o_ref.dtype)


def _matmul_nt(a, w, layer, *, col0, n_out, blk0, nblk, nb, seq, tc):
    k = a.shape[1]
    per, sblk = max(TM // seq, 1), max(seq // TM, 1)
    return pl.pallas_call(
        _mm_nt_kernel,
        grid=(n_out // tc, nblk),
        in_specs=[pl.BlockSpec((None, k, tc), lambda j, i: (layer, 0, col0 // tc + j)),
                  pl.BlockSpec((TM, k), lambda j, i: (blk0 + i, 0))],
        out_specs=pl.BlockSpec((per, tc, TM // per), lambda j, i: (i // sblk, j, i % sblk)),
        out_shape=jax.ShapeDtypeStruct((nb, n_out, seq), BF16),
        scratch_shapes=[pltpu.VMEM((tc, k), BF16)],
        compiler_params=_cparams("arbitrary", "arbitrary"),
        name="matmul_nt",
    )(w, a)


TR = 512


def _norm_modulate(x, g, shift, scale):
    y = x * lax.rsqrt(jnp.mean(x * x, axis=-1, keepdims=True) + EPS)
    return (y * g) * (1.0 + scale) + shift


def _outproj_kernel(mrow_ref, a_ref, w_ref, x_ref, gate_ref, g_ref, sh_ref, sc_ref, x_out, h_out):
    y = jnp.dot(a_ref[...], w_ref[...], preferred_element_type=F32)
    x = x_ref[...] + gate_ref[...] * y
    x_out[...] = x
    h_out[...] = _norm_modulate(x, g_ref[...], sh_ref[...], sc_ref[...]).astype(h_out.dtype)


def _out_proj(a, w, x, mods, mrow, g_norm, nblk):
    m, k = a.shape
    d = w.shape[1]
    mod = lambda idx: pl.BlockSpec((None, None, 1, d), lambda i, mr: (mr[i], idx, 0, 0))
    row = pl.BlockSpec((TR, d), lambda i, mr: (i, 0))
    grid_spec = pltpu.PrefetchScalarGridSpec(
        num_scalar_prefetch=1,
        grid=(nblk,),
        in_specs=[pl.BlockSpec((TR, k), lambda i, mr: (i, 0)), pl.BlockSpec((k, d), lambda i, mr: (0, 0)), row,
                  mod(2), pl.BlockSpec((1, d), lambda i, mr: (0, 0)), mod(3), mod(4)],
        out_specs=[row, row],
    )
    return pl.pallas_call(
        _outproj_kernel,
        grid_spec=grid_spec,
        out_shape=[jax.ShapeDtypeStruct((m, d), F32), jax.ShapeDtypeStruct((m, d), BF16)],
        compiler_params=_cparams("parallel"),
        name="out_proj",
    )(mrow, a, w, x, mods, g_norm.reshape(1, d), mods, mods)


def _merge_kernel(y0_ref, y1_ref, y2_ref, g0_ref, g1_ref, g2_ref, w_ref, o_ref):
    acc = None
    for i, (y_ref, g_ref) in enumerate(((y0_ref, g0_ref), (y1_ref, g1_ref), (y2_ref, g2_ref))):
        t = jnp.dot(y_ref[...], w_ref[i], preferred_element_type=F32)
        t = jax.nn.sigmoid(g_ref[...].astype(F32)) * t
        acc = t if acc is None else acc + t
    o_ref[...] = acc.astype(o_ref.dtype)


def _merge(ys, z, gate_col0, w_branch, nblk, tn):
    m, bw = ys[0].shape
    d = w_branch.shape[2]
    gspec = lambda i_br: pl.BlockSpec((TM, tn), lambda i, j: (i, (gate_col0 + i_br * d) // tn + j))
    yspec = pl.BlockSpec((TM, bw), lambda i, j: (i, 0))
    return pl.pallas_call(
        _merge_kernel,
        grid=(nblk, d // tn),
        in_specs=[yspec, yspec, yspec, gspec(0), gspec(1), gspec(2),
                  pl.BlockSpec((N_BRANCH, bw, tn), lambda i, j: (0, 0, j))],
        out_specs=pl.BlockSpec((TM, tn), lambda i, j: (i, j)),
        out_shape=jax.ShapeDtypeStruct((m, d), BF16),
        compiler_params=_cparams("parallel", "arbitrary"),
        name="merge",
    )(ys[0], ys[1], ys[2], z, z, z, w_branch)


def _ffn_gate_kernel(seq_ref, a_ref, ap_ref, an_ref, b_ref, cw_ref, cb_ref, o_ref):
    i = pl.program_id(0)
    seq_m1 = seq_ref[i] - 1
    a = a_ref[...].astype(F32)
    tm = a.shape[0]
    row = lax.broadcasted_iota(jnp.int32, (tm, 1), 0)
    pos = (i * tm + row) & seq_m1
    prev = jnp.where(row == 0, ap_ref[SUBLANES - 1:SUBLANES, :].astype(F32), pltpu.roll(a, 1, axis=0))
    prev = jnp.where(pos == 0, 0.0, prev)
    nxt = jnp.where(row == tm - 1, an_ref[0:1, :].astype(F32), pltpu.roll(a, tm - 1, axis=0))
    nxt = jnp.where(pos == seq_m1, 0.0, nxt)
    cw = cw_ref[...]
    conv = prev * cw[0:1, :] + a * cw[1:2, :] + nxt * cw[2:3, :] + cb_ref[...]
    o_ref[...] = jax.nn.gelu(conv.astype(BF16)) * b_ref[...]


def _ffn_gate(u, conv_w, conv_b, seqlen, nblk, tc):
    m, ff2 = u.shape
    ff = ff2 // 2
    nrow8 = m // SUBLANES
    r8 = TM // SUBLANES
    grid_spec = pltpu.PrefetchScalarGridSpec(
        num_scalar_prefetch=1,
        grid=(nblk, ff // tc),
        in_specs=[
            pl.BlockSpec((TM, tc), lambda i, j, s: (i, j)),
            pl.BlockSpec((SUBLANES, tc), lambda i, j, s: (jnp.maximum(i * r8 - 1, 0), j)),
            pl.BlockSpec((SUBLANES, tc), lambda i, j, s: (jnp.minimum((i + 1) * r8, nrow8 - 1), j)),
            pl.BlockSpec((TM, tc), lambda i, j, s: (i, ff // tc + j)),
            pl.BlockSpec((3, tc), lambda i, j, s: (0, j)),
            pl.BlockSpec((1, tc), lambda i, j, s: (0, j)),
        ],
        out_specs=pl.BlockSpec((TM, tc), lambda i, j, s: (i, j)),
    )
    return pl.pallas_call(
        _ffn_gate_kernel,
        grid_spec=grid_spec,
        out_shape=jax.ShapeDtypeStruct((m, ff), BF16),
        compiler_params=_cparams("parallel", "arbitrary"),
        name="ffn_gate",
    )(seqlen, u, u, u, u, conv_w, conv_b.reshape(1, ff))


def _ffn_out_kernel(mrow_ref, h_ref, w_ref, x_ref, gate_ref, g_ref, sh_ref, sc_ref, *rest, final):
    outs, xrow = rest[:-1], rest[-1]
    j = pl.program_id(1)
    y = jnp.dot(h_ref[...], w_ref[...], preferred_element_type=F32)
    xt = x_ref[...] + gate_ref[...] * y
    xrow[j] = xt
    if not final:
        outs[0][...] = xt

    @pl.when(j == pl.num_programs(1) - 1)
    def _():
        x = jnp.concatenate([xrow[t] for t in range(xrow.shape[0])], axis=1)
        if final:
            y = x * lax.rsqrt(jnp.mean(x * x, axis=-1, keepdims=True) + EPS)
            outs[0][...] = y * g_ref[...]
        else:
            outs[1][...] = _norm_modulate(x, g_ref[...], sh_ref[...], sc_ref[...]).astype(outs[1].dtype)


def _ffn_out(h, w, x, mods, mods_next, g_next, mrow, nblk, tr, tn, final):
    m, ff = h.shape
    d = w.shape[1]
    mod = lambda arr_idx: pl.BlockSpec((None, None, 1, d), lambda i, j, mr: (mr[i], arr_idx, 0, 0))
    row = pl.BlockSpec((tr, d), lambda i, j, mr: (i, 0))
    tile = pl.BlockSpec((tr, tn), lambda i, j, mr: (i, j))
    grid_spec = pltpu.PrefetchScalarGridSpec(
        num_scalar_prefetch=1,
        grid=(nblk, d // tn),
        in_specs=[
            pl.BlockSpec((tr, ff), lambda i, j, mr: (i, 0)),
            pl.BlockSpec((ff, tn), lambda i, j, mr: (0, j)),
            tile,
            pl.BlockSpec((None, None, 1, tn), lambda i, j, mr: (mr[i], 5, 0, j)),
            pl.BlockSpec((1, d), lambda i, j, mr: (0, 0)), mod(0), mod(1),
        ],
        out_specs=[row] if final else [tile, row],
        scratch_shapes=[pltpu.VMEM((d // tn, tr, tn), F32)],
    )
    out_shape = [jax.ShapeDtypeStruct((m, d), F32)] + ([] if final else [jax.ShapeDtypeStruct((m, d), BF16)])
    return pl.pallas_call(
        functools.partial(_ffn_out_kernel, final=final),
        grid_spec=grid_spec,
        out_shape=out_shape,
        compiler_params=_cparams("parallel", "arbitrary"),
        name="ffn_out",
    )(mrow, h, w, x, mods, g_next.reshape(1, d), mods_next, mods_next)


def _rope_pair(x, cos, sin):
    lane = lax.broadcasted_iota(jnp.int32, x.shape, 1)
    first = lax.rem(lane, 32) < 16
    partner = jnp.where(first, pltpu.roll(x, LANES - 16, axis=1), pltpu.roll(x, 16, axis=1))
    return x * cos + partner * sin


def _ret_kernel(lg_ref, q_ref, k_ref, v_ref, g_ref, cos_ref, sin_ref, s0_ref, *rest, rope, has_prev):
    o_ref, sfin_ref, qr_scr, kr_scr, kv_scr = rest[1:] if has_prev else rest
    hp = pl.program_id(1)
    seq = q_ref.shape[0]
    c = RET_CHUNK
    nchunk = seq // c
    ks = RET_DK ** -0.5
    q = q_ref[...].astype(F32)
    k = k_ref[...].astype(F32)
    if rope:
        q = _rope_pair(q, cos_ref[...], sin_ref[...])
        k = _rope_pair(k, cos_ref[...], sin_ref[...])
    qr_scr[...] = q
    kr_scr[...] = k * ks

    pos_r = lax.broadcasted_iota(jnp.int32, (c, 1), 0).astype(F32)
    rel = (lax.broadcasted_iota(jnp.int32, (c, c), 0) - lax.broadcasted_iota(jnp.int32, (c, c), 1)).astype(F32)
    lane = lax.broadcasted_iota(jnp.int32, (1, LANES), 1)

    heads = []
    for hh in range(2):
        lgf = lg_ref[0, 2 * hp + hh]
        lgb = lg_ref[1, 2 * hp + hh]
        heads.append(dict(
            hm=(lane // RET_DK == hh).astype(F32),
            zeta_f=jnp.exp(lgf * (c - 1 - pos_r)), zeta_b=jnp.exp(lgb * pos_r),
            xi_f=jnp.exp(lgf * (pos_r + 1.0)), xi_b=jnp.exp(lgb * (c - pos_r)),
            dmat=jnp.where(rel >= 0, jnp.exp(lgf * jnp.maximum(rel, 0.0)), jnp.exp(lgb * jnp.maximum(-rel, 0.0))),
            gf=jnp.exp(lgf * c), gb=jnp.exp(lgb * c),
            vcols=slice(hh * RET_DV, (hh + 1) * RET_DV)))

    def kv_body(n, _):
        rows = pl.ds(pl.multiple_of(n * c, c), c)
        kr = kr_scr[rows, :]
        for hh, hd in enumerate(heads):
            kh = kr * hd["hm"]
            kz = jnp.concatenate([kh * hd["zeta_f"], kh * hd["zeta_b"]], axis=1).astype(BF16)
            kv_scr[hh, n] = lax.dot_general(kz, v_ref[rows, hd["vcols"]], (((0,), (0,)), ((), ())),
                                            preferred_element_type=F32)
        return 0

    lax.fori_loop(0, nchunk, kv_body, 0, unroll=2)

    def scan(j, carry):
        nf, nbk = j, nchunk - 1 - j
        out = []
        for hh, hd in enumerate(heads):
            sf, sb = carry[2 * hh], carry[2 * hh + 1]
            tf = kv_scr[hh, nf, 0:LANES, :]
            kv_scr[hh, nf, 0:LANES, :] = sf
            tb = kv_scr[hh, nbk, LANES:2 * LANES, :]
            kv_scr[hh, nbk, LANES:2 * LANES, :] = sb
            out += [hd["gf"] * sf + tf, hd["gb"] * sb + tb]
        return tuple(out)

    fin = lax.fori_loop(0, nchunk, scan, (s0_ref[0, 0], s0_ref[0, 1], s0_ref[1, 0], s0_ref[1, 1]))
    for hh in range(2):
        sfin_ref[hh, 0] = fin[2 * hh]
        sfin_ref[hh, 1] = fin[2 * hh + 1]

    def out_body(n, _):
        rows = pl.ds(pl.multiple_of(n * c, c), c)
        qr = qr_scr[rows, :]
        kb = kr_scr[rows, :].astype(BF16)
        for hh, hd in enumerate(heads):
            qm = qr * hd["hm"]
            a = lax.dot_general(qm.astype(BF16), kb, (((1,), (1,)), ((), ())),
                                preferred_element_type=F32) * hd["dmat"]
            o = jnp.dot(a.astype(BF16), v_ref[rows, hd["vcols"]], preferred_element_type=F32)
            qx = jnp.concatenate([qm * hd["xi_f"], qm * hd["xi_b"]], axis=1).astype(BF16)
            o = o + jnp.dot(qx, kv_scr[hh, n].astype(BF16), preferred_element_type=F32)
            o = o * lax.rsqrt(jnp.mean(o * o, axis=-1, keepdims=True) + EPS)
            g = g_ref[rows, hd["vcols"]].astype(F32)
            o_ref[rows, hd["vcols"]] = (g * jax.nn.sigmoid(g) * o).astype(o_ref.dtype)
        return 0

    lax.fori_loop(0, nchunk, out_body, 0, unroll=2)


def _retention(z, log_decay, cos, sin, s0, *, nb, seq, row0, rope, out_rows, y_prev=None):
    rb0 = row0 // seq
    hpairs = RET_HEADS // 2
    kcol, vcol, qcol, gcol = 0, 512 // 256, 3584 // LANES, 4096 // 256
    has_prev = y_prev is not None
    kernel = functools.partial(_ret_kernel, rope=rope, has_prev=has_prev)
    in_specs = [
        pl.BlockSpec(memory_space=pltpu.SMEM),
        pl.BlockSpec((seq, LANES), lambda b, p: (rb0 + b, qcol + p)),
        pl.BlockSpec((seq, LANES), lambda b, p: (rb0 + b, kcol + p)),
        pl.BlockSpec((seq, 2 * RET_DV), lambda b, p: (rb0 + b, vcol + p)),
        pl.BlockSpec((seq, 2 * RET_DV), lambda b, p: (rb0 + b, gcol + p)),
        pl.BlockSpec((seq, LANES), lambda b, p: (0, 0)),
        pl.BlockSpec((seq, LANES), lambda b, p: (0, 0)),
        pl.BlockSpec((None, 2, 2, LANES, RET_DV), lambda b, p: (b, p, 0, 0, 0)),
    ]
    args = [log_decay, z, z, z, z, cos, sin, s0]
    if has_prev:
        in_specs.append(pl.BlockSpec(memory_space=pl.ANY))
        args.append(y_prev)
    return pl.pallas_call(
        kernel,
        grid=(nb, hpairs),
        in_specs=in_specs,
        out_specs=[
            pl.BlockSpec((seq, 2 * RET_DV), lambda b, p: (rb0 + b, p)),
            pl.BlockSpec((None, 2, 2, LANES, RET_DV), lambda b, p: (b, p, 0, 0, 0)),
        ],
        out_shape=[
            jax.ShapeDtypeStruct((out_rows, RET_HEADS * RET_DV), BF16),
            jax.ShapeDtypeStruct((nb, RET_HEADS, 2, LANES, RET_DV), F32),
        ],
        scratch_shapes=[
            pltpu.VMEM((seq, LANES), F32),
            pltpu.VMEM((seq, LANES), F32),
            pltpu.VMEM((2, seq // RET_CHUNK, 2 * LANES, RET_DV), F32),
        ],
        input_output_aliases={len(args) - 1: 0} if has_prev else {},
        compiler_params=_cparams("parallel", "arbitrary"),
        name="retention_rope" if rope else "retention",
    )(*args)


NA_GROUP = 4
NA_KROWS = 12


def _na_kernel(rpb_ref, q_ref, k_ref, v_ref, kc_ref, vc_ref, o_ref, bias_scr):
    seq = q_ref.shape[0]
    nrow = seq // GRID_W
    ndr, ndc = 2 * NA_WR - 1, 2 * NA_WC - 1
    scale = NA_DH ** -0.5
    nt = (((1,), (1,)), ((), ()))
    h = pl.program_id(0)
    lane = lax.broadcasted_iota(jnp.int32, (GRID_W, LANES), 1)

    @pl.when(pl.program_id(1) == 0)
    def _():
        qi = lax.broadcasted_iota(jnp.int32, (GRID_W, LANES), 0)
        ki = lane & (GRID_W - 1)
        dc = ki - qi + (NA_WC - 1)
        cs = jnp.clip(qi - NA_WC // 2, 0, GRID_W - NA_WC)
        colmask = (ki >= cs) & (ki < cs + NA_WC)

        def toeplitz(d):
            t = jnp.zeros((GRID_W, LANES), F32)
            for j in range(ndc):
                t = jnp.where(dc == j, rpb_ref[h * (ndr * ndc) + d * ndc + j], t)
            return t

        prev = jnp.zeros((GRID_W, LANES), F32)
        for i in range(ndr + 1):
            nxt = toeplitz(i) if i < ndr else jnp.zeros((GRID_W, LANES), F32)
            bias_scr[i] = jnp.where(colmask, jnp.where(lane < GRID_W, prev, nxt), NEG)
            prev = nxt

    kc = kc_ref[...]
    vc = vc_ref[...]
    kwin = NA_KROWS * GRID_W

    half = NA_WR // 2
    ntile = kwin // LANES

    def group(r0, interior):
        if interior:
            us = r0 - half
            qrows = pl.ds(pl.multiple_of(r0 * GRID_W, NA_GROUP * GRID_W), NA_GROUP * GRID_W)
            krows = pl.ds(pl.multiple_of(us * GRID_W, GRID_W), kwin)
        else:
            us = min(max(r0 - half, 0), nrow - NA_KROWS)
            qrows = pl.ds(r0 * GRID_W, NA_GROUP * GRID_W)
            krows = pl.ds(us * GRID_W, kwin)
        q = q_ref[qrows, :]
        s = lax.dot_general(q, k_ref[krows, :], nt, preferred_element_type=F32)
        sc = lax.dot_general(q, kc, nt, preferred_element_type=F32) * scale
        e_rows, ec_rows, den_rows = [], [], []
        for u in range(NA_GROUP):
            qr = slice(u * GRID_W, (u + 1) * GRID_W)
            tiles = []
            for m in range(ntile):
                if interior:
                    off, idx = 2 * m - u, 2 * m - u + half
                else:
                    r = r0 + u
                    off = us + 2 * m - min(max(r - half, 0), nrow - NA_WR)
                    idx = min(max(us + 2 * m - r + NA_WR, 0), ndr)
                ok_lo, ok_hi = 0 <= off < NA_WR, 0 <= off + 1 < NA_WR
                if not (ok_lo or ok_hi):
                    tiles.append(None)
                    continue
                tab = bias_scr[idx]
                if not (ok_lo and ok_hi):
                    tab = tab + jnp.where(lane < GRID_W, 0.0 if ok_lo else NEG, 0.0 if ok_hi else NEG)
                st = s[qr, m * LANES:(m + 1) * LANES]
                tiles.append(jnp.where(tab > 0.5 * NEG, st * scale + tab, NEG))
            live = [t for t in tiles if t is not None]
            tmax = live[0]
            for t in live[1:]:
                tmax = jnp.maximum(tmax, t)
            scu = sc[qr, :]
            mx = jnp.maximum(jnp.max(tmax, axis=-1, keepdims=True), jnp.max(scu, axis=-1, keepdims=True))
            etiles = [None if t is None else jnp.exp(t - mx) for t in tiles]
            ecu = jnp.exp(scu - mx)
            esum = None
            for t in etiles:
                if t is not None:
                    esum = t if esum is None else esum + t
            den_rows.append(jnp.sum(esum, axis=-1, keepdims=True) + jnp.sum(ecu, axis=-1, keepdims=True))
            zero = jnp.zeros((GRID_W, LANES), BF16)
            e_rows.append(jnp.concatenate([zero if t is None else t.astype(BF16) for t in etiles], axis=1))
            ec_rows.append(ecu.astype(BF16))
        o = jnp.dot(jnp.concatenate(e_rows, axis=0), v_ref[krows, :], preferred_element_type=F32)
        o = o + jnp.dot(jnp.concatenate(ec_rows, axis=0), vc, preferred_element_type=F32)
        o_ref[qrows, :] = (o / jnp.concatenate(den_rows, axis=0)).astype(o_ref.dtype)

    ngroup = nrow // NA_GROUP
    g_lo = -(-half // NA_GROUP)
    g_hi = min((nrow - NA_KROWS + half) // NA_GROUP, (nrow - NA_WR + half - NA_GROUP + 1) // NA_GROUP)
    for g in range(g_lo):
        group(g * NA_GROUP, False)

    def body(i, _):
        g = g_lo + 2 * i
        group(g * NA_GROUP, True)
        group((g + 1) * NA_GROUP, True)
        return 0

    npair = (g_hi + 1 - g_lo) // 2
    lax.fori_loop(0, npair, body, 0)
    for g in range(g_lo + 2 * npair, g_hi + 1):
        group(g * NA_GROUP, True)
    for g in range(g_hi + 1, ngroup):
        group(g * NA_GROUP, False)


def _na_attention(z, rpb, *, nb, seq, cseq, crow0, out_rows):
    kcol, vcol, qcol = 1536 // LANES, 2560 // LANES, 5120 // LANES
    crb0 = crow0 // cseq
    nrow = seq // GRID_W
    assert nrow >= NA_KROWS and nrow % NA_GROUP == 0 and GRID_W * 2 == LANES
    assert NA_KROWS % 2 == 0 and NA_KROWS >= NA_WR + NA_GROUP - 1
    return pl.pallas_call(
        _na_kernel,
        grid=(NA_HEADS, nb),
        in_specs=[
            pl.BlockSpec(memory_space=pltpu.SMEM),
            pl.BlockSpec((seq, NA_DH), lambda h, b: (b, qcol + h)),
            pl.BlockSpec((seq, NA_DH), lambda h, b: (b, kcol + h)),
            pl.BlockSpec((seq, NA_DH), lambda h, b: (b, vcol + h)),
            pl.BlockSpec((cseq, NA_DH), lambda h, b: (crb0 + b, kcol + h)),
            pl.BlockSpec((cseq, NA_DH), lambda h, b: (crb0 + b, vcol + h)),
        ],
        out_specs=pl.BlockSpec((seq, NA_DH), lambda h, b: (b, h)),
        out_shape=jax.ShapeDtypeStruct((out_rows, NA_HEADS * NA_DH), BF16),
        scratch_shapes=[pltpu.VMEM((2 * NA_WR, GRID_W, LANES), F32)],
        compiler_params=_cparams("arbitrary", "arbitrary"),
        name="na_attention",
    )(rpb.reshape(-1), z, z, z, z, z)


def _dense_attn_kernel(q_ref, k_ref, v_ref, yprev_ref, o_ref):
    s = lax.dot_general(q_ref[...], k_ref[...], (((1,), (1,)), ((), ())), preferred_element_type=F32) * NA_DH ** -0.5
    e = jnp.exp(s - jnp.max(s, axis=-1, keepdims=True))
    o = jnp.dot(e.astype(BF16), v_ref[...], preferred_element_type=F32)
    o_ref[...] = (o / jnp.sum(e, axis=-1, keepdims=True)).astype(o_ref.dtype)


def _dense_attention(z, y_prev, *, nb, cseq, crow0):
    kcol, vcol, qcol = 1536 // LANES, 2560 // LANES, 5120 // LANES
    crb0 = crow0 // cseq
    return pl.pallas_call(
        _dense_attn_kernel,
        grid=(nb, NA_HEADS),
        in_specs=[
            pl.BlockSpec((cseq, NA_DH), lambda b, h: (crb0 + b, qcol + h)),
            pl.BlockSpec((cseq, NA_DH), lambda b, h: (crb0 + b, kcol + h)),
            pl.BlockSpec((cseq, NA_DH), lambda b, h: (crb0 + b, vcol + h)),
            pl.BlockSpec(memory_space=pl.ANY),
        ],
        out_specs=pl.BlockSpec((cseq, NA_DH), lambda b, h: (crb0 + b, h)),
        out_shape=jax.ShapeDtypeStruct(y_prev.shape, BF16),
        input_output_aliases={3: 0},
        compiler_params=_cparams("parallel", "arbitrary"),
        name="dense_attention",
    )(z, z, z, y_prev)


HY_FEAT_ROWS = 64
HIGHEST = lax.Precision.HIGHEST


def _hy_features(seq):
    t = np.linspace(0.0, 1.0, seq)
    bands = (HY_EMB - 1) // 2
    w = 2.0 * math.pi * np.arange(seq) / seq
    fr = np.linspace(1e-4, bands - 1, bands)
    ang = fr[None] * w[:, None]
    feat = np.concatenate([t[:, None], np.cos(ang), -np.sin(ang)], axis=-1)
    src = np.concatenate([np.arange(seq), np.zeros(1, np.int64), np.arange(seq - 1, 0, -1)])
    feat2 = np.zeros((HY_FEAT_ROWS, 2 * seq), np.float32)
    feat2[:HY_EMB] = feat[src].T
    tt = t[src][None].astype(np.float32)
    mask = np.ones((1, 2 * seq), np.float32)
    mask[0, seq] = 0.0
    return feat2, tt, mask


def _hy_mlp_kernel(feat_ref, w1_ref, b1_ref, w2_ref, b2_ref, fq_ref, o_ref):
    fq = fq_ref[...]
    h = jnp.dot(w1_ref[...], feat_ref[...], preferred_element_type=F32, precision=HIGHEST)
    h = jnp.sin(fq * (h + b1_ref[...]))
    h = jnp.dot(w2_ref[...], h, preferred_element_type=F32, precision=HIGHEST)
    o_ref[...] = jnp.sin(fq * (h + b2_ref[...]))


def _hy_mlp(feat2, w1t, b1, w2t, b2, fq):
    depth, hid, _ = w1t.shape
    n = feat2.shape[1]
    col = lambda a: a.reshape(depth, hid, 1)
    wspec = lambda k: pl.BlockSpec((None, hid, k), lambda l: (l, 0, 0))
    return pl.pallas_call(
        _hy_mlp_kernel,
        grid=(depth,),
        in_specs=[pl.BlockSpec((HY_FEAT_ROWS, n), lambda l: (0, 0)), wspec(HY_FEAT_ROWS), wspec(1), wspec(hid),
                  wspec(1), wspec(1)],
        out_specs=pl.BlockSpec((None, hid, n), lambda l: (l, 0, 0)),
        out_shape=jax.ShapeDtypeStruct((depth, hid, n), F32),
        compiler_params=_cparams("parallel"),
        name="hyena_filter_mlp",
    )(feat2, w1t, col(b1), w2t, col(b2), col(fq))


def _hy_filter_kernel(h_ref, w3_ref, dl_ref, tt_ref, mask_ref, o_ref):
    seq = h_ref.shape[1] // 2
    kf = jnp.dot(w3_ref[0], h_ref[:, :seq], preferred_element_type=F32, precision=HIGHEST)
    kb = jnp.dot(w3_ref[1], h_ref[:, seq:], preferred_element_type=F32, precision=HIGHEST)
    k = jnp.concatenate([kf, kb], axis=1) * (jnp.exp(-tt_ref[...] * dl_ref[...]) * mask_ref[...])
    o_ref[...] = (k * lax.rsqrt(jnp.sum(k * k, axis=1, keepdims=True) + EPS)).astype(o_ref.dtype)


def _hy_filter(h2, w3t, deltas, tt, mask, cb):
    depth, hid, n = h2.shape
    c = w3t.shape[3]
    return pl.pallas_call(
        _hy_filter_kernel,
        grid=(depth, HY_ORDER, c // cb),
        in_specs=[
            pl.BlockSpec((None, hid, n), lambda l, o, j: (l, 0, 0)),
            pl.BlockSpec((None, None, 2, cb, hid), lambda l, o, j: (l, o, 0, j, 0)),
            pl.BlockSpec((cb, 1), lambda l, o, j: (j, 0)),
            pl.BlockSpec((1, n), lambda l, o, j: (0, 0)),
            pl.BlockSpec((1, n), lambda l, o, j: (0, 0)),
        ],
        out_specs=pl.BlockSpec((None, None, cb, n), lambda l, o, j: (l, o, j, 0)),
        out_shape=jax.ShapeDtypeStruct((depth, HY_ORDER, c, n), BF16),
        compiler_params=_cparams("parallel", "parallel", "arbitrary"),
        name="hyena_filter",
    )(h2, w3t, deltas, tt, mask)


FFT_NO = 64
FFT_NI = 128


def _fft_tables():
    n = FFT_NO * FFT_NI
    a = np.arange(FFT_NO)
    fo = np.exp(-2j * np.pi * np.outer(a, a) / FFT_NO)
    i = np.arange(FFT_NI)
    ci = np.exp(-2j * np.pi * np.outer(i, i) / FFT_NI)
    tw = np.exp(-2j * np.pi * np.outer(a, i) / n)
    half = FFT_NO // 2
    f32 = lambda x: np.ascontiguousarray(x, dtype=np.float32)
    g1_real = f32(np.concatenate([fo.real, fo.imag], axis=0))
    g1 = f32(np.block([[fo.real[:, :half], -fo.imag[:, :half]], [fo.imag[:, :half], fo.real[:, :half]]]))
    w2 = f32(np.block([[ci.real, ci.imag], [-ci.imag, ci.real]]))
    w2i = f32(np.block([[ci.real, -ci.imag], [ci.imag, ci.real]]))
    g4 = f32(np.block([[fo.real[:half], fo.imag[:half]], [-fo.imag[:half], fo.real[:half]]]) / n)
    return dict(g1_real=g1_real, g1=g1, w2=w2, w2i=w2i, g4=g4, twr=f32(tw.real), twi=f32(tw.imag))


HY_CHUNK = 8


def _fft_stage1(g1, y2, twr, twi):
    o1 = jnp.dot(g1, y2, preferred_element_type=F32)
    yr, yi = o1[:FFT_NO], o1[FFT_NO:]
    ar = (yr * twr - yi * twi).astype(BF16)
    ai = (yr * twi + yi * twr).astype(BF16)
    lanes = lambda ci: slice(ci * FFT_NI, (ci + 1) * FFT_NI)
    return jnp.concatenate([jnp.concatenate([ar[:, lanes(ci)], ai[:, lanes(ci)]], axis=1)
                            for ci in range(HY_CHUNK)], axis=0)


def _hy_spectrum_kernel(k_ref, g1_ref, w2_ref, twr_ref, twi_ref, o_ref):
    cb = k_ref.shape[0]
    g1 = g1_ref[...]
    w2 = w2_ref[...]
    twr = twr_ref[...]
    twi = twi_ref[...]

    def body(j, _):
        chans = pl.ds(pl.multiple_of(j * HY_CHUNK, HY_CHUNK), HY_CHUNK)
        ks = k_ref[chans]
        k2 = jnp.concatenate([ks[ci] for ci in range(HY_CHUNK)], axis=1).astype(BF16)
        z = jnp.dot(_fft_stage1(g1, k2, twr, twi), w2, preferred_element_type=F32)
        o_ref[chans] = z.reshape(HY_CHUNK, FFT_NO, 2 * FFT_NI)
        return 0

    lax.fori_loop(0, cb // HY_CHUNK, body, 0)


def _tiled_twiddles(tabs):
    return (jnp.asarray(np.tile(tabs["twr"], (1, HY_CHUNK))), jnp.asarray(np.tile(tabs["twi"], (1, HY_CHUNK))))


def _hy_spectrum(kfilt, tabs, cb):
    g, c = kfilt.shape[:2]
    const = lambda a: pl.BlockSpec(a.shape, lambda i, j: (0,) * a.ndim)
    g1 = jnp.asarray(tabs["g1_real"], BF16)
    w2 = jnp.asarray(tabs["w2"], BF16)
    twr, twi = _tiled_twiddles(tabs)
    return pl.pallas_call(
        _hy_spectrum_kernel,
        grid=(g, c // cb),
        in_specs=[pl.BlockSpec((None, cb, FFT_NO, FFT_NI), lambda i, j: (i, j, 0, 0)),
                  const(g1), const(w2), const(twr), const(twi)],
        out_specs=pl.BlockSpec((None, cb, FFT_NO, 2 * FFT_NI), lambda i, j: (i, j, 0, 0)),
        out_shape=jax.ShapeDtypeStruct((g, c, FFT_NO, 2 * FFT_NI), F32),
        compiler_params=_cparams("parallel", "arbitrary"),
        name="hyena_spectrum",
    )(kfilt, g1, w2, twr, twi)


def _shift_conv3(x, w0, w1, w2, b):
    nrow, nlane = x.shape
    row = lax.broadcasted_iota(jnp.int32, x.shape, 0)
    lane = lax.broadcasted_iota(jnp.int32, x.shape, 1)
    r = pltpu.roll(x, 1, axis=1)
    prev = jnp.where(lane == 0, pltpu.roll(r, 1, axis=0), r)
    prev = jnp.where((lane == 0) & (row == 0), 0.0, prev)
    r = pltpu.roll(x, nlane - 1, axis=1)
    nxt = jnp.where(lane == nlane - 1, pltpu.roll(r, nrow - 1, axis=0), r)
    nxt = jnp.where((lane == nlane - 1) & (row == nrow - 1), 0.0, nxt)
    return prev * w0 + x * w1 + nxt * w2 + b


def _hyena_kernel(x_ref, kf_ref, pm_ref, g1_ref, w2_ref, w2i_ref, g4_ref, twr_ref, twi_ref, o_ref):
    cb = x_ref.shape[2]
    half = FFT_NO // 2
    ch = HY_CHUNK
    g1 = g1_ref[...]
    g4 = g4_ref[...]
    w2 = w2_ref[...]
    w2i = w2i_ref[...]
    twr = twr_ref[...]
    twi = twi_ref[...]
    lanes = lambda ci: slice(ci * FFT_NI, (ci + 1) * FFT_NI)
    krows = lambda ci: slice(ci * FFT_NO, (ci + 1) * FFT_NO)

    def body(j, _):
        c0 = pl.multiple_of(j * ch, ch)
        pm = pm_ref[:, pl.ds(c0, ch)]
        conv = []
        for part in range(3):
            halves = []
            for bb in range(2):
                xs = x_ref[bb, part, pl.ds(c0, ch)].astype(F32)
                halves.append(jnp.concatenate(
                    [_shift_conv3(xs[ci], pm[part, ci], pm[3 + part, ci], pm[6 + part, ci], pm[9 + part, ci])
                     for ci in range(ch)], axis=1))
            conv.append(jnp.concatenate(halves, axis=0))
        y = conv[2]
        for order in range(HY_ORDER):
            z = jnp.dot(_fft_stage1(g1, y.astype(BF16), twr, twi), w2, preferred_element_type=F32)
            kf = kf_ref[order, pl.ds(c0, ch)].reshape(ch * FFT_NO, 2 * FFT_NI)
            zr, zi = z[:, :FFT_NI], z[:, FFT_NI:]
            kr, ki = kf[:, :FFT_NI], kf[:, FFT_NI:]
            p = jnp.concatenate([zr * kr - zi * ki, zr * ki + zi * kr], axis=1).astype(BF16)
            q = jnp.dot(p, w2i, preferred_element_type=F32)
            qr = jnp.concatenate([q[krows(ci), :FFT_NI] for ci in range(ch)], axis=1)
            qi = jnp.concatenate([q[krows(ci), FFT_NI:] for ci in range(ch)], axis=1)
            qs = jnp.concatenate([qr * twr + qi * twi, qi * twr - qr * twi], axis=0).astype(BF16)
            cv = jnp.dot(g4, qs, preferred_element_type=F32)
            bias = jnp.concatenate([pm[12 + order, ci] for ci in range(ch)], axis=1)
            y = conv[order] * (cv + bias * y)
        for ci in range(ch):
            o_ref[0, c0 + ci] = y[:half, lanes(ci)].astype(o_ref.dtype)
            o_ref[1, c0 + ci] = y[half:, lanes(ci)].astype(o_ref.dtype)
        return 0

    lax.fori_loop(0, cb // ch, body, 0)


def _hyena(xt, kf, layer, pm, tabs, cb):
    nb, _, c, half, _ = xt.shape
    const = lambda a: pl.BlockSpec(a.shape, lambda j, p: (0,) * a.ndim)
    bf = lambda name: jnp.asarray(tabs[name], BF16)
    g1, w2, w2i, g4 = bf("g1"), bf("w2"), bf("w2i"), bf("g4")
    twr, twi = _tiled_twiddles(tabs)
    return pl.pallas_call(
        _hyena_kernel,
        grid=(c // cb, nb // 2),
        in_specs=[
            pl.BlockSpec((2, 3, cb, half, FFT_NI), lambda j, p: (p, 0, j, 0, 0)),
            pl.BlockSpec((None, HY_ORDER, cb, FFT_NO, 2 * FFT_NI), lambda j, p: (layer, 0, j, 0, 0)),
            pl.BlockSpec((14, cb, 1, FFT_NI), lambda j, p: (0, j, 0, 0)),
            const(g1), const(w2), const(w2i), const(g4), const(twr), const(twi),
        ],
        out_specs=pl.BlockSpec((2, cb, half, FFT_NI), lambda j, p: (p, j, 0, 0)),
        out_shape=jax.ShapeDtypeStruct((nb, c, half, FFT_NI), BF16),
        compiler_params=_cparams("parallel", "arbitrary"),
        name="hyena",
    )(xt, kf, pm, g1, w2, w2i, g4, twr, twi)


def _dft_tables(seq):
    n = 2 * seq
    wmat = np.exp(-2j * np.pi * np.outer(np.arange(n), np.arange(n)) / n)
    f32 = lambda x: np.ascontiguousarray(x, dtype=np.float32)
    fwd_real = f32(np.concatenate([wmat.real, wmat.imag], axis=1))
    ws = wmat[:seq]
    fwd = f32(np.block([[ws.real, ws.imag], [-ws.imag, ws.real]]))
    wi = np.conj(wmat)[:, :seq] / n
    inv = f32(np.block([[wi.real, wi.imag], [-wi.imag, wi.real]]))
    return dict(fwd_real=fwd_real, fwd=fwd, inv=inv)


def _rowdft_kernel(x_ref, w_ref, o_ref):
    o_ref[...] = jnp.dot(x_ref[...].astype(BF16), w_ref[...], preferred_element_type=F32)


def _rowdft(x, w, tr):
    m, k = x.shape
    n = w.shape[1]
    return pl.pallas_call(
        _rowdft_kernel,
        grid=(m // tr,),
        in_specs=[pl.BlockSpec((tr, k), lambda i: (i, 0)), pl.BlockSpec((k, n), lambda i: (0, 0))],
        out_specs=pl.BlockSpec((tr, n), lambda i: (i, 0)),
        out_shape=jax.ShapeDtypeStruct((m, n), F32),
        compiler_params=_cparams("parallel"),
        name="row_dft",
    )(x, w)


def _lane_conv3(x, w0, w1, w2, b):
    seq = x.shape[1]
    lane = lax.broadcasted_iota(jnp.int32, x.shape, 1)
    prev = jnp.where(lane == 0, 0.0, pltpu.roll(x, 1, axis=1))
    nxt = jnp.where(lane == seq - 1, 0.0, pltpu.roll(x, seq - 1, axis=1))
    return prev * w0 + x * w1 + nxt * w2 + b


def _hyena_ctx_kernel(x_ref, kf_ref, pm_ref, fwd_ref, inv_ref, o_ref):
    seq = x_ref.shape[3]
    n = 2 * seq
    conv = [[_lane_conv3(x_ref[bb, part].astype(F32), pm_ref[part], pm_ref[3 + part], pm_ref[6 + part],
                         pm_ref[9 + part]) for bb in range(2)] for part in range(3)]
    ya, yb = conv[2]
    for order in range(HY_ORDER):
        z = jnp.dot(jnp.concatenate([ya, yb], axis=1).astype(BF16), fwd_ref[...], preferred_element_type=F32)
        kf = kf_ref[order]
        zr, zi, kr, ki = z[:, :n], z[:, n:], kf[:, :n], kf[:, n:]
        p = jnp.concatenate([zr * kr - zi * ki, zr * ki + zi * kr], axis=1).astype(BF16)
        cv = jnp.dot(p, inv_ref[...], preferred_element_type=F32)
        bias = pm_ref[12 + order]
        ya = conv[order][0] * (cv[:, :seq] + bias * ya)
        yb = conv[order][1] * (cv[:, seq:] + bias * yb)
    o_ref[0] = ya.astype(o_ref.dtype)
    o_ref[1] = yb.astype(o_ref.dtype)


def _hyena_ctx(xt, kf, pm, tabs, cb):
    nb, _, c, seq = xt.shape
    fwd, inv = jnp.asarray(tabs["fwd"], BF16), jnp.asarray(tabs["inv"], BF16)
    const = lambda a: pl.BlockSpec(a.shape, lambda j, p: (0,) * a.ndim)
    return pl.pallas_call(
        _hyena_ctx_kernel,
        grid=(c // cb, nb // 2),
        in_specs=[
            pl.BlockSpec((2, 3, cb, seq), lambda j, p: (p, 0, j, 0)),
            pl.BlockSpec((HY_ORDER, cb, 4 * seq), lambda j, p: (0, j, 0)),
            pl.BlockSpec((14, cb, 1), lambda j, p: (0, j, 0)),
            const(fwd), const(inv),
        ],
        out_specs=pl.BlockSpec((2, cb, seq), lambda j, p: (p, j, 0)),
        out_shape=jax.ShapeDtypeStruct((nb, c, seq), BF16),
        compiler_params=_cparams("parallel", "arbitrary"),
        name="hyena_ctx",
    )(xt, kf, pm, fwd, inv)


def _rope_tables(seq):
    t = np.arange(seq)
    pos = np.stack([t // GRID_W, t % GRID_W], axis=1).astype(np.float64)
    n = RET_DK // 4
    inv = ROPE_BASE ** (-np.arange(n, dtype=np.float64) / n)
    lane = np.arange(LANES) % RET_DK
    ang = pos[:, lane // (2 * n)] * inv[lane % n][None]
    sign = np.where(lane % (2 * n) < n, -1.0, 1.0)
    return np.cos(ang).astype(np.float32), (np.sin(ang) * sign[None]).astype(np.float32)


def _hy_filters(seq, hy_f_w1, hy_f_b1, hy_f_w2, hy_f_b2, hy_f_w3, hy_f_freq):
    depth, _, hid = hy_f_w1.shape
    c = hy_f_w3.shape[2] // (2 * HY_ORDER)
    feat2, tt, mask = _hy_features(seq)
    w1t = jnp.pad(hy_f_w1.transpose(0, 2, 1), ((0, 0), (0, 0), (0, HY_FEAT_ROWS - HY_EMB)))
    h2 = _hy_mlp(jnp.asarray(feat2), w1t, hy_f_b1, hy_f_w2.transpose(0, 2, 1), hy_f_b2, hy_f_freq)
    w3t = hy_f_w3.reshape(depth, hid, HY_ORDER, 2, c).transpose(0, 2, 3, 4, 1)
    deltas = np.abs(np.linspace(math.log(HY_TARGET) / HY_FAST, math.log(HY_TARGET) / HY_SLOW, c))
    deltas = jnp.asarray(deltas.astype(np.float32).reshape(c, 1))
    return _hy_filter(h2, w3t, deltas, jnp.asarray(tt), jnp.asarray(mask), min(c, 8 * 1024 * 128 // (2 * seq)))


def _hy_params(hy_conv_w, hy_conv_b, hy_bias):
    c = hy_bias.shape[1]
    return jnp.concatenate([hy_conv_w.reshape(9, c), hy_conv_b.reshape(3, c), hy_bias], axis=0)


def kernel(x, c, ctx, c_ctx, w_mod, b_mod, g_norm1, g_norm2, w_in, hy_conv_w, hy_conv_b, hy_f_w1, hy_f_b1, hy_f_w2, hy_f_b2, hy_f_w3, hy_f_freq, hy_bias, ret_log_decay, na_rpb, w_branch, w_out, ffn_w_in, ffn_conv_w, ffn_conv_b, ffn_w_out, g_final):
    nb, seq, d = x.shape
    cseq = ctx.shape[1]
    depth = w_mod.shape[0]
    hy_w = hy_bias.shape[2]
    t_lat, t_ctx = nb * seq, nb * cseq
    assert seq % TM == 0 and t_ctx % TM == 0 and TM % cseq == 0 and nb % 2 == 0 and nb < SUBLANES
    assert 2 * seq == FFT_NO * FFT_NI and seq % GRID_W == 0 and cseq & (cseq - 1) == 0
    blk_lat, blk_all = t_lat // TM, (t_lat + t_ctx) // TM
    mrow = jnp.asarray(np.concatenate([np.repeat(np.arange(nb), seq // TM), np.full(t_ctx // TM, nb)]), jnp.int32)
    seqlen = jnp.asarray(np.concatenate([np.full(blk_lat, seq), np.full(t_ctx // TM, cseq)]), jnp.int32)
    per = TM // TR
    mrow_r = jnp.repeat(mrow, per)
    tn = 1024

    cond = jnp.zeros((SUBLANES, d), F32).at[:nb].set(c).at[nb].set(c_ctx)
    mods = _mods(cond, w_mod, b_mod).reshape(depth, SUBLANES, 6, 1, d)

    cos, sin = (jnp.asarray(a) for a in _rope_tables(seq))
    tabs = _fft_tables()
    ctabs = _dft_tables(cseq)
    filt = _hy_filters(seq, hy_f_w1, hy_f_b1, hy_f_w2, hy_f_b2, hy_f_w3, hy_f_freq)
    spec = _hy_spectrum(filt.reshape(depth * HY_ORDER, hy_w, FFT_NO, FFT_NI), tabs, 32)
    spec = spec.reshape(depth, HY_ORDER, hy_w, FFT_NO, 2 * FFT_NI)
    cfilt = _hy_filters(cseq, hy_f_w1, hy_f_b1, hy_f_w2, hy_f_b2, hy_f_w3, hy_f_freq)
    cspec = _rowdft(cfilt.reshape(depth * HY_ORDER * hy_w, 2 * cseq), jnp.asarray(ctabs["fwd_real"], BF16), 1024)
    cspec = cspec.reshape(depth, HY_ORDER, hy_w, 4 * cseq)

    xs = jnp.concatenate([x.reshape(t_lat, d), ctx.reshape(t_ctx, d)], axis=0)
    hy0 = 6144
    gate0 = hy0 + 3 * hy_w
    s_zero = jnp.zeros((nb, RET_HEADS, 2, LANES, RET_DV), F32)
    hn = _normmod(xs, g_norm1[0], mods[0], mrow, 0, 1, blk_all)
    for l in range(depth):
        last = l == depth - 1
        nblk = blk_lat if last else blk_all
        rows = nblk * TM
        z = _matmul(hn, w_in, l, blk_all, tn, w_in.shape[2] - 3 * hy_w, skip=(hy0 // tn, 3 * hy_w // tn))
        hy_proj = functools.partial(_matmul_nt, hn, w_in, l, col0=hy0, n_out=3 * hy_w, nb=nb, tc=512)
        zh = hy_proj(blk0=0, nblk=blk_lat, seq=seq)

        y_ret, s_ctx = _retention(z, ret_log_decay[l], cos, sin, s_zero, nb=nb, seq=cseq, row0=t_lat, rope=False,
                                  out_rows=blk_all * TM)
        y_ret, _ = _retention(z, ret_log_decay[l], cos, sin, s_ctx, nb=nb, seq=seq, row0=0, rope=True,
                              out_rows=blk_all * TM, y_prev=y_ret)
        y_na = _na_attention(z, na_rpb[l], nb=nb, seq=seq, cseq=cseq, crow0=t_lat, out_rows=rows)
        pm = _hy_params(hy_conv_w[l], hy_conv_b[l], hy_bias[l])
        pm_lat = jnp.broadcast_to(pm[:, :, None, None], (14, hy_w, 1, FFT_NI))
        y_hy = _hyena(zh.reshape(nb, 3, hy_w, FFT_NO // 2, FFT_NI), spec, l, pm_lat, tabs, 32)
        y_hy = y_hy.reshape(nb, hy_w, seq).transpose(0, 2, 1).reshape(t_lat, hy_w)
        if not last:
            y_na = _dense_attention(z, y_na, nb=nb, cseq=cseq, crow0=t_lat)
            zc = hy_proj(blk0=blk_lat, nblk=t_ctx // TM, seq=cseq)
            yc_hy = _hyena_ctx(zc.reshape(nb, 3, hy_w, cseq), cspec[l], pm[:, :, None], ctabs, 256)
            y_hy = jnp.concatenate([y_hy, yc_hy.transpose(0, 2, 1).reshape(t_ctx, hy_w)], axis=0)
        acc = _merge((y_hy, y_ret, y_na), z, hy0, w_branch[l].astype(BF16), nblk, 1024)
        xs, hn = _out_proj(acc, w_out[l].astype(BF16), xs, mods[l], mrow_r, g_norm2[l], nblk * per)
        u = _matmul(hn, ffn_w_in, l, nblk, tn, ffn_w_in.shape[2])
        hg = _ffn_gate(u, ffn_conv_w[l], ffn_conv_b[l], seqlen, nblk, 512)
        nxt = l if last else l + 1
        w_ffo = ffn_w_out[l].astype(BF16)
        if last:
            res = _ffn_out(hg, w_ffo, xs, mods[l], mods[l], g_final, mrow_r, nblk * per, TR, 512, True)
        else:
            res = _ffn_out(hg, w_ffo, xs, mods[l], mods[nxt], g_norm1[nxt], mrow, nblk, TM, 256, False)
        if not last:
            xs, hn = res
    return res[0].reshape(nb, seq, d)
```

```python
import functools
import math

import numpy as np
import jax
import jax.numpy as jnp
from jax import lax
from jax.experimental import pallas as pl
from jax.experimental.pallas import tpu as pltpu

F32 = jnp.float32
BF16 = jnp.bfloat16

GRID_W = 64
N_BRANCH = 3
HY_ORDER = 2
HY_EMB = 33
HY_FAST = 0.3
HY_SLOW = 1.5
HY_TARGET = 1e-2
RET_HEADS = 8
RET_DK = 64
RET_DV = 128
RET_CHUNK = 128
NA_HEADS = 8
NA_DH = 128
NA_WR = 8
NA_WC = 16
ROPE_BASE = 10000.0
EPS = 1e-6
NEG = -1e30
HY_W = 1024

_RET_QK_W, _RET_V_W, _NA_W = RET_HEADS * RET_DK, RET_HEADS * RET_DV, NA_HEADS * NA_DH
COL_RET_K = 0
COL_RET_V = COL_RET_K + _RET_QK_W
COL_NA_K = COL_RET_V + _RET_V_W
COL_NA_V = COL_NA_K + _NA_W
COL_RET_Q = COL_NA_V + _NA_W
COL_RET_G = COL_RET_Q + _RET_QK_W
COL_NA_Q = COL_RET_G + _RET_V_W
COL_HY = COL_NA_Q + _NA_W
COL_GATE = COL_HY + (HY_ORDER + 1) * HY_W

LANES = 128
SUBLANES = 8
VMEM_LIMIT = 56 * 1024 * 1024

TM = 1024


def _cparams(*sem):
    return pltpu.CompilerParams(dimension_semantics=sem, vmem_limit_bytes=VMEM_LIMIT)


def _mods_kernel(a_ref, w_ref, b_ref, o_ref):
    a = a_ref[...]
    a = a * jax.nn.sigmoid(a)
    o_ref[...] = jnp.dot(a.astype(BF16), w_ref[...].astype(BF16), preferred_element_type=F32) + b_ref[...]


def _mods(cond, w_mod, b_mod):
    depth, d, n = w_mod.shape
    tn = 1024
    return pl.pallas_call(
        _mods_kernel,
        grid=(depth, n // tn),
        in_specs=[
            pl.BlockSpec((SUBLANES, d), lambda l, j: (0, 0)),
            pl.BlockSpec((None, d, tn), lambda l, j: (l, 0, j)),
            pl.BlockSpec((None, 1, tn), lambda l, j: (l, 0, j)),
        ],
        out_specs=pl.BlockSpec((None, SUBLANES, tn), lambda l, j: (l, 0, j)),
        out_shape=jax.ShapeDtypeStruct((depth, SUBLANES, n), F32),
        compiler_params=_cparams("parallel", "parallel"),
        name="mods",
    )(cond, w_mod, b_mod.reshape(depth, 1, n))


def _normmod_kernel(mrow_ref, x_ref, g_ref, sh_ref, sc_ref, o_ref):
    x = x_ref[...]
    y = x * lax.rsqrt(jnp.mean(x * x, axis=-1, keepdims=True) + EPS)
    y = y * g_ref[...]
    o_ref[...] = (y * (1.0 + sc_ref[...]) + sh_ref[...]).astype(o_ref.dtype)


def _normmod(x, g, mods, mrow, shift_idx, scale_idx, nblk):
    m, d = x.shape
    grid_spec = pltpu.PrefetchScalarGridSpec(
        num_scalar_prefetch=1,
        grid=(nblk,),
        in_specs=[
            pl.BlockSpec((TM, d), lambda i, mr: (i, 0)),
            pl.BlockSpec((1, d), lambda i, mr: (0, 0)),
            pl.BlockSpec((None, None, 1, d), lambda i, mr: (mr[i], shift_idx, 0, 0)),
            pl.BlockSpec((None, None, 1, d), lambda i, mr: (mr[i], scale_idx, 0, 0)),
        ],
        out_specs=pl.BlockSpec((TM, d), lambda i, mr: (i, 0)),
    )
    return pl.pallas_call(
        _normmod_kernel,
        grid_spec=grid_spec,
        out_shape=jax.ShapeDtypeStruct((m, d), BF16),
        compiler_params=_cparams("parallel"),
        name="normmod",
    )(mrow, x, g.reshape(1, d), mods, mods)


def _mm_kernel(a_ref, w_ref, o_ref, w_scr):
    @pl.when(pl.program_id(1) == 0)
    def _():
        w_scr[...] = w_ref[...].astype(BF16)

    o_ref[...] = jnp.dot(a_ref[...], w_scr[...], preferred_element_type=F32).astype(o_ref.dtype)


def _matmul(a, w, layer, nblk, tn, n_out, skip=None):
    m, k = a.shape
    col = (lambda j: j) if skip is None else (lambda j: j + jnp.where(j >= skip[0], skip[1], 0))
    return pl.pallas_call(
        _mm_kernel,
        grid=(n_out // tn, nblk),
        in_specs=[pl.BlockSpec((TM, k), lambda j, i: (i, 0)),
                  pl.BlockSpec((None, k, tn), lambda j, i: (layer, 0, col(j)))],
        out_specs=pl.BlockSpec((TM, tn), lambda j, i: (i, j)),
        out_shape=jax.ShapeDtypeStruct((m, n_out), BF16),
        scratch_shapes=[pltpu.VMEM((k, tn), BF16)],
        compiler_params=_cparams("arbitrary", "arbitrary"),
        name="matmul",
    )(a, w)


def _mm_nt_kernel(w_ref, a_ref, o_ref, wt_scr):
    @pl.when(pl.program_id(1) == 0)
    def _():
        wt_scr[...] = w_ref[...].T.astype(BF16)

    r = lax.dot_general(wt_scr[...], a_ref[...], (((1,), (1,)), ((), ())), preferred_element_type=F32)
    per = o_ref.shape[0]
    width = r.shape[1] // per
    for s in range(per):
        o_ref[s] = r[:, s * width:(s + 1) * width].astype(o_ref.dtype)


def _matmul_nt(a, w, layer, *, col0, n_out, blk0, nblk, nb, seq, tc):
    k = a.shape[1]
    per, sblk = max(TM // seq, 1), max(seq // TM, 1)
    return pl.pallas_call(
        _mm_nt_kernel,
        grid=(n_out // tc, nblk),
        in_specs=[pl.BlockSpec((None, k, tc), lambda j, i: (layer, 0, col0 // tc + j)),
                  pl.BlockSpec((TM, k), lambda j, i: (blk0 + i, 0))],
        out_specs=pl.BlockSpec((per, tc, TM // per), lambda j, i: (i // sblk, j, i % sblk)),
        out_shape=jax.ShapeDtypeStruct((nb, n_out, seq), BF16),
        scratch_shapes=[pltpu.VMEM((tc, k), BF16)],
        compiler_params=_cparams("arbitrary", "arbitrary"),
        name="matmul_nt",
    )(w, a)


TR = 512


def _norm_modulate(x, g, shift, scale):
    y = x * lax.rsqrt(jnp.mean(x * x, axis=-1, keepdims=True) + EPS)
    return (y * g) * (1.0 + scale) + shift


def _outproj_kernel(mrow_ref, a_ref, w_ref, x_ref, gate_ref, g_ref, sh_ref, sc_ref, x_out, h_out):
    y = jnp.dot(a_ref[...], w_ref[...], preferred_element_type=F32)
    x = x_ref[...] + gate_ref[...] * y
    x_out[...] = x
    h_out[...] = _norm_modulate(x, g_ref[...], sh_ref[...], sc_ref[...]).astype(h_out.dtype)


def _out_proj(a, w, x, mods, mrow, g_norm, nblk):
    m, k = a.shape
    d = w.shape[1]
    mod = lambda idx: pl.BlockSpec((None, None, 1, d), lambda i, mr: (mr[i], idx, 0, 0))
    row = pl.BlockSpec((TR, d), lambda i, mr: (i, 0))
    grid_spec = pltpu.PrefetchScalarGridSpec(
        num_scalar_prefetch=1,
        grid=(nblk,),
        in_specs=[pl.BlockSpec((TR, k), lambda i, mr: (i, 0)), pl.BlockSpec((k, d), lambda i, mr: (0, 0)), row,
                  mod(2), pl.BlockSpec((1, d), lambda i, mr: (0, 0)), mod(3), mod(4)],
        out_specs=[row, row],
    )
    return pl.pallas_call(
        _outproj_kernel,
        grid_spec=grid_spec,
        out_shape=[jax.ShapeDtypeStruct((m, d), F32), jax.ShapeDtypeStruct((m, d), BF16)],
        compiler_params=_cparams("parallel"),
        name="out_proj",
    )(mrow, a, w, x, mods, g_norm.reshape(1, d), mods, mods)


def _merge_kernel(y0_ref, y1_ref, y2_ref, g0_ref, g1_ref, g2_ref, w_ref, o_ref):
    acc = None
    for i, (y_ref, g_ref) in enumerate(((y0_ref, g0_ref), (y1_ref, g1_ref), (y2_ref, g2_ref))):
        t = jnp.dot(y_ref[...], w_ref[i], preferred_element_type=F32)
        t = jax.nn.sigmoid(g_ref[...].astype(F32)) * t
        acc = t if acc is None else acc + t
    o_ref[...] = acc.astype(o_ref.dtype)


def _merge(ys, z, gate_col0, w_branch, nblk, tn):
    m, bw = ys[0].shape
    d = w_branch.shape[2]
    gspec = lambda i_br: pl.BlockSpec((TM, tn), lambda i, j: (i, (gate_col0 + i_br * d) // tn + j))
    yspec = pl.BlockSpec((TM, bw), lambda i, j: (i, 0))
    return pl.pallas_call(
        _merge_kernel,
        grid=(nblk, d // tn),
        in_specs=[yspec, yspec, yspec, gspec(0), gspec(1), gspec(2),
                  pl.BlockSpec((N_BRANCH, bw, tn), lambda i, j: (0, 0, j))],
        out_specs=pl.BlockSpec((TM, tn), lambda i, j: (i, j)),
        out_shape=jax.ShapeDtypeStruct((m, d), BF16),
        compiler_params=_cparams("parallel", "arbitrary"),
        name="merge",
    )(ys[0], ys[1], ys[2], z, z, z, w_branch)


def _ffn_gate_kernel(seq_ref, a_ref, ap_ref, an_ref, b_ref, cw_ref, cb_ref, o_ref):
    i = pl.program_id(0)
    seq_m1 = seq_ref[i] - 1
    a = a_ref[...].astype(F32)
    tm = a.shape[0]
    row = lax.broadcasted_iota(jnp.int32, (tm, 1), 0)
    pos = (i * tm + row) & seq_m1
    prev = jnp.where(row == 0, ap_ref[SUBLANES - 1:SUBLANES, :].astype(F32), pltpu.roll(a, 1, axis=0))
    prev = jnp.where(pos == 0, 0.0, prev)
    nxt = jnp.where(row == tm - 1, an_ref[0:1, :].astype(F32), pltpu.roll(a, tm - 1, axis=0))
    nxt = jnp.where(pos == seq_m1, 0.0, nxt)
    cw = cw_ref[...]
    conv = prev * cw[0:1, :] + a * cw[1:2, :] + nxt * cw[2:3, :] + cb_ref[...]
    o_ref[...] = jax.nn.gelu(conv.astype(BF16)) * b_ref[...]


def _ffn_gate(u, conv_w, conv_b, seqlen, nblk, tc):
    m, ff2 = u.shape
    ff = ff2 // 2
    nrow8 = m // SUBLANES
    r8 = TM // SUBLANES
    grid_spec = pltpu.PrefetchScalarGridSpec(
        num_scalar_prefetch=1,
        grid=(nblk, ff // tc),
        in_specs=[
            pl.BlockSpec((TM, tc), lambda i, j, s: (i, j)),
            pl.BlockSpec((SUBLANES, tc), lambda i, j, s: (jnp.maximum(i * r8 - 1, 0), j)),
            pl.BlockSpec((SUBLANES, tc), lambda i, j, s: (jnp.minimum((i + 1) * r8, nrow8 - 1), j)),
            pl.BlockSpec((TM, tc), lambda i, j, s: (i, ff // tc + j)),
            pl.BlockSpec((3, tc), lambda i, j, s: (0, j)),
            pl.BlockSpec((1, tc), lambda i, j, s: (0, j)),
        ],
        out_specs=pl.BlockSpec((TM, tc), lambda i, j, s: (i, j)),
    )
    return pl.pallas_call(
        _ffn_gate_kernel,
        grid_spec=grid_spec,
        out_shape=jax.ShapeDtypeStruct((m, ff), BF16),
        compiler_params=_cparams("parallel", "arbitrary"),
        name="ffn_gate",
    )(seqlen, u, u, u, u, conv_w, conv_b.reshape(1, ff))


def _ffn_out_kernel(mrow_ref, h_ref, w_ref, x_ref, gate_ref, g_ref, sh_ref, sc_ref, *rest, final):
    outs, xrow = rest[:-1], rest[-1]
    j = pl.program_id(1)
    y = jnp.dot(h_ref[...], w_ref[...], preferred_element_type=F32)
    xt = x_ref[...] + gate_ref[...] * y
    xrow[j] = xt
    if not final:
        outs[0][...] = xt

    @pl.when(j == pl.num_programs(1) - 1)
    def _():
        x = jnp.concatenate([xrow[t] for t in range(xrow.shape[0])], axis=1)
        if final:
            y = x * lax.rsqrt(jnp.mean(x * x, axis=-1, keepdims=True) + EPS)
            outs[0][...] = y * g_ref[...]
        else:
            outs[1][...] = _norm_modulate(x, g_ref[...], sh_ref[...], sc_ref[...]).astype(outs[1].dtype)


def _ffn_out(h, w, x, mods, mods_next, g_next, mrow, nblk, tr, tn, final):
    m, ff = h.shape
    d = w.shape[1]
    mod = lambda arr_idx: pl.BlockSpec((None, None, 1, d), lambda i, j, mr: (mr[i], arr_idx, 0, 0))
    row = pl.BlockSpec((tr, d), lambda i, j, mr: (i, 0))
    tile = pl.BlockSpec((tr, tn), lambda i, j, mr: (i, j))
    grid_spec = pltpu.PrefetchScalarGridSpec(
        num_scalar_prefetch=1,
        grid=(nblk, d // tn),
        in_specs=[
            pl.BlockSpec((tr, ff), lambda i, j, mr: (i, 0)),
            pl.BlockSpec((ff, tn), lambda i, j, mr: (0, j)),
            tile,
            pl.BlockSpec((None, None, 1, tn), lambda i, j, mr: (mr[i], 5, 0, j)),
            pl.BlockSpec((1, d), lambda i, j, mr: (0, 0)), mod(0), mod(1),
        ],
        out_specs=[row] if final else [tile, row],
        scratch_shapes=[pltpu.VMEM((d // tn, tr, tn), F32)],
    )
    out_shape = [jax.ShapeDtypeStruct((m, d), F32)] + ([] if final else [jax.ShapeDtypeStruct((m, d), BF16)])
    return pl.pallas_call(
        functools.partial(_ffn_out_kernel, final=final),
        grid_spec=grid_spec,
        out_shape=out_shape,
        compiler_params=_cparams("parallel", "arbitrary"),
        name="ffn_out",
    )(mrow, h, w, x, mods, g_next.reshape(1, d), mods_next, mods_next)


def _rope_pair(x, cos, sin):
    lane = lax.broadcasted_iota(jnp.int32, x.shape, 1)
    first = lax.rem(lane, 32) < 16
    partner = jnp.where(first, pltpu.roll(x, LANES - 16, axis=1), pltpu.roll(x, 16, axis=1))
    return x * cos + partner * sin


def _ret_kernel(lg_ref, q_ref, k_ref, v_ref, g_ref, cos_ref, sin_ref, s0_ref, *rest, rope, has_prev):
    o_ref, sfin_ref, qr_scr, kr_scr, kv_scr = rest[1:] if has_prev else rest
    hp = pl.program_id(1)
    seq = q_ref.shape[0]
    c = RET_CHUNK
    nchunk = seq // c
    ks = RET_DK ** -0.5
    q = q_ref[...].astype(F32)
    k = k_ref[...].astype(F32)
    if rope:
        q = _rope_pair(q, cos_ref[...], sin_ref[...])
        k = _rope_pair(k, cos_ref[...], sin_ref[...])
    qr_scr[...] = q
    kr_scr[...] = k * ks

    pos_r = lax.broadcasted_iota(jnp.int32, (c, 1), 0).astype(F32)
    rel = (lax.broadcasted_iota(jnp.int32, (c, c), 0) - lax.broadcasted_iota(jnp.int32, (c, c), 1)).astype(F32)
    lane = lax.broadcasted_iota(jnp.int32, (1, LANES), 1)

    heads = []
    for hh in range(2):
        lgf = lg_ref[0, 2 * hp + hh]
        lgb = lg_ref[1, 2 * hp + hh]
        heads.append(dict(
            hm=(lane // RET_DK == hh).astype(F32),
            zeta_f=jnp.exp(lgf * (c - 1 - pos_r)), zeta_b=jnp.exp(lgb * pos_r),
            xi_f=jnp.exp(lgf * (pos_r + 1.0)), xi_b=jnp.exp(lgb * (c - pos_r)),
            dmat=jnp.where(rel >= 0, jnp.exp(lgf * jnp.maximum(rel, 0.0)), jnp.exp(lgb * jnp.maximum(-rel, 0.0))),
            gf=jnp.exp(lgf * c), gb=jnp.exp(lgb * c),
            vcols=slice(hh * RET_DV, (hh + 1) * RET_DV)))

    def kv_body(n, _):
        rows = pl.ds(pl.multiple_of(n * c, c), c)
        kr = kr_scr[rows, :]
        for hh, hd in enumerate(heads):
            kh = kr * hd["hm"]
            kz = jnp.concatenate([kh * hd["zeta_f"], kh * hd["zeta_b"]], axis=1).astype(BF16)
            kv_scr[hh, n] = lax.dot_general(kz, v_ref[rows, hd["vcols"]], (((0,), (0,)), ((), ())),
                                            preferred_element_type=F32)
        return 0

    lax.fori_loop(0, nchunk, kv_body, 0, unroll=2)

    def scan(j, carry):
        nf, nbk = j, nchunk - 1 - j
        out = []
        for hh, hd in enumerate(heads):
            sf, sb = carry[2 * hh], carry[2 * hh + 1]
            tf = kv_scr[hh, nf, 0:LANES, :]
            kv_scr[hh, nf, 0:LANES, :] = sf
            tb = kv_scr[hh, nbk, LANES:2 * LANES, :]
            kv_scr[hh, nbk, LANES:2 * LANES, :] = sb
            out += [hd["gf"] * sf + tf, hd["gb"] * sb + tb]
        return tuple(out)

    fin = lax.fori_loop(0, nchunk, scan, (s0_ref[0, 0], s0_ref[0, 1], s0_ref[1, 0], s0_ref[1, 1]))
    for hh in range(2):
        sfin_ref[hh, 0] = fin[2 * hh]
        sfin_ref[hh, 1] = fin[2 * hh + 1]

    def out_body(n, _):
        rows = pl.ds(pl.multiple_of(n * c, c), c)
        qr = qr_scr[rows, :]
        kb = kr_scr[rows, :].astype(BF16)
        for hh, hd in enumerate(heads):
            qm = qr * hd["hm"]
            a = lax.dot_general(qm.astype(BF16), kb, (((1,), (1,)), ((), ())),
                                preferred_element_type=F32) * hd["dmat"]
            o = jnp.dot(a.astype(BF16), v_ref[rows, hd["vcols"]], preferred_element_type=F32)
            qx = jnp.concatenate([qm * hd["xi_f"], qm * hd["xi_b"]], axis=1).astype(BF16)
            o = o + jnp.dot(qx, kv_scr[hh, n].astype(BF16), preferred_element_type=F32)
            o = o * lax.rsqrt(jnp.mean(o * o, axis=-1, keepdims=True) + EPS)
            g = g_ref[rows, hd["vcols"]].astype(F32)
            o_ref[rows, hd["vcols"]] = (g * jax.nn.sigmoid(g) * o).astype(o_ref.dtype)
        return 0

    lax.fori_loop(0, nchunk, out_body, 0, unroll=2)


def _retention(z, log_decay, cos, sin, s0, *, nb, seq, row0, rope, out_rows, y_prev=None):
    rb0 = row0 // seq
    hpairs = RET_HEADS // 2
    kcol, qcol = COL_RET_K // LANES, COL_RET_Q // LANES
    vcol, gcol = COL_RET_V // (2 * RET_DV), COL_RET_G // (2 * RET_DV)
    has_prev = y_prev is not None
    kernel = functools.partial(_ret_kernel, rope=rope, has_prev=has_prev)
    in_specs = [
        pl.BlockSpec(memory_space=pltpu.SMEM),
        pl.BlockSpec((seq, LANES), lambda b, p: (rb0 + b, qcol + p)),
        pl.BlockSpec((seq, LANES), lambda b, p: (rb0 + b, kcol + p)),
        pl.BlockSpec((seq, 2 * RET_DV), lambda b, p: (rb0 + b, vcol + p)),
        pl.BlockSpec((seq, 2 * RET_DV), lambda b, p: (rb0 + b, gcol + p)),
        pl.BlockSpec((seq, LANES), lambda b, p: (0, 0)),
        pl.BlockSpec((seq, LANES), lambda b, p: (0, 0)),
        pl.BlockSpec((None, 2, 2, LANES, RET_DV), lambda b, p: (b, p, 0, 0, 0)),
    ]
    args = [log_decay, z, z, z, z, cos, sin, s0]
    if has_prev:
        in_specs.append(pl.BlockSpec(memory_space=pl.ANY))
        args.append(y_prev)
    return pl.pallas_call(
        kernel,
        grid=(nb, hpairs),
        in_specs=in_specs,
        out_specs=[
            pl.BlockSpec((seq, 2 * RET_DV), lambda b, p: (rb0 + b, p)),
            pl.BlockSpec((None, 2, 2, LANES, RET_DV), lambda b, p: (b, p, 0, 0, 0)),
        ],
        out_shape=[
            jax.ShapeDtypeStruct((out_rows, RET_HEADS * RET_DV), BF16),
            jax.ShapeDtypeStruct((nb, RET_HEADS, 2, LANES, RET_DV), F32),
        ],
        scratch_shapes=[
            pltpu.VMEM((seq, LANES), F32),
            pltpu.VMEM((seq, LANES), F32),
            pltpu.VMEM((2, seq // RET_CHUNK, 2 * LANES, RET_DV), F32),
        ],
        input_output_aliases={len(args) - 1: 0} if has_prev else {},
        compiler_params=_cparams("parallel", "arbitrary"),
        name="retention_rope" if rope else "retention",
    )(*args)


NA_GROUP = 4
NA_KROWS = 12


def _na_kernel(rpb_ref, q_ref, k_ref, v_ref, kc_ref, vc_ref, o_ref, bias_scr):
    seq = q_ref.shape[0]
    nrow = seq // GRID_W
    ndr, ndc = 2 * NA_WR - 1, 2 * NA_WC - 1
    scale = NA_DH ** -0.5
    nt = (((1,), (1,)), ((), ()))
    h = pl.program_id(0)
    lane = lax.broadcasted_iota(jnp.int32, (GRID_W, LANES), 1)

    @pl.when(pl.program_id(1) == 0)
    def _():
        qi = lax.broadcasted_iota(jnp.int32, (GRID_W, LANES), 0)
        ki = lane & (GRID_W - 1)
        dc = ki - qi + (NA_WC - 1)
        cs = jnp.clip(qi - NA_WC // 2, 0, GRID_W - NA_WC)
        colmask = (ki >= cs) & (ki < cs + NA_WC)

        def toeplitz(d):
            t = jnp.zeros((GRID_W, LANES), F32)
            for j in range(ndc):
                t = jnp.where(dc == j, rpb_ref[h * (ndr * ndc) + d * ndc + j], t)
            return t

        prev = jnp.zeros((GRID_W, LANES), F32)
        for i in range(ndr + 1):
            nxt = toeplitz(i) if i < ndr else jnp.zeros((GRID_W, LANES), F32)
            bias_scr[i] = jnp.where(colmask, jnp.where(lane < GRID_W, prev, nxt), NEG)
            prev = nxt

    kc = kc_ref[...]
    vc = vc_ref[...]
    kwin = NA_KROWS * GRID_W

    half = NA_WR // 2
    ntile = kwin // LANES

    def group(r0, interior):
        if interior:
            us = r0 - half
            qrows = pl.ds(pl.multiple_of(r0 * GRID_W, NA_GROUP * GRID_W), NA_GROUP * GRID_W)
            krows = pl.ds(pl.multiple_of(us * GRID_W, GRID_W), kwin)
        else:
            us = min(max(r0 - half, 0), nrow - NA_KROWS)
            qrows = pl.ds(r0 * GRID_W, NA_GROUP * GRID_W)
            krows = pl.ds(us * GRID_W, kwin)
        q = q_ref[qrows, :]
        s = lax.dot_general(q, k_ref[krows, :], nt, preferred_element_type=F32)
        sc = lax.dot_general(q, kc, nt, preferred_element_type=F32) * scale
        e_rows, ec_rows, den_rows = [], [], []
        for u in range(NA_GROUP):
            qr = slice(u * GRID_W, (u + 1) * GRID_W)
            tiles = []
            for m in range(ntile):
                if interior:
                    off, idx = 2 * m - u, 2 * m - u + half
                else:
                    r = r0 + u
                    off = us + 2 * m - min(max(r - half, 0), nrow - NA_WR)
                    idx = min(max(us + 2 * m - r + NA_WR, 0), ndr)
                ok_lo, ok_hi = 0 <= off < NA_WR, 0 <= off + 1 < NA_WR
                if not (ok_lo or ok_hi):
                    tiles.append(None)
                    continue
                tab = bias_scr[idx]
                if not (ok_lo and ok_hi):
                    tab = tab + jnp.where(lane < GRID_W, 0.0 if ok_lo else NEG, 0.0 if ok_hi else NEG)
                st = s[qr, m * LANES:(m + 1) * LANES]
                tiles.append(jnp.where(tab > 0.5 * NEG, st * scale + tab, NEG))
            live = [t for t in tiles if t is not None]
            tmax = live[0]
            for t in live[1:]:
                tmax = jnp.maximum(tmax, t)
            scu = sc[qr, :]
            mx = jnp.maximum(jnp.max(tmax, axis=-1, keepdims=True), jnp.max(scu, axis=-1, keepdims=True))
            etiles = [None if t is None else jnp.exp(t - mx) for t in tiles]
            ecu = jnp.exp(scu - mx)
            esum = None
            for t in etiles:
                if t is not None:
                    esum = t if esum is None else esum + t
            den_rows.append(jnp.sum(esum, axis=-1, keepdims=True) + jnp.sum(ecu, axis=-1, keepdims=True))
            zero = jnp.zeros((GRID_W, LANES), BF16)
            e_rows.append(jnp.concatenate([zero if t is None else t.astype(BF16) for t in etiles], axis=1))
            ec_rows.append(ecu.astype(BF16))
        o = jnp.dot(jnp.concatenate(e_rows, axis=0), v_ref[krows, :], preferred_element_type=F32)
        o = o + jnp.dot(jnp.concatenate(ec_rows, axis=0), vc, preferred_element_type=F32)
        o_ref[qrows, :] = (o / jnp.concatenate(den_rows, axis=0)).astype(o_ref.dtype)

    ngroup = nrow // NA_GROUP
    g_lo = -(-half // NA_GROUP)
    g_hi = min((nrow - NA_KROWS + half) // NA_GROUP, (nrow - NA_WR + half - NA_GROUP + 1) // NA_GROUP)
    for g in range(g_lo):
        group(g * NA_GROUP, False)

    def body(i, _):
        g = g_lo + 2 * i
        group(g * NA_GROUP, True)
        group((g + 1) * NA_GROUP, True)
        return 0

    npair = (g_hi + 1 - g_lo) // 2
    lax.fori_loop(0, npair, body, 0)
    for g in range(g_lo + 2 * npair, g_hi + 1):
        group(g * NA_GROUP, True)
    for g in range(g_hi + 1, ngroup):
        group(g * NA_GROUP, False)


def _na_attention(z, rpb, *, nb, seq, cseq, crow0, out_rows):
    kcol, vcol, qcol = COL_NA_K // NA_DH, COL_NA_V // NA_DH, COL_NA_Q // NA_DH
    crb0 = crow0 // cseq
    nrow = seq // GRID_W
    assert nrow >= NA_KROWS and nrow % NA_GROUP == 0 and GRID_W * 2 == LANES
    assert NA_KROWS % 2 == 0 and NA_KROWS >= NA_WR + NA_GROUP - 1
    return pl.pallas_call(
        _na_kernel,
        grid=(NA_HEADS, nb),
        in_specs=[
            pl.BlockSpec(memory_space=pltpu.SMEM),
            pl.BlockSpec((seq, NA_DH), lambda h, b: (b, qcol + h)),
            pl.BlockSpec((seq, NA_DH), lambda h, b: (b, kcol + h)),
            pl.BlockSpec((seq, NA_DH), lambda h, b: (b, vcol + h)),
            pl.BlockSpec((cseq, NA_DH), lambda h, b: (crb0 + b, kcol + h)),
            pl.BlockSpec((cseq, NA_DH), lambda h, b: (crb0 + b, vcol + h)),
        ],
        out_specs=pl.BlockSpec((seq, NA_DH), lambda h, b: (b, h)),
        out_shape=jax.ShapeDtypeStruct((out_rows, NA_HEADS * NA_DH), BF16),
        scratch_shapes=[pltpu.VMEM((2 * NA_WR, GRID_W, LANES), F32)],
        compiler_params=_cparams("arbitrary", "arbitrary"),
        name="na_attention",
    )(rpb.reshape(-1), z, z, z, z, z)


def _dense_attn_kernel(q_ref, k_ref, v_ref, yprev_ref, o_ref):
    s = lax.dot_general(q_ref[...], k_ref[...], (((1,), (1,)), ((), ())), preferred_element_type=F32) * NA_DH ** -0.5
    e = jnp.exp(s - jnp.max(s, axis=-1, keepdims=True))
    o = jnp.dot(e.astype(BF16), v_ref[...], preferred_element_type=F32)
    o_ref[...] = (o / jnp.sum(e, axis=-1, keepdims=True)).astype(o_ref.dtype)


def _dense_attention(z, y_prev, *, nb, cseq, crow0):
    kcol, vcol, qcol = COL_NA_K // NA_DH, COL_NA_V // NA_DH, COL_NA_Q // NA_DH
    crb0 = crow0 // cseq
    return pl.pallas_call(
        _dense_attn_kernel,
        grid=(nb, NA_HEADS),
        in_specs=[
            pl.BlockSpec((cseq, NA_DH), lambda b, h: (crb0 + b, qcol + h)),
            pl.BlockSpec((cseq, NA_DH), lambda b, h: (crb0 + b, kcol + h)),
            pl.BlockSpec((cseq, NA_DH), lambda b, h: (crb0 + b, vcol + h)),
            pl.BlockSpec(memory_space=pl.ANY),
        ],
        out_specs=pl.BlockSpec((cseq, NA_DH), lambda b, h: (crb0 + b, h)),
        out_shape=jax.ShapeDtypeStruct(y_prev.shape, BF16),
        input_output_aliases={3: 0},
        compiler_params=_cparams("parallel", "arbitrary"),
        name="dense_attention",
    )(z, z, z, y_prev)


HY_FEAT_ROWS = 64
HIGHEST = lax.Precision.HIGHEST


def _hy_features(seq):
    t = np.linspace(0.0, 1.0, seq)
    bands = (HY_EMB - 1) // 2
    w = 2.0 * math.pi * np.arange(seq) / seq
    fr = np.linspace(1e-4, bands - 1, bands)
    ang = fr[None] * w[:, None]
    feat = np.concatenate([t[:, None], np.cos(ang), -np.sin(ang)], axis=-1)
    src = np.concatenate([np.arange(seq), np.zeros(1, np.int64), np.arange(seq - 1, 0, -1)])
    feat2 = np.zeros((HY_FEAT_ROWS, 2 * seq), np.float32)
    feat2[:HY_EMB] = feat[src].T
    tt = t[src][None].astype(np.float32)
    mask = np.ones((1, 2 * seq), np.float32)
    mask[0, seq] = 0.0
    return feat2, tt, mask


def _hy_mlp_kernel(feat_ref, w1_ref, b1_ref, w2_ref, b2_ref, fq_ref, o_ref):
    fq = fq_ref[...]
    h = jnp.dot(w1_ref[...], feat_ref[...], preferred_element_type=F32, precision=HIGHEST)
    h = jnp.sin(fq * (h + b1_ref[...]))
    h = jnp.dot(w2_ref[...], h, preferred_element_type=F32, precision=HIGHEST)
    o_ref[...] = jnp.sin(fq * (h + b2_ref[...]))


def _hy_mlp(feat2, w1t, b1, w2t, b2, fq):
    depth, hid, _ = w1t.shape
    n = feat2.shape[1]
    col = lambda a: a.reshape(depth, hid, 1)
    wspec = lambda k: pl.BlockSpec((None, hid, k), lambda l: (l, 0, 0))
    return pl.pallas_call(
        _hy_mlp_kernel,
        grid=(depth,),
        in_specs=[pl.BlockSpec((HY_FEAT_ROWS, n), lambda l: (0, 0)), wspec(HY_FEAT_ROWS), wspec(1), wspec(hid),
                  wspec(1), wspec(1)],
        out_specs=pl.BlockSpec((None, hid, n), lambda l: (l, 0, 0)),
        out_shape=jax.ShapeDtypeStruct((depth, hid, n), F32),
        compiler_params=_cparams("parallel"),
        name="hyena_filter_mlp",
    )(feat2, w1t, col(b1), w2t, col(b2), col(fq))


def _hy_filter_kernel(h_ref, w3_ref, dl_ref, tt_ref, mask_ref, o_ref):
    seq = h_ref.shape[1] // 2
    kf = jnp.dot(w3_ref[0], h_ref[:, :seq], preferred_element_type=F32, precision=HIGHEST)
    kb = jnp.dot(w3_ref[1], h_ref[:, seq:], preferred_element_type=F32, precision=HIGHEST)
    k = jnp.concatenate([kf, kb], axis=1) * (jnp.exp(-tt_ref[...] * dl_ref[...]) * mask_ref[...])
    o_ref[...] = (k * lax.rsqrt(jnp.sum(k * k, axis=1, keepdims=True) + EPS)).astype(o_ref.dtype)


def _hy_filter(h2, w3t, deltas, tt, mask, cb):
    depth, hid, n = h2.shape
    c = w3t.shape[3]
    return pl.pallas_call(
        _hy_filter_kernel,
        grid=(depth, HY_ORDER, c // cb),
        in_specs=[
            pl.BlockSpec((None, hid, n), lambda l, o, j: (l, 0, 0)),
            pl.BlockSpec((None, None, 2, cb, hid), lambda l, o, j: (l, o, 0, j, 0)),
            pl.BlockSpec((cb, 1), lambda l, o, j: (j, 0)),
            pl.BlockSpec((1, n), lambda l, o, j: (0, 0)),
            pl.BlockSpec((1, n), lambda l, o, j: (0, 0)),
        ],
        out_specs=pl.BlockSpec((None, None, cb, n), lambda l, o, j: (l, o, j, 0)),
        out_shape=jax.ShapeDtypeStruct((depth, HY_ORDER, c, n), BF16),
        compiler_params=_cparams("parallel", "parallel", "arbitrary"),
        name="hyena_filter",
    )(h2, w3t, deltas, tt, mask)


FFT_NO = 64
FFT_NI = 128


def _fft_tables():
    n = FFT_NO * FFT_NI
    a = np.arange(FFT_NO)
    fo = np.exp(-2j * np.pi * np.outer(a, a) / FFT_NO)
    i = np.arange(FFT_NI)
    ci = np.exp(-2j * np.pi * np.outer(i, i) / FFT_NI)
    tw = np.exp(-2j * np.pi * np.outer(a, i) / n)
    half = FFT_NO // 2
    f32 = lambda x: np.ascontiguousarray(x, dtype=np.float32)
    g1_real = f32(np.concatenate([fo.real, fo.imag], axis=0))
    g1 = f32(np.block([[fo.real[:, :half], -fo.imag[:, :half]], [fo.imag[:, :half], fo.real[:, :half]]]))
    w2 = f32(np.block([[ci.real, ci.imag], [-ci.imag, ci.real]]))
    w2i = f32(np.block([[ci.real, -ci.imag], [ci.imag, ci.real]]))
    g4 = f32(np.block([[fo.real[:half], fo.imag[:half]], [-fo.imag[:half], fo.real[:half]]]) / n)
    return dict(g1_real=g1_real, g1=g1, w2=w2, w2i=w2i, g4=g4, twr=f32(tw.real), twi=f32(tw.imag))


HY_CHUNK = 16
HY_SPEC_CHUNK = 32


def _fft_stage1(g1, y2, twr, twi):
    nch = y2.shape[1] // FFT_NI
    o1 = jnp.dot(g1, y2, preferred_element_type=F32)
    yr, yi = o1[:FFT_NO].astype(BF16), o1[FFT_NO:].astype(BF16)
    ar = yr * twr - yi * twi
    ai = yr * twi + yi * twr
    lanes = lambda ci: slice(ci * FFT_NI, (ci + 1) * FFT_NI)
    return jnp.concatenate([jnp.concatenate([ar[:, lanes(ci)], ai[:, lanes(ci)]], axis=1)
                            for ci in range(nch)], axis=0)


def _hy_spectrum_kernel(k_ref, g1_ref, w2_ref, twr_ref, twi_ref, o_ref):
    cb = k_ref.shape[0]
    ch = HY_SPEC_CHUNK
    g1 = g1_ref[...]
    w2 = w2_ref[...]
    twr = twr_ref[...]
    twi = twi_ref[...]

    def body(j, _):
        chans = pl.ds(pl.multiple_of(j * ch, ch), ch)
        ks = k_ref[chans]
        k2 = jnp.concatenate([ks[ci] for ci in range(ch)], axis=1).astype(BF16)
        z = jnp.dot(_fft_stage1(g1, k2, twr, twi), w2, preferred_element_type=F32)
        o_ref[chans] = z.reshape(ch, FFT_NO, 2 * FFT_NI).astype(o_ref.dtype)
        return 0

    lax.fori_loop(0, cb // ch, body, 0)


def _tiled_twiddles(tabs, nch):
    tile = lambda a: jnp.asarray(np.tile(a, (1, nch)), BF16)
    return tile(tabs["twr"]), tile(tabs["twi"])


def _hy_spectrum(kfilt, tabs, cb):
    g, c = kfilt.shape[:2]
    const = lambda a: pl.BlockSpec(a.shape, lambda i, j: (0,) * a.ndim)
    g1 = jnp.asarray(tabs["g1_real"], BF16)
    w2 = jnp.asarray(tabs["w2"], BF16)
    twr, twi = _tiled_twiddles(tabs, HY_SPEC_CHUNK)
    return pl.pallas_call(
        _hy_spectrum_kernel,
        grid=(g, c // cb),
        in_specs=[pl.BlockSpec((None, cb, FFT_NO, FFT_NI), lambda i, j: (i, j, 0, 0)),
                  const(g1), const(w2), const(twr), const(twi)],
        out_specs=pl.BlockSpec((None, cb, FFT_NO, 2 * FFT_NI), lambda i, j: (i, j, 0, 0)),
        out_shape=jax.ShapeDtypeStruct((g, c, FFT_NO, 2 * FFT_NI), BF16),
        compiler_params=_cparams("parallel", "arbitrary"),
        name="hyena_spectrum",
    )(kfilt, g1, w2, twr, twi)


def _shift_conv3(x, w0, w1, w2, b):
    nrow, nlane = x.shape
    row = lax.broadcasted_iota(jnp.int32, x.shape, 0)
    lane = lax.broadcasted_iota(jnp.int32, x.shape, 1)
    r = pltpu.roll(x, 1, axis=1)
    prev = jnp.where(lane == 0, pltpu.roll(r, 1, axis=0), r)
    prev = jnp.where((lane == 0) & (row == 0), 0.0, prev)
    r = pltpu.roll(x, nlane - 1, axis=1)
    nxt = jnp.where(lane == nlane - 1, pltpu.roll(r, nrow - 1, axis=0), r)
    nxt = jnp.where((lane == nlane - 1) & (row == nrow - 1), 0.0, nxt)
    return prev * w0 + x * w1 + nxt * w2 + b


def _hyena_kernel(x_ref, kf_ref, pm_ref, g1_ref, w2_ref, w2i_ref, g4_ref, twr_ref, twi_ref, o_ref):
    cb = x_ref.shape[2]
    half = FFT_NO // 2
    ch = HY_CHUNK
    g1 = g1_ref[...]
    g4 = g4_ref[...]
    w2 = w2_ref[...]
    w2i = w2i_ref[...]
    twr = twr_ref[...]
    twi = twi_ref[...]
    lanes = lambda ci: slice(ci * FFT_NI, (ci + 1) * FFT_NI)
    krows = lambda ci: slice(ci * FFT_NO, (ci + 1) * FFT_NO)

    def body(j, _):
        c0 = pl.multiple_of(j * ch, ch)
        pm = pm_ref[:, pl.ds(c0, ch)]
        conv = []
        for part in range(3):
            halves = []
            for bb in range(2):
                xs = x_ref[bb, part, pl.ds(c0, ch)].astype(F32)
                halves.append(jnp.concatenate(
                    [_shift_conv3(xs[ci], pm[part, ci], pm[3 + part, ci], pm[6 + part, ci], pm[9 + part, ci])
                     for ci in range(ch)], axis=1))
            conv.append(jnp.concatenate(halves, axis=0))
        y = conv[2]
        for order in range(HY_ORDER):
            z = jnp.dot(_fft_stage1(g1, y.astype(BF16), twr, twi), w2, preferred_element_type=F32)
            kf = kf_ref[order, pl.ds(c0, ch)].reshape(ch * FFT_NO, 2 * FFT_NI)
            zr, zi = z[:, :FFT_NI].astype(BF16), z[:, FFT_NI:].astype(BF16)
            kr, ki = kf[:, :FFT_NI], kf[:, FFT_NI:]
            p = jnp.concatenate([zr * kr - zi * ki, zr * ki + zi * kr], axis=1)
            q = jnp.dot(p, w2i, preferred_element_type=F32).astype(BF16)
            qr = jnp.concatenate([q[krows(ci), :FFT_NI] for ci in range(ch)], axis=1)
            qi = jnp.concatenate([q[krows(ci), FFT_NI:] for ci in range(ch)], axis=1)
            qs = jnp.concatenate([qr * twr + qi * twi, qi * twr - qr * twi], axis=0)
            cv = jnp.dot(g4, qs, preferred_element_type=F32)
            bias = jnp.concatenate([pm[12 + order, ci] for ci in range(ch)], axis=1)
            y = conv[order] * (cv + bias * y)
        for ci in range(ch):
            o_ref[0, c0 + ci] = y[:half, lanes(ci)].astype(o_ref.dtype)
            o_ref[1, c0 + ci] = y[half:, lanes(ci)].astype(o_ref.dtype)
        return 0

    lax.fori_loop(0, cb // ch, body, 0)


def _hyena(xt, kf, layer, pm, tabs, cb):
    nb, _, c, half, _ = xt.shape
    const = lambda a: pl.BlockSpec(a.shape, lambda j, p: (0,) * a.ndim)
    bf = lambda name: jnp.asarray(tabs[name], BF16)
    g1, w2, w2i, g4 = bf("g1"), bf("w2"), bf("w2i"), bf("g4")
    twr, twi = _tiled_twiddles(tabs, HY_CHUNK)
    return pl.pallas_call(
        _hyena_kernel,
        grid=(c // cb, nb // 2),
        in_specs=[
            pl.BlockSpec((2, 3, cb, half, FFT_NI), lambda j, p: (p, 0, j, 0, 0)),
            pl.BlockSpec((None, HY_ORDER, cb, FFT_NO, 2 * FFT_NI), lambda j, p: (layer, 0, j, 0, 0)),
            pl.BlockSpec((14, cb, 1, FFT_NI), lambda j, p: (0, j, 0, 0)),
            const(g1), const(w2), const(w2i), const(g4), const(twr), const(twi),
        ],
        out_specs=pl.BlockSpec((2, cb, half, FFT_NI), lambda j, p: (p, j, 0, 0)),
        out_shape=jax.ShapeDtypeStruct((nb, c, half, FFT_NI), BF16),
        compiler_params=_cparams("parallel", "arbitrary"),
        name="hyena",
    )(xt, kf, pm, g1, w2, w2i, g4, twr, twi)


def _dft_tables(seq):
    n = 2 * seq
    wmat = np.exp(-2j * np.pi * np.outer(np.arange(n), np.arange(n)) / n)
    f32 = lambda x: np.ascontiguousarray(x, dtype=np.float32)
    fwd_real = f32(np.concatenate([wmat.real, wmat.imag], axis=1))
    ws = wmat[:seq]
    fwd = f32(np.block([[ws.real, ws.imag], [-ws.imag, ws.real]]))
    wi = np.conj(wmat)[:, :seq] / n
    inv = f32(np.block([[wi.real, wi.imag], [-wi.imag, wi.real]]))
    return dict(fwd_real=fwd_real, fwd=fwd, inv=inv)


def _rowdft_kernel(x_ref, w_ref, o_ref):
    o_ref[...] = jnp.dot(x_ref[...].astype(BF16), w_ref[...], preferred_element_type=F32)


def _rowdft(x, w, tr):
    m, k = x.shape
    n = w.shape[1]
    return pl.pallas_call(
        _rowdft_kernel,
        grid=(m // tr,),
        in_specs=[pl.BlockSpec((tr, k), lambda i: (i, 0)), pl.BlockSpec((k, n), lambda i: (0, 0))],
        out_specs=pl.BlockSpec((tr, n), lambda i: (i, 0)),
        out_shape=jax.ShapeDtypeStruct((m, n), F32),
        compiler_params=_cparams("parallel"),
        name="row_dft",
    )(x, w)


def _lane_conv3(x, w0, w1, w2, b):
    seq = x.shape[1]
    lane = lax.broadcasted_iota(jnp.int32, x.shape, 1)
    prev = jnp.where(lane == 0, 0.0, pltpu.roll(x, 1, axis=1))
    nxt = jnp.where(lane == seq - 1, 0.0, pltpu.roll(x, seq - 1, axis=1))
    return prev * w0 + x * w1 + nxt * w2 + b


def _hyena_ctx_kernel(x_ref, kf_ref, pm_ref, fwd_ref, inv_ref, o_ref):
    seq = x_ref.shape[3]
    n = 2 * seq
    conv = [[_lane_conv3(x_ref[bb, part].astype(F32), pm_ref[part], pm_ref[3 + part], pm_ref[6 + part],
                         pm_ref[9 + part]) for bb in range(2)] for part in range(3)]
    ya, yb = conv[2]
    for order in range(HY_ORDER):
        z = jnp.dot(jnp.concatenate([ya, yb], axis=1).astype(BF16), fwd_ref[...], preferred_element_type=F32)
        kf = kf_ref[order]
        zr, zi, kr, ki = z[:, :n], z[:, n:], kf[:, :n], kf[:, n:]
        p = jnp.concatenate([zr * kr - zi * ki, zr * ki + zi * kr], axis=1).astype(BF16)
        cv = jnp.dot(p, inv_ref[...], preferred_element_type=F32)
        bias = pm_ref[12 + order]
        ya = conv[order][0] * (cv[:, :seq] + bias * ya)
        yb = conv[order][1] * (cv[:, seq:] + bias * yb)
    o_ref[0] = ya.astype(o_ref.dtype)
    o_ref[1] = yb.astype(o_ref.dtype)


def _hyena_ctx(xt, kf, pm, tabs, cb):
    nb, _, c, seq = xt.shape
    fwd, inv = jnp.asarray(tabs["fwd"], BF16), jnp.asarray(tabs["inv"], BF16)
    const = lambda a: pl.BlockSpec(a.shape, lambda j, p: (0,) * a.ndim)
    return pl.pallas_call(
        _hyena_ctx_kernel,
        grid=(c // cb, nb // 2),
        in_specs=[
            pl.BlockSpec((2, 3, cb, seq), lambda j, p: (p, 0, j, 0)),
            pl.BlockSpec((HY_ORDER, cb, 4 * seq), lambda j, p: (0, j, 0)),
            pl.BlockSpec((14, cb, 1), lambda j, p: (0, j, 0)),
            const(fwd), const(inv),
        ],
        out_specs=pl.BlockSpec((2, cb, seq), lambda j, p: (p, j, 0)),
        out_shape=jax.ShapeDtypeStruct((nb, c, seq), BF16),
        compiler_params=_cparams("parallel", "arbitrary"),
        name="hyena_ctx",
    )(xt, kf, pm, fwd, inv)


def _rope_tables(seq):
    t = np.arange(seq)
    pos = np.stack([t // GRID_W, t % GRID_W], axis=1).astype(np.float64)
    n = RET_DK // 4
    inv = ROPE_BASE ** (-np.arange(n, dtype=np.float64) / n)
    lane = np.arange(LANES) % RET_DK
    ang = pos[:, lane // (2 * n)] * inv[lane % n][None]
    sign = np.where(lane % (2 * n) < n, -1.0, 1.0)
    return np.cos(ang).astype(np.float32), (np.sin(ang) * sign[None]).astype(np.float32)


def _hy_filters(seq, hy_f_w1, hy_f_b1, hy_f_w2, hy_f_b2, hy_f_w3, hy_f_freq):
    depth, _, hid = hy_f_w1.shape
    c = hy_f_w3.shape[2] // (2 * HY_ORDER)
    feat2, tt, mask = _hy_features(seq)
    w1t = jnp.pad(hy_f_w1.transpose(0, 2, 1), ((0, 0), (0, 0), (0, HY_FEAT_ROWS - HY_EMB)))
    h2 = _hy_mlp(jnp.asarray(feat2), w1t, hy_f_b1, hy_f_w2.transpose(0, 2, 1), hy_f_b2, hy_f_freq)
    w3t = hy_f_w3.reshape(depth, hid, HY_ORDER, 2, c).transpose(0, 2, 3, 4, 1)
    deltas = np.abs(np.linspace(math.log(HY_TARGET) / HY_FAST, math.log(HY_TARGET) / HY_SLOW, c))
    deltas = jnp.asarray(deltas.astype(np.float32).reshape(c, 1))
    return _hy_filter(h2, w3t, deltas, jnp.asarray(tt), jnp.asarray(mask), min(c, 8 * 1024 * 128 // (2 * seq)))


def _hy_params(hy_conv_w, hy_conv_b, hy_bias):
    c = hy_bias.shape[1]
    return jnp.concatenate([hy_conv_w.reshape(9, c), hy_conv_b.reshape(3, c), hy_bias], axis=0)


def kernel(x, c, ctx, c_ctx, w_mod, b_mod, g_norm1, g_norm2, w_in, hy_conv_w, hy_conv_b, hy_f_w1, hy_f_b1, hy_f_w2, hy_f_b2, hy_f_w3, hy_f_freq, hy_bias, ret_log_decay, na_rpb, w_branch, w_out, ffn_w_in, ffn_conv_w, ffn_conv_b, ffn_w_out, g_final):
    nb, seq, d = x.shape
    cseq = ctx.shape[1]
    depth = w_mod.shape[0]
    hy_w = hy_bias.shape[2]
    t_lat, t_ctx = nb * seq, nb * cseq
    assert seq % TM == 0 and t_ctx % TM == 0 and TM % cseq == 0 and nb % 2 == 0 and nb < SUBLANES
    assert 2 * seq == FFT_NO * FFT_NI and seq % GRID_W == 0 and cseq & (cseq - 1) == 0
    blk_lat, blk_all = t_lat // TM, (t_lat + t_ctx) // TM
    mrow = jnp.asarray(np.concatenate([np.repeat(np.arange(nb), seq // TM), np.full(t_ctx // TM, nb)]), jnp.int32)
    seqlen = jnp.asarray(np.concatenate([np.full(blk_lat, seq), np.full(t_ctx // TM, cseq)]), jnp.int32)
    per = TM // TR
    mrow_r = jnp.repeat(mrow, per)
    tn = 1024

    cond = jnp.zeros((SUBLANES, d), F32).at[:nb].set(c).at[nb].set(c_ctx)
    mods = _mods(cond, w_mod, b_mod).reshape(depth, SUBLANES, 6, 1, d)

    cos, sin = (jnp.asarray(a) for a in _rope_tables(seq))
    tabs = _fft_tables()
    ctabs = _dft_tables(cseq)
    filt = _hy_filters(seq, hy_f_w1, hy_f_b1, hy_f_w2, hy_f_b2, hy_f_w3, hy_f_freq)
    spec = _hy_spectrum(filt.reshape(depth * HY_ORDER, hy_w, FFT_NO, FFT_NI), tabs, 32)
    spec = spec.reshape(depth, HY_ORDER, hy_w, FFT_NO, 2 * FFT_NI)
    cfilt = _hy_filters(cseq, hy_f_w1, hy_f_b1, hy_f_w2, hy_f_b2, hy_f_w3, hy_f_freq)
    cspec = _rowdft(cfilt.reshape(depth * HY_ORDER * hy_w, 2 * cseq), jnp.asarray(ctabs["fwd_real"], BF16), 1024)
    cspec = cspec.reshape(depth, HY_ORDER, hy_w, 4 * cseq)

    xs = jnp.concatenate([x.reshape(t_lat, d), ctx.reshape(t_ctx, d)], axis=0)
    assert hy_w == HY_W and w_in.shape[2] == COL_GATE + N_BRANCH * d
    hy_cols = COL_GATE - COL_HY
    s_zero = jnp.zeros((nb, RET_HEADS, 2, LANES, RET_DV), F32)
    hn = _normmod(xs, g_norm1[0], mods[0], mrow, 0, 1, blk_all)
    for l in range(depth):
        last = l == depth - 1
        nblk = blk_lat if last else blk_all
        rows = nblk * TM
        z = _matmul(hn, w_in, l, blk_all, tn, w_in.shape[2] - hy_cols, skip=(COL_HY // tn, hy_cols // tn))
        hy_proj = functools.partial(_matmul_nt, hn, w_in, l, col0=COL_HY, n_out=hy_cols, nb=nb, tc=512)
        zh = hy_proj(blk0=0, nblk=blk_lat, seq=seq)

        y_ret, s_ctx = _retention(z, ret_log_decay[l], cos, sin, s_zero, nb=nb, seq=cseq, row0=t_lat, rope=False,
                                  out_rows=blk_all * TM)
        y_ret, _ = _retention(z, ret_log_decay[l], cos, sin, s_ctx, nb=nb, seq=seq, row0=0, rope=True,
                              out_rows=blk_all * TM, y_prev=y_ret)
        y_na = _na_attention(z, na_rpb[l], nb=nb, seq=seq, cseq=cseq, crow0=t_lat, out_rows=rows)
        pm = _hy_params(hy_conv_w[l], hy_conv_b[l], hy_bias[l])
        pm_lat = jnp.broadcast_to(pm[:, :, None, None], (14, hy_w, 1, FFT_NI))
        y_hy = _hyena(zh.reshape(nb, 3, hy_w, FFT_NO // 2, FFT_NI), spec, l, pm_lat, tabs, 32)
        y_hy = y_hy.reshape(nb, hy_w, seq).transpose(0, 2, 1).reshape(t_lat, hy_w)
        if not last:
            y_na = _dense_attention(z, y_na, nb=nb, cseq=cseq, crow0=t_lat)
            zc = hy_proj(blk0=blk_lat, nblk=t_ctx // TM, seq=cseq)
            yc_hy = _hyena_ctx(zc.reshape(nb, 3, hy_w, cseq), cspec[l], pm[:, :, None], ctabs, 256)
            y_hy = jnp.concatenate([y_hy, yc_hy.transpose(0, 2, 1).reshape(t_ctx, hy_w)], axis=0)
        acc = _merge((y_hy, y_ret, y_na), z, COL_HY, w_branch[l].astype(BF16), nblk, 1024)
        xs, hn = _out_proj(acc, w_out[l].astype(BF16), xs, mods[l], mrow_r, g_norm2[l], nblk * per)
        u = _matmul(hn, ffn_w_in, l, nblk, tn, ffn_w_in.shape[2])
        hg = _ffn_gate(u, ffn_conv_w[l], ffn_conv_b[l], seqlen, nblk, 512)
        nxt = l if last else l + 1
        w_ffo = ffn_w_out[l].astype(BF16)
        if last:
            res = _ffn_out(hg, w_ffo, xs, mods[l], mods[l], g_final, mrow_r, nblk * per, TR, 512, True)
        else:
            res = _ffn_out(hg, w_ffo, xs, mods[l], mods[nxt], g_norm1[nxt], mrow, nblk, TM, 256, False)
        if not last:
            xs, hn = res
    return res[0].reshape(nb, seq, d)
```

```python
import functools
import math

import numpy as np
import jax
import jax.numpy as jnp
from jax import lax
from jax.experimental import pallas as pl
from jax.experimental.pallas import tpu as pltpu

F32 = jnp.float32
BF16 = jnp.bfloat16

GRID_W = 64
N_BRANCH = 3
HY_ORDER = 2
HY_EMB = 33
HY_FAST = 0.3
HY_SLOW = 1.5
HY_TARGET = 1e-2
RET_HEADS = 8
RET_DK = 64
RET_DV = 128
RET_CHUNK = 128
NA_HEADS = 8
NA_DH = 128
NA_WR = 8
NA_WC = 16
ROPE_BASE = 10000.0
EPS = 1e-6
NEG = -1e30
HY_W = 1024

_RET_QK_W, _RET_V_W, _NA_W = RET_HEADS * RET_DK, RET_HEADS * RET_DV, NA_HEADS * NA_DH
COL_RET_K = 0
COL_RET_V = COL_RET_K + _RET_QK_W
COL_NA_K = COL_RET_V + _RET_V_W
COL_NA_V = COL_NA_K + _NA_W
COL_RET_Q = COL_NA_V + _NA_W
COL_RET_G = COL_RET_Q + _RET_QK_W
COL_NA_Q = COL_RET_G + _RET_V_W
COL_HY = COL_NA_Q + _NA_W
COL_GATE = COL_HY + (HY_ORDER + 1) * HY_W

LANES = 128
SUBLANES = 8
VMEM_LIMIT = 56 * 1024 * 1024

TM = 1024


def _cparams(*sem):
    return pltpu.CompilerParams(dimension_semantics=sem, vmem_limit_bytes=VMEM_LIMIT)


def _mods_kernel(a_ref, w_ref, b_ref, o_ref):
    a = a_ref[...]
    a = a * jax.nn.sigmoid(a)
    o_ref[...] = jnp.dot(a.astype(BF16), w_ref[...].astype(BF16), preferred_element_type=F32) + b_ref[...]


def _mods(cond, w_mod, b_mod):
    depth, d, n = w_mod.shape
    tn = 1024
    return pl.pallas_call(
        _mods_kernel,
        grid=(depth, n // tn),
        in_specs=[
            pl.BlockSpec((SUBLANES, d), lambda l, j: (0, 0)),
            pl.BlockSpec((None, d, tn), lambda l, j: (l, 0, j)),
            pl.BlockSpec((None, 1, tn), lambda l, j: (l, 0, j)),
        ],
        out_specs=pl.BlockSpec((None, SUBLANES, tn), lambda l, j: (l, 0, j)),
        out_shape=jax.ShapeDtypeStruct((depth, SUBLANES, n), F32),
        compiler_params=_cparams("parallel", "parallel"),
        name="mods",
    )(cond, w_mod, b_mod.reshape(depth, 1, n))


def _normmod_kernel(mrow_ref, x_ref, g_ref, sh_ref, sc_ref, o_ref):
    x = x_ref[...]
    y = x * lax.rsqrt(jnp.mean(x * x, axis=-1, keepdims=True) + EPS)
    y = y * g_ref[...]
    o_ref[...] = (y * (1.0 + sc_ref[...]) + sh_ref[...]).astype(o_ref.dtype)


def _normmod(x, g, mods, mrow, shift_idx, scale_idx, nblk):
    m, d = x.shape
    grid_spec = pltpu.PrefetchScalarGridSpec(
        num_scalar_prefetch=1,
        grid=(nblk,),
        in_specs=[
            pl.BlockSpec((TM, d), lambda i, mr: (i, 0)),
            pl.BlockSpec((1, d), lambda i, mr: (0, 0)),
            pl.BlockSpec((None, None, 1, d), lambda i, mr: (mr[i], shift_idx, 0, 0)),
            pl.BlockSpec((None, None, 1, d), lambda i, mr: (mr[i], scale_idx, 0, 0)),
        ],
        out_specs=pl.BlockSpec((TM, d), lambda i, mr: (i, 0)),
    )
    return pl.pallas_call(
        _normmod_kernel,
        grid_spec=grid_spec,
        out_shape=jax.ShapeDtypeStruct((m, d), BF16),
        compiler_params=_cparams("parallel"),
        name="normmod",
    )(mrow, x, g.reshape(1, d), mods, mods)


def _mm_kernel(a_ref, w_ref, o_ref, w_scr):
    @pl.when(pl.program_id(1) == 0)
    def _():
        w_scr[...] = w_ref[...].astype(BF16)

    o_ref[...] = jnp.dot(a_ref[...], w_scr[...], preferred_element_type=F32).astype(o_ref.dtype)


def _matmul(a, w, layer, nblk, tn, n_out, skip=None):
    m, k = a.shape
    col = (lambda j: j) if skip is None else (lambda j: j + jnp.where(j >= skip[0], skip[1], 0))
    return pl.pallas_call(
        _mm_kernel,
        grid=(n_out // tn, nblk),
        in_specs=[pl.BlockSpec((TM, k), lambda j, i: (i, 0)),
                  pl.BlockSpec((None, k, tn), lambda j, i: (layer, 0, col(j)))],
        out_specs=pl.BlockSpec((TM, tn), lambda j, i: (i, j)),
        out_shape=jax.ShapeDtypeStruct((m, n_out), BF16),
        scratch_shapes=[pltpu.VMEM((k, tn), BF16)],
        compiler_params=_cparams("arbitrary", "arbitrary"),
        name="matmul",
    )(a, w)


def _mm_nt_kernel(w_ref, a_ref, o_ref, wt_scr):
    @pl.when(pl.program_id(1) == 0)
    def _():
        wt_scr[...] = w_ref[...].T.astype(BF16)

    r = lax.dot_general(wt_scr[...], a_ref[...], (((1,), (1,)), ((), ())), preferred_element_type=F32)
    per = o_ref.shape[0]
    width = r.shape[1] // per
    for s in range(per):
        o_ref[s] = r[:, s * width:(s + 1) * width].astype(o_ref.dtype)


def _matmul_nt(a, w, layer, *, col0, n_out, blk0, nblk, nb, seq, tc):
    k = a.shape[1]
    per, sblk = max(TM // seq, 1), max(seq // TM, 1)
    return pl.pallas_call(
        _mm_nt_kernel,
        grid=(n_out // tc, nblk),
        in_specs=[pl.BlockSpec((None, k, tc), lambda j, i: (layer, 0, col0 // tc + j)),
                  pl.BlockSpec((TM, k), lambda j, i: (blk0 + i, 0))],
        out_specs=pl.BlockSpec((per, tc, TM // per), lambda j, i: (i // sblk, j, i % sblk)),
        out_shape=jax.ShapeDtypeStruct((nb, n_out, seq), BF16),
        scratch_shapes=[pltpu.VMEM((tc, k), BF16)],
        compiler_params=_cparams("arbitrary", "arbitrary"),
        name="matmul_nt",
    )(w, a)


TR = 512


def _norm_modulate(x, g, shift, scale):
    y = x * lax.rsqrt(jnp.mean(x * x, axis=-1, keepdims=True) + EPS)
    return (y * g) * (1.0 + scale) + shift


def _outproj_kernel(mrow_ref, a_ref, w_ref, x_ref, gate_ref, g_ref, sh_ref, sc_ref, x_out, h_out):
    y = jnp.dot(a_ref[...], w_ref[...], preferred_element_type=F32)
    x = x_ref[...] + gate_ref[...] * y
    x_out[...] = x
    h_out[...] = _norm_modulate(x, g_ref[...], sh_ref[...], sc_ref[...]).astype(h_out.dtype)


def _out_proj(a, w, x, mods, mrow, g_norm, nblk):
    m, k = a.shape
    d = w.shape[1]
    mod = lambda idx: pl.BlockSpec((None, None, 1, d), lambda i, mr: (mr[i], idx, 0, 0))
    row = pl.BlockSpec((TR, d), lambda i, mr: (i, 0))
    grid_spec = pltpu.PrefetchScalarGridSpec(
        num_scalar_prefetch=1,
        grid=(nblk,),
        in_specs=[pl.BlockSpec((TR, k), lambda i, mr: (i, 0)), pl.BlockSpec((k, d), lambda i, mr: (0, 0)), row,
                  mod(2), pl.BlockSpec((1, d), lambda i, mr: (0, 0)), mod(3), mod(4)],
        out_specs=[row, row],
    )
    return pl.pallas_call(
        _outproj_kernel,
        grid_spec=grid_spec,
        out_shape=[jax.ShapeDtypeStruct((m, d), F32), jax.ShapeDtypeStruct((m, d), BF16)],
        compiler_params=_cparams("parallel"),
        name="out_proj",
    )(mrow, a, w, x, mods, g_norm.reshape(1, d), mods, mods)


def _merge_kernel(y0_ref, y1_ref, y2_ref, g0_ref, g1_ref, g2_ref, w_ref, o_ref):
    acc = None
    for i, (y_ref, g_ref) in enumerate(((y0_ref, g0_ref), (y1_ref, g1_ref), (y2_ref, g2_ref))):
        t = jnp.dot(y_ref[...], w_ref[i], preferred_element_type=F32)
        t = jax.nn.sigmoid(g_ref[...].astype(F32)) * t
        acc = t if acc is None else acc + t
    o_ref[...] = acc.astype(o_ref.dtype)


def _merge(ys, z, gate_col0, w_branch, nblk, tn):
    m, bw = ys[0].shape
    d = w_branch.shape[2]
    gspec = lambda i_br: pl.BlockSpec((TM, tn), lambda i, j: (i, (gate_col0 + i_br * d) // tn + j))
    yspec = pl.BlockSpec((TM, bw), lambda i, j: (i, 0))
    return pl.pallas_call(
        _merge_kernel,
        grid=(nblk, d // tn),
        in_specs=[yspec, yspec, yspec, gspec(0), gspec(1), gspec(2),
                  pl.BlockSpec((N_BRANCH, bw, tn), lambda i, j: (0, 0, j))],
        out_specs=pl.BlockSpec((TM, tn), lambda i, j: (i, j)),
        out_shape=jax.ShapeDtypeStruct((m, d), BF16),
        compiler_params=_cparams("parallel", "arbitrary"),
        name="merge",
    )(ys[0], ys[1], ys[2], z, z, z, w_branch)


def _ffn_gate_kernel(seq_ref, a_ref, ap_ref, an_ref, b_ref, cw_ref, cb_ref, o_ref):
    i = pl.program_id(0)
    seq_m1 = seq_ref[i] - 1
    a = a_ref[...].astype(F32)
    tm = a.shape[0]
    row = lax.broadcasted_iota(jnp.int32, (tm, 1), 0)
    pos = (i * tm + row) & seq_m1
    prev = jnp.where(row == 0, ap_ref[SUBLANES - 1:SUBLANES, :].astype(F32), pltpu.roll(a, 1, axis=0))
    prev = jnp.where(pos == 0, 0.0, prev)
    nxt = jnp.where(row == tm - 1, an_ref[0:1, :].astype(F32), pltpu.roll(a, tm - 1, axis=0))
    nxt = jnp.where(pos == seq_m1, 0.0, nxt)
    cw = cw_ref[...]
    conv = prev * cw[0:1, :] + a * cw[1:2, :] + nxt * cw[2:3, :] + cb_ref[...]
    o_ref[...] = jax.nn.gelu(conv.astype(BF16)) * b_ref[...]


def _ffn_gate(u, conv_w, conv_b, seqlen, nblk, tc):
    m, ff2 = u.shape
    ff = ff2 // 2
    nrow8 = m // SUBLANES
    r8 = TM // SUBLANES
    grid_spec = pltpu.PrefetchScalarGridSpec(
        num_scalar_prefetch=1,
        grid=(nblk, ff // tc),
        in_specs=[
            pl.BlockSpec((TM, tc), lambda i, j, s: (i, j)),
            pl.BlockSpec((SUBLANES, tc), lambda i, j, s: (jnp.maximum(i * r8 - 1, 0), j)),
            pl.BlockSpec((SUBLANES, tc), lambda i, j, s: (jnp.minimum((i + 1) * r8, nrow8 - 1), j)),
            pl.BlockSpec((TM, tc), lambda i, j, s: (i, ff // tc + j)),
            pl.BlockSpec((3, tc), lambda i, j, s: (0, j)),
            pl.BlockSpec((1, tc), lambda i, j, s: (0, j)),
        ],
        out_specs=pl.BlockSpec((TM, tc), lambda i, j, s: (i, j)),
    )
    return pl.pallas_call(
        _ffn_gate_kernel,
        grid_spec=grid_spec,
        out_shape=jax.ShapeDtypeStruct((m, ff), BF16),
        compiler_params=_cparams("parallel", "arbitrary"),
        name="ffn_gate",
    )(seqlen, u, u, u, u, conv_w, conv_b.reshape(1, ff))


def _ffn_out_kernel(mrow_ref, h_ref, w_ref, x_ref, gate_ref, g_ref, sh_ref, sc_ref, *rest, final):
    outs, xrow = rest[:-1], rest[-1]
    j = pl.program_id(1)
    y = jnp.dot(h_ref[...], w_ref[...], preferred_element_type=F32)
    xt = x_ref[...] + gate_ref[...] * y
    xrow[j] = xt
    if not final:
        outs[0][...] = xt

    @pl.when(j == pl.num_programs(1) - 1)
    def _():
        x = jnp.concatenate([xrow[t] for t in range(xrow.shape[0])], axis=1)
        if final:
            y = x * lax.rsqrt(jnp.mean(x * x, axis=-1, keepdims=True) + EPS)
            outs[0][...] = y * g_ref[...]
        else:
            outs[1][...] = _norm_modulate(x, g_ref[...], sh_ref[...], sc_ref[...]).astype(outs[1].dtype)


def _ffn_out(h, w, x, mods, mods_next, g_next, mrow, nblk, tr, tn, final):
    m, ff = h.shape
    d = w.shape[1]
    mod = lambda arr_idx: pl.BlockSpec((None, None, 1, d), lambda i, j, mr: (mr[i], arr_idx, 0, 0))
    row = pl.BlockSpec((tr, d), lambda i, j, mr: (i, 0))
    tile = pl.BlockSpec((tr, tn), lambda i, j, mr: (i, j))
    grid_spec = pltpu.PrefetchScalarGridSpec(
        num_scalar_prefetch=1,
        grid=(nblk, d // tn),
        in_specs=[
            pl.BlockSpec((tr, ff), lambda i, j, mr: (i, 0)),
            pl.BlockSpec((ff, tn), lambda i, j, mr: (0, j)),
            tile,
            pl.BlockSpec((None, None, 1, tn), lambda i, j, mr: (mr[i], 5, 0, j)),
            pl.BlockSpec((1, d), lambda i, j, mr: (0, 0)), mod(0), mod(1),
        ],
        out_specs=[row] if final else [tile, row],
        scratch_shapes=[pltpu.VMEM((d // tn, tr, tn), F32)],
    )
    out_shape = [jax.ShapeDtypeStruct((m, d), F32)] + ([] if final else [jax.ShapeDtypeStruct((m, d), BF16)])
    return pl.pallas_call(
        functools.partial(_ffn_out_kernel, final=final),
        grid_spec=grid_spec,
        out_shape=out_shape,
        compiler_params=_cparams("parallel", "arbitrary"),
        name="ffn_out",
    )(mrow, h, w, x, mods, g_next.reshape(1, d), mods_next, mods_next)


def _rope_pair(x, cos, sin):
    lane = lax.broadcasted_iota(jnp.int32, x.shape, 1)
    first = lax.rem(lane, 32) < 16
    partner = jnp.where(first, pltpu.roll(x, LANES - 16, axis=1), pltpu.roll(x, 16, axis=1))
    return x * cos + partner * sin


def _ret_kernel(lg_ref, q_ref, k_ref, v_ref, g_ref, cos_ref, sin_ref, s0_ref, *rest, rope, has_prev):
    o_ref, sfin_ref, qr_scr, kr_scr, kv_scr = rest[1:] if has_prev else rest
    hp = pl.program_id(1)
    seq = q_ref.shape[0]
    c = RET_CHUNK
    nchunk = seq // c
    ks = RET_DK ** -0.5
    q = q_ref[...].astype(F32)
    k = k_ref[...].astype(F32)
    if rope:
        q = _rope_pair(q, cos_ref[...], sin_ref[...])
        k = _rope_pair(k, cos_ref[...], sin_ref[...])
    qr_scr[...] = q
    kr_scr[...] = k * ks

    pos_r = lax.broadcasted_iota(jnp.int32, (c, 1), 0).astype(F32)
    rel = (lax.broadcasted_iota(jnp.int32, (c, c), 0) - lax.broadcasted_iota(jnp.int32, (c, c), 1)).astype(F32)
    lane = lax.broadcasted_iota(jnp.int32, (1, LANES), 1)

    heads = []
    for hh in range(2):
        lgf = lg_ref[0, 2 * hp + hh]
        lgb = lg_ref[1, 2 * hp + hh]
        heads.append(dict(
            hm=(lane // RET_DK == hh).astype(F32),
            zeta_f=jnp.exp(lgf * (c - 1 - pos_r)), zeta_b=jnp.exp(lgb * pos_r),
            xi_f=jnp.exp(lgf * (pos_r + 1.0)), xi_b=jnp.exp(lgb * (c - pos_r)),
            dmat=jnp.where(rel >= 0, jnp.exp(lgf * jnp.maximum(rel, 0.0)), jnp.exp(lgb * jnp.maximum(-rel, 0.0))),
            gf=jnp.exp(lgf * c), gb=jnp.exp(lgb * c),
            vcols=slice(hh * RET_DV, (hh + 1) * RET_DV)))

    def kv_body(n, _):
        rows = pl.ds(pl.multiple_of(n * c, c), c)
        kr = kr_scr[rows, :]
        for hh, hd in enumerate(heads):
            kh = kr * hd["hm"]
            kz = jnp.concatenate([kh * hd["zeta_f"], kh * hd["zeta_b"]], axis=1).astype(BF16)
            kv_scr[hh, n] = lax.dot_general(kz, v_ref[rows, hd["vcols"]], (((0,), (0,)), ((), ())),
                                            preferred_element_type=F32)
        return 0

    lax.fori_loop(0, nchunk, kv_body, 0, unroll=min(8, nchunk))

    def scan(j, carry):
        nf, nbk = j, nchunk - 1 - j
        out = []
        for hh, hd in enumerate(heads):
            sf, sb = carry[2 * hh], carry[2 * hh + 1]
            tf = kv_scr[hh, nf, 0:LANES, :]
            kv_scr[hh, nf, 0:LANES, :] = sf
            tb = kv_scr[hh, nbk, LANES:2 * LANES, :]
            kv_scr[hh, nbk, LANES:2 * LANES, :] = sb
            out += [hd["gf"] * sf + tf, hd["gb"] * sb + tb]
        return tuple(out)

    fin = lax.fori_loop(0, nchunk, scan, (s0_ref[0, 0], s0_ref[0, 1], s0_ref[1, 0], s0_ref[1, 1]))
    for hh in range(2):
        sfin_ref[hh, 0] = fin[2 * hh]
        sfin_ref[hh, 1] = fin[2 * hh + 1]

    def out_body(n, _):
        rows = pl.ds(pl.multiple_of(n * c, c), c)
        qr = qr_scr[rows, :]
        kb = kr_scr[rows, :].astype(BF16)
        for hh, hd in enumerate(heads):
            qm = qr * hd["hm"]
            a = lax.dot_general(qm.astype(BF16), kb, (((1,), (1,)), ((), ())),
                                preferred_element_type=F32) * hd["dmat"]
            o = jnp.dot(a.astype(BF16), v_ref[rows, hd["vcols"]], preferred_element_type=F32)
            qx = jnp.concatenate([qm * hd["xi_f"], qm * hd["xi_b"]], axis=1).astype(BF16)
            o = o + jnp.dot(qx, kv_scr[hh, n].astype(BF16), preferred_element_type=F32)
            o = o * lax.rsqrt(jnp.mean(o * o, axis=-1, keepdims=True) + EPS)
            g = g_ref[rows, hd["vcols"]].astype(F32)
            o_ref[rows, hd["vcols"]] = (g * jax.nn.sigmoid(g) * o).astype(o_ref.dtype)
        return 0

    lax.fori_loop(0, nchunk, out_body, 0, unroll=min(8, nchunk))


def _retention(z, log_decay, cos, sin, s0, *, nb, seq, row0, rope, out_rows, y_prev=None):
    rb0 = row0 // seq
    hpairs = RET_HEADS // 2
    kcol, qcol = COL_RET_K // LANES, COL_RET_Q // LANES
    vcol, gcol = COL_RET_V // (2 * RET_DV), COL_RET_G // (2 * RET_DV)
    has_prev = y_prev is not None
    kernel = functools.partial(_ret_kernel, rope=rope, has_prev=has_prev)
    in_specs = [
        pl.BlockSpec(memory_space=pltpu.SMEM),
        pl.BlockSpec((seq, LANES), lambda b, p: (rb0 + b, qcol + p)),
        pl.BlockSpec((seq, LANES), lambda b, p: (rb0 + b, kcol + p)),
        pl.BlockSpec((seq, 2 * RET_DV), lambda b, p: (rb0 + b, vcol + p)),
        pl.BlockSpec((seq, 2 * RET_DV), lambda b, p: (rb0 + b, gcol + p)),
        pl.BlockSpec((seq, LANES), lambda b, p: (0, 0)),
        pl.BlockSpec((seq, LANES), lambda b, p: (0, 0)),
        pl.BlockSpec((None, 2, 2, LANES, RET_DV), lambda b, p: (b, p, 0, 0, 0)),
    ]
    args = [log_decay, z, z, z, z, cos, sin, s0]
    if has_prev:
        in_specs.append(pl.BlockSpec(memory_space=pl.ANY))
        args.append(y_prev)
    return pl.pallas_call(
        kernel,
        grid=(nb, hpairs),
        in_specs=in_specs,
        out_specs=[
            pl.BlockSpec((seq, 2 * RET_DV), lambda b, p: (rb0 + b, p)),
            pl.BlockSpec((None, 2, 2, LANES, RET_DV), lambda b, p: (b, p, 0, 0, 0)),
        ],
        out_shape=[
            jax.ShapeDtypeStruct((out_rows, RET_HEADS * RET_DV), BF16),
            jax.ShapeDtypeStruct((nb, RET_HEADS, 2, LANES, RET_DV), F32),
        ],
        scratch_shapes=[
            pltpu.VMEM((seq, LANES), F32),
            pltpu.VMEM((seq, LANES), F32),
            pltpu.VMEM((2, seq // RET_CHUNK, 2 * LANES, RET_DV), F32),
        ],
        input_output_aliases={len(args) - 1: 0} if has_prev else {},
        compiler_params=_cparams("parallel", "arbitrary"),
        name="retention_rope" if rope else "retention",
    )(*args)


NA_GROUP = 4
NA_KROWS = 12
NA_OVERLAP = 4


def _na_kernel(rpb_ref, q_ref, k_ref, v_ref, kc_ref, vc_ref, o_ref, bias_scr):
    seq = q_ref.shape[0]
    nrow = seq // GRID_W
    ndr, ndc = 2 * NA_WR - 1, 2 * NA_WC - 1
    scale = NA_DH ** -0.5
    nt = (((1,), (1,)), ((), ()))
    h = pl.program_id(0)
    lane = lax.broadcasted_iota(jnp.int32, (GRID_W, LANES), 1)

    @pl.when(pl.program_id(1) == 0)
    def _():
        qi = lax.broadcasted_iota(jnp.int32, (GRID_W, LANES), 0)
        ki = lane & (GRID_W - 1)
        dc = ki - qi + (NA_WC - 1)
        cs = jnp.clip(qi - NA_WC // 2, 0, GRID_W - NA_WC)
        colmask = (ki >= cs) & (ki < cs + NA_WC)

        def toeplitz(d):
            t = jnp.zeros((GRID_W, LANES), F32)
            for j in range(ndc):
                t = jnp.where(dc == j, rpb_ref[h * (ndr * ndc) + d * ndc + j], t)
            return t

        prev = jnp.zeros((GRID_W, LANES), F32)
        for i in range(ndr + 1):
            nxt = toeplitz(i) if i < ndr else jnp.zeros((GRID_W, LANES), F32)
            bias_scr[i] = jnp.where(colmask, jnp.where(lane < GRID_W, prev, nxt), NEG)
            prev = nxt

    kc = kc_ref[...]
    vc = vc_ref[...]
    kwin = NA_KROWS * GRID_W

    half = NA_WR // 2
    ntile = kwin // LANES

    def group(r0, interior):
        us = r0 - half if interior else min(max(r0 - half, 0), nrow - NA_KROWS)
        if isinstance(r0, int):
            qrows = pl.ds(r0 * GRID_W, NA_GROUP * GRID_W)
            krows = pl.ds(us * GRID_W, kwin)
        else:
            qrows = pl.ds(pl.multiple_of(r0 * GRID_W, NA_GROUP * GRID_W), NA_GROUP * GRID_W)
            krows = pl.ds(pl.multiple_of(us * GRID_W, GRID_W), kwin)
        q = q_ref[qrows, :]
        s = lax.dot_general(q, k_ref[krows, :], nt, preferred_element_type=F32)
        sc = lax.dot_general(q, kc, nt, preferred_element_type=F32) * scale
        e_rows, ec_rows, den_rows = [], [], []
        for u in range(NA_GROUP):
            qr = slice(u * GRID_W, (u + 1) * GRID_W)
            tiles = []
            for m in range(ntile):
                if interior:
                    off, idx = 2 * m - u, 2 * m - u + half
                else:
                    r = r0 + u
                    off = us + 2 * m - min(max(r - half, 0), nrow - NA_WR)
                    idx = min(max(us + 2 * m - r + NA_WR, 0), ndr)
                ok_lo, ok_hi = 0 <= off < NA_WR, 0 <= off + 1 < NA_WR
                if not (ok_lo or ok_hi):
                    tiles.append(None)
                    continue
                tab = bias_scr[idx]
                if not (ok_lo and ok_hi):
                    tab = tab + jnp.where(lane < GRID_W, 0.0 if ok_lo else NEG, 0.0 if ok_hi else NEG)
                st = s[qr, m * LANES:(m + 1) * LANES]
                tiles.append(jnp.where(tab > 0.5 * NEG, st * scale + tab, NEG))
            live = [t for t in tiles if t is not None]
            tmax = live[0]
            for t in live[1:]:
                tmax = jnp.maximum(tmax, t)
            scu = sc[qr, :]
            mx = jnp.maximum(jnp.max(tmax, axis=-1, keepdims=True), jnp.max(scu, axis=-1, keepdims=True))
            etiles = [None if t is None else jnp.exp(t - mx) for t in tiles]
            ecu = jnp.exp(scu - mx)
            esum = None
            for t in etiles:
                if t is not None:
                    esum = t if esum is None else esum + t
            den_rows.append(jnp.sum(esum, axis=-1, keepdims=True) + jnp.sum(ecu, axis=-1, keepdims=True))
            zero = jnp.zeros((GRID_W, LANES), BF16)
            e_rows.append(jnp.concatenate([zero if t is None else t.astype(BF16) for t in etiles], axis=1))
            ec_rows.append(ecu.astype(BF16))
        o = jnp.dot(jnp.concatenate(e_rows, axis=0), v_ref[krows, :], preferred_element_type=F32)
        o = o + jnp.dot(jnp.concatenate(ec_rows, axis=0), vc, preferred_element_type=F32)
        o_ref[qrows, :] = (o / jnp.concatenate(den_rows, axis=0)).astype(o_ref.dtype)

    ngroup = nrow // NA_GROUP
    g_lo = -(-half // NA_GROUP)
    g_hi = min((nrow - NA_KROWS + half) // NA_GROUP, (nrow - NA_WR + half - NA_GROUP + 1) // NA_GROUP)
    for g in range(g_lo):
        group(g * NA_GROUP, False)

    def body(i, _):
        for t in range(NA_OVERLAP):
            group((g_lo + NA_OVERLAP * i + t) * NA_GROUP, True)
        return 0

    nloop = (g_hi + 1 - g_lo) // NA_OVERLAP
    lax.fori_loop(0, nloop, body, 0)
    for g in range(g_lo + NA_OVERLAP * nloop, g_hi + 1):
        group(g * NA_GROUP, True)
    for g in range(g_hi + 1, ngroup):
        group(g * NA_GROUP, False)


def _na_attention(z, rpb, *, nb, seq, cseq, crow0, out_rows):
    kcol, vcol, qcol = COL_NA_K // NA_DH, COL_NA_V // NA_DH, COL_NA_Q // NA_DH
    crb0 = crow0 // cseq
    nrow = seq // GRID_W
    assert nrow >= NA_KROWS and nrow % NA_GROUP == 0 and GRID_W * 2 == LANES
    assert NA_KROWS % 2 == 0 and NA_KROWS >= NA_WR + NA_GROUP - 1
    return pl.pallas_call(
        _na_kernel,
        grid=(NA_HEADS, nb),
        in_specs=[
            pl.BlockSpec(memory_space=pltpu.SMEM),
            pl.BlockSpec((seq, NA_DH), lambda h, b: (b, qcol + h)),
            pl.BlockSpec((seq, NA_DH), lambda h, b: (b, kcol + h)),
            pl.BlockSpec((seq, NA_DH), lambda h, b: (b, vcol + h)),
            pl.BlockSpec((cseq, NA_DH), lambda h, b: (crb0 + b, kcol + h)),
            pl.BlockSpec((cseq, NA_DH), lambda h, b: (crb0 + b, vcol + h)),
        ],
        out_specs=pl.BlockSpec((seq, NA_DH), lambda h, b: (b, h)),
        out_shape=jax.ShapeDtypeStruct((out_rows, NA_HEADS * NA_DH), BF16),
        scratch_shapes=[pltpu.VMEM((2 * NA_WR, GRID_W, LANES), F32)],
        compiler_params=_cparams("arbitrary", "arbitrary"),
        name="na_attention",
    )(rpb.reshape(-1), z, z, z, z, z)


def _dense_attn_kernel(q_ref, k_ref, v_ref, yprev_ref, o_ref):
    s = lax.dot_general(q_ref[...], k_ref[...], (((1,), (1,)), ((), ())), preferred_element_type=F32) * NA_DH ** -0.5
    e = jnp.exp(s - jnp.max(s, axis=-1, keepdims=True))
    o = jnp.dot(e.astype(BF16), v_ref[...], preferred_element_type=F32)
    o_ref[...] = (o / jnp.sum(e, axis=-1, keepdims=True)).astype(o_ref.dtype)


def _dense_attention(z, y_prev, *, nb, cseq, crow0):
    kcol, vcol, qcol = COL_NA_K // NA_DH, COL_NA_V // NA_DH, COL_NA_Q // NA_DH
    crb0 = crow0 // cseq
    return pl.pallas_call(
        _dense_attn_kernel,
        grid=(nb, NA_HEADS),
        in_specs=[
            pl.BlockSpec((cseq, NA_DH), lambda b, h: (crb0 + b, qcol + h)),
            pl.BlockSpec((cseq, NA_DH), lambda b, h: (crb0 + b, kcol + h)),
            pl.BlockSpec((cseq, NA_DH), lambda b, h: (crb0 + b, vcol + h)),
            pl.BlockSpec(memory_space=pl.ANY),
        ],
        out_specs=pl.BlockSpec((cseq, NA_DH), lambda b, h: (crb0 + b, h)),
        out_shape=jax.ShapeDtypeStruct(y_prev.shape, BF16),
        input_output_aliases={3: 0},
        compiler_params=_cparams("parallel", "arbitrary"),
        name="dense_attention",
    )(z, z, z, y_prev)


HY_FEAT_ROWS = 64
HIGHEST = lax.Precision.HIGHEST


def _hy_features(seq):
    t = np.linspace(0.0, 1.0, seq)
    bands = (HY_EMB - 1) // 2
    w = 2.0 * math.pi * np.arange(seq) / seq
    fr = np.linspace(1e-4, bands - 1, bands)
    ang = fr[None] * w[:, None]
    feat = np.concatenate([t[:, None], np.cos(ang), -np.sin(ang)], axis=-1)
    src = np.concatenate([np.arange(seq), np.zeros(1, np.int64), np.arange(seq - 1, 0, -1)])
    feat2 = np.zeros((HY_FEAT_ROWS, 2 * seq), np.float32)
    feat2[:HY_EMB] = feat[src].T
    tt = t[src][None].astype(np.float32)
    mask = np.ones((1, 2 * seq), np.float32)
    mask[0, seq] = 0.0
    return feat2, tt, mask


def _hy_mlp_kernel(feat_ref, w1_ref, b1_ref, w2_ref, b2_ref, fq_ref, o_ref):
    fq = fq_ref[...]
    h = jnp.dot(w1_ref[...], feat_ref[...], preferred_element_type=F32, precision=HIGHEST)
    h = jnp.sin(fq * (h + b1_ref[...]))
    h = jnp.dot(w2_ref[...], h, preferred_element_type=F32, precision=HIGHEST)
    o_ref[...] = jnp.sin(fq * (h + b2_ref[...]))


def _hy_mlp(feat2, w1t, b1, w2t, b2, fq):
    depth, hid, _ = w1t.shape
    n = feat2.shape[1]
    col = lambda a: a.reshape(depth, hid, 1)
    wspec = lambda k: pl.BlockSpec((None, hid, k), lambda l: (l, 0, 0))
    return pl.pallas_call(
        _hy_mlp_kernel,
        grid=(depth,),
        in_specs=[pl.BlockSpec((HY_FEAT_ROWS, n), lambda l: (0, 0)), wspec(HY_FEAT_ROWS), wspec(1), wspec(hid),
                  wspec(1), wspec(1)],
        out_specs=pl.BlockSpec((None, hid, n), lambda l: (l, 0, 0)),
        out_shape=jax.ShapeDtypeStruct((depth, hid, n), F32),
        compiler_params=_cparams("parallel"),
        name="hyena_filter_mlp",
    )(feat2, w1t, col(b1), w2t, col(b2), col(fq))


def _hy_filter_kernel(h_ref, w3_ref, dl_ref, tt_ref, mask_ref, o_ref):
    seq = h_ref.shape[1] // 2
    kf = jnp.dot(w3_ref[0], h_ref[:, :seq], preferred_element_type=F32, precision=HIGHEST)
    kb = jnp.dot(w3_ref[1], h_ref[:, seq:], preferred_element_type=F32, precision=HIGHEST)
    k = jnp.concatenate([kf, kb], axis=1) * (jnp.exp(-tt_ref[...] * dl_ref[...]) * mask_ref[...])
    o_ref[...] = (k * lax.rsqrt(jnp.sum(k * k, axis=1, keepdims=True) + EPS)).astype(o_ref.dtype)


def _hy_filter(h2, w3t, deltas, tt, mask, cb):
    depth, hid, n = h2.shape
    c = w3t.shape[3]
    return pl.pallas_call(
        _hy_filter_kernel,
        grid=(depth, HY_ORDER, c // cb),
        in_specs=[
            pl.BlockSpec((None, hid, n), lambda l, o, j: (l, 0, 0)),
            pl.BlockSpec((None, None, 2, cb, hid), lambda l, o, j: (l, o, 0, j, 0)),
            pl.BlockSpec((cb, 1), lambda l, o, j: (j, 0)),
            pl.BlockSpec((1, n), lambda l, o, j: (0, 0)),
            pl.BlockSpec((1, n), lambda l, o, j: (0, 0)),
        ],
        out_specs=pl.BlockSpec((None, None, cb, n), lambda l, o, j: (l, o, j, 0)),
        out_shape=jax.ShapeDtypeStruct((depth, HY_ORDER, c, n), BF16),
        compiler_params=_cparams("parallel", "parallel", "arbitrary"),
        name="hyena_filter",
    )(h2, w3t, deltas, tt, mask)


FFT_NO = 64
FFT_NI = 128


def _fft_tables():
    n = FFT_NO * FFT_NI
    a = np.arange(FFT_NO)
    fo = np.exp(-2j * np.pi * np.outer(a, a) / FFT_NO)
    i = np.arange(FFT_NI)
    ci = np.exp(-2j * np.pi * np.outer(i, i) / FFT_NI)
    tw = np.exp(-2j * np.pi * np.outer(a, i) / n)
    half = FFT_NO // 2
    f32 = lambda x: np.ascontiguousarray(x, dtype=np.float32)
    g1_real = f32(np.concatenate([fo.real, fo.imag], axis=0))
    g1 = f32(np.block([[fo.real[:, :half], -fo.imag[:, :half]], [fo.imag[:, :half], fo.real[:, :half]]]))
    w2 = f32(np.block([[ci.real, ci.imag], [-ci.imag, ci.real]]))
    w2i = f32(np.block([[ci.real, -ci.imag], [ci.imag, ci.real]]))
    g4 = f32(np.block([[fo.real[:half], fo.imag[:half]], [-fo.imag[:half], fo.real[:half]]]) / n)
    return dict(g1_real=g1_real, g1=g1, w2=w2, w2i=w2i, g4=g4, twr=f32(tw.real), twi=f32(tw.imag))


HY_CHUNK = 16
HY_SPEC_CHUNK = 32


def _fft_stage1(g1, y2, twr, twi):
    nch = y2.shape[1] // FFT_NI
    o1 = jnp.dot(g1, y2, preferred_element_type=F32)
    yr, yi = o1[:FFT_NO].astype(BF16), o1[FFT_NO:].astype(BF16)
    ar = yr * twr - yi * twi
    ai = yr * twi + yi * twr
    lanes = lambda ci: slice(ci * FFT_NI, (ci + 1) * FFT_NI)
    return jnp.concatenate([jnp.concatenate([ar[:, lanes(ci)], ai[:, lanes(ci)]], axis=1)
                            for ci in range(nch)], axis=0)


def _hy_spectrum_kernel(k_ref, g1_ref, w2_ref, twr_ref, twi_ref, o_ref):
    cb = k_ref.shape[0]
    ch = HY_SPEC_CHUNK
    g1 = g1_ref[...]
    w2 = w2_ref[...]
    twr = twr_ref[...]
    twi = twi_ref[...]

    def body(j, _):
        chans = pl.ds(pl.multiple_of(j * ch, ch), ch)
        ks = k_ref[chans]
        k2 = jnp.concatenate([ks[ci] for ci in range(ch)], axis=1).astype(BF16)
        z = jnp.dot(_fft_stage1(g1, k2, twr, twi), w2, preferred_element_type=F32)
        o_ref[chans] = z.reshape(ch, FFT_NO, 2 * FFT_NI).astype(o_ref.dtype)
        return 0

    lax.fori_loop(0, cb // ch, body, 0)


def _tiled_twiddles(tabs, nch):
    tile = lambda a: jnp.asarray(np.tile(a, (1, nch)), BF16)
    return tile(tabs["twr"]), tile(tabs["twi"])


def _hy_spectrum(kfilt, tabs, cb):
    g, c = kfilt.shape[:2]
    const = lambda a: pl.BlockSpec(a.shape, lambda i, j: (0,) * a.ndim)
    g1 = jnp.asarray(tabs["g1_real"], BF16)
    w2 = jnp.asarray(tabs["w2"], BF16)
    twr, twi = _tiled_twiddles(tabs, HY_SPEC_CHUNK)
    return pl.pallas_call(
        _hy_spectrum_kernel,
        grid=(g, c // cb),
        in_specs=[pl.BlockSpec((None, cb, FFT_NO, FFT_NI), lambda i, j: (i, j, 0, 0)),
                  const(g1), const(w2), const(twr), const(twi)],
        out_specs=pl.BlockSpec((None, cb, FFT_NO, 2 * FFT_NI), lambda i, j: (i, j, 0, 0)),
        out_shape=jax.ShapeDtypeStruct((g, c, FFT_NO, 2 * FFT_NI), BF16),
        compiler_params=_cparams("parallel", "arbitrary"),
        name="hyena_spectrum",
    )(kfilt, g1, w2, twr, twi)


def _shift_conv3(x, w0, w1, w2, b):
    nrow, nlane = x.shape
    row = lax.broadcasted_iota(jnp.int32, x.shape, 0)
    lane = lax.broadcasted_iota(jnp.int32, x.shape, 1)
    r = pltpu.roll(x, 1, axis=1)
    prev = jnp.where(lane == 0, pltpu.roll(r, 1, axis=0), r)
    prev = jnp.where((lane == 0) & (row == 0), 0.0, prev)
    r = pltpu.roll(x, nlane - 1, axis=1)
    nxt = jnp.where(lane == nlane - 1, pltpu.roll(r, nrow - 1, axis=0), r)
    nxt = jnp.where((lane == nlane - 1) & (row == nrow - 1), 0.0, nxt)
    return prev * w0 + x * w1 + nxt * w2 + b


def _hyena_kernel(x_ref, kf_ref, pm_ref, g1_ref, w2_ref, w2i_ref, g4_ref, twr_ref, twi_ref, o_ref):
    cb = x_ref.shape[2]
    half = FFT_NO // 2
    ch = HY_CHUNK
    g1 = g1_ref[...]
    g4 = g4_ref[...]
    w2 = w2_ref[...]
    w2i = w2i_ref[...]
    twr = twr_ref[...]
    twi = twi_ref[...]
    lanes = lambda ci: slice(ci * FFT_NI, (ci + 1) * FFT_NI)
    krows = lambda ci: slice(ci * FFT_NO, (ci + 1) * FFT_NO)

    def body(j, _):
        c0 = pl.multiple_of(j * ch, ch)
        pm = pm_ref[:, pl.ds(c0, ch)]
        conv = []
        for part in range(3):
            halves = []
            for bb in range(2):
                xs = x_ref[bb, part, pl.ds(c0, ch)].astype(F32)
                halves.append(jnp.concatenate(
                    [_shift_conv3(xs[ci], pm[part, ci], pm[3 + part, ci], pm[6 + part, ci], pm[9 + part, ci])
                     for ci in range(ch)], axis=1))
            conv.append(jnp.concatenate(halves, axis=0))
        y = conv[2]
        for order in range(HY_ORDER):
            z = jnp.dot(_fft_stage1(g1, y.astype(BF16), twr, twi), w2, preferred_element_type=F32)
            kf = kf_ref[order, pl.ds(c0, ch)].reshape(ch * FFT_NO, 2 * FFT_NI)
            zr, zi = z[:, :FFT_NI].astype(BF16), z[:, FFT_NI:].astype(BF16)
            kr, ki = kf[:, :FFT_NI], kf[:, FFT_NI:]
            p = jnp.concatenate([zr * kr - zi * ki, zr * ki + zi * kr], axis=1)
            q = jnp.dot(p, w2i, preferred_element_type=F32).astype(BF16)
            qr = jnp.concatenate([q[krows(ci), :FFT_NI] for ci in range(ch)], axis=1)
            qi = jnp.concatenate([q[krows(ci), FFT_NI:] for ci in range(ch)], axis=1)
            qs = jnp.concatenate([qr * twr + qi * twi, qi * twr - qr * twi], axis=0)
            cv = jnp.dot(g4, qs, preferred_element_type=F32)
            bias = jnp.concatenate([pm[12 + order, ci] for ci in range(ch)], axis=1)
            y = conv[order] * (cv + bias * y)
        for ci in range(ch):
            o_ref[0, c0 + ci] = y[:half, lanes(ci)].astype(o_ref.dtype)
            o_ref[1, c0 + ci] = y[half:, lanes(ci)].astype(o_ref.dtype)
        return 0

    lax.fori_loop(0, cb // ch, body, 0)


def _hyena(xt, kf, layer, pm, tabs, cb):
    nb, _, c, half, _ = xt.shape
    const = lambda a: pl.BlockSpec(a.shape, lambda j, p: (0,) * a.ndim)
    bf = lambda name: jnp.asarray(tabs[name], BF16)
    g1, w2, w2i, g4 = bf("g1"), bf("w2"), bf("w2i"), bf("g4")
    twr, twi = _tiled_twiddles(tabs, HY_CHUNK)
    return pl.pallas_call(
        _hyena_kernel,
        grid=(c // cb, nb // 2),
        in_specs=[
            pl.BlockSpec((2, 3, cb, half, FFT_NI), lambda j, p: (p, 0, j, 0, 0)),
            pl.BlockSpec((None, HY_ORDER, cb, FFT_NO, 2 * FFT_NI), lambda j, p: (layer, 0, j, 0, 0)),
            pl.BlockSpec((14, cb, 1, FFT_NI), lambda j, p: (0, j, 0, 0)),
            const(g1), const(w2), const(w2i), const(g4), const(twr), const(twi),
        ],
        out_specs=pl.BlockSpec((2, cb, half, FFT_NI), lambda j, p: (p, j, 0, 0)),
        out_shape=jax.ShapeDtypeStruct((nb, c, half, FFT_NI), BF16),
        compiler_params=_cparams("parallel", "arbitrary"),
        name="hyena",
    )(xt, kf, pm, g1, w2, w2i, g4, twr, twi)


def _dft_tables(seq):
    n = 2 * seq
    wmat = np.exp(-2j * np.pi * np.outer(np.arange(n), np.arange(n)) / n)
    f32 = lambda x: np.ascontiguousarray(x, dtype=np.float32)
    fwd_real = f32(np.concatenate([wmat.real, wmat.imag], axis=1))
    ws = wmat[:seq]
    fwd = f32(np.block([[ws.real, ws.imag], [-ws.imag, ws.real]]))
    wi = np.conj(wmat)[:, :seq] / n
    inv = f32(np.block([[wi.real, wi.imag], [-wi.imag, wi.real]]))
    return dict(fwd_real=fwd_real, fwd=fwd, inv=inv)


def _rowdft_kernel(x_ref, w_ref, o_ref):
    o_ref[...] = jnp.dot(x_ref[...].astype(BF16), w_ref[...], preferred_element_type=F32)


def _rowdft(x, w, tr):
    m, k = x.shape
    n = w.shape[1]
    return pl.pallas_call(
        _rowdft_kernel,
        grid=(m // tr,),
        in_specs=[pl.BlockSpec((tr, k), lambda i: (i, 0)), pl.BlockSpec((k, n), lambda i: (0, 0))],
        out_specs=pl.BlockSpec((tr, n), lambda i: (i, 0)),
        out_shape=jax.ShapeDtypeStruct((m, n), F32),
        compiler_params=_cparams("parallel"),
        name="row_dft",
    )(x, w)


def _lane_conv3(x, w0, w1, w2, b):
    seq = x.shape[1]
    lane = lax.broadcasted_iota(jnp.int32, x.shape, 1)
    prev = jnp.where(lane == 0, 0.0, pltpu.roll(x, 1, axis=1))
    nxt = jnp.where(lane == seq - 1, 0.0, pltpu.roll(x, seq - 1, axis=1))
    return prev * w0 + x * w1 + nxt * w2 + b


def _hyena_ctx_kernel(x_ref, kf_ref, pm_ref, fwd_ref, inv_ref, o_ref):
    seq = x_ref.shape[3]
    n = 2 * seq
    conv = [[_lane_conv3(x_ref[bb, part].astype(F32), pm_ref[part], pm_ref[3 + part], pm_ref[6 + part],
                         pm_ref[9 + part]) for bb in range(2)] for part in range(3)]
    ya, yb = conv[2]
    for order in range(HY_ORDER):
        z = jnp.dot(jnp.concatenate([ya, yb], axis=1).astype(BF16), fwd_ref[...], preferred_element_type=F32)
        kf = kf_ref[order]
        zr, zi, kr, ki = z[:, :n], z[:, n:], kf[:, :n], kf[:, n:]
        p = jnp.concatenate([zr * kr - zi * ki, zr * ki + zi * kr], axis=1).astype(BF16)
        cv = jnp.dot(p, inv_ref[...], preferred_element_type=F32)
        bias = pm_ref[12 + order]
        ya = conv[order][0] * (cv[:, :seq] + bias * ya)
        yb = conv[order][1] * (cv[:, seq:] + bias * yb)
    o_ref[0] = ya.astype(o_ref.dtype)
    o_ref[1] = yb.astype(o_ref.dtype)


def _hyena_ctx(xt, kf, pm, tabs, cb):
    nb, _, c, seq = xt.shape
    fwd, inv = jnp.asarray(tabs["fwd"], BF16), jnp.asarray(tabs["inv"], BF16)
    const = lambda a: pl.BlockSpec(a.shape, lambda j, p: (0,) * a.ndim)
    return pl.pallas_call(
        _hyena_ctx_kernel,
        grid=(c // cb, nb // 2),
        in_specs=[
            pl.BlockSpec((2, 3, cb, seq), lambda j, p: (p, 0, j, 0)),
            pl.BlockSpec((HY_ORDER, cb, 4 * seq), lambda j, p: (0, j, 0)),
            pl.BlockSpec((14, cb, 1), lambda j, p: (0, j, 0)),
            const(fwd), const(inv),
        ],
        out_specs=pl.BlockSpec((2, cb, seq), lambda j, p: (p, j, 0)),
        out_shape=jax.ShapeDtypeStruct((nb, c, seq), BF16),
        compiler_params=_cparams("parallel", "arbitrary"),
        name="hyena_ctx",
    )(xt, kf, pm, fwd, inv)


def _rope_tables(seq):
    t = np.arange(seq)
    pos = np.stack([t // GRID_W, t % GRID_W], axis=1).astype(np.float64)
    n = RET_DK // 4
    inv = ROPE_BASE ** (-np.arange(n, dtype=np.float64) / n)
    lane = np.arange(LANES) % RET_DK
    ang = pos[:, lane // (2 * n)] * inv[lane % n][None]
    sign = np.where(lane % (2 * n) < n, -1.0, 1.0)
    return np.cos(ang).astype(np.float32), (np.sin(ang) * sign[None]).astype(np.float32)


def _hy_filters(seq, hy_f_w1, hy_f_b1, hy_f_w2, hy_f_b2, hy_f_w3, hy_f_freq):
    depth, _, hid = hy_f_w1.shape
    c = hy_f_w3.shape[2] // (2 * HY_ORDER)
    feat2, tt, mask = _hy_features(seq)
    w1t = jnp.pad(hy_f_w1.transpose(0, 2, 1), ((0, 0), (0, 0), (0, HY_FEAT_ROWS - HY_EMB)))
    h2 = _hy_mlp(jnp.asarray(feat2), w1t, hy_f_b1, hy_f_w2.transpose(0, 2, 1), hy_f_b2, hy_f_freq)
    w3t = hy_f_w3.reshape(depth, hid, HY_ORDER, 2, c).transpose(0, 2, 3, 4, 1)
    deltas = np.abs(np.linspace(math.log(HY_TARGET) / HY_FAST, math.log(HY_TARGET) / HY_SLOW, c))
    deltas = jnp.asarray(deltas.astype(np.float32).reshape(c, 1))
    return _hy_filter(h2, w3t, deltas, jnp.asarray(tt), jnp.asarray(mask), min(c, 8 * 1024 * 128 // (2 * seq)))


def _hy_params(hy_conv_w, hy_conv_b, hy_bias):
    c = hy_bias.shape[1]
    return jnp.concatenate([hy_conv_w.reshape(9, c), hy_conv_b.reshape(3, c), hy_bias], axis=0)


def kernel(x, c, ctx, c_ctx, w_mod, b_mod, g_norm1, g_norm2, w_in, hy_conv_w, hy_conv_b, hy_f_w1, hy_f_b1, hy_f_w2, hy_f_b2, hy_f_w3, hy_f_freq, hy_bias, ret_log_decay, na_rpb, w_branch, w_out, ffn_w_in, ffn_conv_w, ffn_conv_b, ffn_w_out, g_final):
    nb, seq, d = x.shape
    cseq = ctx.shape[1]
    depth = w_mod.shape[0]
    hy_w = hy_bias.shape[2]
    t_lat, t_ctx = nb * seq, nb * cseq
    assert seq % TM == 0 and t_ctx % TM == 0 and TM % cseq == 0 and nb % 2 == 0 and nb < SUBLANES
    assert 2 * seq == FFT_NO * FFT_NI and seq % GRID_W == 0 and cseq & (cseq - 1) == 0
    blk_lat, blk_all = t_lat // TM, (t_lat + t_ctx) // TM
    mrow = jnp.asarray(np.concatenate([np.repeat(np.arange(nb), seq // TM), np.full(t_ctx // TM, nb)]), jnp.int32)
    seqlen = jnp.asarray(np.concatenate([np.full(blk_lat, seq), np.full(t_ctx // TM, cseq)]), jnp.int32)
    per = TM // TR
    mrow_r = jnp.repeat(mrow, per)
    tn = 1024

    cond = jnp.zeros((SUBLANES, d), F32).at[:nb].set(c).at[nb].set(c_ctx)
    mods = _mods(cond, w_mod, b_mod).reshape(depth, SUBLANES, 6, 1, d)

    cos, sin = (jnp.asarray(a) for a in _rope_tables(seq))
    tabs = _fft_tables()
    ctabs = _dft_tables(cseq)
    filt = _hy_filters(seq, hy_f_w1, hy_f_b1, hy_f_w2, hy_f_b2, hy_f_w3, hy_f_freq)
    spec = _hy_spectrum(filt.reshape(depth * HY_ORDER, hy_w, FFT_NO, FFT_NI), tabs, 32)
    spec = spec.reshape(depth, HY_ORDER, hy_w, FFT_NO, 2 * FFT_NI)
    cfilt = _hy_filters(cseq, hy_f_w1, hy_f_b1, hy_f_w2, hy_f_b2, hy_f_w3, hy_f_freq)
    cspec = _rowdft(cfilt.reshape(depth * HY_ORDER * hy_w, 2 * cseq), jnp.asarray(ctabs["fwd_real"], BF16), 1024)
    cspec = cspec.reshape(depth, HY_ORDER, hy_w, 4 * cseq)

    xs = jnp.concatenate([x.reshape(t_lat, d), ctx.reshape(t_ctx, d)], axis=0)
    assert hy_w == HY_W and w_in.shape[2] == COL_GATE + N_BRANCH * d
    hy_cols = COL_GATE - COL_HY
    s_zero = jnp.zeros((nb, RET_HEADS, 2, LANES, RET_DV), F32)
    hn = _normmod(xs, g_norm1[0], mods[0], mrow, 0, 1, blk_all)
    for l in range(depth):
        last = l == depth - 1
        nblk = blk_lat if last else blk_all
        rows = nblk * TM
        z = _matmul(hn, w_in, l, blk_all, tn, w_in.shape[2] - hy_cols, skip=(COL_HY // tn, hy_cols // tn))
        hy_proj = functools.partial(_matmul_nt, hn, w_in, l, col0=COL_HY, n_out=hy_cols, nb=nb, tc=512)
        zh = hy_proj(blk0=0, nblk=blk_lat, seq=seq)

        y_ret, s_ctx = _retention(z, ret_log_decay[l], cos, sin, s_zero, nb=nb, seq=cseq, row0=t_lat, rope=False,
                                  out_rows=blk_all * TM)
        y_ret, _ = _retention(z, ret_log_decay[l], cos, sin, s_ctx, nb=nb, seq=seq, row0=0, rope=True,
                              out_rows=blk_all * TM, y_prev=y_ret)
        y_na = _na_attention(z, na_rpb[l], nb=nb, seq=seq, cseq=cseq, crow0=t_lat, out_rows=rows)
        pm = _hy_params(hy_conv_w[l], hy_conv_b[l], hy_bias[l])
        pm_lat = jnp.broadcast_to(pm[:, :, None, None], (14, hy_w, 1, FFT_NI))
        y_hy = _hyena(zh.reshape(nb, 3, hy_w, FFT_NO // 2, FFT_NI), spec, l, pm_lat, tabs, 32)
        y_hy = y_hy.reshape(nb, hy_w, seq).transpose(0, 2, 1).reshape(t_lat, hy_w)
        if not last:
            y_na = _dense_attention(z, y_na, nb=nb, cseq=cseq, crow0=t_lat)
            zc = hy_proj(blk0=blk_lat, nblk=t_ctx // TM, seq=cseq)
            yc_hy = _hyena_ctx(zc.reshape(nb, 3, hy_w, cseq), cspec[l], pm[:, :, None], ctabs, 256)
            y_hy = jnp.concatenate([y_hy, yc_hy.transpose(0, 2, 1).reshape(t_ctx, hy_w)], axis=0)
        acc = _merge((y_hy, y_ret, y_na), z, COL_HY, w_branch[l].astype(BF16), nblk, 1024)
        xs, hn = _out_proj(acc, w_out[l].astype(BF16), xs, mods[l], mrow_r, g_norm2[l], nblk * per)
        u = _matmul(hn, ffn_w_in, l, nblk, tn, ffn_w_in.shape[2])
        hg = _ffn_gate(u, ffn_conv_w[l], ffn_conv_b[l], seqlen, nblk, 512)
        nxt = l if last else l + 1
        w_ffo = ffn_w_out[l].astype(BF16)
        if last:
            res = _ffn_out(hg, w_ffo, xs, mods[l], mods[l], g_final, mrow_r, nblk * per, TR, 512, True)
        else:
            res = _ffn_out(hg, w_ffo, xs, mods[l], mods[nxt], g_norm1[nxt], mrow, nblk, TM, 256, False)
        if not last:
            xs, hn = res
    return res[0].reshape(nb, seq, d)
```

```python
import functools
import math

import numpy as np
import jax
import jax.numpy as jnp
from jax import lax
from jax.experimental import pallas as pl
from jax.experimental.pallas import tpu as pltpu

F32 = jnp.float32
BF16 = jnp.bfloat16

GRID_W = 64
N_BRANCH = 3
HY_ORDER = 2
HY_EMB = 33
HY_FAST = 0.3
HY_SLOW = 1.5
HY_TARGET = 1e-2
RET_HEADS = 8
RET_DK = 64
RET_DV = 128
RET_CHUNK = 128
NA_HEADS = 8
NA_DH = 128
NA_WR = 8
NA_WC = 16
ROPE_BASE = 10000.0
EPS = 1e-6
NEG = -1e30
HY_W = 1024

_RET_QK_W, _RET_V_W, _NA_W = RET_HEADS * RET_DK, RET_HEADS * RET_DV, NA_HEADS * NA_DH
COL_RET_K = 0
COL_RET_V = COL_RET_K + _RET_QK_W
COL_NA_K = COL_RET_V + _RET_V_W
COL_NA_V = COL_NA_K + _NA_W
COL_RET_Q = COL_NA_V + _NA_W
COL_RET_G = COL_RET_Q + _RET_QK_W
COL_NA_Q = COL_RET_G + _RET_V_W
COL_HY = COL_NA_Q + _NA_W
COL_GATE = COL_HY + (HY_ORDER + 1) * HY_W

LANES = 128
SUBLANES = 8
VMEM_LIMIT = 56 * 1024 * 1024

TM = 1024


def _cparams(*sem):
    return pltpu.CompilerParams(dimension_semantics=sem, vmem_limit_bytes=VMEM_LIMIT)


def _mods_kernel(a_ref, w_ref, b_ref, o_ref):
    k = pl.program_id(1)
    a = a_ref[...]
    a = a * jax.nn.sigmoid(a)
    part = jnp.dot(a.astype(BF16), w_ref[...].astype(BF16), preferred_element_type=F32)

    @pl.when(k == 0)
    def _():
        o_ref[...] = part + b_ref[...]

    @pl.when(k > 0)
    def _():
        o_ref[...] += part


def _mods(cond, w_mod, b_mod):
    depth, d, n = w_mod.shape
    tk = LANES
    return pl.pallas_call(
        _mods_kernel,
        grid=(depth, d // tk),
        in_specs=[
            pl.BlockSpec((SUBLANES, tk), lambda l, k: (0, k)),
            pl.BlockSpec((None, tk, n), lambda l, k: (l, k, 0)),
            pl.BlockSpec((None, 1, n), lambda l, k: (l, 0, 0)),
        ],
        out_specs=pl.BlockSpec((None, SUBLANES, n), lambda l, k: (l, 0, 0)),
        out_shape=jax.ShapeDtypeStruct((depth, SUBLANES, n), F32),
        compiler_params=_cparams("parallel", "arbitrary"),
        name="mods",
    )(cond, w_mod, b_mod.reshape(depth, 1, n))


def _normmod_kernel(mrow_ref, x_ref, g_ref, sh_ref, sc_ref, o_ref):
    x = x_ref[...]
    y = x * lax.rsqrt(jnp.mean(x * x, axis=-1, keepdims=True) + EPS)
    y = y * g_ref[...]
    o_ref[...] = (y * (1.0 + sc_ref[...]) + sh_ref[...]).astype(o_ref.dtype)


def _normmod(x, g, mods, mrow, shift_idx, scale_idx, nblk):
    m, d = x.shape
    grid_spec = pltpu.PrefetchScalarGridSpec(
        num_scalar_prefetch=1,
        grid=(nblk,),
        in_specs=[
            pl.BlockSpec((TM, d), lambda i, mr: (i, 0)),
            pl.BlockSpec((1, d), lambda i, mr: (0, 0)),
            pl.BlockSpec((None, None, 1, d), lambda i, mr: (mr[i], shift_idx, 0, 0)),
            pl.BlockSpec((None, None, 1, d), lambda i, mr: (mr[i], scale_idx, 0, 0)),
        ],
        out_specs=pl.BlockSpec((TM, d), lambda i, mr: (i, 0)),
    )
    return pl.pallas_call(
        _normmod_kernel,
        grid_spec=grid_spec,
        out_shape=jax.ShapeDtypeStruct((m, d), BF16),
        compiler_params=_cparams("parallel"),
        name="normmod",
    )(mrow, x, g.reshape(1, d), mods, mods)


def _mm_kernel(a_ref, w_ref, o_ref, w_scr):
    @pl.when(pl.program_id(1) == 0)
    def _():
        w_scr[...] = w_ref[...].astype(BF16)

    o_ref[...] = jnp.dot(a_ref[...], w_scr[...], preferred_element_type=F32).astype(o_ref.dtype)


def _matmul(a, w, layer, nblk, tn, n_out, skip=None):
    m, k = a.shape
    col = (lambda j: j) if skip is None else (lambda j: j + jnp.where(j >= skip[0], skip[1], 0))
    return pl.pallas_call(
        _mm_kernel,
        grid=(n_out // tn, nblk),
        in_specs=[pl.BlockSpec((TM, k), lambda j, i: (i, 0)),
                  pl.BlockSpec((None, k, tn), lambda j, i: (layer, 0, col(j)))],
        out_specs=pl.BlockSpec((TM, tn), lambda j, i: (i, j)),
        out_shape=jax.ShapeDtypeStruct((m, n_out), BF16),
        scratch_shapes=[pltpu.VMEM((k, tn), BF16)],
        compiler_params=_cparams("arbitrary", "arbitrary"),
        name="matmul",
    )(a, w)


def _mm_nt_kernel(w_ref, a_ref, o_ref, wt_scr):
    @pl.when(pl.program_id(1) == 0)
    def _():
        wt_scr[...] = w_ref[...].T.astype(BF16)

    r = lax.dot_general(wt_scr[...], a_ref[...], (((1,), (1,)), ((), ())), preferred_element_type=F32)
    per = o_ref.shape[0]
    width = r.shape[1] // per
    for s in range(per):
        o_ref[s] = r[:, s * width:(s + 1) * width].astype(o_ref.dtype)


def _matmul_nt(a, w, layer, *, col0, n_out, blk0, nblk, nb, seq, tc):
    k = a.shape[1]
    per, sblk = max(TM // seq, 1), max(seq // TM, 1)
    return pl.pallas_call(
        _mm_nt_kernel,
        grid=(n_out // tc, nblk),
        in_specs=[pl.BlockSpec((None, k, tc), lambda j, i: (layer, 0, col0 // tc + j)),
                  pl.BlockSpec((TM, k), lambda j, i: (blk0 + i, 0))],
        out_specs=pl.BlockSpec((per, tc, TM // per), lambda j, i: (i // sblk, j, i % sblk)),
        out_shape=jax.ShapeDtypeStruct((nb, n_out, seq), BF16),
        scratch_shapes=[pltpu.VMEM((tc, k), BF16)],
        compiler_params=_cparams("arbitrary", "arbitrary"),
        name="matmul_nt",
    )(w, a)


TR = 512


def _norm_modulate(x, g, shift, scale):
    y = x * lax.rsqrt(jnp.mean(x * x, axis=-1, keepdims=True) + EPS)
    return (y * g) * (1.0 + scale) + shift


def _outproj_kernel(mrow_ref, a_ref, w_ref, x_ref, gate_ref, g_ref, sh_ref, sc_ref, x_out, h_out):
    y = jnp.dot(a_ref[...], w_ref[...], preferred_element_type=F32)
    x = x_ref[...] + gate_ref[...] * y
    x_out[...] = x
    h_out[...] = _norm_modulate(x, g_ref[...], sh_ref[...], sc_ref[...]).astype(h_out.dtype)


def _out_proj(a, w, x, mods, mrow, g_norm, nblk):
    m, k = a.shape
    d = w.shape[1]
    mod = lambda idx: pl.BlockSpec((None, None, 1, d), lambda i, mr: (mr[i], idx, 0, 0))
    row = pl.BlockSpec((TR, d), lambda i, mr: (i, 0))
    grid_spec = pltpu.PrefetchScalarGridSpec(
        num_scalar_prefetch=1,
        grid=(nblk,),
        in_specs=[pl.BlockSpec((TR, k), lambda i, mr: (i, 0)), pl.BlockSpec((k, d), lambda i, mr: (0, 0)), row,
                  mod(2), pl.BlockSpec((1, d), lambda i, mr: (0, 0)), mod(3), mod(4)],
        out_specs=[row, row],
    )
    return pl.pallas_call(
        _outproj_kernel,
        grid_spec=grid_spec,
        out_shape=[jax.ShapeDtypeStruct((m, d), F32), jax.ShapeDtypeStruct((m, d), BF16)],
        compiler_params=_cparams("parallel"),
        name="out_proj",
    )(mrow, a, w, x, mods, g_norm.reshape(1, d), mods, mods)


def _merge_kernel(y0_ref, y1_ref, y2_ref, g0_ref, g1_ref, g2_ref, w_ref, o_ref):
    acc = None
    for i, (y_ref, g_ref) in enumerate(((y0_ref, g0_ref), (y1_ref, g1_ref), (y2_ref, g2_ref))):
        t = jnp.dot(y_ref[...], w_ref[i], preferred_element_type=F32)
        t = jax.nn.sigmoid(g_ref[...].astype(F32)) * t
        acc = t if acc is None else acc + t
    o_ref[...] = acc.astype(o_ref.dtype)


def _merge(ys, z, gate_col0, w_branch, nblk, tn):
    m, bw = ys[0].shape
    d = w_branch.shape[2]
    gspec = lambda i_br: pl.BlockSpec((TM, tn), lambda i, j: (i, (gate_col0 + i_br * d) // tn + j))
    yspec = pl.BlockSpec((TM, bw), lambda i, j: (i, 0))
    return pl.pallas_call(
        _merge_kernel,
        grid=(nblk, d // tn),
        in_specs=[yspec, yspec, yspec, gspec(0), gspec(1), gspec(2),
                  pl.BlockSpec((N_BRANCH, bw, tn), lambda i, j: (0, 0, j))],
        out_specs=pl.BlockSpec((TM, tn), lambda i, j: (i, j)),
        out_shape=jax.ShapeDtypeStruct((m, d), BF16),
        compiler_params=_cparams("parallel", "arbitrary"),
        name="merge",
    )(ys[0], ys[1], ys[2], z, z, z, w_branch)


def _ffn_gate_kernel(seq_ref, a_ref, ap_ref, an_ref, b_ref, cw_ref, cb_ref, o_ref):
    i = pl.program_id(0)
    seq_m1 = seq_ref[i] - 1
    tm = a_ref.shape[0]
    row = lax.broadcasted_iota(jnp.int32, (tm, 1), 0)
    halo_p = jnp.where(((i * tm) & seq_m1) == 0, 0.0, ap_ref[SUBLANES - 1:SUBLANES, :].astype(F32))
    halo_n = jnp.where((((i + 1) * tm) & seq_m1) == 0, 0.0, an_ref[0:1, :].astype(F32))

    def finish(inner_edges):
        a = a_ref[...].astype(F32)
        prev = jnp.where(row == 0, halo_p, pltpu.roll(a, 1, axis=0))
        nxt = jnp.where(row == tm - 1, halo_n, pltpu.roll(a, tm - 1, axis=0))
        if inner_edges:
            pos = row & seq_m1
            prev = jnp.where((pos == 0) & (row > 0), 0.0, prev)
            nxt = jnp.where((pos == seq_m1) & (row < tm - 1), 0.0, nxt)
        cw = cw_ref[...]
        conv = prev * cw[0:1, :] + a * cw[1:2, :] + nxt * cw[2:3, :] + cb_ref[...]
        o_ref[...] = jax.nn.gelu(conv.astype(BF16)) * b_ref[...]

    @pl.when(seq_m1 >= tm - 1)
    def _():
        finish(False)

    @pl.when(seq_m1 < tm - 1)
    def _():
        finish(True)


def _ffn_gate(u, conv_w, conv_b, seqlen, nblk, tc):
    m, ff2 = u.shape
    ff = ff2 // 2
    nrow8 = m // SUBLANES
    r8 = TM // SUBLANES
    grid_spec = pltpu.PrefetchScalarGridSpec(
        num_scalar_prefetch=1,
        grid=(nblk, ff // tc),
        in_specs=[
            pl.BlockSpec((TM, tc), lambda i, j, s: (i, j)),
            pl.BlockSpec((SUBLANES, tc), lambda i, j, s: (jnp.maximum(i * r8 - 1, 0), j)),
            pl.BlockSpec((SUBLANES, tc), lambda i, j, s: (jnp.minimum((i + 1) * r8, nrow8 - 1), j)),
            pl.BlockSpec((TM, tc), lambda i, j, s: (i, ff // tc + j)),
            pl.BlockSpec((3, tc), lambda i, j, s: (0, j)),
            pl.BlockSpec((1, tc), lambda i, j, s: (0, j)),
        ],
        out_specs=pl.BlockSpec((TM, tc), lambda i, j, s: (i, j)),
    )
    return pl.pallas_call(
        _ffn_gate_kernel,
        grid_spec=grid_spec,
        out_shape=jax.ShapeDtypeStruct((m, ff), BF16),
        compiler_params=_cparams("parallel", "arbitrary"),
        name="ffn_gate",
    )(seqlen, u, u, u, u, conv_w, conv_b.reshape(1, ff))


def _ffn_out_kernel(mrow_ref, h_ref, w_ref, x_ref, gate_ref, g_ref, sh_ref, sc_ref, *rest, final):
    outs, xrow = rest[:-1], rest[-1]
    j = pl.program_id(1)
    y = jnp.dot(h_ref[...], w_ref[...], preferred_element_type=F32)
    xt = x_ref[...] + gate_ref[...] * y
    xrow[j] = xt
    if not final:
        outs[0][...] = xt

    @pl.when(j == pl.num_programs(1) - 1)
    def _():
        x = jnp.concatenate([xrow[t] for t in range(xrow.shape[0])], axis=1)
        if final:
            y = x * lax.rsqrt(jnp.mean(x * x, axis=-1, keepdims=True) + EPS)
            outs[0][...] = y * g_ref[...]
        else:
            outs[1][...] = _norm_modulate(x, g_ref[...], sh_ref[...], sc_ref[...]).astype(outs[1].dtype)


def _ffn_out(h, w, x, mods, mods_next, g_next, mrow, nblk, tr, tn, final):
    m, ff = h.shape
    d = w.shape[1]
    mod = lambda arr_idx: pl.BlockSpec((None, None, 1, d), lambda i, j, mr: (mr[i], arr_idx, 0, 0))
    row = pl.BlockSpec((tr, d), lambda i, j, mr: (i, 0))
    tile = pl.BlockSpec((tr, tn), lambda i, j, mr: (i, j))
    grid_spec = pltpu.PrefetchScalarGridSpec(
        num_scalar_prefetch=1,
        grid=(nblk, d // tn),
        in_specs=[
            pl.BlockSpec((tr, ff), lambda i, j, mr: (i, 0)),
            pl.BlockSpec((ff, tn), lambda i, j, mr: (0, j)),
            tile,
            pl.BlockSpec((None, None, 1, tn), lambda i, j, mr: (mr[i], 5, 0, j)),
            pl.BlockSpec((1, d), lambda i, j, mr: (0, 0)), mod(0), mod(1),
        ],
        out_specs=[row] if final else [tile, row],
        scratch_shapes=[pltpu.VMEM((d // tn, tr, tn), F32)],
    )
    out_shape = [jax.ShapeDtypeStruct((m, d), F32)] + ([] if final else [jax.ShapeDtypeStruct((m, d), BF16)])
    return pl.pallas_call(
        functools.partial(_ffn_out_kernel, final=final),
        grid_spec=grid_spec,
        out_shape=out_shape,
        compiler_params=_cparams("parallel", "arbitrary"),
        name="ffn_out",
    )(mrow, h, w, x, mods, g_next.reshape(1, d), mods_next, mods_next)


def _rope_pair(x, cos, sin):
    lane = lax.broadcasted_iota(jnp.int32, x.shape, 1)
    first = lax.rem(lane, 32) < 16
    partner = jnp.where(first, pltpu.roll(x, LANES - 16, axis=1), pltpu.roll(x, 16, axis=1))
    return x * cos + partner * sin


def _ret_kernel(lg_ref, q_ref, k_ref, v_ref, g_ref, cos_ref, sin_ref, s0_ref, *rest, rope, has_prev):
    o_ref, sfin_ref, qr_scr, kr_scr, kv_scr = rest[1:] if has_prev else rest
    hp = pl.program_id(1)
    seq = q_ref.shape[0]
    c = RET_CHUNK
    nchunk = seq // c
    ks = RET_DK ** -0.5
    q = q_ref[...].astype(F32)
    k = k_ref[...].astype(F32)
    if rope:
        q = _rope_pair(q, cos_ref[...], sin_ref[...])
        k = _rope_pair(k, cos_ref[...], sin_ref[...])
    qr_scr[...] = q
    kr_scr[...] = k * ks

    pos_r = lax.broadcasted_iota(jnp.int32, (c, 1), 0).astype(F32)
    rel = (lax.broadcasted_iota(jnp.int32, (c, c), 0) - lax.broadcasted_iota(jnp.int32, (c, c), 1)).astype(F32)
    lane = lax.broadcasted_iota(jnp.int32, (1, LANES), 1)

    heads = []
    for hh in range(2):
        lgf = lg_ref[0, 2 * hp + hh]
        lgb = lg_ref[1, 2 * hp + hh]
        heads.append(dict(
            hm=(lane // RET_DK == hh).astype(F32),
            zeta_f=jnp.exp(lgf * (c - 1 - pos_r)), zeta_b=jnp.exp(lgb * pos_r),
            xi_f=jnp.exp(lgf * (pos_r + 1.0)), xi_b=jnp.exp(lgb * (c - pos_r)),
            dmat=jnp.where(rel >= 0, jnp.exp(lgf * jnp.maximum(rel, 0.0)), jnp.exp(lgb * jnp.maximum(-rel, 0.0))),
            gf=jnp.exp(lgf * c), gb=jnp.exp(lgb * c),
            vcols=slice(hh * RET_DV, (hh + 1) * RET_DV)))

    def kv_body(n, _):
        rows = pl.ds(pl.multiple_of(n * c, c), c)
        kr = kr_scr[rows, :]
        for hh, hd in enumerate(heads):
            kh = kr * hd["hm"]
            kz = jnp.concatenate([kh * hd["zeta_f"], kh * hd["zeta_b"]], axis=1).astype(BF16)
            kv_scr[hh, n] = lax.dot_general(kz, v_ref[rows, hd["vcols"]], (((0,), (0,)), ((), ())),
                                            preferred_element_type=F32)
        return 0

    lax.fori_loop(0, nchunk, kv_body, 0, unroll=min(8, nchunk))

    def scan(j, carry):
        nf, nbk = j, nchunk - 1 - j
        out = []
        for hh, hd in enumerate(heads):
            sf, sb = carry[2 * hh], carry[2 * hh + 1]
            tf = kv_scr[hh, nf, 0:LANES, :]
            kv_scr[hh, nf, 0:LANES, :] = sf
            tb = kv_scr[hh, nbk, LANES:2 * LANES, :]
            kv_scr[hh, nbk, LANES:2 * LANES, :] = sb
            out += [hd["gf"] * sf + tf, hd["gb"] * sb + tb]
        return tuple(out)

    fin = lax.fori_loop(0, nchunk, scan, (s0_ref[0, 0], s0_ref[0, 1], s0_ref[1, 0], s0_ref[1, 1]))
    for hh in range(2):
        sfin_ref[hh, 0] = fin[2 * hh]
        sfin_ref[hh, 1] = fin[2 * hh + 1]

    def out_body(n, _):
        rows = pl.ds(pl.multiple_of(n * c, c), c)
        qr = qr_scr[rows, :]
        kb = kr_scr[rows, :].astype(BF16)
        for hh, hd in enumerate(heads):
            qm = qr * hd["hm"]
            a = lax.dot_general(qm.astype(BF16), kb, (((1,), (1,)), ((), ())),
                                preferred_element_type=F32) * hd["dmat"]
            o = jnp.dot(a.astype(BF16), v_ref[rows, hd["vcols"]], preferred_element_type=F32)
            qx = jnp.concatenate([qm * hd["xi_f"], qm * hd["xi_b"]], axis=1).astype(BF16)
            o = o + jnp.dot(qx, kv_scr[hh, n].astype(BF16), preferred_element_type=F32)
            o = o * lax.rsqrt(jnp.mean(o * o, axis=-1, keepdims=True) + EPS)
            g = g_ref[rows, hd["vcols"]].astype(F32)
            o_ref[rows, hd["vcols"]] = (g * jax.nn.sigmoid(g) * o).astype(o_ref.dtype)
        return 0

    lax.fori_loop(0, nchunk, out_body, 0, unroll=min(8, nchunk))


def _retention(z, log_decay, cos, sin, s0, *, nb, seq, row0, rope, out_rows, y_prev=None):
    rb0 = row0 // seq
    hpairs = RET_HEADS // 2
    kcol, qcol = COL_RET_K // LANES, COL_RET_Q // LANES
    vcol, gcol = COL_RET_V // (2 * RET_DV), COL_RET_G // (2 * RET_DV)
    has_prev = y_prev is not None
    kernel = functools.partial(_ret_kernel, rope=rope, has_prev=has_prev)
    in_specs = [
        pl.BlockSpec(memory_space=pltpu.SMEM),
        pl.BlockSpec((seq, LANES), lambda b, p: (rb0 + b, qcol + p)),
        pl.BlockSpec((seq, LANES), lambda b, p: (rb0 + b, kcol + p)),
        pl.BlockSpec((seq, 2 * RET_DV), lambda b, p: (rb0 + b, vcol + p)),
        pl.BlockSpec((seq, 2 * RET_DV), lambda b, p: (rb0 + b, gcol + p)),
        pl.BlockSpec((seq, LANES), lambda b, p: (0, 0)),
        pl.BlockSpec((seq, LANES), lambda b, p: (0, 0)),
        pl.BlockSpec((None, 2, 2, LANES, RET_DV), lambda b, p: (b, p, 0, 0, 0)),
    ]
    args = [log_decay, z, z, z, z, cos, sin, s0]
    if has_prev:
        in_specs.append(pl.BlockSpec(memory_space=pl.ANY))
        args.append(y_prev)
    return pl.pallas_call(
        kernel,
        grid=(nb, hpairs),
        in_specs=in_specs,
        out_specs=[
            pl.BlockSpec((seq, 2 * RET_DV), lambda b, p: (rb0 + b, p)),
            pl.BlockSpec((None, 2, 2, LANES, RET_DV), lambda b, p: (b, p, 0, 0, 0)),
        ],
        out_shape=[
            jax.ShapeDtypeStruct((out_rows, RET_HEADS * RET_DV), BF16),
            jax.ShapeDtypeStruct((nb, RET_HEADS, 2, LANES, RET_DV), F32),
        ],
        scratch_shapes=[
            pltpu.VMEM((seq, LANES), F32),
            pltpu.VMEM((seq, LANES), F32),
            pltpu.VMEM((2, seq // RET_CHUNK, 2 * LANES, RET_DV), F32),
        ],
        input_output_aliases={len(args) - 1: 0} if has_prev else {},
        compiler_params=_cparams("parallel", "arbitrary"),
        name="retention_rope" if rope else "retention",
    )(*args)


NA_GROUP = 4
NA_KROWS = 12
NA_OVERLAP = 4


def _na_kernel(rpb_ref, q_ref, k_ref, v_ref, kc_ref, vc_ref, o_ref, bias_scr):
    seq = q_ref.shape[0]
    nrow = seq // GRID_W
    ndr, ndc = 2 * NA_WR - 1, 2 * NA_WC - 1
    scale = NA_DH ** -0.5
    nt = (((1,), (1,)), ((), ()))
    h = pl.program_id(0)
    lane = lax.broadcasted_iota(jnp.int32, (GRID_W, LANES), 1)

    @pl.when(pl.program_id(1) == 0)
    def _():
        qi = lax.broadcasted_iota(jnp.int32, (GRID_W, LANES), 0)
        ki = lane & (GRID_W - 1)
        dc = ki - qi + (NA_WC - 1)
        cs = jnp.clip(qi - NA_WC // 2, 0, GRID_W - NA_WC)
        colmask = (ki >= cs) & (ki < cs + NA_WC)

        def toeplitz(d):
            t = jnp.zeros((GRID_W, LANES), F32)
            for j in range(ndc):
                t = jnp.where(dc == j, rpb_ref[h * (ndr * ndc) + d * ndc + j], t)
            return t

        prev = jnp.zeros((GRID_W, LANES), F32)
        for i in range(ndr + 1):
            nxt = toeplitz(i) if i < ndr else jnp.zeros((GRID_W, LANES), F32)
            bias_scr[i] = jnp.where(colmask, jnp.where(lane < GRID_W, prev, nxt), NEG)
            prev = nxt

    kc = kc_ref[...]
    vc = vc_ref[...]
    kwin = NA_KROWS * GRID_W

    half = NA_WR // 2
    ntile = kwin // LANES

    def group(r0, interior):
        us = r0 - half if interior else min(max(r0 - half, 0), nrow - NA_KROWS)
        if isinstance(r0, int):
            qrows = pl.ds(r0 * GRID_W, NA_GROUP * GRID_W)
            krows = pl.ds(us * GRID_W, kwin)
        else:
            qrows = pl.ds(pl.multiple_of(r0 * GRID_W, NA_GROUP * GRID_W), NA_GROUP * GRID_W)
            krows = pl.ds(pl.multiple_of(us * GRID_W, GRID_W), kwin)
        q = q_ref[qrows, :]
        s = lax.dot_general(q, k_ref[krows, :], nt, preferred_element_type=F32)
        sc = lax.dot_general(q, kc, nt, preferred_element_type=F32) * scale
        e_rows, ec_rows, den_rows = [], [], []
        for u in range(NA_GROUP):
            qr = slice(u * GRID_W, (u + 1) * GRID_W)
            tiles = []
            for m in range(ntile):
                if interior:
                    off, idx = 2 * m - u, 2 * m - u + half
                else:
                    r = r0 + u
                    off = us + 2 * m - min(max(r - half, 0), nrow - NA_WR)
                    idx = min(max(us + 2 * m - r + NA_WR, 0), ndr)
                ok_lo, ok_hi = 0 <= off < NA_WR, 0 <= off + 1 < NA_WR
                if not (ok_lo or ok_hi):
                    tiles.append(None)
                    continue
                tab = bias_scr[idx]
                if not (ok_lo and ok_hi):
                    tab = tab + jnp.where(lane < GRID_W, 0.0 if ok_lo else NEG, 0.0 if ok_hi else NEG)
                st = s[qr, m * LANES:(m + 1) * LANES]
                tiles.append(jnp.where(tab > 0.5 * NEG, st * scale + tab, NEG))
            live = [t for t in tiles if t is not None]
            tmax = live[0]
            for t in live[1:]:
                tmax = jnp.maximum(tmax, t)
            scu = sc[qr, :]
            mx = jnp.maximum(jnp.max(tmax, axis=-1, keepdims=True), jnp.max(scu, axis=-1, keepdims=True))
            etiles = [None if t is None else jnp.exp(t - mx) for t in tiles]
            ecu = jnp.exp(scu - mx)
            esum = None
            for t in etiles:
                if t is not None:
                    esum = t if esum is None else esum + t
            den_rows.append(jnp.sum(esum, axis=-1, keepdims=True) + jnp.sum(ecu, axis=-1, keepdims=True))
            zero = jnp.zeros((GRID_W, LANES), BF16)
            e_rows.append(jnp.concatenate([zero if t is None else t.astype(BF16) for t in etiles], axis=1))
            ec_rows.append(ecu.astype(BF16))
        o = jnp.dot(jnp.concatenate(e_rows, axis=0), v_ref[krows, :], preferred_element_type=F32)
        o = o + jnp.dot(jnp.concatenate(ec_rows, axis=0), vc, preferred_element_type=F32)
        o_ref[qrows, :] = (o / jnp.concatenate(den_rows, axis=0)).astype(o_ref.dtype)

    ngroup = nrow // NA_GROUP
    g_lo = -(-half // NA_GROUP)
    g_hi = min((nrow - NA_KROWS + half) // NA_GROUP, (nrow - NA_WR + half - NA_GROUP + 1) // NA_GROUP)
    for g in range(g_lo):
        group(g * NA_GROUP, False)

    def body(i, _):
        for t in range(NA_OVERLAP):
            group((g_lo + NA_OVERLAP * i + t) * NA_GROUP, True)
        return 0

    nloop = (g_hi + 1 - g_lo) // NA_OVERLAP
    lax.fori_loop(0, nloop, body, 0)
    for g in range(g_lo + NA_OVERLAP * nloop, g_hi + 1):
        group(g * NA_GROUP, True)
    for g in range(g_hi + 1, ngroup):
        group(g * NA_GROUP, False)


def _na_attention(z, rpb, *, nb, seq, cseq, crow0, out_rows):
    kcol, vcol, qcol = COL_NA_K // NA_DH, COL_NA_V // NA_DH, COL_NA_Q // NA_DH
    crb0 = crow0 // cseq
    nrow = seq // GRID_W
    assert nrow >= NA_KROWS and nrow % NA_GROUP == 0 and GRID_W * 2 == LANES
    assert NA_KROWS % 2 == 0 and NA_KROWS >= NA_WR + NA_GROUP - 1
    return pl.pallas_call(
        _na_kernel,
        grid=(NA_HEADS, nb),
        in_specs=[
            pl.BlockSpec(memory_space=pltpu.SMEM),
            pl.BlockSpec((seq, NA_DH), lambda h, b: (b, qcol + h)),
            pl.BlockSpec((seq, NA_DH), lambda h, b: (b, kcol + h)),
            pl.BlockSpec((seq, NA_DH), lambda h, b: (b, vcol + h)),
            pl.BlockSpec((cseq, NA_DH), lambda h, b: (crb0 + b, kcol + h)),
            pl.BlockSpec((cseq, NA_DH), lambda h, b: (crb0 + b, vcol + h)),
        ],
        out_specs=pl.BlockSpec((seq, NA_DH), lambda h, b: (b, h)),
        out_shape=jax.ShapeDtypeStruct((out_rows, NA_HEADS * NA_DH), BF16),
        scratch_shapes=[pltpu.VMEM((2 * NA_WR, GRID_W, LANES), F32)],
        compiler_params=_cparams("arbitrary", "arbitrary"),
        name="na_attention",
    )(rpb.reshape(-1), z, z, z, z, z)


def _dense_attn_kernel(q_ref, k_ref, v_ref, yprev_ref, o_ref):
    s = lax.dot_general(q_ref[...], k_ref[...], (((1,), (1,)), ((), ())), preferred_element_type=F32) * NA_DH ** -0.5
    e = jnp.exp(s - jnp.max(s, axis=-1, keepdims=True))
    o = jnp.dot(e.astype(BF16), v_ref[...], preferred_element_type=F32)
    o_ref[...] = (o / jnp.sum(e, axis=-1, keepdims=True)).astype(o_ref.dtype)


def _dense_attention(z, y_prev, *, nb, cseq, crow0):
    kcol, vcol, qcol = COL_NA_K // NA_DH, COL_NA_V // NA_DH, COL_NA_Q // NA_DH
    crb0 = crow0 // cseq
    return pl.pallas_call(
        _dense_attn_kernel,
        grid=(nb, NA_HEADS),
        in_specs=[
            pl.BlockSpec((cseq, NA_DH), lambda b, h: (crb0 + b, qcol + h)),
            pl.BlockSpec((cseq, NA_DH), lambda b, h: (crb0 + b, kcol + h)),
            pl.BlockSpec((cseq, NA_DH), lambda b, h: (crb0 + b, vcol + h)),
            pl.BlockSpec(memory_space=pl.ANY),
        ],
        out_specs=pl.BlockSpec((cseq, NA_DH), lambda b, h: (crb0 + b, h)),
        out_shape=jax.ShapeDtypeStruct(y_prev.shape, BF16),
        input_output_aliases={3: 0},
        compiler_params=_cparams("parallel", "arbitrary"),
        name="dense_attention",
    )(z, z, z, y_prev)


HY_FEAT_ROWS = 64
HIGHEST = lax.Precision.HIGHEST


def _hy_features(seq):
    t = np.linspace(0.0, 1.0, seq)
    bands = (HY_EMB - 1) // 2
    w = 2.0 * math.pi * np.arange(seq) / seq
    fr = np.linspace(1e-4, bands - 1, bands)
    ang = fr[None] * w[:, None]
    feat = np.concatenate([t[:, None], np.cos(ang), -np.sin(ang)], axis=-1)
    src = np.concatenate([np.arange(seq), np.zeros(1, np.int64), np.arange(seq - 1, 0, -1)])
    feat2 = np.zeros((HY_FEAT_ROWS, 2 * seq), np.float32)
    feat2[:HY_EMB] = feat[src].T
    tt = t[src][None].astype(np.float32)
    mask = np.ones((1, 2 * seq), np.float32)
    mask[0, seq] = 0.0
    return feat2, tt, mask


def _hy_mlp_kernel(feat_ref, w1_ref, b1_ref, w2_ref, b2_ref, fq_ref, o_ref):
    fq = fq_ref[...]
    h = jnp.dot(w1_ref[...], feat_ref[...], preferred_element_type=F32, precision=HIGHEST)
    h = jnp.sin(fq * (h + b1_ref[...]))
    h = jnp.dot(w2_ref[...], h, preferred_element_type=F32, precision=HIGHEST)
    o_ref[...] = jnp.sin(fq * (h + b2_ref[...]))


def _hy_mlp(feat2, w1t, b1, w2t, b2, fq):
    depth, hid, _ = w1t.shape
    n = feat2.shape[1]
    col = lambda a: a.reshape(depth, hid, 1)
    wspec = lambda k: pl.BlockSpec((None, hid, k), lambda l: (l, 0, 0))
    return pl.pallas_call(
        _hy_mlp_kernel,
        grid=(depth,),
        in_specs=[pl.BlockSpec((HY_FEAT_ROWS, n), lambda l: (0, 0)), wspec(HY_FEAT_ROWS), wspec(1), wspec(hid),
                  wspec(1), wspec(1)],
        out_specs=pl.BlockSpec((None, hid, n), lambda l: (l, 0, 0)),
        out_shape=jax.ShapeDtypeStruct((depth, hid, n), F32),
        compiler_params=_cparams("parallel"),
        name="hyena_filter_mlp",
    )(feat2, w1t, col(b1), w2t, col(b2), col(fq))


def _hy_filter_kernel(h_ref, w3_ref, dl_ref, tt_ref, mask_ref, o_ref):
    seq = h_ref.shape[1] // 2
    kf = jnp.dot(w3_ref[0], h_ref[:, :seq], preferred_element_type=F32, precision=HIGHEST)
    kb = jnp.dot(w3_ref[1], h_ref[:, seq:], preferred_element_type=F32, precision=HIGHEST)
    k = jnp.concatenate([kf, kb], axis=1) * (jnp.exp(-tt_ref[...] * dl_ref[...]) * mask_ref[...])
    o_ref[...] = (k * lax.rsqrt(jnp.sum(k * k, axis=1, keepdims=True) + EPS)).astype(o_ref.dtype)


def _hy_filter(h2, w3t, deltas, tt, mask, cb):
    depth, hid, n = h2.shape
    c = w3t.shape[3]
    return pl.pallas_call(
        _hy_filter_kernel,
        grid=(depth, HY_ORDER, c // cb),
        in_specs=[
            pl.BlockSpec((None, hid, n), lambda l, o, j: (l, 0, 0)),
            pl.BlockSpec((None, None, 2, cb, hid), lambda l, o, j: (l, o, 0, j, 0)),
            pl.BlockSpec((cb, 1), lambda l, o, j: (j, 0)),
            pl.BlockSpec((1, n), lambda l, o, j: (0, 0)),
            pl.BlockSpec((1, n), lambda l, o, j: (0, 0)),
        ],
        out_specs=pl.BlockSpec((None, None, cb, n), lambda l, o, j: (l, o, j, 0)),
        out_shape=jax.ShapeDtypeStruct((depth, HY_ORDER, c, n), BF16),
        compiler_params=_cparams("parallel", "parallel", "arbitrary"),
        name="hyena_filter",
    )(h2, w3t, deltas, tt, mask)


FFT_NO = 64
FFT_NI = 128


def _fft_tables():
    n = FFT_NO * FFT_NI
    a = np.arange(FFT_NO)
    fo = np.exp(-2j * np.pi * np.outer(a, a) / FFT_NO)
    i = np.arange(FFT_NI)
    ci = np.exp(-2j * np.pi * np.outer(i, i) / FFT_NI)
    tw = np.exp(-2j * np.pi * np.outer(a, i) / n)
    half = FFT_NO // 2
    f32 = lambda x: np.ascontiguousarray(x, dtype=np.float32)
    g1_real = f32(np.concatenate([fo.real, fo.imag], axis=0))
    g1 = f32(np.block([[fo.real[:, :half], -fo.imag[:, :half]], [fo.imag[:, :half], fo.real[:, :half]]]))
    w2 = f32(np.block([[ci.real, ci.imag], [-ci.imag, ci.real]]))
    w2i = f32(np.block([[ci.real, -ci.imag], [ci.imag, ci.real]]))
    g4 = f32(np.block([[fo.real[:half], fo.imag[:half]], [-fo.imag[:half], fo.real[:half]]]) / n)
    return dict(g1_real=g1_real, g1=g1, w2=w2, w2i=w2i, g4=g4, twr=f32(tw.real), twi=f32(tw.imag))


HY_CHUNK = 16
HY_SPEC_CHUNK = 32


def _fft_stage1(g1, y2, twr, twi):
    nch = y2.shape[1] // FFT_NI
    o1 = jnp.dot(g1, y2, preferred_element_type=F32)
    yr, yi = o1[:FFT_NO].astype(BF16), o1[FFT_NO:].astype(BF16)
    ar = yr * twr - yi * twi
    ai = yr * twi + yi * twr
    lanes = lambda ci: slice(ci * FFT_NI, (ci + 1) * FFT_NI)
    return jnp.concatenate([jnp.concatenate([ar[:, lanes(ci)], ai[:, lanes(ci)]], axis=1)
                            for ci in range(nch)], axis=0)


def _hy_spectrum_kernel(k_ref, g1_ref, w2_ref, twr_ref, twi_ref, o_ref):
    cb = k_ref.shape[0]
    ch = HY_SPEC_CHUNK
    g1 = g1_ref[...]
    w2 = w2_ref[...]
    twr = twr_ref[...]
    twi = twi_ref[...]

    def body(j, _):
        chans = pl.ds(pl.multiple_of(j * ch, ch), ch)
        ks = k_ref[chans]
        k2 = jnp.concatenate([ks[ci] for ci in range(ch)], axis=1).astype(BF16)
        z = jnp.dot(_fft_stage1(g1, k2, twr, twi), w2, preferred_element_type=F32)
        o_ref[chans] = z.reshape(ch, FFT_NO, 2 * FFT_NI).astype(o_ref.dtype)
        return 0

    lax.fori_loop(0, cb // ch, body, 0)


def _tiled_twiddles(tabs, nch):
    tile = lambda a: jnp.asarray(np.tile(a, (1, nch)), BF16)
    return tile(tabs["twr"]), tile(tabs["twi"])


def _hy_spectrum(kfilt, tabs, cb):
    g, c = kfilt.shape[:2]
    const = lambda a: pl.BlockSpec(a.shape, lambda i, j: (0,) * a.ndim)
    g1 = jnp.asarray(tabs["g1_real"], BF16)
    w2 = jnp.asarray(tabs["w2"], BF16)
    twr, twi = _tiled_twiddles(tabs, HY_SPEC_CHUNK)
    return pl.pallas_call(
        _hy_spectrum_kernel,
        grid=(g, c // cb),
        in_specs=[pl.BlockSpec((None, cb, FFT_NO, FFT_NI), lambda i, j: (i, j, 0, 0)),
                  const(g1), const(w2), const(twr), const(twi)],
        out_specs=pl.BlockSpec((None, cb, FFT_NO, 2 * FFT_NI), lambda i, j: (i, j, 0, 0)),
        out_shape=jax.ShapeDtypeStruct((g, c, FFT_NO, 2 * FFT_NI), BF16),
        compiler_params=_cparams("parallel", "arbitrary"),
        name="hyena_spectrum",
    )(kfilt, g1, w2, twr, twi)


def _shift_conv3(x, w0, w1, w2, b):
    nrow, nlane = x.shape
    row = lax.broadcasted_iota(jnp.int32, x.shape, 0)
    lane = lax.broadcasted_iota(jnp.int32, x.shape, 1)
    r = pltpu.roll(x, 1, axis=1)
    prev = jnp.where(lane == 0, pltpu.roll(r, 1, axis=0), r)
    prev = jnp.where((lane == 0) & (row == 0), 0.0, prev)
    r = pltpu.roll(x, nlane - 1, axis=1)
    nxt = jnp.where(lane == nlane - 1, pltpu.roll(r, nrow - 1, axis=0), r)
    nxt = jnp.where((lane == nlane - 1) & (row == nrow - 1), 0.0, nxt)
    return prev * w0 + x * w1 + nxt * w2 + b


def _hyena_kernel(x_ref, kf_ref, pm_ref, g1_ref, w2_ref, w2i_ref, g4_ref, twr_ref, twi_ref, o_ref):
    cb = x_ref.shape[2]
    half = FFT_NO // 2
    ch = HY_CHUNK
    g1 = g1_ref[...]
    g4 = g4_ref[...]
    w2 = w2_ref[...]
    w2i = w2i_ref[...]
    twr = twr_ref[...]
    twi = twi_ref[...]
    lanes = lambda ci: slice(ci * FFT_NI, (ci + 1) * FFT_NI)
    krows = lambda ci: slice(ci * FFT_NO, (ci + 1) * FFT_NO)

    def body(j, _):
        c0 = pl.multiple_of(j * ch, ch)
        pm = pm_ref[:, pl.ds(c0, ch)]
        conv = []
        for part in range(3):
            halves = []
            for bb in range(2):
                xs = x_ref[bb, part, pl.ds(c0, ch)].astype(F32)
                halves.append(jnp.concatenate(
                    [_shift_conv3(xs[ci], pm[part, ci], pm[3 + part, ci], pm[6 + part, ci], pm[9 + part, ci])
                     for ci in range(ch)], axis=1))
            conv.append(jnp.concatenate(halves, axis=0))
        y = conv[2]
        for order in range(HY_ORDER):
            z = jnp.dot(_fft_stage1(g1, y.astype(BF16), twr, twi), w2, preferred_element_type=F32)
            kf = kf_ref[order, pl.ds(c0, ch)].reshape(ch * FFT_NO, 2 * FFT_NI)
            zr, zi = z[:, :FFT_NI].astype(BF16), z[:, FFT_NI:].astype(BF16)
            kr, ki = kf[:, :FFT_NI], kf[:, FFT_NI:]
            p = jnp.concatenate([zr * kr - zi * ki, zr * ki + zi * kr], axis=1)
            q = jnp.dot(p, w2i, preferred_element_type=F32).astype(BF16)
            qr = jnp.concatenate([q[krows(ci), :FFT_NI] for ci in range(ch)], axis=1)
            qi = jnp.concatenate([q[krows(ci), FFT_NI:] for ci in range(ch)], axis=1)
            qs = jnp.concatenate([qr * twr + qi * twi, qi * twr - qr * twi], axis=0)
            cv = jnp.dot(g4, qs, preferred_element_type=F32)
            bias = jnp.concatenate([pm[12 + order, ci] for ci in range(ch)], axis=1)
            y = conv[order] * (cv + bias * y)
        for ci in range(ch):
            o_ref[0, c0 + ci] = y[:half, lanes(ci)].astype(o_ref.dtype)
            o_ref[1, c0 + ci] = y[half:, lanes(ci)].astype(o_ref.dtype)
        return 0

    lax.fori_loop(0, cb // ch, body, 0)


def _hyena(xt, kf, layer, pm, tabs, cb):
    nb, _, c, half, _ = xt.shape
    const = lambda a: pl.BlockSpec(a.shape, lambda j, p: (0,) * a.ndim)
    bf = lambda name: jnp.asarray(tabs[name], BF16)
    g1, w2, w2i, g4 = bf("g1"), bf("w2"), bf("w2i"), bf("g4")
    twr, twi = _tiled_twiddles(tabs, HY_CHUNK)
    return pl.pallas_call(
        _hyena_kernel,
        grid=(c // cb, nb // 2),
        in_specs=[
            pl.BlockSpec((2, 3, cb, half, FFT_NI), lambda j, p: (p, 0, j, 0, 0)),
            pl.BlockSpec((None, HY_ORDER, cb, FFT_NO, 2 * FFT_NI), lambda j, p: (layer, 0, j, 0, 0)),
            pl.BlockSpec((14, cb, 1, FFT_NI), lambda j, p: (0, j, 0, 0)),
            const(g1), const(w2), const(w2i), const(g4), const(twr), const(twi),
        ],
        out_specs=pl.BlockSpec((2, cb, half, FFT_NI), lambda j, p: (p, j, 0, 0)),
        out_shape=jax.ShapeDtypeStruct((nb, c, half, FFT_NI), BF16),
        compiler_params=_cparams("parallel", "arbitrary"),
        name="hyena",
    )(xt, kf, pm, g1, w2, w2i, g4, twr, twi)


def _dft_tables(seq):
    n = 2 * seq
    wmat = np.exp(-2j * np.pi * np.outer(np.arange(n), np.arange(n)) / n)
    f32 = lambda x: np.ascontiguousarray(x, dtype=np.float32)
    fwd_real = f32(np.concatenate([wmat.real, wmat.imag], axis=1))
    ws = wmat[:seq]
    fwd = f32(np.block([[ws.real, ws.imag], [-ws.imag, ws.real]]))
    wi = np.conj(wmat)[:, :seq] / n
    inv = f32(np.block([[wi.real, wi.imag], [-wi.imag, wi.real]]))
    return dict(fwd_real=fwd_real, fwd=fwd, inv=inv)


def _rowdft_kernel(x_ref, w_ref, o_ref):
    o_ref[...] = jnp.dot(x_ref[...].astype(BF16), w_ref[...], preferred_element_type=F32)


def _rowdft(x, w, tr):
    m, k = x.shape
    n = w.shape[1]
    return pl.pallas_call(
        _rowdft_kernel,
        grid=(m // tr,),
        in_specs=[pl.BlockSpec((tr, k), lambda i: (i, 0)), pl.BlockSpec((k, n), lambda i: (0, 0))],
        out_specs=pl.BlockSpec((tr, n), lambda i: (i, 0)),
        out_shape=jax.ShapeDtypeStruct((m, n), F32),
        compiler_params=_cparams("parallel"),
        name="row_dft",
    )(x, w)


def _lane_conv3(x, w0, w1, w2, b):
    seq = x.shape[1]
    lane = lax.broadcasted_iota(jnp.int32, x.shape, 1)
    prev = jnp.where(lane == 0, 0.0, pltpu.roll(x, 1, axis=1))
    nxt = jnp.where(lane == seq - 1, 0.0, pltpu.roll(x, seq - 1, axis=1))
    return prev * w0 + x * w1 + nxt * w2 + b


def _hyena_ctx_kernel(x_ref, kf_ref, pm_ref, fwd_ref, inv_ref, o_ref):
    seq = x_ref.shape[3]
    n = 2 * seq
    conv = [[_lane_conv3(x_ref[bb, part].astype(F32), pm_ref[part], pm_ref[3 + part], pm_ref[6 + part],
                         pm_ref[9 + part]) for bb in range(2)] for part in range(3)]
    ya, yb = conv[2]
    for order in range(HY_ORDER):
        z = jnp.dot(jnp.concatenate([ya, yb], axis=1).astype(BF16), fwd_ref[...], preferred_element_type=F32)
        kf = kf_ref[order]
        zr, zi, kr, ki = z[:, :n], z[:, n:], kf[:, :n], kf[:, n:]
        p = jnp.concatenate([zr * kr - zi * ki, zr * ki + zi * kr], axis=1).astype(BF16)
        cv = jnp.dot(p, inv_ref[...], preferred_element_type=F32)
        bias = pm_ref[12 + order]
        ya = conv[order][0] * (cv[:, :seq] + bias * ya)
        yb = conv[order][1] * (cv[:, seq:] + bias * yb)
    o_ref[0] = ya.astype(o_ref.dtype)
    o_ref[1] = yb.astype(o_ref.dtype)


def _hyena_ctx(xt, kf, pm, tabs, cb):
    nb, _, c, seq = xt.shape
    fwd, inv = jnp.asarray(tabs["fwd"], BF16), jnp.asarray(tabs["inv"], BF16)
    const = lambda a: pl.BlockSpec(a.shape, lambda j, p: (0,) * a.ndim)
    return pl.pallas_call(
        _hyena_ctx_kernel,
        grid=(c // cb, nb // 2),
        in_specs=[
            pl.BlockSpec((2, 3, cb, seq), lambda j, p: (p, 0, j, 0)),
            pl.BlockSpec((HY_ORDER, cb, 4 * seq), lambda j, p: (0, j, 0)),
            pl.BlockSpec((14, cb, 1), lambda j, p: (0, j, 0)),
            const(fwd), const(inv),
        ],
        out_specs=pl.BlockSpec((2, cb, seq), lambda j, p: (p, j, 0)),
        out_shape=jax.ShapeDtypeStruct((nb, c, seq), BF16),
        compiler_params=_cparams("parallel", "arbitrary"),
        name="hyena_ctx",
    )(xt, kf, pm, fwd, inv)


def _rope_tables(seq):
    t = np.arange(seq)
    pos = np.stack([t // GRID_W, t % GRID_W], axis=1).astype(np.float64)
    n = RET_DK // 4
    inv = ROPE_BASE ** (-np.arange(n, dtype=np.float64) / n)
    lane = np.arange(LANES) % RET_DK
    ang = pos[:, lane // (2 * n)] * inv[lane % n][None]
    sign = np.where(lane % (2 * n) < n, -1.0, 1.0)
    return np.cos(ang).astype(np.float32), (np.sin(ang) * sign[None]).astype(np.float32)


def _hy_filters(seq, hy_f_w1, hy_f_b1, hy_f_w2, hy_f_b2, hy_f_w3, hy_f_freq):
    depth, _, hid = hy_f_w1.shape
    c = hy_f_w3.shape[2] // (2 * HY_ORDER)
    feat2, tt, mask = _hy_features(seq)
    w1t = jnp.pad(hy_f_w1.transpose(0, 2, 1), ((0, 0), (0, 0), (0, HY_FEAT_ROWS - HY_EMB)))
    h2 = _hy_mlp(jnp.asarray(feat2), w1t, hy_f_b1, hy_f_w2.transpose(0, 2, 1), hy_f_b2, hy_f_freq)
    w3t = hy_f_w3.reshape(depth, hid, HY_ORDER, 2, c).transpose(0, 2, 3, 4, 1)
    deltas = np.abs(np.linspace(math.log(HY_TARGET) / HY_FAST, math.log(HY_TARGET) / HY_SLOW, c))
    deltas = jnp.asarray(deltas.astype(np.float32).reshape(c, 1))
    cb = min(c, (1024 * 1024) // (2 * seq))
    return _hy_filter(h2, w3t, deltas, jnp.asarray(tt), jnp.asarray(mask), cb)


def _hy_params(hy_conv_w, hy_conv_b, hy_bias):
    c = hy_bias.shape[1]
    return jnp.concatenate([hy_conv_w.reshape(9, c), hy_conv_b.reshape(3, c), hy_bias], axis=0)


def kernel(x, c, ctx, c_ctx, w_mod, b_mod, g_norm1, g_norm2, w_in, hy_conv_w, hy_conv_b, hy_f_w1, hy_f_b1, hy_f_w2, hy_f_b2, hy_f_w3, hy_f_freq, hy_bias, ret_log_decay, na_rpb, w_branch, w_out, ffn_w_in, ffn_conv_w, ffn_conv_b, ffn_w_out, g_final):
    nb, seq, d = x.shape
    cseq = ctx.shape[1]
    depth = w_mod.shape[0]
    hy_w = hy_bias.shape[2]
    t_lat, t_ctx = nb * seq, nb * cseq
    assert seq % TM == 0 and t_ctx % TM == 0 and TM % cseq == 0 and nb % 2 == 0 and nb < SUBLANES
    assert 2 * seq == FFT_NO * FFT_NI and seq % GRID_W == 0 and cseq & (cseq - 1) == 0
    blk_lat, blk_all = t_lat // TM, (t_lat + t_ctx) // TM
    mrow = jnp.asarray(np.concatenate([np.repeat(np.arange(nb), seq // TM), np.full(t_ctx // TM, nb)]), jnp.int32)
    seqlen = jnp.asarray(np.concatenate([np.full(blk_lat, seq), np.full(t_ctx // TM, cseq)]), jnp.int32)
    per = TM // TR
    mrow_r = jnp.repeat(mrow, per)
    tn = 1024

    cond = jnp.zeros((SUBLANES, d), F32).at[:nb].set(c).at[nb].set(c_ctx)
    mods = _mods(cond, w_mod, b_mod).reshape(depth, SUBLANES, 6, 1, d)

    cos, sin = (jnp.asarray(a) for a in _rope_tables(seq))
    tabs = _fft_tables()
    ctabs = _dft_tables(cseq)
    filt = _hy_filters(seq, hy_f_w1, hy_f_b1, hy_f_w2, hy_f_b2, hy_f_w3, hy_f_freq)
    spec = _hy_spectrum(filt.reshape(depth * HY_ORDER, hy_w, FFT_NO, FFT_NI), tabs, 32)
    spec = spec.reshape(depth, HY_ORDER, hy_w, FFT_NO, 2 * FFT_NI)
    cfilt = _hy_filters(cseq, hy_f_w1, hy_f_b1, hy_f_w2, hy_f_b2, hy_f_w3, hy_f_freq)
    cspec = _rowdft(cfilt.reshape(depth * HY_ORDER * hy_w, 2 * cseq), jnp.asarray(ctabs["fwd_real"], BF16), 1024)
    cspec = cspec.reshape(depth, HY_ORDER, hy_w, 4 * cseq)

    xs = jnp.concatenate([x.reshape(t_lat, d), ctx.reshape(t_ctx, d)], axis=0)
    assert hy_w == HY_W and w_in.shape[2] == COL_GATE + N_BRANCH * d
    hy_cols = COL_GATE - COL_HY
    s_zero = jnp.zeros((nb, RET_HEADS, 2, LANES, RET_DV), F32)
    hn = _normmod(xs, g_norm1[0], mods[0], mrow, 0, 1, blk_all)
    for l in range(depth):
        last = l == depth - 1
        nblk = blk_lat if last else blk_all
        rows = nblk * TM
        z = _matmul(hn, w_in, l, blk_all, tn, w_in.shape[2] - hy_cols, skip=(COL_HY // tn, hy_cols // tn))
        hy_proj = functools.partial(_matmul_nt, hn, w_in, l, col0=COL_HY, n_out=hy_cols, nb=nb, tc=512)
        zh = hy_proj(blk0=0, nblk=blk_lat, seq=seq)

        y_ret, s_ctx = _retention(z, ret_log_decay[l], cos, sin, s_zero, nb=nb, seq=cseq, row0=t_lat, rope=False,
                                  out_rows=blk_all * TM)
        y_ret, _ = _retention(z, ret_log_decay[l], cos, sin, s_ctx, nb=nb, seq=seq, row0=0, rope=True,
                              out_rows=blk_all * TM, y_prev=y_ret)
        y_na = _na_attention(z, na_rpb[l], nb=nb, seq=seq, cseq=cseq, crow0=t_lat, out_rows=rows)
        pm = _hy_params(hy_conv_w[l], hy_conv_b[l], hy_bias[l])
        pm_lat = jnp.broadcast_to(pm[:, :, None, None], (14, hy_w, 1, FFT_NI))
        y_hy = _hyena(zh.reshape(nb, 3, hy_w, FFT_NO // 2, FFT_NI), spec, l, pm_lat, tabs, 32)
        y_hy = y_hy.reshape(nb, hy_w, seq).transpose(0, 2, 1).reshape(t_lat, hy_w)
        if not last:
            y_na = _dense_attention(z, y_na, nb=nb, cseq=cseq, crow0=t_lat)
            zc = hy_proj(blk0=blk_lat, nblk=t_ctx // TM, seq=cseq)
            yc_hy = _hyena_ctx(zc.reshape(nb, 3, hy_w, cseq), cspec[l], pm[:, :, None], ctabs, 256)
            y_hy = jnp.concatenate([y_hy, yc_hy.transpose(0, 2, 1).reshape(t_ctx, hy_w)], axis=0)
        acc = _merge((y_hy, y_ret, y_na), z, COL_HY, w_branch[l].astype(BF16), nblk, 1024)
        xs, hn = _out_proj(acc, w_out[l].astype(BF16), xs, mods[l], mrow_r, g_norm2[l], nblk * per)
        u = _matmul(hn, ffn_w_in, l, nblk, tn, ffn_w_in.shape[2])
        hg = _ffn_gate(u, ffn_conv_w[l], ffn_conv_b[l], seqlen, nblk, 512)
        nxt = l if last else l + 1
        w_ffo = ffn_w_out[l].astype(BF16)
        if last:
            res = _ffn_out(hg, w_ffo, xs, mods[l], mods[l], g_final, mrow_r, nblk * per, TR, 512, True)
        else:
            res = _ffn_out(hg, w_ffo, xs, mods[l], mods[nxt], g_norm1[nxt], mrow, nblk, TM, 256, False)
        if not last:
            xs, hn = res
    return res[0].reshape(nb, seq, d)
```

```python
import functools
import math

import numpy as np
import jax
import jax.numpy as jnp
from jax import lax
from jax.experimental import pallas as pl
from jax.experimental.pallas import tpu as pltpu

F32 = jnp.float32
BF16 = jnp.bfloat16

GRID_W = 64
N_BRANCH = 3
HY_ORDER = 2
HY_EMB = 33
HY_FAST = 0.3
HY_SLOW = 1.5
HY_TARGET = 1e-2
RET_HEADS = 8
RET_DK = 64
RET_DV = 128
RET_CHUNK = 128
NA_HEADS = 8
NA_DH = 128
NA_WR = 8
NA_WC = 16
ROPE_BASE = 10000.0
EPS = 1e-6
NEG = -1e30
HY_W = 1024

_RET_QK_W, _RET_V_W, _NA_W = RET_HEADS * RET_DK, RET_HEADS * RET_DV, NA_HEADS * NA_DH
COL_RET_K = 0
COL_RET_V = COL_RET_K + _RET_QK_W
COL_NA_K = COL_RET_V + _RET_V_W
COL_NA_V = COL_NA_K + _NA_W
COL_RET_Q = COL_NA_V + _NA_W
COL_RET_G = COL_RET_Q + _RET_QK_W
COL_NA_Q = COL_RET_G + _RET_V_W
COL_HY = COL_NA_Q + _NA_W
COL_GATE = COL_HY + (HY_ORDER + 1) * HY_W

LANES = 128
SUBLANES = 8
VMEM_LIMIT = 56 * 1024 * 1024

TM = 1024


def _cparams(*sem):
    return pltpu.CompilerParams(dimension_semantics=sem, vmem_limit_bytes=VMEM_LIMIT)


def _mods_kernel(a_ref, w0_ref, w1_ref, b_ref, o_ref):
    k = pl.program_id(1)
    a = a_ref[...]
    a = (a * jax.nn.sigmoid(a)).astype(BF16)
    tk = w0_ref.shape[0]
    part = (jnp.dot(a[:, :tk], w0_ref[...].astype(BF16), preferred_element_type=F32)
            + jnp.dot(a[:, tk:], w1_ref[...].astype(BF16), preferred_element_type=F32))

    @pl.when(k == 0)
    def _():
        o_ref[...] = part + b_ref[...]

    @pl.when(k > 0)
    def _():
        o_ref[...] += part


def _mods(cond, w_mod, b_mod):
    depth, d, n = w_mod.shape
    tk = LANES
    return pl.pallas_call(
        _mods_kernel,
        grid=(depth, d // (2 * tk)),
        in_specs=[
            pl.BlockSpec((SUBLANES, 2 * tk), lambda l, k: (0, k)),
            pl.BlockSpec((None, tk, n), lambda l, k: (l, 2 * k, 0)),
            pl.BlockSpec((None, tk, n), lambda l, k: (l, 2 * k + 1, 0)),
            pl.BlockSpec((None, 1, n), lambda l, k: (l, 0, 0)),
        ],
        out_specs=pl.BlockSpec((None, SUBLANES, n), lambda l, k: (l, 0, 0)),
        out_shape=jax.ShapeDtypeStruct((depth, SUBLANES, n), F32),
        compiler_params=_cparams("parallel", "arbitrary"),
        name="mods",
    )(cond, w_mod, w_mod, b_mod.reshape(depth, 1, n))


def _normmod_kernel(mrow_ref, x_ref, g_ref, sh_ref, sc_ref, o_ref):
    x = x_ref[...]
    y = x * lax.rsqrt(jnp.mean(x * x, axis=-1, keepdims=True) + EPS)
    y = y * g_ref[...]
    o_ref[...] = (y * (1.0 + sc_ref[...]) + sh_ref[...]).astype(o_ref.dtype)


def _normmod(x, g, mods, mrow, shift_idx, scale_idx, nblk):
    m, d = x.shape
    grid_spec = pltpu.PrefetchScalarGridSpec(
        num_scalar_prefetch=1,
        grid=(nblk,),
        in_specs=[
            pl.BlockSpec((TM, d), lambda i, mr: (i, 0)),
            pl.BlockSpec((1, d), lambda i, mr: (0, 0)),
            pl.BlockSpec((None, None, 1, d), lambda i, mr: (mr[i], shift_idx, 0, 0)),
            pl.BlockSpec((None, None, 1, d), lambda i, mr: (mr[i], scale_idx, 0, 0)),
        ],
        out_specs=pl.BlockSpec((TM, d), lambda i, mr: (i, 0)),
    )
    return pl.pallas_call(
        _normmod_kernel,
        grid_spec=grid_spec,
        out_shape=jax.ShapeDtypeStruct((m, d), BF16),
        compiler_params=_cparams("parallel"),
        name="normmod",
    )(mrow, x, g.reshape(1, d), mods, mods)


def _mm_kernel(a_ref, w_ref, o_ref, w_scr):
    @pl.when(pl.program_id(1) == 0)
    def _():
        w_scr[...] = w_ref[...].astype(BF16)

    o_ref[...] = jnp.dot(a_ref[...], w_scr[...], preferred_element_type=F32).astype(o_ref.dtype)


def _matmul(a, w, layer, nblk, tn, n_out, skip=None):
    m, k = a.shape
    col = (lambda j: j) if skip is None else (lambda j: j + jnp.where(j >= skip[0], skip[1], 0))
    return pl.pallas_call(
        _mm_kernel,
        grid=(n_out // tn, nblk),
        in_specs=[pl.BlockSpec((TM, k), lambda j, i: (i, 0)),
                  pl.BlockSpec((None, k, tn), lambda j, i: (layer, 0, col(j)))],
        out_specs=pl.BlockSpec((TM, tn), lambda j, i: (i, j)),
        out_shape=jax.ShapeDtypeStruct((m, n_out), BF16),
        scratch_shapes=[pltpu.VMEM((k, tn), BF16)],
        compiler_params=_cparams("arbitrary", "arbitrary"),
        name="matmul",
    )(a, w)


def _mm_nt_kernel(w_ref, a_ref, o_ref, wt_scr):
    @pl.when(pl.program_id(1) == 0)
    def _():
        wt_scr[...] = w_ref[...].T.astype(BF16)

    r = lax.dot_general(wt_scr[...], a_ref[...], (((1,), (1,)), ((), ())), preferred_element_type=F32)
    per = o_ref.shape[0]
    width = r.shape[1] // per
    for s in range(per):
        o_ref[s] = r[:, s * width:(s + 1) * width].astype(o_ref.dtype)


def _matmul_nt(a, w, layer, *, col0, n_out, blk0, nblk, nb, seq, tc):
    k = a.shape[1]
    per, sblk = max(TM // seq, 1), max(seq // TM, 1)
    return pl.pallas_call(
        _mm_nt_kernel,
        grid=(n_out // tc, nblk),
        in_specs=[pl.BlockSpec((None, k, tc), lambda j, i: (layer, 0, col0 // tc + j)),
                  pl.BlockSpec((TM, k), lambda j, i: (blk0 + i, 0))],
        out_specs=pl.BlockSpec((per, tc, TM // per), lambda j, i: (i // sblk, j, i % sblk)),
        out_shape=jax.ShapeDtypeStruct((nb, n_out, seq), BF16),
        scratch_shapes=[pltpu.VMEM((tc, k), BF16)],
        compiler_params=_cparams("arbitrary", "arbitrary"),
        name="matmul_nt",
    )(w, a)


TR = 512


def _norm_modulate(x, g, shift, scale):
    y = x * lax.rsqrt(jnp.mean(x * x, axis=-1, keepdims=True) + EPS)
    return (y * g) * (1.0 + scale) + shift


def _outproj_kernel(mrow_ref, a_ref, w_ref, x_ref, gate_ref, g_ref, sh_ref, sc_ref, x_out, h_out):
    y = jnp.dot(a_ref[...], w_ref[...], preferred_element_type=F32)
    x = x_ref[...] + gate_ref[...] * y
    x_out[...] = x
    h_out[...] = _norm_modulate(x, g_ref[...], sh_ref[...], sc_ref[...]).astype(h_out.dtype)


def _out_proj(a, w, x, mods, mrow, g_norm, nblk):
    m, k = a.shape
    d = w.shape[1]
    mod = lambda idx: pl.BlockSpec((None, None, 1, d), lambda i, mr: (mr[i], idx, 0, 0))
    row = pl.BlockSpec((TR, d), lambda i, mr: (i, 0))
    grid_spec = pltpu.PrefetchScalarGridSpec(
        num_scalar_prefetch=1,
        grid=(nblk,),
        in_specs=[pl.BlockSpec((TR, k), lambda i, mr: (i, 0)), pl.BlockSpec((k, d), lambda i, mr: (0, 0)), row,
                  mod(2), pl.BlockSpec((1, d), lambda i, mr: (0, 0)), mod(3), mod(4)],
        out_specs=[row, row],
    )
    return pl.pallas_call(
        _outproj_kernel,
        grid_spec=grid_spec,
        out_shape=[jax.ShapeDtypeStruct((m, d), F32), jax.ShapeDtypeStruct((m, d), BF16)],
        compiler_params=_cparams("parallel"),
        name="out_proj",
    )(mrow, a, w, x, mods, g_norm.reshape(1, d), mods, mods)


def _merge_kernel(y0_ref, y1_ref, y2_ref, g0_ref, g1_ref, g2_ref, w_ref, o_ref):
    acc = None
    for i, (y_ref, g_ref) in enumerate(((y0_ref, g0_ref), (y1_ref, g1_ref), (y2_ref, g2_ref))):
        t = jnp.dot(y_ref[...], w_ref[i], preferred_element_type=F32)
        t = jax.nn.sigmoid(g_ref[...].astype(F32)) * t
        acc = t if acc is None else acc + t
    o_ref[...] = acc.astype(o_ref.dtype)


def _merge(ys, z, gate_col0, w_branch, nblk, tn):
    m, bw = ys[0].shape
    d = w_branch.shape[2]
    gspec = lambda i_br: pl.BlockSpec((TM, tn), lambda i, j: (i, (gate_col0 + i_br * d) // tn + j))
    yspec = pl.BlockSpec((TM, bw), lambda i, j: (i, 0))
    return pl.pallas_call(
        _merge_kernel,
        grid=(nblk, d // tn),
        in_specs=[yspec, yspec, yspec, gspec(0), gspec(1), gspec(2),
                  pl.BlockSpec((N_BRANCH, bw, tn), lambda i, j: (0, 0, j))],
        out_specs=pl.BlockSpec((TM, tn), lambda i, j: (i, j)),
        out_shape=jax.ShapeDtypeStruct((m, d), BF16),
        compiler_params=_cparams("parallel", "arbitrary"),
        name="merge",
    )(ys[0], ys[1], ys[2], z, z, z, w_branch)


def _ffn_gate_kernel(seq_ref, a_ref, ap_ref, an_ref, b_ref, cw_ref, cb_ref, o_ref):
    i = pl.program_id(0)
    seq_m1 = seq_ref[i] - 1
    tm = a_ref.shape[0]
    row = lax.broadcasted_iota(jnp.int32, (tm, 1), 0)
    halo_p = jnp.where(((i * tm) & seq_m1) == 0, 0.0, ap_ref[SUBLANES - 1:SUBLANES, :].astype(F32))
    halo_n = jnp.where((((i + 1) * tm) & seq_m1) == 0, 0.0, an_ref[0:1, :].astype(F32))

    def finish(inner_edges):
        a = a_ref[...].astype(F32)
        prev = jnp.where(row == 0, halo_p, pltpu.roll(a, 1, axis=0))
        nxt = jnp.where(row == tm - 1, halo_n, pltpu.roll(a, tm - 1, axis=0))
        if inner_edges:
            pos = row & seq_m1
            prev = jnp.where((pos == 0) & (row > 0), 0.0, prev)
            nxt = jnp.where((pos == seq_m1) & (row < tm - 1), 0.0, nxt)
        cw = cw_ref[...]
        conv = prev * cw[0:1, :] + a * cw[1:2, :] + nxt * cw[2:3, :] + cb_ref[...]
        o_ref[...] = jax.nn.gelu(conv.astype(BF16)) * b_ref[...]

    @pl.when(seq_m1 >= tm - 1)
    def _():
        finish(False)

    @pl.when(seq_m1 < tm - 1)
    def _():
        finish(True)


def _ffn_gate(u, conv_w, conv_b, seqlen, nblk, tc):
    m, ff2 = u.shape
    ff = ff2 // 2
    nrow8 = m // SUBLANES
    r8 = TM // SUBLANES
    grid_spec = pltpu.PrefetchScalarGridSpec(
        num_scalar_prefetch=1,
        grid=(nblk, ff // tc),
        in_specs=[
            pl.BlockSpec((TM, tc), lambda i, j, s: (i, j)),
            pl.BlockSpec((SUBLANES, tc), lambda i, j, s: (jnp.maximum(i * r8 - 1, 0), j)),
            pl.BlockSpec((SUBLANES, tc), lambda i, j, s: (jnp.minimum((i + 1) * r8, nrow8 - 1), j)),
            pl.BlockSpec((TM, tc), lambda i, j, s: (i, ff // tc + j)),
            pl.BlockSpec((3, tc), lambda i, j, s: (0, j)),
            pl.BlockSpec((1, tc), lambda i, j, s: (0, j)),
        ],
        out_specs=pl.BlockSpec((TM, tc), lambda i, j, s: (i, j)),
    )
    return pl.pallas_call(
        _ffn_gate_kernel,
        grid_spec=grid_spec,
        out_shape=jax.ShapeDtypeStruct((m, ff), BF16),
        compiler_params=_cparams("parallel", "arbitrary"),
        name="ffn_gate",
    )(seqlen, u, u, u, u, conv_w, conv_b.reshape(1, ff))


def _ffn_out_kernel(mrow_ref, h_ref, w_ref, x_ref, gate_ref, g_ref, sh_ref, sc_ref, *rest, final):
    outs, xrow = rest[:-1], rest[-1]
    j = pl.program_id(1)
    y = jnp.dot(h_ref[...], w_ref[...], preferred_element_type=F32)
    xt = x_ref[...] + gate_ref[...] * y
    xrow[j] = xt
    if not final:
        outs[0][...] = xt

    @pl.when(j == pl.num_programs(1) - 1)
    def _():
        x = jnp.concatenate([xrow[t] for t in range(xrow.shape[0])], axis=1)
        if final:
            y = x * lax.rsqrt(jnp.mean(x * x, axis=-1, keepdims=True) + EPS)
            outs[0][...] = y * g_ref[...]
        else:
            outs[1][...] = _norm_modulate(x, g_ref[...], sh_ref[...], sc_ref[...]).astype(outs[1].dtype)


def _ffn_out(h, w, x, mods, mods_next, g_next, mrow, nblk, tr, tn, final):
    m, ff = h.shape
    d = w.shape[1]
    mod = lambda arr_idx: pl.BlockSpec((None, None, 1, d), lambda i, j, mr: (mr[i], arr_idx, 0, 0))
    row = pl.BlockSpec((tr, d), lambda i, j, mr: (i, 0))
    tile = pl.BlockSpec((tr, tn), lambda i, j, mr: (i, j))
    grid_spec = pltpu.PrefetchScalarGridSpec(
        num_scalar_prefetch=1,
        grid=(nblk, d // tn),
        in_specs=[
            pl.BlockSpec((tr, ff), lambda i, j, mr: (i, 0)),
            pl.BlockSpec((ff, tn), lambda i, j, mr: (0, j)),
            tile,
            pl.BlockSpec((None, None, 1, tn), lambda i, j, mr: (mr[i], 5, 0, j)),
            pl.BlockSpec((1, d), lambda i, j, mr: (0, 0)), mod(0), mod(1),
        ],
        out_specs=[row] if final else [tile, row],
        scratch_shapes=[pltpu.VMEM((d // tn, tr, tn), F32)],
    )
    out_shape = [jax.ShapeDtypeStruct((m, d), F32)] + ([] if final else [jax.ShapeDtypeStruct((m, d), BF16)])
    return pl.pallas_call(
        functools.partial(_ffn_out_kernel, final=final),
        grid_spec=grid_spec,
        out_shape=out_shape,
        compiler_params=_cparams("parallel", "arbitrary"),
        name="ffn_out",
    )(mrow, h, w, x, mods, g_next.reshape(1, d), mods_next, mods_next)


def _rope_pair(x, cos, sin):
    lane = lax.broadcasted_iota(jnp.int32, x.shape, 1)
    first = lax.rem(lane, 32) < 16
    partner = jnp.where(first, pltpu.roll(x, LANES - 16, axis=1), pltpu.roll(x, 16, axis=1))
    return x * cos + partner * sin


def _ret_kernel(lg_ref, q_ref, k_ref, v_ref, g_ref, cos_ref, sin_ref, s0_ref, *rest, rope, has_prev):
    o_ref, sfin_ref, qr_scr, kr_scr, kv_scr = rest[1:] if has_prev else rest
    hp = pl.program_id(1)
    seq = q_ref.shape[0]
    c = RET_CHUNK
    nchunk = seq // c
    ks = RET_DK ** -0.5
    q = q_ref[...].astype(F32)
    k = k_ref[...].astype(F32)
    if rope:
        q = _rope_pair(q, cos_ref[...], sin_ref[...])
        k = _rope_pair(k, cos_ref[...], sin_ref[...])
    qr_scr[...] = q
    kr_scr[...] = k * ks

    pos_r = lax.broadcasted_iota(jnp.int32, (c, 1), 0).astype(F32)
    rel = (lax.broadcasted_iota(jnp.int32, (c, c), 0) - lax.broadcasted_iota(jnp.int32, (c, c), 1)).astype(F32)
    lane = lax.broadcasted_iota(jnp.int32, (1, LANES), 1)

    heads = []
    for hh in range(2):
        lgf = lg_ref[0, 2 * hp + hh]
        lgb = lg_ref[1, 2 * hp + hh]
        heads.append(dict(
            hm=(lane // RET_DK == hh).astype(F32),
            zeta_f=jnp.exp(lgf * (c - 1 - pos_r)), zeta_b=jnp.exp(lgb * pos_r),
            xi_f=jnp.exp(lgf * (pos_r + 1.0)), xi_b=jnp.exp(lgb * (c - pos_r)),
            dmat=jnp.where(rel >= 0, jnp.exp(lgf * jnp.maximum(rel, 0.0)), jnp.exp(lgb * jnp.maximum(-rel, 0.0))),
            gf=jnp.exp(lgf * c), gb=jnp.exp(lgb * c),
            vcols=slice(hh * RET_DV, (hh + 1) * RET_DV)))

    def kv_body(n, _):
        rows = pl.ds(pl.multiple_of(n * c, c), c)
        kr = kr_scr[rows, :]
        for hh, hd in enumerate(heads):
            kh = kr * hd["hm"]
            kz = jnp.concatenate([kh * hd["zeta_f"], kh * hd["zeta_b"]], axis=1).astype(BF16)
            kv_scr[hh, n] = lax.dot_general(kz, v_ref[rows, hd["vcols"]], (((0,), (0,)), ((), ())),
                                            preferred_element_type=F32)
        return 0

    lax.fori_loop(0, nchunk, kv_body, 0, unroll=min(8, nchunk))

    def scan(j, carry):
        nf, nbk = j, nchunk - 1 - j
        out = []
        for hh, hd in enumerate(heads):
            sf, sb = carry[2 * hh], carry[2 * hh + 1]
            tf = kv_scr[hh, nf, 0:LANES, :]
            kv_scr[hh, nf, 0:LANES, :] = sf
            tb = kv_scr[hh, nbk, LANES:2 * LANES, :]
            kv_scr[hh, nbk, LANES:2 * LANES, :] = sb
            out += [hd["gf"] * sf + tf, hd["gb"] * sb + tb]
        return tuple(out)

    fin = lax.fori_loop(0, nchunk, scan, (s0_ref[0, 0], s0_ref[0, 1], s0_ref[1, 0], s0_ref[1, 1]))
    for hh in range(2):
        sfin_ref[hh, 0] = fin[2 * hh]
        sfin_ref[hh, 1] = fin[2 * hh + 1]

    def out_body(n, _):
        rows = pl.ds(pl.multiple_of(n * c, c), c)
        qr = qr_scr[rows, :]
        kb = kr_scr[rows, :].astype(BF16)
        for hh, hd in enumerate(heads):
            qm = qr * hd["hm"]
            a = lax.dot_general(qm.astype(BF16), kb, (((1,), (1,)), ((), ())),
                                preferred_element_type=F32) * hd["dmat"]
            o = jnp.dot(a.astype(BF16), v_ref[rows, hd["vcols"]], preferred_element_type=F32)
            qx = jnp.concatenate([qm * hd["xi_f"], qm * hd["xi_b"]], axis=1).astype(BF16)
            o = o + jnp.dot(qx, kv_scr[hh, n].astype(BF16), preferred_element_type=F32)
            o = o * lax.rsqrt(jnp.mean(o * o, axis=-1, keepdims=True) + EPS)
            g = g_ref[rows, hd["vcols"]].astype(F32)
            o_ref[rows, hd["vcols"]] = (g * jax.nn.sigmoid(g) * o).astype(o_ref.dtype)
        return 0

    lax.fori_loop(0, nchunk, out_body, 0, unroll=min(8, nchunk))


def _retention(z, log_decay, cos, sin, s0, *, nb, seq, row0, rope, out_rows, y_prev=None):
    rb0 = row0 // seq
    hpairs = RET_HEADS // 2
    kcol, qcol = COL_RET_K // LANES, COL_RET_Q // LANES
    vcol, gcol = COL_RET_V // (2 * RET_DV), COL_RET_G // (2 * RET_DV)
    has_prev = y_prev is not None
    kernel = functools.partial(_ret_kernel, rope=rope, has_prev=has_prev)
    in_specs = [
        pl.BlockSpec(memory_space=pltpu.SMEM),
        pl.BlockSpec((seq, LANES), lambda b, p: (rb0 + b, qcol + p)),
        pl.BlockSpec((seq, LANES), lambda b, p: (rb0 + b, kcol + p)),
        pl.BlockSpec((seq, 2 * RET_DV), lambda b, p: (rb0 + b, vcol + p)),
        pl.BlockSpec((seq, 2 * RET_DV), lambda b, p: (rb0 + b, gcol + p)),
        pl.BlockSpec((seq, LANES), lambda b, p: (0, 0)),
        pl.BlockSpec((seq, LANES), lambda b, p: (0, 0)),
        pl.BlockSpec((None, 2, 2, LANES, RET_DV), lambda b, p: (b, p, 0, 0, 0)),
    ]
    args = [log_decay, z, z, z, z, cos, sin, s0]
    if has_prev:
        in_specs.append(pl.BlockSpec(memory_space=pl.ANY))
        args.append(y_prev)
    return pl.pallas_call(
        kernel,
        grid=(nb, hpairs),
        in_specs=in_specs,
        out_specs=[
            pl.BlockSpec((seq, 2 * RET_DV), lambda b, p: (rb0 + b, p)),
            pl.BlockSpec((None, 2, 2, LANES, RET_DV), lambda b, p: (b, p, 0, 0, 0)),
        ],
        out_shape=[
            jax.ShapeDtypeStruct((out_rows, RET_HEADS * RET_DV), BF16),
            jax.ShapeDtypeStruct((nb, RET_HEADS, 2, LANES, RET_DV), F32),
        ],
        scratch_shapes=[
            pltpu.VMEM((seq, LANES), F32),
            pltpu.VMEM((seq, LANES), F32),
            pltpu.VMEM((2, seq // RET_CHUNK, 2 * LANES, RET_DV), F32),
        ],
        input_output_aliases={len(args) - 1: 0} if has_prev else {},
        compiler_params=_cparams("parallel", "arbitrary"),
        name="retention_rope" if rope else "retention",
    )(*args)


NA_GROUP = 4
NA_KROWS = 12
NA_OVERLAP = 4


def _na_kernel(rpb_ref, q_ref, k_ref, v_ref, kc_ref, vc_ref, o_ref, bias_scr):
    seq = q_ref.shape[0]
    nrow = seq // GRID_W
    ndr, ndc = 2 * NA_WR - 1, 2 * NA_WC - 1
    scale = NA_DH ** -0.5
    nt = (((1,), (1,)), ((), ()))
    h = pl.program_id(0)
    lane = lax.broadcasted_iota(jnp.int32, (GRID_W, LANES), 1)

    @pl.when(pl.program_id(1) == 0)
    def _():
        qi = lax.broadcasted_iota(jnp.int32, (GRID_W, LANES), 0)
        ki = lane & (GRID_W - 1)
        dc = ki - qi + (NA_WC - 1)
        cs = jnp.clip(qi - NA_WC // 2, 0, GRID_W - NA_WC)
        colmask = (ki >= cs) & (ki < cs + NA_WC)

        def toeplitz(d):
            t = jnp.zeros((GRID_W, LANES), F32)
            for j in range(ndc):
                t = jnp.where(dc == j, rpb_ref[h * (ndr * ndc) + d * ndc + j], t)
            return t

        prev = jnp.zeros((GRID_W, LANES), F32)
        for i in range(ndr + 1):
            nxt = toeplitz(i) if i < ndr else jnp.zeros((GRID_W, LANES), F32)
            bias_scr[i] = jnp.where(colmask, jnp.where(lane < GRID_W, prev, nxt), NEG)
            prev = nxt

    kc = kc_ref[...]
    vc = vc_ref[...]
    kwin = NA_KROWS * GRID_W

    half = NA_WR // 2
    ntile = kwin // LANES

    def group(r0, interior):
        us = r0 - half if interior else min(max(r0 - half, 0), nrow - NA_KROWS)
        if isinstance(r0, int):
            qrows = pl.ds(r0 * GRID_W, NA_GROUP * GRID_W)
            krows = pl.ds(us * GRID_W, kwin)
        else:
            qrows = pl.ds(pl.multiple_of(r0 * GRID_W, NA_GROUP * GRID_W), NA_GROUP * GRID_W)
            krows = pl.ds(pl.multiple_of(us * GRID_W, GRID_W), kwin)
        q = q_ref[qrows, :]
        s = lax.dot_general(q, k_ref[krows, :], nt, preferred_element_type=F32)
        sc = lax.dot_general(q, kc, nt, preferred_element_type=F32) * scale
        e_rows, ec_rows, den_rows = [], [], []
        for u in range(NA_GROUP):
            qr = slice(u * GRID_W, (u + 1) * GRID_W)
            tiles = []
            for m in range(ntile):
                if interior:
                    off, idx = 2 * m - u, 2 * m - u + half
                else:
                    r = r0 + u
                    off = us + 2 * m - min(max(r - half, 0), nrow - NA_WR)
                    idx = min(max(us + 2 * m - r + NA_WR, 0), ndr)
                ok_lo, ok_hi = 0 <= off < NA_WR, 0 <= off + 1 < NA_WR
                if not (ok_lo or ok_hi):
                    tiles.append(None)
                    continue
                tab = bias_scr[idx]
                if not (ok_lo and ok_hi):
                    tab = tab + jnp.where(lane < GRID_W, 0.0 if ok_lo else NEG, 0.0 if ok_hi else NEG)
                st = s[qr, m * LANES:(m + 1) * LANES]
                tiles.append(jnp.where(tab > 0.5 * NEG, st * scale + tab, NEG))
            live = [t for t in tiles if t is not None]
            tmax = live[0]
            for t in live[1:]:
                tmax = jnp.maximum(tmax, t)
            scu = sc[qr, :]
            mx = jnp.maximum(jnp.max(tmax, axis=-1, keepdims=True), jnp.max(scu, axis=-1, keepdims=True))
            etiles = [None if t is None else jnp.exp(t - mx) for t in tiles]
            ecu = jnp.exp(scu - mx)
            esum = None
            for t in etiles:
                if t is not None:
                    esum = t if esum is None else esum + t
            den_rows.append(jnp.sum(esum, axis=-1, keepdims=True) + jnp.sum(ecu, axis=-1, keepdims=True))
            zero = jnp.zeros((GRID_W, LANES), BF16)
            e_rows.append(jnp.concatenate([zero if t is None else t.astype(BF16) for t in etiles], axis=1))
            ec_rows.append(ecu.astype(BF16))
        o = jnp.dot(jnp.concatenate(e_rows, axis=0), v_ref[krows, :], preferred_element_type=F32)
        o = o + jnp.dot(jnp.concatenate(ec_rows, axis=0), vc, preferred_element_type=F32)
        o_ref[qrows, :] = (o / jnp.concatenate(den_rows, axis=0)).astype(o_ref.dtype)

    ngroup = nrow // NA_GROUP
    g_lo = -(-half // NA_GROUP)
    g_hi = min((nrow - NA_KROWS + half) // NA_GROUP, (nrow - NA_WR + half - NA_GROUP + 1) // NA_GROUP)
    for g in range(g_lo):
        group(g * NA_GROUP, False)

    def body(i, _):
        for t in range(NA_OVERLAP):
            group((g_lo + NA_OVERLAP * i + t) * NA_GROUP, True)
        return 0

    nloop = (g_hi + 1 - g_lo) // NA_OVERLAP
    lax.fori_loop(0, nloop, body, 0)
    for g in range(g_lo + NA_OVERLAP * nloop, g_hi + 1):
        group(g * NA_GROUP, True)
    for g in range(g_hi + 1, ngroup):
        group(g * NA_GROUP, False)


def _na_attention(z, rpb, *, nb, seq, cseq, crow0, out_rows):
    kcol, vcol, qcol = COL_NA_K // NA_DH, COL_NA_V // NA_DH, COL_NA_Q // NA_DH
    crb0 = crow0 // cseq
    nrow = seq // GRID_W
    assert nrow >= NA_KROWS and nrow % NA_GROUP == 0 and GRID_W * 2 == LANES
    assert NA_KROWS % 2 == 0 and NA_KROWS >= NA_WR + NA_GROUP - 1
    return pl.pallas_call(
        _na_kernel,
        grid=(NA_HEADS, nb),
        in_specs=[
            pl.BlockSpec(memory_space=pltpu.SMEM),
            pl.BlockSpec((seq, NA_DH), lambda h, b: (b, qcol + h)),
            pl.BlockSpec((seq, NA_DH), lambda h, b: (b, kcol + h)),
            pl.BlockSpec((seq, NA_DH), lambda h, b: (b, vcol + h)),
            pl.BlockSpec((cseq, NA_DH), lambda h, b: (crb0 + b, kcol + h)),
            pl.BlockSpec((cseq, NA_DH), lambda h, b: (crb0 + b, vcol + h)),
        ],
        out_specs=pl.BlockSpec((seq, NA_DH), lambda h, b: (b, h)),
        out_shape=jax.ShapeDtypeStruct((out_rows, NA_HEADS * NA_DH), BF16),
        scratch_shapes=[pltpu.VMEM((2 * NA_WR, GRID_W, LANES), F32)],
        compiler_params=_cparams("arbitrary", "arbitrary"),
        name="na_attention",
    )(rpb.reshape(-1), z, z, z, z, z)


def _dense_attn_kernel(q_ref, k_ref, v_ref, yprev_ref, o_ref):
    s = lax.dot_general(q_ref[...], k_ref[...], (((1,), (1,)), ((), ())), preferred_element_type=F32) * NA_DH ** -0.5
    e = jnp.exp(s - jnp.max(s, axis=-1, keepdims=True))
    o = jnp.dot(e.astype(BF16), v_ref[...], preferred_element_type=F32)
    o_ref[...] = (o / jnp.sum(e, axis=-1, keepdims=True)).astype(o_ref.dtype)


def _dense_attention(z, y_prev, *, nb, cseq, crow0):
    kcol, vcol, qcol = COL_NA_K // NA_DH, COL_NA_V // NA_DH, COL_NA_Q // NA_DH
    crb0 = crow0 // cseq
    return pl.pallas_call(
        _dense_attn_kernel,
        grid=(nb, NA_HEADS),
        in_specs=[
            pl.BlockSpec((cseq, NA_DH), lambda b, h: (crb0 + b, qcol + h)),
            pl.BlockSpec((cseq, NA_DH), lambda b, h: (crb0 + b, kcol + h)),
            pl.BlockSpec((cseq, NA_DH), lambda b, h: (crb0 + b, vcol + h)),
            pl.BlockSpec(memory_space=pl.ANY),
        ],
        out_specs=pl.BlockSpec((cseq, NA_DH), lambda b, h: (crb0 + b, h)),
        out_shape=jax.ShapeDtypeStruct(y_prev.shape, BF16),
        input_output_aliases={3: 0},
        compiler_params=_cparams("parallel", "arbitrary"),
        name="dense_attention",
    )(z, z, z, y_prev)


HY_FEAT_ROWS = 64
HIGHEST = lax.Precision.HIGHEST


def _hy_features(seq):
    t = np.linspace(0.0, 1.0, seq)
    bands = (HY_EMB - 1) // 2
    w = 2.0 * math.pi * np.arange(seq) / seq
    fr = np.linspace(1e-4, bands - 1, bands)
    ang = fr[None] * w[:, None]
    feat = np.concatenate([t[:, None], np.cos(ang), -np.sin(ang)], axis=-1)
    src = np.concatenate([np.arange(seq), np.zeros(1, np.int64), np.arange(seq - 1, 0, -1)])
    feat2 = np.zeros((HY_FEAT_ROWS, 2 * seq), np.float32)
    feat2[:HY_EMB] = feat[src].T
    tt = t[src][None].astype(np.float32)
    mask = np.ones((1, 2 * seq), np.float32)
    mask[0, seq] = 0.0
    return feat2, tt, mask


def _hy_mlp_kernel(feat_ref, w1_ref, b1_ref, w2_ref, b2_ref, fq_ref, o_ref):
    fq = fq_ref[...]
    h = jnp.dot(w1_ref[...], feat_ref[...], preferred_element_type=F32, precision=HIGHEST)
    h = jnp.sin(fq * (h + b1_ref[...]))
    h = jnp.dot(w2_ref[...], h, preferred_element_type=F32, precision=HIGHEST)
    o_ref[...] = jnp.sin(fq * (h + b2_ref[...]))


def _hy_mlp(feat2, w1t, b1, w2t, b2, fq):
    depth, hid, _ = w1t.shape
    n = feat2.shape[1]
    col = lambda a: a.reshape(depth, hid, 1)
    wspec = lambda k: pl.BlockSpec((None, hid, k), lambda l: (l, 0, 0))
    return pl.pallas_call(
        _hy_mlp_kernel,
        grid=(depth,),
        in_specs=[pl.BlockSpec((HY_FEAT_ROWS, n), lambda l: (0, 0)), wspec(HY_FEAT_ROWS), wspec(1), wspec(hid),
                  wspec(1), wspec(1)],
        out_specs=pl.BlockSpec((None, hid, n), lambda l: (l, 0, 0)),
        out_shape=jax.ShapeDtypeStruct((depth, hid, n), F32),
        compiler_params=_cparams("parallel"),
        name="hyena_filter_mlp",
    )(feat2, w1t, col(b1), w2t, col(b2), col(fq))


def _hy_filter_kernel(h_ref, w3_ref, dl_ref, tt_ref, mask_ref, o_ref):
    seq = h_ref.shape[1] // 2
    kf = jnp.dot(w3_ref[0], h_ref[:, :seq], preferred_element_type=F32, precision=HIGHEST)
    kb = jnp.dot(w3_ref[1], h_ref[:, seq:], preferred_element_type=F32, precision=HIGHEST)
    k = jnp.concatenate([kf, kb], axis=1) * (jnp.exp(-tt_ref[...] * dl_ref[...]) * mask_ref[...])
    o_ref[...] = (k * lax.rsqrt(jnp.sum(k * k, axis=1, keepdims=True) + EPS)).astype(o_ref.dtype)


def _hy_filter(h2, w3t, deltas, tt, mask, cb):
    depth, hid, n = h2.shape
    c = w3t.shape[3]
    return pl.pallas_call(
        _hy_filter_kernel,
        grid=(depth, HY_ORDER, c // cb),
        in_specs=[
            pl.BlockSpec((None, hid, n), lambda l, o, j: (l, 0, 0)),
            pl.BlockSpec((None, None, 2, cb, hid), lambda l, o, j: (l, o, 0, j, 0)),
            pl.BlockSpec((cb, 1), lambda l, o, j: (j, 0)),
            pl.BlockSpec((1, n), lambda l, o, j: (0, 0)),
            pl.BlockSpec((1, n), lambda l, o, j: (0, 0)),
        ],
        out_specs=pl.BlockSpec((None, None, cb, n), lambda l, o, j: (l, o, j, 0)),
        out_shape=jax.ShapeDtypeStruct((depth, HY_ORDER, c, n), BF16),
        compiler_params=_cparams("parallel", "parallel", "arbitrary"),
        name="hyena_filter",
    )(h2, w3t, deltas, tt, mask)


FFT_NO = 64
FFT_NI = 128


def _fft_tables():
    n = FFT_NO * FFT_NI
    a = np.arange(FFT_NO)
    fo = np.exp(-2j * np.pi * np.outer(a, a) / FFT_NO)
    i = np.arange(FFT_NI)
    ci = np.exp(-2j * np.pi * np.outer(i, i) / FFT_NI)
    tw = np.exp(-2j * np.pi * np.outer(a, i) / n)
    half = FFT_NO // 2
    f32 = lambda x: np.ascontiguousarray(x, dtype=np.float32)
    g1_real = f32(np.concatenate([fo.real, fo.imag], axis=0))
    g1 = f32(np.block([[fo.real[:, :half], -fo.imag[:, :half]], [fo.imag[:, :half], fo.real[:, :half]]]))
    w2 = f32(np.block([[ci.real, ci.imag], [-ci.imag, ci.real]]))
    w2i = f32(np.block([[ci.real, -ci.imag], [ci.imag, ci.real]]))
    g4 = f32(np.block([[fo.real[:half], fo.imag[:half]], [-fo.imag[:half], fo.real[:half]]]) / n)
    return dict(g1_real=g1_real, g1=g1, w2=w2, w2i=w2i, g4=g4, twr=f32(tw.real), twi=f32(tw.imag))


HY_CHUNK = 16
HY_SPEC_CHUNK = 32


def _fft_stage1(g1, y2, twr, twi):
    nch = y2.shape[1] // FFT_NI
    o1 = jnp.dot(g1, y2, preferred_element_type=F32)
    yr, yi = o1[:FFT_NO].astype(BF16), o1[FFT_NO:].astype(BF16)
    ar = yr * twr - yi * twi
    ai = yr * twi + yi * twr
    lanes = lambda ci: slice(ci * FFT_NI, (ci + 1) * FFT_NI)
    return jnp.concatenate([jnp.concatenate([ar[:, lanes(ci)], ai[:, lanes(ci)]], axis=1)
                            for ci in range(nch)], axis=0)


def _hy_spectrum_kernel(k_ref, g1_ref, w2_ref, twr_ref, twi_ref, o_ref):
    cb = k_ref.shape[0]
    ch = HY_SPEC_CHUNK
    g1 = g1_ref[...]
    w2 = w2_ref[...]
    twr = twr_ref[...]
    twi = twi_ref[...]

    def body(j, _):
        chans = pl.ds(pl.multiple_of(j * ch, ch), ch)
        ks = k_ref[chans]
        k2 = jnp.concatenate([ks[ci] for ci in range(ch)], axis=1).astype(BF16)
        z = jnp.dot(_fft_stage1(g1, k2, twr, twi), w2, preferred_element_type=F32)
        o_ref[chans] = z.reshape(ch, FFT_NO, 2 * FFT_NI).astype(o_ref.dtype)
        return 0

    lax.fori_loop(0, cb // ch, body, 0)


def _tiled_twiddles(tabs, nch):
    tile = lambda a: jnp.asarray(np.tile(a, (1, nch)), BF16)
    return tile(tabs["twr"]), tile(tabs["twi"])


def _hy_spectrum(kfilt, tabs, cb):
    g, c = kfilt.shape[:2]
    const = lambda a: pl.BlockSpec(a.shape, lambda i, j: (0,) * a.ndim)
    g1 = jnp.asarray(tabs["g1_real"], BF16)
    w2 = jnp.asarray(tabs["w2"], BF16)
    twr, twi = _tiled_twiddles(tabs, HY_SPEC_CHUNK)
    return pl.pallas_call(
        _hy_spectrum_kernel,
        grid=(g, c // cb),
        in_specs=[pl.BlockSpec((None, cb, FFT_NO, FFT_NI), lambda i, j: (i, j, 0, 0)),
                  const(g1), const(w2), const(twr), const(twi)],
        out_specs=pl.BlockSpec((None, cb, FFT_NO, 2 * FFT_NI), lambda i, j: (i, j, 0, 0)),
        out_shape=jax.ShapeDtypeStruct((g, c, FFT_NO, 2 * FFT_NI), BF16),
        compiler_params=_cparams("parallel", "arbitrary"),
        name="hyena_spectrum",
    )(kfilt, g1, w2, twr, twi)


def _shift_conv3(x, w0, w1, w2, b):
    nrow, nlane = x.shape
    row = lax.broadcasted_iota(jnp.int32, x.shape, 0)
    lane = lax.broadcasted_iota(jnp.int32, x.shape, 1)
    r = pltpu.roll(x, 1, axis=1)
    prev = jnp.where(lane == 0, pltpu.roll(r, 1, axis=0), r)
    prev = jnp.where((lane == 0) & (row == 0), 0.0, prev)
    r = pltpu.roll(x, nlane - 1, axis=1)
    nxt = jnp.where(lane == nlane - 1, pltpu.roll(r, nrow - 1, axis=0), r)
    nxt = jnp.where((lane == nlane - 1) & (row == nrow - 1), 0.0, nxt)
    return prev * w0 + x * w1 + nxt * w2 + b


def _hyena_kernel(x_ref, kf_ref, pm_ref, g1_ref, w2_ref, w2i_ref, g4_ref, twr_ref, twi_ref, o_ref):
    cb = x_ref.shape[2]
    half = FFT_NO // 2
    ch = HY_CHUNK
    g1 = g1_ref[...]
    g4 = g4_ref[...]
    w2 = w2_ref[...]
    w2i = w2i_ref[...]
    twr = twr_ref[...]
    twi = twi_ref[...]
    lanes = lambda ci: slice(ci * FFT_NI, (ci + 1) * FFT_NI)
    krows = lambda ci: slice(ci * FFT_NO, (ci + 1) * FFT_NO)

    def body(j, _):
        c0 = pl.multiple_of(j * ch, ch)
        pm = pm_ref[:, pl.ds(c0, ch)]
        conv = []
        for part in range(3):
            halves = []
            for bb in range(2):
                xs = x_ref[bb, part, pl.ds(c0, ch)].astype(F32)
                halves.append(jnp.concatenate(
                    [_shift_conv3(xs[ci], pm[part, ci], pm[3 + part, ci], pm[6 + part, ci], pm[9 + part, ci])
                     for ci in range(ch)], axis=1))
            conv.append(jnp.concatenate(halves, axis=0))
        y = conv[2]
        for order in range(HY_ORDER):
            z = jnp.dot(_fft_stage1(g1, y.astype(BF16), twr, twi), w2, preferred_element_type=F32)
            kf = kf_ref[order, pl.ds(c0, ch)].reshape(ch * FFT_NO, 2 * FFT_NI)
            zr, zi = z[:, :FFT_NI].astype(BF16), z[:, FFT_NI:].astype(BF16)
            kr, ki = kf[:, :FFT_NI], kf[:, FFT_NI:]
            p = jnp.concatenate([zr * kr - zi * ki, zr * ki + zi * kr], axis=1)
            q = jnp.dot(p, w2i, preferred_element_type=F32).astype(BF16)
            qr = jnp.concatenate([q[krows(ci), :FFT_NI] for ci in range(ch)], axis=1)
            qi = jnp.concatenate([q[krows(ci), FFT_NI:] for ci in range(ch)], axis=1)
            qs = jnp.concatenate([qr * twr + qi * twi, qi * twr - qr * twi], axis=0)
            cv = jnp.dot(g4, qs, preferred_element_type=F32)
            bias = jnp.concatenate([pm[12 + order, ci] for ci in range(ch)], axis=1)
            y = conv[order] * (cv + bias * y)
        for ci in range(ch):
            o_ref[0, c0 + ci] = y[:half, lanes(ci)].astype(o_ref.dtype)
            o_ref[1, c0 + ci] = y[half:, lanes(ci)].astype(o_ref.dtype)
        return 0

    lax.fori_loop(0, cb // ch, body, 0)


def _hyena(xt, kf, layer, pm, tabs, cb):
    nb, _, c, half, _ = xt.shape
    const = lambda a: pl.BlockSpec(a.shape, lambda j, p: (0,) * a.ndim)
    bf = lambda name: jnp.asarray(tabs[name], BF16)
    g1, w2, w2i, g4 = bf("g1"), bf("w2"), bf("w2i"), bf("g4")
    twr, twi = _tiled_twiddles(tabs, HY_CHUNK)
    return pl.pallas_call(
        _hyena_kernel,
        grid=(c // cb, nb // 2),
        in_specs=[
            pl.BlockSpec((2, 3, cb, half, FFT_NI), lambda j, p: (p, 0, j, 0, 0)),
            pl.BlockSpec((None, HY_ORDER, cb, FFT_NO, 2 * FFT_NI), lambda j, p: (layer, 0, j, 0, 0)),
            pl.BlockSpec((14, cb, 1, FFT_NI), lambda j, p: (0, j, 0, 0)),
            const(g1), const(w2), const(w2i), const(g4), const(twr), const(twi),
        ],
        out_specs=pl.BlockSpec((2, cb, half, FFT_NI), lambda j, p: (p, j, 0, 0)),
        out_shape=jax.ShapeDtypeStruct((nb, c, half, FFT_NI), BF16),
        compiler_params=_cparams("parallel", "arbitrary"),
        name="hyena",
    )(xt, kf, pm, g1, w2, w2i, g4, twr, twi)


def _dft_tables(seq):
    n = 2 * seq
    wmat = np.exp(-2j * np.pi * np.outer(np.arange(n), np.arange(n)) / n)
    f32 = lambda x: np.ascontiguousarray(x, dtype=np.float32)
    fwd_real = f32(np.concatenate([wmat.real, wmat.imag], axis=1))
    ws = wmat[:seq]
    fwd = f32(np.block([[ws.real, ws.imag], [-ws.imag, ws.real]]))
    wi = np.conj(wmat)[:, :seq] / n
    inv = f32(np.block([[wi.real, wi.imag], [-wi.imag, wi.real]]))
    return dict(fwd_real=fwd_real, fwd=fwd, inv=inv)


def _rowdft_kernel(x_ref, w_ref, o_ref):
    o_ref[...] = jnp.dot(x_ref[...].astype(BF16), w_ref[...], preferred_element_type=F32)


def _rowdft(x, w, tr):
    m, k = x.shape
    n = w.shape[1]
    return pl.pallas_call(
        _rowdft_kernel,
        grid=(m // tr,),
        in_specs=[pl.BlockSpec((tr, k), lambda i: (i, 0)), pl.BlockSpec((k, n), lambda i: (0, 0))],
        out_specs=pl.BlockSpec((tr, n), lambda i: (i, 0)),
        out_shape=jax.ShapeDtypeStruct((m, n), F32),
        compiler_params=_cparams("parallel"),
        name="row_dft",
    )(x, w)


def _lane_conv3(x, w0, w1, w2, b):
    seq = x.shape[1]
    lane = lax.broadcasted_iota(jnp.int32, x.shape, 1)
    prev = jnp.where(lane == 0, 0.0, pltpu.roll(x, 1, axis=1))
    nxt = jnp.where(lane == seq - 1, 0.0, pltpu.roll(x, seq - 1, axis=1))
    return prev * w0 + x * w1 + nxt * w2 + b


def _hyena_ctx_kernel(x_ref, kf_ref, pm_ref, fwd_ref, inv_ref, o_ref):
    seq = x_ref.shape[3]
    n = 2 * seq
    conv = [[_lane_conv3(x_ref[bb, part].astype(F32), pm_ref[part], pm_ref[3 + part], pm_ref[6 + part],
                         pm_ref[9 + part]) for bb in range(2)] for part in range(3)]
    ya, yb = conv[2]
    for order in range(HY_ORDER):
        z = jnp.dot(jnp.concatenate([ya, yb], axis=1).astype(BF16), fwd_ref[...], preferred_element_type=F32)
        kf = kf_ref[order]
        zr, zi, kr, ki = z[:, :n], z[:, n:], kf[:, :n], kf[:, n:]
        p = jnp.concatenate([zr * kr - zi * ki, zr * ki + zi * kr], axis=1).astype(BF16)
        cv = jnp.dot(p, inv_ref[...], preferred_element_type=F32)
        bias = pm_ref[12 + order]
        ya = conv[order][0] * (cv[:, :seq] + bias * ya)
        yb = conv[order][1] * (cv[:, seq:] + bias * yb)
    o_ref[0] = ya.astype(o_ref.dtype)
    o_ref[1] = yb.astype(o_ref.dtype)


def _hyena_ctx(xt, kf, pm, tabs, cb):
    nb, _, c, seq = xt.shape
    fwd, inv = jnp.asarray(tabs["fwd"], BF16), jnp.asarray(tabs["inv"], BF16)
    const = lambda a: pl.BlockSpec(a.shape, lambda j, p: (0,) * a.ndim)
    return pl.pallas_call(
        _hyena_ctx_kernel,
        grid=(c // cb, nb // 2),
        in_specs=[
            pl.BlockSpec((2, 3, cb, seq), lambda j, p: (p, 0, j, 0)),
            pl.BlockSpec((HY_ORDER, cb, 4 * seq), lambda j, p: (0, j, 0)),
            pl.BlockSpec((14, cb, 1), lambda j, p: (0, j, 0)),
            const(fwd), const(inv),
        ],
        out_specs=pl.BlockSpec((2, cb, seq), lambda j, p: (p, j, 0)),
        out_shape=jax.ShapeDtypeStruct((nb, c, seq), BF16),
        compiler_params=_cparams("parallel", "arbitrary"),
        name="hyena_ctx",
    )(xt, kf, pm, fwd, inv)


def _rope_tables(seq):
    t = np.arange(seq)
    pos = np.stack([t // GRID_W, t % GRID_W], axis=1).astype(np.float64)
    n = RET_DK // 4
    inv = ROPE_BASE ** (-np.arange(n, dtype=np.float64) / n)
    lane = np.arange(LANES) % RET_DK
    ang = pos[:, lane // (2 * n)] * inv[lane % n][None]
    sign = np.where(lane % (2 * n) < n, -1.0, 1.0)
    return np.cos(ang).astype(np.float32), (np.sin(ang) * sign[None]).astype(np.float32)


def _hy_filters(seq, hy_f_w1, hy_f_b1, hy_f_w2, hy_f_b2, hy_f_w3, hy_f_freq):
    depth, _, hid = hy_f_w1.shape
    c = hy_f_w3.shape[2] // (2 * HY_ORDER)
    feat2, tt, mask = _hy_features(seq)
    w1t = jnp.pad(hy_f_w1.transpose(0, 2, 1), ((0, 0), (0, 0), (0, HY_FEAT_ROWS - HY_EMB)))
    h2 = _hy_mlp(jnp.asarray(feat2), w1t, hy_f_b1, hy_f_w2.transpose(0, 2, 1), hy_f_b2, hy_f_freq)
    w3t = hy_f_w3.reshape(depth, hid, HY_ORDER, 2, c).transpose(0, 2, 3, 4, 1)
    deltas = np.abs(np.linspace(math.log(HY_TARGET) / HY_FAST, math.log(HY_TARGET) / HY_SLOW, c))
    deltas = jnp.asarray(deltas.astype(np.float32).reshape(c, 1))
    cb = min(c, (1024 * 1024) // (2 * seq))
    return _hy_filter(h2, w3t, deltas, jnp.asarray(tt), jnp.asarray(mask), cb)


def _hy_params(hy_conv_w, hy_conv_b, hy_bias):
    c = hy_bias.shape[1]
    return jnp.concatenate([hy_conv_w.reshape(9, c), hy_conv_b.reshape(3, c), hy_bias], axis=0)


def kernel(x, c, ctx, c_ctx, w_mod, b_mod, g_norm1, g_norm2, w_in, hy_conv_w, hy_conv_b, hy_f_w1, hy_f_b1, hy_f_w2, hy_f_b2, hy_f_w3, hy_f_freq, hy_bias, ret_log_decay, na_rpb, w_branch, w_out, ffn_w_in, ffn_conv_w, ffn_conv_b, ffn_w_out, g_final):
    nb, seq, d = x.shape
    cseq = ctx.shape[1]
    depth = w_mod.shape[0]
    hy_w = hy_bias.shape[2]
    t_lat, t_ctx = nb * seq, nb * cseq
    assert seq % TM == 0 and t_ctx % TM == 0 and TM % cseq == 0 and nb % 2 == 0 and nb < SUBLANES
    assert 2 * seq == FFT_NO * FFT_NI and seq % GRID_W == 0 and cseq & (cseq - 1) == 0
    blk_lat, blk_all = t_lat // TM, (t_lat + t_ctx) // TM
    mrow = jnp.asarray(np.concatenate([np.repeat(np.arange(nb), seq // TM), np.full(t_ctx // TM, nb)]), jnp.int32)
    seqlen = jnp.asarray(np.concatenate([np.full(blk_lat, seq), np.full(t_ctx // TM, cseq)]), jnp.int32)
    per = TM // TR
    mrow_r = jnp.repeat(mrow, per)
    tn = 1024

    cond = jnp.zeros((SUBLANES, d), F32).at[:nb].set(c).at[nb].set(c_ctx)
    mods = _mods(cond, w_mod, b_mod).reshape(depth, SUBLANES, 6, 1, d)

    cos, sin = (jnp.asarray(a) for a in _rope_tables(seq))
    tabs = _fft_tables()
    ctabs = _dft_tables(cseq)
    filt = _hy_filters(seq, hy_f_w1, hy_f_b1, hy_f_w2, hy_f_b2, hy_f_w3, hy_f_freq)
    spec = _hy_spectrum(filt.reshape(depth * HY_ORDER, hy_w, FFT_NO, FFT_NI), tabs, 32)
    spec = spec.reshape(depth, HY_ORDER, hy_w, FFT_NO, 2 * FFT_NI)
    cfilt = _hy_filters(cseq, hy_f_w1, hy_f_b1, hy_f_w2, hy_f_b2, hy_f_w3, hy_f_freq)
    cspec = _rowdft(cfilt.reshape(depth * HY_ORDER * hy_w, 2 * cseq), jnp.asarray(ctabs["fwd_real"], BF16), 1024)
    cspec = cspec.reshape(depth, HY_ORDER, hy_w, 4 * cseq)

    xs = jnp.concatenate([x.reshape(t_lat, d), ctx.reshape(t_ctx, d)], axis=0)
    assert hy_w == HY_W and w_in.shape[2] == COL_GATE + N_BRANCH * d
    hy_cols = COL_GATE - COL_HY
    s_zero = jnp.zeros((nb, RET_HEADS, 2, LANES, RET_DV), F32)
    hn = _normmod(xs, g_norm1[0], mods[0], mrow, 0, 1, blk_all)
    for l in range(depth):
        last = l == depth - 1
        nblk = blk_lat if last else blk_all
        rows = nblk * TM
        z = _matmul(hn, w_in, l, blk_all, tn, w_in.shape[2] - hy_cols, skip=(COL_HY // tn, hy_cols // tn))
        hy_proj = functools.partial(_matmul_nt, hn, w_in, l, col0=COL_HY, n_out=hy_cols, nb=nb, tc=512)
        zh = hy_proj(blk0=0, nblk=blk_lat, seq=seq)

        y_ret, s_ctx = _retention(z, ret_log_decay[l], cos, sin, s_zero, nb=nb, seq=cseq, row0=t_lat, rope=False,
                                  out_rows=blk_all * TM)
        y_ret, _ = _retention(z, ret_log_decay[l], cos, sin, s_ctx, nb=nb, seq=seq, row0=0, rope=True,
                              out_rows=blk_all * TM, y_prev=y_ret)
        y_na = _na_attention(z, na_rpb[l], nb=nb, seq=seq, cseq=cseq, crow0=t_lat, out_rows=rows)
        pm = _hy_params(hy_conv_w[l], hy_conv_b[l], hy_bias[l])
        pm_lat = jnp.broadcast_to(pm[:, :, None, None], (14, hy_w, 1, FFT_NI))
        y_hy = _hyena(zh.reshape(nb, 3, hy_w, FFT_NO // 2, FFT_NI), spec, l, pm_lat, tabs, 32)
        y_hy = y_hy.reshape(nb, hy_w, seq).transpose(0, 2, 1).reshape(t_lat, hy_w)
        if not last:
            y_na = _dense_attention(z, y_na, nb=nb, cseq=cseq, crow0=t_lat)
            zc = hy_proj(blk0=blk_lat, nblk=t_ctx // TM, seq=cseq)
            yc_hy = _hyena_ctx(zc.reshape(nb, 3, hy_w, cseq), cspec[l], pm[:, :, None], ctabs, 256)
            y_hy = jnp.concatenate([y_hy, yc_hy.transpose(0, 2, 1).reshape(t_ctx, hy_w)], axis=0)
        acc = _merge((y_hy, y_ret, y_na), z, COL_HY, w_branch[l].astype(BF16), nblk, 1024)
        xs, hn = _out_proj(acc, w_out[l].astype(BF16), xs, mods[l], mrow_r, g_norm2[l], nblk * per)
        u = _matmul(hn, ffn_w_in, l, nblk, tn, ffn_w_in.shape[2])
        hg = _ffn_gate(u, ffn_conv_w[l], ffn_conv_b[l], seqlen, nblk, 512)
        nxt = l if last else l + 1
        w_ffo = ffn_w_out[l].astype(BF16)
        if last:
            res = _ffn_out(hg, w_ffo, xs, mods[l], mods[l], g_final, mrow_r, nblk * per, TR, 512, True)
        else:
            res = _ffn_out(hg, w_ffo, xs, mods[l], mods[nxt], g_norm1[nxt], mrow, nblk, TM, 256, False)
        if not last:
            xs, hn = res
    return res[0].reshape(nb, seq, d)
```

```python
import functools
import math

import numpy as np
import jax
import jax.numpy as jnp
from jax import lax
from jax.experimental import pallas as pl
from jax.experimental.pallas import tpu as pltpu

F32 = jnp.float32
BF16 = jnp.bfloat16

GRID_W = 64
N_BRANCH = 3
HY_ORDER = 2
HY_EMB = 33
HY_FAST = 0.3
HY_SLOW = 1.5
HY_TARGET = 1e-2
RET_HEADS = 8
RET_DK = 64
RET_DV = 128
RET_CHUNK = 128
NA_HEADS = 8
NA_DH = 128
NA_WR = 8
NA_WC = 16
ROPE_BASE = 10000.0
EPS = 1e-6
NEG = -1e30
HY_W = 1024

_RET_QK_W, _RET_V_W, _NA_W = RET_HEADS * RET_DK, RET_HEADS * RET_DV, NA_HEADS * NA_DH
COL_RET_K = 0
COL_RET_V = COL_RET_K + _RET_QK_W
COL_NA_K = COL_RET_V + _RET_V_W
COL_NA_V = COL_NA_K + _NA_W
COL_RET_Q = COL_NA_V + _NA_W
COL_RET_G = COL_RET_Q + _RET_QK_W
COL_NA_Q = COL_RET_G + _RET_V_W
COL_HY = COL_NA_Q + _NA_W
COL_GATE = COL_HY + (HY_ORDER + 1) * HY_W

LANES = 128
SUBLANES = 8
VMEM_LIMIT = 56 * 1024 * 1024

TM = 1024


def _cparams(*sem):
    return pltpu.CompilerParams(dimension_semantics=sem, vmem_limit_bytes=VMEM_LIMIT)


def _mods_kernel(a_ref, w_ref, b_ref, o_ref):
    k = pl.program_id(1)
    a = a_ref[...]
    a = a * jax.nn.sigmoid(a)
    part = jnp.dot(a.astype(BF16), w_ref[...].astype(BF16), preferred_element_type=F32)

    @pl.when(k == 0)
    def _():
        o_ref[...] = part + b_ref[...]

    @pl.when(k > 0)
    def _():
        o_ref[...] += part


def _mods(cond, w_mod, b_mod):
    depth, d, n = w_mod.shape
    tk = LANES
    return pl.pallas_call(
        _mods_kernel,
        grid=(depth, d // tk),
        in_specs=[
            pl.BlockSpec((SUBLANES, tk), lambda l, k: (0, k)),
            pl.BlockSpec((None, tk, n), lambda l, k: (l, k, 0)),
            pl.BlockSpec((None, 1, n), lambda l, k: (l, 0, 0)),
        ],
        out_specs=pl.BlockSpec((None, SUBLANES, n), lambda l, k: (l, 0, 0)),
        out_shape=jax.ShapeDtypeStruct((depth, SUBLANES, n), F32),
        compiler_params=_cparams("parallel", "arbitrary"),
        name="mods",
    )(cond, w_mod, b_mod.reshape(depth, 1, n))


def _normmod_kernel(mrow_ref, x_ref, g_ref, sh_ref, sc_ref, o_ref):
    x = x_ref[...]
    y = x * lax.rsqrt(jnp.mean(x * x, axis=-1, keepdims=True) + EPS)
    y = y * g_ref[...]
    o_ref[...] = (y * (1.0 + sc_ref[...]) + sh_ref[...]).astype(o_ref.dtype)


def _normmod(x, g, mods, mrow, shift_idx, scale_idx, nblk):
    m, d = x.shape
    grid_spec = pltpu.PrefetchScalarGridSpec(
        num_scalar_prefetch=1,
        grid=(nblk,),
        in_specs=[
            pl.BlockSpec((TM, d), lambda i, mr: (i, 0)),
            pl.BlockSpec((1, d), lambda i, mr: (0, 0)),
            pl.BlockSpec((None, None, 1, d), lambda i, mr: (mr[i], shift_idx, 0, 0)),
            pl.BlockSpec((None, None, 1, d), lambda i, mr: (mr[i], scale_idx, 0, 0)),
        ],
        out_specs=pl.BlockSpec((TM, d), lambda i, mr: (i, 0)),
    )
    return pl.pallas_call(
        _normmod_kernel,
        grid_spec=grid_spec,
        out_shape=jax.ShapeDtypeStruct((m, d), BF16),
        compiler_params=_cparams("parallel"),
        name="normmod",
    )(mrow, x, g.reshape(1, d), mods, mods)


def _mm_kernel(a_ref, w_ref, o_ref, w_scr):
    @pl.when(pl.program_id(1) == 0)
    def _():
        w_scr[...] = w_ref[...].astype(BF16)

    o_ref[...] = jnp.dot(a_ref[...], w_scr[...], preferred_element_type=F32).astype(o_ref.dtype)


def _matmul(a, w, layer, nblk, tn, n_out, skip=None):
    m, k = a.shape
    col = (lambda j: j) if skip is None else (lambda j: j + jnp.where(j >= skip[0], skip[1], 0))
    return pl.pallas_call(
        _mm_kernel,
        grid=(n_out // tn, nblk),
        in_specs=[pl.BlockSpec((TM, k), lambda j, i: (i, 0)),
                  pl.BlockSpec((None, k, tn), lambda j, i: (layer, 0, col(j)))],
        out_specs=pl.BlockSpec((TM, tn), lambda j, i: (i, j)),
        out_shape=jax.ShapeDtypeStruct((m, n_out), BF16),
        scratch_shapes=[pltpu.VMEM((k, tn), BF16)],
        compiler_params=_cparams("arbitrary", "arbitrary"),
        name="matmul",
    )(a, w)


def _mm_nt_kernel(w_ref, a_ref, o_ref, wt_scr):
    @pl.when(pl.program_id(1) == 0)
    def _():
        wt_scr[...] = w_ref[...].T.astype(BF16)

    r = lax.dot_general(wt_scr[...], a_ref[...], (((1,), (1,)), ((), ())), preferred_element_type=F32)
    per = o_ref.shape[0]
    width = r.shape[1] // per
    for s in range(per):
        o_ref[s] = r[:, s * width:(s + 1) * width].astype(o_ref.dtype)


def _matmul_nt(a, w, layer, *, col0, n_out, blk0, nblk, nb, seq, tc):
    k = a.shape[1]
    per, sblk = max(TM // seq, 1), max(seq // TM, 1)
    return pl.pallas_call(
        _mm_nt_kernel,
        grid=(n_out // tc, nblk),
        in_specs=[pl.BlockSpec((None, k, tc), lambda j, i: (layer, 0, col0 // tc + j)),
                  pl.BlockSpec((TM, k), lambda j, i: (blk0 + i, 0))],
        out_specs=pl.BlockSpec((per, tc, TM // per), lambda j, i: (i // sblk, j, i % sblk)),
        out_shape=jax.ShapeDtypeStruct((nb, n_out, seq), BF16),
        scratch_shapes=[pltpu.VMEM((tc, k), BF16)],
        compiler_params=_cparams("arbitrary", "arbitrary"),
        name="matmul_nt",
    )(w, a)


TR = 512


def _norm_modulate(x, g, shift, scale):
    y = x * lax.rsqrt(jnp.mean(x * x, axis=-1, keepdims=True) + EPS)
    return (y * g) * (1.0 + scale) + shift


def _outproj_kernel(mrow_ref, a_ref, w_ref, x_ref, gate_ref, g_ref, sh_ref, sc_ref, x_out, h_out):
    y = jnp.dot(a_ref[...], w_ref[...], preferred_element_type=F32)
    x = x_ref[...] + gate_ref[...] * y
    x_out[...] = x
    h_out[...] = _norm_modulate(x, g_ref[...], sh_ref[...], sc_ref[...]).astype(h_out.dtype)


def _out_proj(a, w, x, mods, mrow, g_norm, nblk):
    m, k = a.shape
    d = w.shape[1]
    mod = lambda idx: pl.BlockSpec((None, None, 1, d), lambda i, mr: (mr[i], idx, 0, 0))
    row = pl.BlockSpec((TR, d), lambda i, mr: (i, 0))
    grid_spec = pltpu.PrefetchScalarGridSpec(
        num_scalar_prefetch=1,
        grid=(nblk,),
        in_specs=[pl.BlockSpec((TR, k), lambda i, mr: (i, 0)), pl.BlockSpec((k, d), lambda i, mr: (0, 0)), row,
                  mod(2), pl.BlockSpec((1, d), lambda i, mr: (0, 0)), mod(3), mod(4)],
        out_specs=[row, row],
    )
    return pl.pallas_call(
        _outproj_kernel,
        grid_spec=grid_spec,
        out_shape=[jax.ShapeDtypeStruct((m, d), F32), jax.ShapeDtypeStruct((m, d), BF16)],
        compiler_params=_cparams("parallel"),
        name="out_proj",
    )(mrow, a, w, x, mods, g_norm.reshape(1, d), mods, mods)


def _merge_kernel(y0_ref, y1_ref, y2_ref, g0_ref, g1_ref, g2_ref, w_ref, o_ref):
    acc = None
    for i, (y_ref, g_ref) in enumerate(((y0_ref, g0_ref), (y1_ref, g1_ref), (y2_ref, g2_ref))):
        t = jnp.dot(y_ref[...], w_ref[i], preferred_element_type=F32)
        t = jax.nn.sigmoid(g_ref[...].astype(F32)) * t
        acc = t if acc is None else acc + t
    o_ref[...] = acc.astype(o_ref.dtype)


def _merge(ys, z, gate_col0, w_branch, nblk, tr, tn):
    m, bw = ys[0].shape
    d = w_branch.shape[2]
    assert gate_col0 % tn == 0 and d % tn == 0
    gspec = lambda i_br: pl.BlockSpec((tr, tn), lambda i, j: (i, (gate_col0 + i_br * d) // tn + j))
    yspec = pl.BlockSpec((tr, bw), lambda i, j: (i, 0))
    return pl.pallas_call(
        _merge_kernel,
        grid=(nblk, d // tn),
        in_specs=[yspec, yspec, yspec, gspec(0), gspec(1), gspec(2),
                  pl.BlockSpec((N_BRANCH, bw, tn), lambda i, j: (0, 0, j))],
        out_specs=pl.BlockSpec((tr, tn), lambda i, j: (i, j)),
        out_shape=jax.ShapeDtypeStruct((m, d), BF16),
        compiler_params=_cparams("parallel", "arbitrary"),
        name="merge",
    )(ys[0], ys[1], ys[2], z, z, z, w_branch)


def _ffn_gate_kernel(seq_ref, a_ref, ap_ref, an_ref, b_ref, cw_ref, cb_ref, o_ref):
    i = pl.program_id(0)
    seq_m1 = seq_ref[i] - 1
    tm = a_ref.shape[0]
    row = lax.broadcasted_iota(jnp.int32, (tm, 1), 0)
    halo_p = jnp.where(((i * tm) & seq_m1) == 0, 0.0, ap_ref[SUBLANES - 1:SUBLANES, :].astype(F32))
    halo_n = jnp.where((((i + 1) * tm) & seq_m1) == 0, 0.0, an_ref[0:1, :].astype(F32))

    def finish(inner_edges):
        a = a_ref[...].astype(F32)
        prev = jnp.where(row == 0, halo_p, pltpu.roll(a, 1, axis=0))
        nxt = jnp.where(row == tm - 1, halo_n, pltpu.roll(a, tm - 1, axis=0))
        if inner_edges:
            pos = row & seq_m1
            prev = jnp.where((pos == 0) & (row > 0), 0.0, prev)
            nxt = jnp.where((pos == seq_m1) & (row < tm - 1), 0.0, nxt)
        cw = cw_ref[...]
        conv = prev * cw[0:1, :] + a * cw[1:2, :] + nxt * cw[2:3, :] + cb_ref[...]
        o_ref[...] = jax.nn.gelu(conv.astype(BF16)) * b_ref[...]

    @pl.when(seq_m1 >= tm - 1)
    def _():
        finish(False)

    @pl.when(seq_m1 < tm - 1)
    def _():
        finish(True)


def _ffn_gate(u, conv_w, conv_b, seqlen, nblk, tc):
    m, ff2 = u.shape
    ff = ff2 // 2
    nrow8 = m // SUBLANES
    r8 = TM // SUBLANES
    grid_spec = pltpu.PrefetchScalarGridSpec(
        num_scalar_prefetch=1,
        grid=(nblk, ff // tc),
        in_specs=[
            pl.BlockSpec((TM, tc), lambda i, j, s: (i, j)),
            pl.BlockSpec((SUBLANES, tc), lambda i, j, s: (jnp.maximum(i * r8 - 1, 0), j)),
            pl.BlockSpec((SUBLANES, tc), lambda i, j, s: (jnp.minimum((i + 1) * r8, nrow8 - 1), j)),
            pl.BlockSpec((TM, tc), lambda i, j, s: (i, ff // tc + j)),
            pl.BlockSpec((3, tc), lambda i, j, s: (0, j)),
            pl.BlockSpec((1, tc), lambda i, j, s: (0, j)),
        ],
        out_specs=pl.BlockSpec((TM, tc), lambda i, j, s: (i, j)),
    )
    return pl.pallas_call(
        _ffn_gate_kernel,
        grid_spec=grid_spec,
        out_shape=jax.ShapeDtypeStruct((m, ff), BF16),
        compiler_params=_cparams("parallel", "arbitrary"),
        name="ffn_gate",
    )(seqlen, u, u, u, u, conv_w, conv_b.reshape(1, ff))


def _ffn_out_kernel(mrow_ref, h_ref, w_ref, x_ref, gate_ref, g_ref, sh_ref, sc_ref, *rest, final):
    outs, xrow = rest[:-1], rest[-1]
    j = pl.program_id(1)
    y = jnp.dot(h_ref[...], w_ref[...], preferred_element_type=F32)
    xt = x_ref[...] + gate_ref[...] * y
    xrow[j] = xt
    if not final:
        outs[0][...] = xt

    @pl.when(j == pl.num_programs(1) - 1)
    def _():
        x = jnp.concatenate([xrow[t] for t in range(xrow.shape[0])], axis=1)
        if final:
            y = x * lax.rsqrt(jnp.mean(x * x, axis=-1, keepdims=True) + EPS)
            outs[0][...] = y * g_ref[...]
        else:
            outs[1][...] = _norm_modulate(x, g_ref[...], sh_ref[...], sc_ref[...]).astype(outs[1].dtype)


def _ffn_out(h, w, x, mods, mods_next, g_next, mrow, nblk, tr, tn, final):
    m, ff = h.shape
    d = w.shape[1]
    mod = lambda arr_idx: pl.BlockSpec((None, None, 1, d), lambda i, j, mr: (mr[i], arr_idx, 0, 0))
    row = pl.BlockSpec((tr, d), lambda i, j, mr: (i, 0))
    tile = pl.BlockSpec((tr, tn), lambda i, j, mr: (i, j))
    grid_spec = pltpu.PrefetchScalarGridSpec(
        num_scalar_prefetch=1,
        grid=(nblk, d // tn),
        in_specs=[
            pl.BlockSpec((tr, ff), lambda i, j, mr: (i, 0)),
            pl.BlockSpec((ff, tn), lambda i, j, mr: (0, j)),
            tile,
            pl.BlockSpec((None, None, 1, tn), lambda i, j, mr: (mr[i], 5, 0, j)),
            pl.BlockSpec((1, d), lambda i, j, mr: (0, 0)), mod(0), mod(1),
        ],
        out_specs=[row] if final else [tile, row],
        scratch_shapes=[pltpu.VMEM((d // tn, tr, tn), F32)],
    )
    out_shape = [jax.ShapeDtypeStruct((m, d), F32)] + ([] if final else [jax.ShapeDtypeStruct((m, d), BF16)])
    return pl.pallas_call(
        functools.partial(_ffn_out_kernel, final=final),
        grid_spec=grid_spec,
        out_shape=out_shape,
        compiler_params=_cparams("parallel", "arbitrary"),
        name="ffn_out",
    )(mrow, h, w, x, mods, g_next.reshape(1, d), mods_next, mods_next)


def _rope_pair(x, cos, sin):
    lane = lax.broadcasted_iota(jnp.int32, x.shape, 1)
    first = lax.rem(lane, 32) < 16
    partner = jnp.where(first, pltpu.roll(x, LANES - 16, axis=1), pltpu.roll(x, 16, axis=1))
    return x * cos + partner * sin


def _ret_kernel(lg_ref, q_ref, k_ref, v_ref, g_ref, cos_ref, sin_ref, s0_ref, *rest, rope, has_prev):
    o_ref, sfin_ref, qr_scr, kr_scr, kv_scr = rest[1:] if has_prev else rest
    hp = pl.program_id(1)
    seq = q_ref.shape[0]
    c = RET_CHUNK
    nchunk = seq // c
    ks = RET_DK ** -0.5
    q = q_ref[...].astype(F32)
    k = k_ref[...].astype(F32)
    if rope:
        q = _rope_pair(q, cos_ref[...], sin_ref[...])
        k = _rope_pair(k, cos_ref[...], sin_ref[...])
    qr_scr[...] = q
    kr_scr[...] = k * ks

    pos_r = lax.broadcasted_iota(jnp.int32, (c, 1), 0).astype(F32)
    rel = (lax.broadcasted_iota(jnp.int32, (c, c), 0) - lax.broadcasted_iota(jnp.int32, (c, c), 1)).astype(F32)
    lane = lax.broadcasted_iota(jnp.int32, (1, LANES), 1)

    heads = []
    for hh in range(2):
        lgf = lg_ref[0, 2 * hp + hh]
        lgb = lg_ref[1, 2 * hp + hh]
        heads.append(dict(
            hm=(lane // RET_DK == hh).astype(F32),
            zeta_f=jnp.exp(lgf * (c - 1 - pos_r)), zeta_b=jnp.exp(lgb * pos_r),
            xi_f=jnp.exp(lgf * (pos_r + 1.0)), xi_b=jnp.exp(lgb * (c - pos_r)),
            dmat=jnp.where(rel >= 0, jnp.exp(lgf * jnp.maximum(rel, 0.0)), jnp.exp(lgb * jnp.maximum(-rel, 0.0))),
            gf=jnp.exp(lgf * c), gb=jnp.exp(lgb * c),
            vcols=slice(hh * RET_DV, (hh + 1) * RET_DV)))

    def kv_body(n, _):
        rows = pl.ds(pl.multiple_of(n * c, c), c)
        kr = kr_scr[rows, :]
        for hh, hd in enumerate(heads):
            kh = kr * hd["hm"]
            kz = jnp.concatenate([kh * hd["zeta_f"], kh * hd["zeta_b"]], axis=1).astype(BF16)
            kv_scr[hh, n] = lax.dot_general(kz, v_ref[rows, hd["vcols"]], (((0,), (0,)), ((), ())),
                                            preferred_element_type=F32)
        return 0

    lax.fori_loop(0, nchunk, kv_body, 0, unroll=min(8, nchunk))

    def scan(j, carry):
        nf, nbk = j, nchunk - 1 - j
        out = []
        for hh, hd in enumerate(heads):
            sf, sb = carry[2 * hh], carry[2 * hh + 1]
            tf = kv_scr[hh, nf, 0:LANES, :]
            kv_scr[hh, nf, 0:LANES, :] = sf
            tb = kv_scr[hh, nbk, LANES:2 * LANES, :]
            kv_scr[hh, nbk, LANES:2 * LANES, :] = sb
            out += [hd["gf"] * sf + tf, hd["gb"] * sb + tb]
        return tuple(out)

    fin = lax.fori_loop(0, nchunk, scan, (s0_ref[0, 0], s0_ref[0, 1], s0_ref[1, 0], s0_ref[1, 1]))
    for hh in range(2):
        sfin_ref[hh, 0] = fin[2 * hh]
        sfin_ref[hh, 1] = fin[2 * hh + 1]

    def out_body(n, _):
        rows = pl.ds(pl.multiple_of(n * c, c), c)
        qr = qr_scr[rows, :]
        kb = kr_scr[rows, :].astype(BF16)
        for hh, hd in enumerate(heads):
            qm = qr * hd["hm"]
            a = lax.dot_general(qm.astype(BF16), kb, (((1,), (1,)), ((), ())),
                                preferred_element_type=F32) * hd["dmat"]
            o = jnp.dot(a.astype(BF16), v_ref[rows, hd["vcols"]], preferred_element_type=F32)
            qx = jnp.concatenate([qm * hd["xi_f"], qm * hd["xi_b"]], axis=1).astype(BF16)
            o = o + jnp.dot(qx, kv_scr[hh, n].astype(BF16), preferred_element_type=F32)
            o = o * lax.rsqrt(jnp.mean(o * o, axis=-1, keepdims=True) + EPS)
            g = g_ref[rows, hd["vcols"]].astype(F32)
            o_ref[rows, hd["vcols"]] = (g * jax.nn.sigmoid(g) * o).astype(o_ref.dtype)
        return 0

    lax.fori_loop(0, nchunk, out_body, 0, unroll=min(8, nchunk))


def _retention(z, log_decay, cos, sin, s0, *, nb, seq, row0, rope, out_rows, y_prev=None):
    rb0 = row0 // seq
    hpairs = RET_HEADS // 2
    kcol, qcol = COL_RET_K // LANES, COL_RET_Q // LANES
    vcol, gcol = COL_RET_V // (2 * RET_DV), COL_RET_G // (2 * RET_DV)
    has_prev = y_prev is not None
    kernel = functools.partial(_ret_kernel, rope=rope, has_prev=has_prev)
    in_specs = [
        pl.BlockSpec(memory_space=pltpu.SMEM),
        pl.BlockSpec((seq, LANES), lambda b, p: (rb0 + b, qcol + p)),
        pl.BlockSpec((seq, LANES), lambda b, p: (rb0 + b, kcol + p)),
        pl.BlockSpec((seq, 2 * RET_DV), lambda b, p: (rb0 + b, vcol + p)),
        pl.BlockSpec((seq, 2 * RET_DV), lambda b, p: (rb0 + b, gcol + p)),
        pl.BlockSpec((seq, LANES), lambda b, p: (0, 0)),
        pl.BlockSpec((seq, LANES), lambda b, p: (0, 0)),
        pl.BlockSpec((None, 2, 2, LANES, RET_DV), lambda b, p: (b, p, 0, 0, 0)),
    ]
    args = [log_decay, z, z, z, z, cos, sin, s0]
    if has_prev:
        in_specs.append(pl.BlockSpec(memory_space=pl.ANY))
        args.append(y_prev)
    return pl.pallas_call(
        kernel,
        grid=(nb, hpairs),
        in_specs=in_specs,
        out_specs=[
            pl.BlockSpec((seq, 2 * RET_DV), lambda b, p: (rb0 + b, p)),
            pl.BlockSpec((None, 2, 2, LANES, RET_DV), lambda b, p: (b, p, 0, 0, 0)),
        ],
        out_shape=[
            jax.ShapeDtypeStruct((out_rows, RET_HEADS * RET_DV), BF16),
            jax.ShapeDtypeStruct((nb, RET_HEADS, 2, LANES, RET_DV), F32),
        ],
        scratch_shapes=[
            pltpu.VMEM((seq, LANES), F32),
            pltpu.VMEM((seq, LANES), F32),
            pltpu.VMEM((2, seq // RET_CHUNK, 2 * LANES, RET_DV), F32),
        ],
        input_output_aliases={len(args) - 1: 0} if has_prev else {},
        compiler_params=_cparams("parallel", "arbitrary"),
        name="retention_rope" if rope else "retention",
    )(*args)


NA_GROUP = 4
NA_KROWS = 12
NA_OVERLAP = 4


def _na_kernel(rpb_ref, q_ref, k_ref, v_ref, kc_ref, vc_ref, o_ref, bias_scr):
    seq = q_ref.shape[0]
    nrow = seq // GRID_W
    ndr, ndc = 2 * NA_WR - 1, 2 * NA_WC - 1
    scale = NA_DH ** -0.5
    nt = (((1,), (1,)), ((), ()))
    h = pl.program_id(0)
    lane = lax.broadcasted_iota(jnp.int32, (GRID_W, LANES), 1)

    @pl.when(pl.program_id(1) == 0)
    def _():
        qi = lax.broadcasted_iota(jnp.int32, (GRID_W, LANES), 0)
        ki = lane & (GRID_W - 1)
        dc = ki - qi + (NA_WC - 1)
        cs = jnp.clip(qi - NA_WC // 2, 0, GRID_W - NA_WC)
        colmask = (ki >= cs) & (ki < cs + NA_WC)

        def toeplitz(d):
            t = jnp.zeros((GRID_W, LANES), F32)
            for j in range(ndc):
                t = jnp.where(dc == j, rpb_ref[h * (ndr * ndc) + d * ndc + j], t)
            return t

        prev = jnp.zeros((GRID_W, LANES), F32)
        for i in range(ndr + 1):
            nxt = toeplitz(i) if i < ndr else jnp.zeros((GRID_W, LANES), F32)
            bias_scr[i] = jnp.where(colmask, jnp.where(lane < GRID_W, prev, nxt), NEG)
            prev = nxt

    kc = kc_ref[...]
    vc = vc_ref[...]
    kwin = NA_KROWS * GRID_W

    half = NA_WR // 2
    ntile = kwin // LANES

    def group(r0, interior):
        us = r0 - half if interior else min(max(r0 - half, 0), nrow - NA_KROWS)
        if isinstance(r0, int):
            qrows = pl.ds(r0 * GRID_W, NA_GROUP * GRID_W)
            krows = pl.ds(us * GRID_W, kwin)
        else:
            qrows = pl.ds(pl.multiple_of(r0 * GRID_W, NA_GROUP * GRID_W), NA_GROUP * GRID_W)
            krows = pl.ds(pl.multiple_of(us * GRID_W, GRID_W), kwin)
        q = q_ref[qrows, :]
        s = lax.dot_general(q, k_ref[krows, :], nt, preferred_element_type=F32)
        sc = lax.dot_general(q, kc, nt, preferred_element_type=F32) * scale
        e_rows, ec_rows, den_rows = [], [], []
        for u in range(NA_GROUP):
            qr = slice(u * GRID_W, (u + 1) * GRID_W)
            tiles = []
            for m in range(ntile):
                if interior:
                    off, idx = 2 * m - u, 2 * m - u + half
                else:
                    r = r0 + u
                    off = us + 2 * m - min(max(r - half, 0), nrow - NA_WR)
                    idx = min(max(us + 2 * m - r + NA_WR, 0), ndr)
                ok_lo, ok_hi = 0 <= off < NA_WR, 0 <= off + 1 < NA_WR
                if not (ok_lo or ok_hi):
                    tiles.append(None)
                    continue
                tab = bias_scr[idx]
                if not (ok_lo and ok_hi):
                    tab = tab + jnp.where(lane < GRID_W, 0.0 if ok_lo else NEG, 0.0 if ok_hi else NEG)
                st = s[qr, m * LANES:(m + 1) * LANES]
                tiles.append(jnp.where(tab > 0.5 * NEG, st * scale + tab, NEG))
            live = [t for t in tiles if t is not None]
            tmax = live[0]
            for t in live[1:]:
                tmax = jnp.maximum(tmax, t)
            scu = sc[qr, :]
            mx = jnp.maximum(jnp.max(tmax, axis=-1, keepdims=True), jnp.max(scu, axis=-1, keepdims=True))
            etiles = [None if t is None else jnp.exp(t - mx) for t in tiles]
            ecu = jnp.exp(scu - mx)
            esum = None
            for t in etiles:
                if t is not None:
                    esum = t if esum is None else esum + t
            den_rows.append(jnp.sum(esum, axis=-1, keepdims=True) + jnp.sum(ecu, axis=-1, keepdims=True))
            zero = jnp.zeros((GRID_W, LANES), BF16)
            e_rows.append(jnp.concatenate([zero if t is None else t.astype(BF16) for t in etiles], axis=1))
            ec_rows.append(ecu.astype(BF16))
        o = jnp.dot(jnp.concatenate(e_rows, axis=0), v_ref[krows, :], preferred_element_type=F32)
        o = o + jnp.dot(jnp.concatenate(ec_rows, axis=0), vc, preferred_element_type=F32)
        o_ref[qrows, :] = (o / jnp.concatenate(den_rows, axis=0)).astype(o_ref.dtype)

    ngroup = nrow // NA_GROUP
    g_lo = -(-half // NA_GROUP)
    g_hi = min((nrow - NA_KROWS + half) // NA_GROUP, (nrow - NA_WR + half - NA_GROUP + 1) // NA_GROUP)
    for g in range(g_lo):
        group(g * NA_GROUP, False)

    def body(i, _):
        for t in range(NA_OVERLAP):
            group((g_lo + NA_OVERLAP * i + t) * NA_GROUP, True)
        return 0

    nloop = (g_hi + 1 - g_lo) // NA_OVERLAP
    lax.fori_loop(0, nloop, body, 0)
    for g in range(g_lo + NA_OVERLAP * nloop, g_hi + 1):
        group(g * NA_GROUP, True)
    for g in range(g_hi + 1, ngroup):
        group(g * NA_GROUP, False)


def _na_attention(z, rpb, *, nb, seq, cseq, crow0, out_rows):
    kcol, vcol, qcol = COL_NA_K // NA_DH, COL_NA_V // NA_DH, COL_NA_Q // NA_DH
    crb0 = crow0 // cseq
    nrow = seq // GRID_W
    assert nrow >= NA_KROWS and nrow % NA_GROUP == 0 and GRID_W * 2 == LANES
    assert NA_KROWS % 2 == 0 and NA_KROWS >= NA_WR + NA_GROUP - 1
    return pl.pallas_call(
        _na_kernel,
        grid=(NA_HEADS, nb),
        in_specs=[
            pl.BlockSpec(memory_space=pltpu.SMEM),
            pl.BlockSpec((seq, NA_DH), lambda h, b: (b, qcol + h)),
            pl.BlockSpec((seq, NA_DH), lambda h, b: (b, kcol + h)),
            pl.BlockSpec((seq, NA_DH), lambda h, b: (b, vcol + h)),
            pl.BlockSpec((cseq, NA_DH), lambda h, b: (crb0 + b, kcol + h)),
            pl.BlockSpec((cseq, NA_DH), lambda h, b: (crb0 + b, vcol + h)),
        ],
        out_specs=pl.BlockSpec((seq, NA_DH), lambda h, b: (b, h)),
        out_shape=jax.ShapeDtypeStruct((out_rows, NA_HEADS * NA_DH), BF16),
        scratch_shapes=[pltpu.VMEM((2 * NA_WR, GRID_W, LANES), F32)],
        compiler_params=_cparams("arbitrary", "arbitrary"),
        name="na_attention",
    )(rpb.reshape(-1), z, z, z, z, z)


def _dense_attn_kernel(q_ref, k_ref, v_ref, yprev_ref, o_ref):
    s = lax.dot_general(q_ref[...], k_ref[...], (((1,), (1,)), ((), ())), preferred_element_type=F32) * NA_DH ** -0.5
    e = jnp.exp(s - jnp.max(s, axis=-1, keepdims=True))
    o = jnp.dot(e.astype(BF16), v_ref[...], preferred_element_type=F32)
    o_ref[...] = (o / jnp.sum(e, axis=-1, keepdims=True)).astype(o_ref.dtype)


def _dense_attention(z, y_prev, *, nb, cseq, crow0):
    kcol, vcol, qcol = COL_NA_K // NA_DH, COL_NA_V // NA_DH, COL_NA_Q // NA_DH
    crb0 = crow0 // cseq
    return pl.pallas_call(
        _dense_attn_kernel,
        grid=(nb, NA_HEADS),
        in_specs=[
            pl.BlockSpec((cseq, NA_DH), lambda b, h: (crb0 + b, qcol + h)),
            pl.BlockSpec((cseq, NA_DH), lambda b, h: (crb0 + b, kcol + h)),
            pl.BlockSpec((cseq, NA_DH), lambda b, h: (crb0 + b, vcol + h)),
            pl.BlockSpec(memory_space=pl.ANY),
        ],
        out_specs=pl.BlockSpec((cseq, NA_DH), lambda b, h: (crb0 + b, h)),
        out_shape=jax.ShapeDtypeStruct(y_prev.shape, BF16),
        input_output_aliases={3: 0},
        compiler_params=_cparams("parallel", "arbitrary"),
        name="dense_attention",
    )(z, z, z, y_prev)


HY_FEAT_ROWS = 64
HIGHEST = lax.Precision.HIGHEST


def _hy_features(seq):
    t = np.linspace(0.0, 1.0, seq)
    bands = (HY_EMB - 1) // 2
    w = 2.0 * math.pi * np.arange(seq) / seq
    fr = np.linspace(1e-4, bands - 1, bands)
    ang = fr[None] * w[:, None]
    feat = np.concatenate([t[:, None], np.cos(ang), -np.sin(ang)], axis=-1)
    src = np.concatenate([np.arange(seq), np.zeros(1, np.int64), np.arange(seq - 1, 0, -1)])
    feat2 = np.zeros((HY_FEAT_ROWS, 2 * seq), np.float32)
    feat2[:HY_EMB] = feat[src].T
    tt = t[src][None].astype(np.float32)
    mask = np.ones((1, 2 * seq), np.float32)
    mask[0, seq] = 0.0
    return feat2, tt, mask


def _hy_mlp_kernel(feat_ref, w1_ref, b1_ref, w2_ref, b2_ref, fq_ref, o_ref):
    fq = fq_ref[...]
    h = jnp.dot(w1_ref[...], feat_ref[...], preferred_element_type=F32, precision=HIGHEST)
    h = jnp.sin(fq * (h + b1_ref[...]))
    h = jnp.dot(w2_ref[...], h, preferred_element_type=F32, precision=HIGHEST)
    o_ref[...] = jnp.sin(fq * (h + b2_ref[...]))


def _hy_mlp(feat2, w1t, b1, w2t, b2, fq):
    depth, hid, _ = w1t.shape
    n = feat2.shape[1]
    col = lambda a: a.reshape(depth, hid, 1)
    wspec = lambda k: pl.BlockSpec((None, hid, k), lambda l: (l, 0, 0))
    return pl.pallas_call(
        _hy_mlp_kernel,
        grid=(depth,),
        in_specs=[pl.BlockSpec((HY_FEAT_ROWS, n), lambda l: (0, 0)), wspec(HY_FEAT_ROWS), wspec(1), wspec(hid),
                  wspec(1), wspec(1)],
        out_specs=pl.BlockSpec((None, hid, n), lambda l: (l, 0, 0)),
        out_shape=jax.ShapeDtypeStruct((depth, hid, n), F32),
        compiler_params=_cparams("parallel"),
        name="hyena_filter_mlp",
    )(feat2, w1t, col(b1), w2t, col(b2), col(fq))


def _hy_filter_kernel(h_ref, w3_ref, dl_ref, tt_ref, mask_ref, o_ref):
    seq = h_ref.shape[1] // 2
    kf = jnp.dot(w3_ref[0], h_ref[:, :seq], preferred_element_type=F32, precision=HIGHEST)
    kb = jnp.dot(w3_ref[1], h_ref[:, seq:], preferred_element_type=F32, precision=HIGHEST)
    k = jnp.concatenate([kf, kb], axis=1) * (jnp.exp(-tt_ref[...] * dl_ref[...]) * mask_ref[...])
    o_ref[...] = (k * lax.rsqrt(jnp.sum(k * k, axis=1, keepdims=True) + EPS)).astype(o_ref.dtype)


def _hy_filter(h2, w3t, deltas, tt, mask, cb):
    depth, hid, n = h2.shape
    c = w3t.shape[3]
    return pl.pallas_call(
        _hy_filter_kernel,
        grid=(depth, HY_ORDER, c // cb),
        in_specs=[
            pl.BlockSpec((None, hid, n), lambda l, o, j: (l, 0, 0)),
            pl.BlockSpec((None, None, 2, cb, hid), lambda l, o, j: (l, o, 0, j, 0)),
            pl.BlockSpec((cb, 1), lambda l, o, j: (j, 0)),
            pl.BlockSpec((1, n), lambda l, o, j: (0, 0)),
            pl.BlockSpec((1, n), lambda l, o, j: (0, 0)),
        ],
        out_specs=pl.BlockSpec((None, None, cb, n), lambda l, o, j: (l, o, j, 0)),
        out_shape=jax.ShapeDtypeStruct((depth, HY_ORDER, c, n), BF16),
        compiler_params=_cparams("parallel", "parallel", "arbitrary"),
        name="hyena_filter",
    )(h2, w3t, deltas, tt, mask)


FFT_NO = 64
FFT_NI = 128


def _fft_tables():
    n = FFT_NO * FFT_NI
    a = np.arange(FFT_NO)
    fo = np.exp(-2j * np.pi * np.outer(a, a) / FFT_NO)
    i = np.arange(FFT_NI)
    ci = np.exp(-2j * np.pi * np.outer(i, i) / FFT_NI)
    tw = np.exp(-2j * np.pi * np.outer(a, i) / n)
    half = FFT_NO // 2
    f32 = lambda x: np.ascontiguousarray(x, dtype=np.float32)
    g1_real = f32(np.concatenate([fo.real, fo.imag], axis=0))
    g1 = f32(np.block([[fo.real[:, :half], -fo.imag[:, :half]], [fo.imag[:, :half], fo.real[:, :half]]]))
    w2 = f32(np.block([[ci.real, ci.imag], [-ci.imag, ci.real]]))
    w2i = f32(np.block([[ci.real, -ci.imag], [ci.imag, ci.real]]))
    g4 = f32(np.block([[fo.real[:half], fo.imag[:half]], [-fo.imag[:half], fo.real[:half]]]) / n)
    return dict(g1_real=g1_real, g1=g1, w2=w2, w2i=w2i, g4=g4, twr=f32(tw.real), twi=f32(tw.imag))


HY_CHUNK = 16
HY_SPEC_CHUNK = 32


def _fft_stage1(g1, y2, twr, twi):
    nch = y2.shape[1] // FFT_NI
    o1 = jnp.dot(g1, y2, preferred_element_type=F32)
    yr, yi = o1[:FFT_NO].astype(BF16), o1[FFT_NO:].astype(BF16)
    ar = yr * twr - yi * twi
    ai = yr * twi + yi * twr
    lanes = lambda ci: slice(ci * FFT_NI, (ci + 1) * FFT_NI)
    return jnp.concatenate([jnp.concatenate([ar[:, lanes(ci)], ai[:, lanes(ci)]], axis=1)
                            for ci in range(nch)], axis=0)


def _hy_spectrum_kernel(k_ref, g1_ref, w2_ref, twr_ref, twi_ref, o_ref):
    cb = k_ref.shape[0]
    ch = HY_SPEC_CHUNK
    g1 = g1_ref[...]
    w2 = w2_ref[...]
    twr = twr_ref[...]
    twi = twi_ref[...]

    def body(j, _):
        chans = pl.ds(pl.multiple_of(j * ch, ch), ch)
        ks = k_ref[chans]
        k2 = jnp.concatenate([ks[ci] for ci in range(ch)], axis=1).astype(BF16)
        z = jnp.dot(_fft_stage1(g1, k2, twr, twi), w2, preferred_element_type=F32)
        o_ref[chans] = z.reshape(ch, FFT_NO, 2 * FFT_NI).astype(o_ref.dtype)
        return 0

    lax.fori_loop(0, cb // ch, body, 0)


def _tiled_twiddles(tabs, nch):
    tile = lambda a: jnp.asarray(np.tile(a, (1, nch)), BF16)
    return tile(tabs["twr"]), tile(tabs["twi"])


def _hy_spectrum(kfilt, tabs, cb):
    g, c = kfilt.shape[:2]
    const = lambda a: pl.BlockSpec(a.shape, lambda i, j: (0,) * a.ndim)
    g1 = jnp.asarray(tabs["g1_real"], BF16)
    w2 = jnp.asarray(tabs["w2"], BF16)
    twr, twi = _tiled_twiddles(tabs, HY_SPEC_CHUNK)
    return pl.pallas_call(
        _hy_spectrum_kernel,
        grid=(g, c // cb),
        in_specs=[pl.BlockSpec((None, cb, FFT_NO, FFT_NI), lambda i, j: (i, j, 0, 0)),
                  const(g1), const(w2), const(twr), const(twi)],
        out_specs=pl.BlockSpec((None, cb, FFT_NO, 2 * FFT_NI), lambda i, j: (i, j, 0, 0)),
        out_shape=jax.ShapeDtypeStruct((g, c, FFT_NO, 2 * FFT_NI), BF16),
        compiler_params=_cparams("parallel", "arbitrary"),
        name="hyena_spectrum",
    )(kfilt, g1, w2, twr, twi)


def _shift_conv3(x, w0, w1, w2, b):
    nrow, nlane = x.shape
    row = lax.broadcasted_iota(jnp.int32, x.shape, 0)
    lane = lax.broadcasted_iota(jnp.int32, x.shape, 1)
    r = pltpu.roll(x, 1, axis=1)
    prev = jnp.where(lane == 0, pltpu.roll(r, 1, axis=0), r)
    prev = jnp.where((lane == 0) & (row == 0), 0.0, prev)
    r = pltpu.roll(x, nlane - 1, axis=1)
    nxt = jnp.where(lane == nlane - 1, pltpu.roll(r, nrow - 1, axis=0), r)
    nxt = jnp.where((lane == nlane - 1) & (row == nrow - 1), 0.0, nxt)
    return prev * w0 + x * w1 + nxt * w2 + b


def _hyena_kernel(x_ref, kf_ref, pm_ref, g1_ref, w2_ref, w2i_ref, g4_ref, twr_ref, twi_ref, o_ref):
    cb = x_ref.shape[2]
    half = FFT_NO // 2
    ch = HY_CHUNK
    g1 = g1_ref[...]
    g4 = g4_ref[...]
    w2 = w2_ref[...]
    w2i = w2i_ref[...]
    twr = twr_ref[...]
    twi = twi_ref[...]
    lanes = lambda ci: slice(ci * FFT_NI, (ci + 1) * FFT_NI)
    krows = lambda ci: slice(ci * FFT_NO, (ci + 1) * FFT_NO)

    def body(j, _):
        c0 = pl.multiple_of(j * ch, ch)
        pm = pm_ref[:, pl.ds(c0, ch)]
        conv = []
        for part in range(3):
            halves = []
            for bb in range(2):
                xs = x_ref[bb, part, pl.ds(c0, ch)].astype(F32)
                halves.append(jnp.concatenate(
                    [_shift_conv3(xs[ci], pm[part, ci], pm[3 + part, ci], pm[6 + part, ci], pm[9 + part, ci])
                     for ci in range(ch)], axis=1))
            conv.append(jnp.concatenate(halves, axis=0))
        y = conv[2]
        for order in range(HY_ORDER):
            z = jnp.dot(_fft_stage1(g1, y.astype(BF16), twr, twi), w2, preferred_element_type=F32)
            kf = kf_ref[order, pl.ds(c0, ch)].reshape(ch * FFT_NO, 2 * FFT_NI)
            zr, zi = z[:, :FFT_NI].astype(BF16), z[:, FFT_NI:].astype(BF16)
            kr, ki = kf[:, :FFT_NI], kf[:, FFT_NI:]
            p = jnp.concatenate([zr * kr - zi * ki, zr * ki + zi * kr], axis=1)
            q = jnp.dot(p, w2i, preferred_element_type=F32).astype(BF16)
            qr = jnp.concatenate([q[krows(ci), :FFT_NI] for ci in range(ch)], axis=1)
            qi = jnp.concatenate([q[krows(ci), FFT_NI:] for ci in range(ch)], axis=1)
            qs = jnp.concatenate([qr * twr + qi * twi, qi * twr - qr * twi], axis=0)
            cv = jnp.dot(g4, qs, preferred_element_type=F32)
            bias = jnp.concatenate([pm[12 + order, ci] for ci in range(ch)], axis=1)
            y = conv[order] * (cv + bias * y)
        for ci in range(ch):
            o_ref[0, c0 + ci] = y[:half, lanes(ci)].astype(o_ref.dtype)
            o_ref[1, c0 + ci] = y[half:, lanes(ci)].astype(o_ref.dtype)
        return 0

    lax.fori_loop(0, cb // ch, body, 0)


def _hyena(xt, kf, layer, pm, tabs, cb):
    nb, _, c, half, _ = xt.shape
    const = lambda a: pl.BlockSpec(a.shape, lambda j, p: (0,) * a.ndim)
    bf = lambda name: jnp.asarray(tabs[name], BF16)
    g1, w2, w2i, g4 = bf("g1"), bf("w2"), bf("w2i"), bf("g4")
    twr, twi = _tiled_twiddles(tabs, HY_CHUNK)
    return pl.pallas_call(
        _hyena_kernel,
        grid=(c // cb, nb // 2),
        in_specs=[
            pl.BlockSpec((2, 3, cb, half, FFT_NI), lambda j, p: (p, 0, j, 0, 0)),
            pl.BlockSpec((None, HY_ORDER, cb, FFT_NO, 2 * FFT_NI), lambda j, p: (layer, 0, j, 0, 0)),
            pl.BlockSpec((14, cb, 1, FFT_NI), lambda j, p: (0, j, 0, 0)),
            const(g1), const(w2), const(w2i), const(g4), const(twr), const(twi),
        ],
        out_specs=pl.BlockSpec((2, cb, half, FFT_NI), lambda j, p: (p, j, 0, 0)),
        out_shape=jax.ShapeDtypeStruct((nb, c, half, FFT_NI), BF16),
        compiler_params=_cparams("parallel", "arbitrary"),
        name="hyena",
    )(xt, kf, pm, g1, w2, w2i, g4, twr, twi)


def _dft_tables(seq):
    n = 2 * seq
    wmat = np.exp(-2j * np.pi * np.outer(np.arange(n), np.arange(n)) / n)
    f32 = lambda x: np.ascontiguousarray(x, dtype=np.float32)
    fwd_real = f32(np.concatenate([wmat.real, wmat.imag], axis=1))
    ws = wmat[:seq]
    fwd = f32(np.block([[ws.real, ws.imag], [-ws.imag, ws.real]]))
    wi = np.conj(wmat)[:, :seq] / n
    inv = f32(np.block([[wi.real, wi.imag], [-wi.imag, wi.real]]))
    return dict(fwd_real=fwd_real, fwd=fwd, inv=inv)


def _rowdft_kernel(x_ref, w_ref, o_ref):
    o_ref[...] = jnp.dot(x_ref[...].astype(BF16), w_ref[...], preferred_element_type=F32)


def _rowdft(x, w, tr):
    m, k = x.shape
    n = w.shape[1]
    return pl.pallas_call(
        _rowdft_kernel,
        grid=(m // tr,),
        in_specs=[pl.BlockSpec((tr, k), lambda i: (i, 0)), pl.BlockSpec((k, n), lambda i: (0, 0))],
        out_specs=pl.BlockSpec((tr, n), lambda i: (i, 0)),
        out_shape=jax.ShapeDtypeStruct((m, n), F32),
        compiler_params=_cparams("parallel"),
        name="row_dft",
    )(x, w)


def _lane_conv3(x, w0, w1, w2, b):
    seq = x.shape[1]
    lane = lax.broadcasted_iota(jnp.int32, x.shape, 1)
    prev = jnp.where(lane == 0, 0.0, pltpu.roll(x, 1, axis=1))
    nxt = jnp.where(lane == seq - 1, 0.0, pltpu.roll(x, seq - 1, axis=1))
    return prev * w0 + x * w1 + nxt * w2 + b


def _hyena_ctx_kernel(x_ref, kf_ref, pm_ref, fwd_ref, inv_ref, o_ref):
    seq = x_ref.shape[3]
    n = 2 * seq
    conv = [[_lane_conv3(x_ref[bb, part].astype(F32), pm_ref[part], pm_ref[3 + part], pm_ref[6 + part],
                         pm_ref[9 + part]) for bb in range(2)] for part in range(3)]
    ya, yb = conv[2]
    for order in range(HY_ORDER):
        z = jnp.dot(jnp.concatenate([ya, yb], axis=1).astype(BF16), fwd_ref[...], preferred_element_type=F32)
        kf = kf_ref[order]
        zr, zi, kr, ki = z[:, :n], z[:, n:], kf[:, :n], kf[:, n:]
        p = jnp.concatenate([zr * kr - zi * ki, zr * ki + zi * kr], axis=1).astype(BF16)
        cv = jnp.dot(p, inv_ref[...], preferred_element_type=F32)
        bias = pm_ref[12 + order]
        ya = conv[order][0] * (cv[:, :seq] + bias * ya)
        yb = conv[order][1] * (cv[:, seq:] + bias * yb)
    o_ref[0] = ya.astype(o_ref.dtype)
    o_ref[1] = yb.astype(o_ref.dtype)


def _hyena_ctx(xt, kf, pm, tabs, cb):
    nb, _, c, seq = xt.shape
    fwd, inv = jnp.asarray(tabs["fwd"], BF16), jnp.asarray(tabs["inv"], BF16)
    const = lambda a: pl.BlockSpec(a.shape, lambda j, p: (0,) * a.ndim)
    return pl.pallas_call(
        _hyena_ctx_kernel,
        grid=(c // cb, nb // 2),
        in_specs=[
            pl.BlockSpec((2, 3, cb, seq), lambda j, p: (p, 0, j, 0)),
            pl.BlockSpec((HY_ORDER, cb, 4 * seq), lambda j, p: (0, j, 0)),
            pl.BlockSpec((14, cb, 1), lambda j, p: (0, j, 0)),
            const(fwd), const(inv),
        ],
        out_specs=pl.BlockSpec((2, cb, seq), lambda j, p: (p, j, 0)),
        out_shape=jax.ShapeDtypeStruct((nb, c, seq), BF16),
        compiler_params=_cparams("parallel", "arbitrary"),
        name="hyena_ctx",
    )(xt, kf, pm, fwd, inv)


def _rope_tables(seq):
    t = np.arange(seq)
    pos = np.stack([t // GRID_W, t % GRID_W], axis=1).astype(np.float64)
    n = RET_DK // 4
    inv = ROPE_BASE ** (-np.arange(n, dtype=np.float64) / n)
    lane = np.arange(LANES) % RET_DK
    ang = pos[:, lane // (2 * n)] * inv[lane % n][None]
    sign = np.where(lane % (2 * n) < n, -1.0, 1.0)
    return np.cos(ang).astype(np.float32), (np.sin(ang) * sign[None]).astype(np.float32)


def _hy_filters(seq, hy_f_w1, hy_f_b1, hy_f_w2, hy_f_b2, hy_f_w3, hy_f_freq):
    depth, _, hid = hy_f_w1.shape
    c = hy_f_w3.shape[2] // (2 * HY_ORDER)
    feat2, tt, mask = _hy_features(seq)
    w1t = jnp.pad(hy_f_w1.transpose(0, 2, 1), ((0, 0), (0, 0), (0, HY_FEAT_ROWS - HY_EMB)))
    h2 = _hy_mlp(jnp.asarray(feat2), w1t, hy_f_b1, hy_f_w2.transpose(0, 2, 1), hy_f_b2, hy_f_freq)
    w3t = hy_f_w3.reshape(depth, hid, HY_ORDER, 2, c).transpose(0, 2, 3, 4, 1)
    deltas = np.abs(np.linspace(math.log(HY_TARGET) / HY_FAST, math.log(HY_TARGET) / HY_SLOW, c))
    deltas = jnp.asarray(deltas.astype(np.float32).reshape(c, 1))
    cb = min(c, (1024 * 1024) // (2 * seq))
    return _hy_filter(h2, w3t, deltas, jnp.asarray(tt), jnp.asarray(mask), cb)


def _hy_params(hy_conv_w, hy_conv_b, hy_bias):
    c = hy_bias.shape[1]
    return jnp.concatenate([hy_conv_w.reshape(9, c), hy_conv_b.reshape(3, c), hy_bias], axis=0)


def kernel(x, c, ctx, c_ctx, w_mod, b_mod, g_norm1, g_norm2, w_in, hy_conv_w, hy_conv_b, hy_f_w1, hy_f_b1, hy_f_w2, hy_f_b2, hy_f_w3, hy_f_freq, hy_bias, ret_log_decay, na_rpb, w_branch, w_out, ffn_w_in, ffn_conv_w, ffn_conv_b, ffn_w_out, g_final):
    nb, seq, d = x.shape
    cseq = ctx.shape[1]
    depth = w_mod.shape[0]
    hy_w = hy_bias.shape[2]
    t_lat, t_ctx = nb * seq, nb * cseq
    assert seq % TM == 0 and t_ctx % TM == 0 and TM % cseq == 0 and nb % 2 == 0 and nb < SUBLANES
    assert 2 * seq == FFT_NO * FFT_NI and seq % GRID_W == 0 and cseq & (cseq - 1) == 0
    blk_lat, blk_all = t_lat // TM, (t_lat + t_ctx) // TM
    mrow = jnp.asarray(np.concatenate([np.repeat(np.arange(nb), seq // TM), np.full(t_ctx // TM, nb)]), jnp.int32)
    seqlen = jnp.asarray(np.concatenate([np.full(blk_lat, seq), np.full(t_ctx // TM, cseq)]), jnp.int32)
    per = TM // TR
    mrow_r = jnp.repeat(mrow, per)
    tn = 1024

    cond = jnp.zeros((SUBLANES, d), F32).at[:nb].set(c).at[nb].set(c_ctx)
    mods = _mods(cond, w_mod, b_mod).reshape(depth, SUBLANES, 6, 1, d)

    cos, sin = (jnp.asarray(a) for a in _rope_tables(seq))
    tabs = _fft_tables()
    ctabs = _dft_tables(cseq)
    filt = _hy_filters(seq, hy_f_w1, hy_f_b1, hy_f_w2, hy_f_b2, hy_f_w3, hy_f_freq)
    spec = _hy_spectrum(filt.reshape(depth * HY_ORDER, hy_w, FFT_NO, FFT_NI), tabs, 32)
    spec = spec.reshape(depth, HY_ORDER, hy_w, FFT_NO, 2 * FFT_NI)
    cfilt = _hy_filters(cseq, hy_f_w1, hy_f_b1, hy_f_w2, hy_f_b2, hy_f_w3, hy_f_freq)
    cspec = _rowdft(cfilt.reshape(depth * HY_ORDER * hy_w, 2 * cseq), jnp.asarray(ctabs["fwd_real"], BF16), 1024)
    cspec = cspec.reshape(depth, HY_ORDER, hy_w, 4 * cseq)

    xs = jnp.concatenate([x.reshape(t_lat, d), ctx.reshape(t_ctx, d)], axis=0)
    assert hy_w == HY_W and w_in.shape[2] == COL_GATE + N_BRANCH * d
    hy_cols = COL_GATE - COL_HY
    s_zero = jnp.zeros((nb, RET_HEADS, 2, LANES, RET_DV), F32)
    hn = _normmod(xs, g_norm1[0], mods[0], mrow, 0, 1, blk_all)
    for l in range(depth):
        last = l == depth - 1
        nblk = blk_lat if last else blk_all
        rows = nblk * TM
        z = _matmul(hn, w_in, l, blk_all, tn, w_in.shape[2] - hy_cols, skip=(COL_HY // tn, hy_cols // tn))
        hy_proj = functools.partial(_matmul_nt, hn, w_in, l, col0=COL_HY, n_out=hy_cols, nb=nb, tc=1024)
        zh = hy_proj(blk0=0, nblk=blk_lat, seq=seq)

        y_ret, s_ctx = _retention(z, ret_log_decay[l], cos, sin, s_zero, nb=nb, seq=cseq, row0=t_lat, rope=False,
                                  out_rows=blk_all * TM)
        y_ret, _ = _retention(z, ret_log_decay[l], cos, sin, s_ctx, nb=nb, seq=seq, row0=0, rope=True,
                              out_rows=blk_all * TM, y_prev=y_ret)
        y_na = _na_attention(z, na_rpb[l], nb=nb, seq=seq, cseq=cseq, crow0=t_lat, out_rows=rows)
        pm = _hy_params(hy_conv_w[l], hy_conv_b[l], hy_bias[l])
        pm_lat = jnp.broadcast_to(pm[:, :, None, None], (14, hy_w, 1, FFT_NI))
        y_hy = _hyena(zh.reshape(nb, 3, hy_w, FFT_NO // 2, FFT_NI), spec, l, pm_lat, tabs, 32)
        y_hy = y_hy.reshape(nb, hy_w, seq).transpose(0, 2, 1).reshape(t_lat, hy_w)
        if not last:
            y_na = _dense_attention(z, y_na, nb=nb, cseq=cseq, crow0=t_lat)
            zc = hy_proj(blk0=blk_lat, nblk=t_ctx // TM, seq=cseq)
            yc_hy = _hyena_ctx(zc.reshape(nb, 3, hy_w, cseq), cspec[l], pm[:, :, None], ctabs, 256)
            y_hy = jnp.concatenate([y_hy, yc_hy.transpose(0, 2, 1).reshape(t_ctx, hy_w)], axis=0)
        acc = _merge((y_hy, y_ret, y_na), z, COL_HY, w_branch[l].astype(BF16), nblk * per, TR, d)
        xs, hn = _out_proj(acc, w_out[l].astype(BF16), xs, mods[l], mrow_r, g_norm2[l], nblk * per)
        u = _matmul(hn, ffn_w_in, l, nblk, tn, ffn_w_in.shape[2])
        hg = _ffn_gate(u, ffn_conv_w[l], ffn_conv_b[l], seqlen, nblk, 512)
        nxt = l if last else l + 1
        w_ffo = ffn_w_out[l].astype(BF16)
        if last:
            res = _ffn_out(hg, w_ffo, xs, mods[l], mods[l], g_final, mrow_r, nblk * per, TR, 512, True)
        else:
            res = _ffn_out(hg, w_ffo, xs, mods[l], mods[nxt], g_norm1[nxt], mrow, nblk, TM, 256, False)
        if not last:
            xs, hn = res
    return res[0].reshape(nb, seq, d)
```

```python
import functools
import math

import numpy as np
import jax
import jax.numpy as jnp
from jax import lax
from jax.experimental import pallas as pl
from jax.experimental.pallas import tpu as pltpu

F32 = jnp.float32
BF16 = jnp.bfloat16

GRID_W = 64
N_BRANCH = 3
HY_ORDER = 2
HY_EMB = 33
HY_FAST = 0.3
HY_SLOW = 1.5
HY_TARGET = 1e-2
RET_HEADS = 8
RET_DK = 64
RET_DV = 128
RET_CHUNK = 128
NA_HEADS = 8
NA_DH = 128
NA_WR = 8
NA_WC = 16
ROPE_BASE = 10000.0
EPS = 1e-6
NEG = -1e30
HY_W = 1024

_RET_QK_W, _RET_V_W, _NA_W = RET_HEADS * RET_DK, RET_HEADS * RET_DV, NA_HEADS * NA_DH
COL_RET_K = 0
COL_RET_V = COL_RET_K + _RET_QK_W
COL_NA_K = COL_RET_V + _RET_V_W
COL_NA_V = COL_NA_K + _NA_W
COL_RET_Q = COL_NA_V + _NA_W
COL_RET_G = COL_RET_Q + _RET_QK_W
COL_NA_Q = COL_RET_G + _RET_V_W
COL_HY = COL_NA_Q + _NA_W
COL_GATE = COL_HY + (HY_ORDER + 1) * HY_W

LANES = 128
SUBLANES = 8
VMEM_LIMIT = 56 * 1024 * 1024

TM = 1024


def _cparams(*sem):
    return pltpu.CompilerParams(dimension_semantics=sem, vmem_limit_bytes=VMEM_LIMIT)


def _mods_kernel(a_ref, w_ref, b_ref, o_ref):
    k = pl.program_id(1)
    a = a_ref[...]
    a = a * jax.nn.sigmoid(a)
    part = jnp.dot(a.astype(BF16), w_ref[...].astype(BF16), preferred_element_type=F32)

    @pl.when(k == 0)
    def _():
        o_ref[...] = part + b_ref[...]

    @pl.when(k > 0)
    def _():
        o_ref[...] += part


def _mods(cond, w_mod, b_mod):
    depth, d, n = w_mod.shape
    tk = LANES
    return pl.pallas_call(
        _mods_kernel,
        grid=(depth, d // tk),
        in_specs=[
            pl.BlockSpec((SUBLANES, tk), lambda l, k: (0, k)),
            pl.BlockSpec((None, tk, n), lambda l, k: (l, k, 0)),
            pl.BlockSpec((None, 1, n), lambda l, k: (l, 0, 0)),
        ],
        out_specs=pl.BlockSpec((None, SUBLANES, n), lambda l, k: (l, 0, 0)),
        out_shape=jax.ShapeDtypeStruct((depth, SUBLANES, n), F32),
        compiler_params=_cparams("parallel", "arbitrary"),
        name="mods",
    )(cond, w_mod, b_mod.reshape(depth, 1, n))


def _normmod_kernel(mrow_ref, x_ref, g_ref, sh_ref, sc_ref, o_ref):
    x = x_ref[...]
    y = x * lax.rsqrt(jnp.mean(x * x, axis=-1, keepdims=True) + EPS)
    y = y * g_ref[...]
    o_ref[...] = (y * (1.0 + sc_ref[...]) + sh_ref[...]).astype(o_ref.dtype)


def _normmod(x, g, mods, mrow, shift_idx, scale_idx, nblk):
    m, d = x.shape
    grid_spec = pltpu.PrefetchScalarGridSpec(
        num_scalar_prefetch=1,
        grid=(nblk,),
        in_specs=[
            pl.BlockSpec((TM, d), lambda i, mr: (i, 0)),
            pl.BlockSpec((1, d), lambda i, mr: (0, 0)),
            pl.BlockSpec((None, None, 1, d), lambda i, mr: (mr[i], shift_idx, 0, 0)),
            pl.BlockSpec((None, None, 1, d), lambda i, mr: (mr[i], scale_idx, 0, 0)),
        ],
        out_specs=pl.BlockSpec((TM, d), lambda i, mr: (i, 0)),
    )
    return pl.pallas_call(
        _normmod_kernel,
        grid_spec=grid_spec,
        out_shape=jax.ShapeDtypeStruct((m, d), BF16),
        compiler_params=_cparams("parallel"),
        name="normmod",
    )(mrow, x, g.reshape(1, d), mods, mods)


def _mm_kernel(a_ref, w_ref, o_ref, w_scr):
    @pl.when(pl.program_id(1) == 0)
    def _():
        w_scr[...] = w_ref[...].astype(BF16)

    o_ref[...] = jnp.dot(a_ref[...], w_scr[...], preferred_element_type=F32).astype(o_ref.dtype)


def _matmul(a, w, layer, nblk, tn, n_out, skip=None):
    m, k = a.shape
    col = (lambda j: j) if skip is None else (lambda j: j + jnp.where(j >= skip[0], skip[1], 0))
    return pl.pallas_call(
        _mm_kernel,
        grid=(n_out // tn, nblk),
        in_specs=[pl.BlockSpec((TM, k), lambda j, i: (i, 0)),
                  pl.BlockSpec((None, k, tn), lambda j, i: (layer, 0, col(j)))],
        out_specs=pl.BlockSpec((TM, tn), lambda j, i: (i, j)),
        out_shape=jax.ShapeDtypeStruct((m, n_out), BF16),
        scratch_shapes=[pltpu.VMEM((k, tn), BF16)],
        compiler_params=_cparams("arbitrary", "arbitrary"),
        name="matmul",
    )(a, w)


def _mm_nt_kernel(w_ref, a_ref, o_ref, wt_scr):
    @pl.when(pl.program_id(1) == 0)
    def _():
        wt_scr[...] = w_ref[...].T.astype(BF16)

    r = lax.dot_general(wt_scr[...], a_ref[...], (((1,), (1,)), ((), ())), preferred_element_type=F32)
    per = o_ref.shape[0]
    width = r.shape[1] // per
    for s in range(per):
        o_ref[s] = r[:, s * width:(s + 1) * width].astype(o_ref.dtype)


def _matmul_nt(a, w, layer, *, col0, n_out, blk0, nblk, nb, seq, tc):
    k = a.shape[1]
    per, sblk = max(TM // seq, 1), max(seq // TM, 1)
    return pl.pallas_call(
        _mm_nt_kernel,
        grid=(n_out // tc, nblk),
        in_specs=[pl.BlockSpec((None, k, tc), lambda j, i: (layer, 0, col0 // tc + j)),
                  pl.BlockSpec((TM, k), lambda j, i: (blk0 + i, 0))],
        out_specs=pl.BlockSpec((per, tc, TM // per), lambda j, i: (i // sblk, j, i % sblk)),
        out_shape=jax.ShapeDtypeStruct((nb, n_out, seq), BF16),
        scratch_shapes=[pltpu.VMEM((tc, k), BF16)],
        compiler_params=_cparams("arbitrary", "arbitrary"),
        name="matmul_nt",
    )(w, a)


TR = 512


def _norm_modulate(x, g, shift, scale):
    y = x * lax.rsqrt(jnp.mean(x * x, axis=-1, keepdims=True) + EPS)
    return (y * g) * (1.0 + scale) + shift


def _outproj_kernel(mrow_ref, a_ref, w_ref, x_ref, gate_ref, g_ref, sh_ref, sc_ref, x_out, h_out):
    y = jnp.dot(a_ref[...], w_ref[...], preferred_element_type=F32)
    x = x_ref[...] + gate_ref[...] * y
    x_out[...] = x
    h_out[...] = _norm_modulate(x, g_ref[...], sh_ref[...], sc_ref[...]).astype(h_out.dtype)


def _out_proj(a, w, x, mods, mrow, g_norm, nblk):
    m, k = a.shape
    d = w.shape[1]
    mod = lambda idx: pl.BlockSpec((None, None, 1, d), lambda i, mr: (mr[i], idx, 0, 0))
    row = pl.BlockSpec((TR, d), lambda i, mr: (i, 0))
    grid_spec = pltpu.PrefetchScalarGridSpec(
        num_scalar_prefetch=1,
        grid=(nblk,),
        in_specs=[pl.BlockSpec((TR, k), lambda i, mr: (i, 0)), pl.BlockSpec((k, d), lambda i, mr: (0, 0)), row,
                  mod(2), pl.BlockSpec((1, d), lambda i, mr: (0, 0)), mod(3), mod(4)],
        out_specs=[row, row],
    )
    return pl.pallas_call(
        _outproj_kernel,
        grid_spec=grid_spec,
        out_shape=[jax.ShapeDtypeStruct((m, d), F32), jax.ShapeDtypeStruct((m, d), BF16)],
        compiler_params=_cparams("parallel"),
        name="out_proj",
    )(mrow, a, w, x, mods, g_norm.reshape(1, d), mods, mods)


def _merge_kernel(y0_ref, y1_ref, y2_ref, g0_ref, g1_ref, g2_ref, w_ref, o_ref):
    acc = None
    for i, (y_ref, g_ref) in enumerate(((y0_ref, g0_ref), (y1_ref, g1_ref), (y2_ref, g2_ref))):
        t = jnp.dot(y_ref[...], w_ref[i], preferred_element_type=F32)
        t = jax.nn.sigmoid(g_ref[...].astype(F32)) * t
        acc = t if acc is None else acc + t
    o_ref[...] = acc.astype(o_ref.dtype)


def _merge(ys, z, gate_col0, w_branch, nblk, tr, tn):
    m, bw = ys[0].shape
    d = w_branch.shape[2]
    assert gate_col0 % tn == 0 and d % tn == 0
    gspec = lambda i_br: pl.BlockSpec((tr, tn), lambda i, j: (i, (gate_col0 + i_br * d) // tn + j))
    yspec = pl.BlockSpec((tr, bw), lambda i, j: (i, 0))
    return pl.pallas_call(
        _merge_kernel,
        grid=(nblk, d // tn),
        in_specs=[yspec, yspec, yspec, gspec(0), gspec(1), gspec(2),
                  pl.BlockSpec((N_BRANCH, bw, tn), lambda i, j: (0, 0, j))],
        out_specs=pl.BlockSpec((tr, tn), lambda i, j: (i, j)),
        out_shape=jax.ShapeDtypeStruct((m, d), BF16),
        compiler_params=_cparams("parallel", "arbitrary"),
        name="merge",
    )(ys[0], ys[1], ys[2], z, z, z, w_branch)


def _ffn_gate_kernel(seq_ref, a_ref, ap_ref, an_ref, b_ref, cw_ref, cb_ref, o_ref):
    i = pl.program_id(0)
    seq_m1 = seq_ref[i] - 1
    tm = a_ref.shape[0]
    row = lax.broadcasted_iota(jnp.int32, (tm, 1), 0)
    halo_p = jnp.where(((i * tm) & seq_m1) == 0, 0.0, ap_ref[SUBLANES - 1:SUBLANES, :].astype(F32))
    halo_n = jnp.where((((i + 1) * tm) & seq_m1) == 0, 0.0, an_ref[0:1, :].astype(F32))

    def finish(inner_edges):
        a = a_ref[...].astype(F32)
        prev = jnp.where(row == 0, halo_p, pltpu.roll(a, 1, axis=0))
        nxt = jnp.where(row == tm - 1, halo_n, pltpu.roll(a, tm - 1, axis=0))
        if inner_edges:
            pos = row & seq_m1
            prev = jnp.where((pos == 0) & (row > 0), 0.0, prev)
            nxt = jnp.where((pos == seq_m1) & (row < tm - 1), 0.0, nxt)
        cw = cw_ref[...]
        conv = prev * cw[0:1, :] + a * cw[1:2, :] + nxt * cw[2:3, :] + cb_ref[...]
        o_ref[...] = jax.nn.gelu(conv.astype(BF16)) * b_ref[...]

    @pl.when(seq_m1 >= tm - 1)
    def _():
        finish(False)

    @pl.when(seq_m1 < tm - 1)
    def _():
        finish(True)


def _ffn_gate(u, conv_w, conv_b, seqlen, nblk, tc):
    m, ff2 = u.shape
    ff = ff2 // 2
    nrow8 = m // SUBLANES
    r8 = TM // SUBLANES
    grid_spec = pltpu.PrefetchScalarGridSpec(
        num_scalar_prefetch=1,
        grid=(nblk, ff // tc),
        in_specs=[
            pl.BlockSpec((TM, tc), lambda i, j, s: (i, j)),
            pl.BlockSpec((SUBLANES, tc), lambda i, j, s: (jnp.maximum(i * r8 - 1, 0), j)),
            pl.BlockSpec((SUBLANES, tc), lambda i, j, s: (jnp.minimum((i + 1) * r8, nrow8 - 1), j)),
            pl.BlockSpec((TM, tc), lambda i, j, s: (i, ff // tc + j)),
            pl.BlockSpec((3, tc), lambda i, j, s: (0, j)),
            pl.BlockSpec((1, tc), lambda i, j, s: (0, j)),
        ],
        out_specs=pl.BlockSpec((TM, tc), lambda i, j, s: (i, j)),
    )
    return pl.pallas_call(
        _ffn_gate_kernel,
        grid_spec=grid_spec,
        out_shape=jax.ShapeDtypeStruct((m, ff), BF16),
        compiler_params=_cparams("parallel", "arbitrary"),
        name="ffn_gate",
    )(seqlen, u, u, u, u, conv_w, conv_b.reshape(1, ff))


def _ffn_out_kernel(mrow_ref, h_ref, w_ref, x_ref, gate_ref, g_ref, sh_ref, sc_ref, *rest, final):
    outs, xrow = rest[:-1], rest[-1]
    j = pl.program_id(1)
    y = jnp.dot(h_ref[...], w_ref[...], preferred_element_type=F32)
    xt = x_ref[...] + gate_ref[...] * y
    xrow[j] = xt
    if not final:
        outs[0][...] = xt

    @pl.when(j == pl.num_programs(1) - 1)
    def _():
        x = jnp.concatenate([xrow[t] for t in range(xrow.shape[0])], axis=1)
        if final:
            y = x * lax.rsqrt(jnp.mean(x * x, axis=-1, keepdims=True) + EPS)
            outs[0][...] = y * g_ref[...]
        else:
            outs[1][...] = _norm_modulate(x, g_ref[...], sh_ref[...], sc_ref[...]).astype(outs[1].dtype)


def _ffn_out(h, w, x, mods, mods_next, g_next, mrow, nblk, tr, tn, final):
    m, ff = h.shape
    d = w.shape[1]
    mod = lambda arr_idx: pl.BlockSpec((None, None, 1, d), lambda i, j, mr: (mr[i], arr_idx, 0, 0))
    row = pl.BlockSpec((tr, d), lambda i, j, mr: (i, 0))
    tile = pl.BlockSpec((tr, tn), lambda i, j, mr: (i, j))
    grid_spec = pltpu.PrefetchScalarGridSpec(
        num_scalar_prefetch=1,
        grid=(nblk, d // tn),
        in_specs=[
            pl.BlockSpec((tr, ff), lambda i, j, mr: (i, 0)),
            pl.BlockSpec((ff, tn), lambda i, j, mr: (0, j)),
            tile,
            pl.BlockSpec((None, None, 1, tn), lambda i, j, mr: (mr[i], 5, 0, j)),
            pl.BlockSpec((1, d), lambda i, j, mr: (0, 0)), mod(0), mod(1),
        ],
        out_specs=[row] if final else [tile, row],
        scratch_shapes=[pltpu.VMEM((d // tn, tr, tn), F32)],
    )
    out_shape = [jax.ShapeDtypeStruct((m, d), F32)] + ([] if final else [jax.ShapeDtypeStruct((m, d), BF16)])
    return pl.pallas_call(
        functools.partial(_ffn_out_kernel, final=final),
        grid_spec=grid_spec,
        out_shape=out_shape,
        compiler_params=_cparams("parallel", "arbitrary"),
        name="ffn_out",
    )(mrow, h, w, x, mods, g_next.reshape(1, d), mods_next, mods_next)


def _rope_pair(x, cos, sin):
    lane = lax.broadcasted_iota(jnp.int32, x.shape, 1)
    first = lax.rem(lane, 32) < 16
    partner = jnp.where(first, pltpu.roll(x, LANES - 16, axis=1), pltpu.roll(x, 16, axis=1))
    return x * cos + partner * sin


def _ret_kernel(lg_ref, q_ref, k_ref, v_ref, g_ref, cos_ref, sin_ref, s0_ref, *rest, rope, has_prev):
    o_ref, sfin_ref, qr_scr, kr_scr, kv_scr = rest[1:] if has_prev else rest
    hp = pl.program_id(1)
    seq = q_ref.shape[0]
    c = RET_CHUNK
    nchunk = seq // c
    ks = RET_DK ** -0.5
    q = q_ref[...].astype(F32)
    k = k_ref[...].astype(F32)
    if rope:
        q = _rope_pair(q, cos_ref[...], sin_ref[...])
        k = _rope_pair(k, cos_ref[...], sin_ref[...])
    qr_scr[...] = q
    kr_scr[...] = k * ks

    pos_r = lax.broadcasted_iota(jnp.int32, (c, 1), 0).astype(F32)
    rel = (lax.broadcasted_iota(jnp.int32, (c, c), 0) - lax.broadcasted_iota(jnp.int32, (c, c), 1)).astype(F32)
    lane = lax.broadcasted_iota(jnp.int32, (1, LANES), 1)

    heads = []
    for hh in range(2):
        lgf = lg_ref[0, 2 * hp + hh]
        lgb = lg_ref[1, 2 * hp + hh]
        heads.append(dict(
            hm=(lane // RET_DK == hh).astype(F32),
            zeta_f=jnp.exp(lgf * (c - 1 - pos_r)), zeta_b=jnp.exp(lgb * pos_r),
            xi_f=jnp.exp(lgf * (pos_r + 1.0)), xi_b=jnp.exp(lgb * (c - pos_r)),
            dmat=jnp.where(rel >= 0, jnp.exp(lgf * jnp.maximum(rel, 0.0)), jnp.exp(lgb * jnp.maximum(-rel, 0.0))),
            gf=jnp.exp(lgf * c), gb=jnp.exp(lgb * c),
            vcols=slice(hh * RET_DV, (hh + 1) * RET_DV)))

    def kv_body(n, _):
        rows = pl.ds(pl.multiple_of(n * c, c), c)
        kr = kr_scr[rows, :]
        for hh, hd in enumerate(heads):
            kh = kr * hd["hm"]
            kz = jnp.concatenate([kh * hd["zeta_f"], kh * hd["zeta_b"]], axis=1).astype(BF16)
            kv_scr[hh, n] = lax.dot_general(kz, v_ref[rows, hd["vcols"]], (((0,), (0,)), ((), ())),
                                            preferred_element_type=F32)
        return 0

    lax.fori_loop(0, nchunk, kv_body, 0, unroll=min(8, nchunk))

    def scan(j, carry):
        nf, nbk = j, nchunk - 1 - j
        out = []
        for hh, hd in enumerate(heads):
            sf, sb = carry[2 * hh], carry[2 * hh + 1]
            tf = kv_scr[hh, nf, 0:LANES, :]
            kv_scr[hh, nf, 0:LANES, :] = sf
            tb = kv_scr[hh, nbk, LANES:2 * LANES, :]
            kv_scr[hh, nbk, LANES:2 * LANES, :] = sb
            out += [hd["gf"] * sf + tf, hd["gb"] * sb + tb]
        return tuple(out)

    fin = lax.fori_loop(0, nchunk, scan, (s0_ref[0, 0], s0_ref[0, 1], s0_ref[1, 0], s0_ref[1, 1]))
    for hh in range(2):
        sfin_ref[hh, 0] = fin[2 * hh]
        sfin_ref[hh, 1] = fin[2 * hh + 1]

    def out_body(n, _):
        rows = pl.ds(pl.multiple_of(n * c, c), c)
        qr = qr_scr[rows, :]
        kb = kr_scr[rows, :].astype(BF16)
        for hh, hd in enumerate(heads):
            qm = qr * hd["hm"]
            a = lax.dot_general(qm.astype(BF16), kb, (((1,), (1,)), ((), ())),
                                preferred_element_type=F32) * hd["dmat"]
            o = jnp.dot(a.astype(BF16), v_ref[rows, hd["vcols"]], preferred_element_type=F32)
            qx = jnp.concatenate([qm * hd["xi_f"], qm * hd["xi_b"]], axis=1).astype(BF16)
            o = o + jnp.dot(qx, kv_scr[hh, n].astype(BF16), preferred_element_type=F32)
            o = o * lax.rsqrt(jnp.mean(o * o, axis=-1, keepdims=True) + EPS)
            g = g_ref[rows, hd["vcols"]].astype(F32)
            o_ref[rows, hd["vcols"]] = (g * jax.nn.sigmoid(g) * o).astype(o_ref.dtype)
        return 0

    lax.fori_loop(0, nchunk, out_body, 0, unroll=min(8, nchunk))


def _retention(z, log_decay, cos, sin, s0, *, nb, seq, row0, rope, out_rows, y_prev=None):
    rb0 = row0 // seq
    hpairs = RET_HEADS // 2
    kcol, qcol = COL_RET_K // LANES, COL_RET_Q // LANES
    vcol, gcol = COL_RET_V // (2 * RET_DV), COL_RET_G // (2 * RET_DV)
    has_prev = y_prev is not None
    kernel = functools.partial(_ret_kernel, rope=rope, has_prev=has_prev)
    in_specs = [
        pl.BlockSpec(memory_space=pltpu.SMEM),
        pl.BlockSpec((seq, LANES), lambda b, p: (rb0 + b, qcol + p)),
        pl.BlockSpec((seq, LANES), lambda b, p: (rb0 + b, kcol + p)),
        pl.BlockSpec((seq, 2 * RET_DV), lambda b, p: (rb0 + b, vcol + p)),
        pl.BlockSpec((seq, 2 * RET_DV), lambda b, p: (rb0 + b, gcol + p)),
        pl.BlockSpec((seq, LANES), lambda b, p: (0, 0)),
        pl.BlockSpec((seq, LANES), lambda b, p: (0, 0)),
        pl.BlockSpec((None, 2, 2, LANES, RET_DV), lambda b, p: (b, p, 0, 0, 0)),
    ]
    args = [log_decay, z, z, z, z, cos, sin, s0]
    if has_prev:
        in_specs.append(pl.BlockSpec(memory_space=pl.ANY))
        args.append(y_prev)
    return pl.pallas_call(
        kernel,
        grid=(nb, hpairs),
        in_specs=in_specs,
        out_specs=[
            pl.BlockSpec((seq, 2 * RET_DV), lambda b, p: (rb0 + b, p)),
            pl.BlockSpec((None, 2, 2, LANES, RET_DV), lambda b, p: (b, p, 0, 0, 0)),
        ],
        out_shape=[
            jax.ShapeDtypeStruct((out_rows, RET_HEADS * RET_DV), BF16),
            jax.ShapeDtypeStruct((nb, RET_HEADS, 2, LANES, RET_DV), F32),
        ],
        scratch_shapes=[
            pltpu.VMEM((seq, LANES), F32),
            pltpu.VMEM((seq, LANES), F32),
            pltpu.VMEM((2, seq // RET_CHUNK, 2 * LANES, RET_DV), F32),
        ],
        input_output_aliases={len(args) - 1: 0} if has_prev else {},
        compiler_params=_cparams("parallel", "arbitrary"),
        name="retention_rope" if rope else "retention",
    )(*args)


NA_GROUP = 4
NA_KROWS = 12
NA_OVERLAP = 4


def _na_kernel(rpb_ref, q_ref, k_ref, v_ref, kc_ref, vc_ref, o_ref, bias_scr):
    seq = q_ref.shape[0]
    nrow = seq // GRID_W
    ndr, ndc = 2 * NA_WR - 1, 2 * NA_WC - 1
    scale = NA_DH ** -0.5
    nt = (((1,), (1,)), ((), ()))
    h = pl.program_id(0)
    lane = lax.broadcasted_iota(jnp.int32, (GRID_W, LANES), 1)

    @pl.when(pl.program_id(1) == 0)
    def _():
        qi = lax.broadcasted_iota(jnp.int32, (GRID_W, LANES), 0)
        ki = lane & (GRID_W - 1)
        dc = ki - qi + (NA_WC - 1)
        cs = jnp.clip(qi - NA_WC // 2, 0, GRID_W - NA_WC)
        colmask = (ki >= cs) & (ki < cs + NA_WC)

        def toeplitz(d):
            t = jnp.zeros((GRID_W, LANES), F32)
            for j in range(ndc):
                t = jnp.where(dc == j, rpb_ref[h * (ndr * ndc) + d * ndc + j], t)
            return t

        prev = jnp.zeros((GRID_W, LANES), F32)
        for i in range(ndr + 1):
            nxt = toeplitz(i) if i < ndr else jnp.zeros((GRID_W, LANES), F32)
            bias_scr[i] = jnp.where(colmask, jnp.where(lane < GRID_W, prev, nxt), NEG)
            prev = nxt

    kc = kc_ref[...]
    vc = vc_ref[...]
    kwin = NA_KROWS * GRID_W

    half = NA_WR // 2
    ntile = kwin // LANES

    def group(r0, interior):
        us = r0 - half if interior else min(max(r0 - half, 0), nrow - NA_KROWS)
        if isinstance(r0, int):
            qrows = pl.ds(r0 * GRID_W, NA_GROUP * GRID_W)
            krows = pl.ds(us * GRID_W, kwin)
        else:
            qrows = pl.ds(pl.multiple_of(r0 * GRID_W, NA_GROUP * GRID_W), NA_GROUP * GRID_W)
            krows = pl.ds(pl.multiple_of(us * GRID_W, GRID_W), kwin)
        q = q_ref[qrows, :]
        s = lax.dot_general(q, k_ref[krows, :], nt, preferred_element_type=F32)
        sc = lax.dot_general(q, kc, nt, preferred_element_type=F32) * scale
        e_rows, ec_rows, den_rows = [], [], []
        for u in range(NA_GROUP):
            qr = slice(u * GRID_W, (u + 1) * GRID_W)
            tiles = []
            for m in range(ntile):
                if interior:
                    off, idx = 2 * m - u, 2 * m - u + half
                else:
                    r = r0 + u
                    off = us + 2 * m - min(max(r - half, 0), nrow - NA_WR)
                    idx = min(max(us + 2 * m - r + NA_WR, 0), ndr)
                ok_lo, ok_hi = 0 <= off < NA_WR, 0 <= off + 1 < NA_WR
                if not (ok_lo or ok_hi):
                    tiles.append(None)
                    continue
                tab = bias_scr[idx]
                if not (ok_lo and ok_hi):
                    tab = tab + jnp.where(lane < GRID_W, 0.0 if ok_lo else NEG, 0.0 if ok_hi else NEG)
                st = s[qr, m * LANES:(m + 1) * LANES]
                tiles.append(jnp.where(tab > 0.5 * NEG, st * scale + tab, NEG))
            live = [t for t in tiles if t is not None]
            tmax = live[0]
            for t in live[1:]:
                tmax = jnp.maximum(tmax, t)
            scu = sc[qr, :]
            mx = jnp.maximum(jnp.max(tmax, axis=-1, keepdims=True), jnp.max(scu, axis=-1, keepdims=True))
            etiles = [None if t is None else jnp.exp(t - mx) for t in tiles]
            ecu = jnp.exp(scu - mx)
            esum = None
            for t in etiles:
                if t is not None:
                    esum = t if esum is None else esum + t
            den_rows.append(jnp.sum(esum, axis=-1, keepdims=True) + jnp.sum(ecu, axis=-1, keepdims=True))
            zero = jnp.zeros((GRID_W, LANES), BF16)
            e_rows.append(jnp.concatenate([zero if t is None else t.astype(BF16) for t in etiles], axis=1))
            ec_rows.append(ecu.astype(BF16))
        o = jnp.dot(jnp.concatenate(e_rows, axis=0), v_ref[krows, :], preferred_element_type=F32)
        o = o + jnp.dot(jnp.concatenate(ec_rows, axis=0), vc, preferred_element_type=F32)
        o_ref[qrows, :] = (o / jnp.concatenate(den_rows, axis=0)).astype(o_ref.dtype)

    ngroup = nrow // NA_GROUP
    g_lo = -(-half // NA_GROUP)
    g_hi = min((nrow - NA_KROWS + half) // NA_GROUP, (nrow - NA_WR + half - NA_GROUP + 1) // NA_GROUP)
    for g in range(g_lo):
        group(g * NA_GROUP, False)

    def body(i, _):
        for t in range(NA_OVERLAP):
            group((g_lo + NA_OVERLAP * i + t) * NA_GROUP, True)
        return 0

    nloop = (g_hi + 1 - g_lo) // NA_OVERLAP
    lax.fori_loop(0, nloop, body, 0)
    for g in range(g_lo + NA_OVERLAP * nloop, g_hi + 1):
        group(g * NA_GROUP, True)
    for g in range(g_hi + 1, ngroup):
        group(g * NA_GROUP, False)


def _na_attention(z, rpb, *, nb, seq, cseq, crow0, out_rows):
    kcol, vcol, qcol = COL_NA_K // NA_DH, COL_NA_V // NA_DH, COL_NA_Q // NA_DH
    crb0 = crow0 // cseq
    nrow = seq // GRID_W
    assert nrow >= NA_KROWS and nrow % NA_GROUP == 0 and GRID_W * 2 == LANES
    assert NA_KROWS % 2 == 0 and NA_KROWS >= NA_WR + NA_GROUP - 1
    return pl.pallas_call(
        _na_kernel,
        grid=(NA_HEADS, nb),
        in_specs=[
            pl.BlockSpec(memory_space=pltpu.SMEM),
            pl.BlockSpec((seq, NA_DH), lambda h, b: (b, qcol + h)),
            pl.BlockSpec((seq, NA_DH), lambda h, b: (b, kcol + h)),
            pl.BlockSpec((seq, NA_DH), lambda h, b: (b, vcol + h)),
            pl.BlockSpec((cseq, NA_DH), lambda h, b: (crb0 + b, kcol + h)),
            pl.BlockSpec((cseq, NA_DH), lambda h, b: (crb0 + b, vcol + h)),
        ],
        out_specs=pl.BlockSpec((seq, NA_DH), lambda h, b: (b, h)),
        out_shape=jax.ShapeDtypeStruct((out_rows, NA_HEADS * NA_DH), BF16),
        scratch_shapes=[pltpu.VMEM((2 * NA_WR, GRID_W, LANES), F32)],
        compiler_params=_cparams("arbitrary", "arbitrary"),
        name="na_attention",
    )(rpb.reshape(-1), z, z, z, z, z)


def _dense_attn_kernel(q_ref, k_ref, v_ref, yprev_ref, o_ref):
    s = lax.dot_general(q_ref[...], k_ref[...], (((1,), (1,)), ((), ())), preferred_element_type=F32) * NA_DH ** -0.5
    e = jnp.exp(s - jnp.max(s, axis=-1, keepdims=True))
    o = jnp.dot(e.astype(BF16), v_ref[...], preferred_element_type=F32)
    o_ref[...] = (o / jnp.sum(e, axis=-1, keepdims=True)).astype(o_ref.dtype)


def _dense_attention(z, y_prev, *, nb, cseq, crow0):
    kcol, vcol, qcol = COL_NA_K // NA_DH, COL_NA_V // NA_DH, COL_NA_Q // NA_DH
    crb0 = crow0 // cseq
    return pl.pallas_call(
        _dense_attn_kernel,
        grid=(nb, NA_HEADS),
        in_specs=[
            pl.BlockSpec((cseq, NA_DH), lambda b, h: (crb0 + b, qcol + h)),
            pl.BlockSpec((cseq, NA_DH), lambda b, h: (crb0 + b, kcol + h)),
            pl.BlockSpec((cseq, NA_DH), lambda b, h: (crb0 + b, vcol + h)),
            pl.BlockSpec(memory_space=pl.ANY),
        ],
        out_specs=pl.BlockSpec((cseq, NA_DH), lambda b, h: (crb0 + b, h)),
        out_shape=jax.ShapeDtypeStruct(y_prev.shape, BF16),
        input_output_aliases={3: 0},
        compiler_params=_cparams("parallel", "arbitrary"),
        name="dense_attention",
    )(z, z, z, y_prev)


HY_FEAT_ROWS = 64
HIGHEST = lax.Precision.HIGHEST


def _hy_features(seq):
    t = np.linspace(0.0, 1.0, seq)
    bands = (HY_EMB - 1) // 2
    w = 2.0 * math.pi * np.arange(seq) / seq
    fr = np.linspace(1e-4, bands - 1, bands)
    ang = fr[None] * w[:, None]
    feat = np.concatenate([t[:, None], np.cos(ang), -np.sin(ang)], axis=-1)
    src = np.concatenate([np.arange(seq), np.zeros(1, np.int64), np.arange(seq - 1, 0, -1)])
    feat2 = np.zeros((HY_FEAT_ROWS, 2 * seq), np.float32)
    feat2[:HY_EMB] = feat[src].T
    tt = t[src][None].astype(np.float32)
    mask = np.ones((1, 2 * seq), np.float32)
    mask[0, seq] = 0.0
    return feat2, tt, mask


def _hy_mlp_kernel(feat_ref, w1_ref, b1_ref, w2_ref, b2_ref, fq_ref, o_ref):
    fq = fq_ref[...]
    h = jnp.dot(w1_ref[...], feat_ref[...], preferred_element_type=F32, precision=HIGHEST)
    h = jnp.sin(fq * (h + b1_ref[...]))
    h = jnp.dot(w2_ref[...], h, preferred_element_type=F32, precision=HIGHEST)
    o_ref[...] = jnp.sin(fq * (h + b2_ref[...]))


def _hy_mlp(feat2, w1t, b1, w2t, b2, fq):
    depth, hid, _ = w1t.shape
    n = feat2.shape[1]
    col = lambda a: a.reshape(depth, hid, 1)
    wspec = lambda k: pl.BlockSpec((None, hid, k), lambda l: (l, 0, 0))
    return pl.pallas_call(
        _hy_mlp_kernel,
        grid=(depth,),
        in_specs=[pl.BlockSpec((HY_FEAT_ROWS, n), lambda l: (0, 0)), wspec(HY_FEAT_ROWS), wspec(1), wspec(hid),
                  wspec(1), wspec(1)],
        out_specs=pl.BlockSpec((None, hid, n), lambda l: (l, 0, 0)),
        out_shape=jax.ShapeDtypeStruct((depth, hid, n), F32),
        compiler_params=_cparams("parallel"),
        name="hyena_filter_mlp",
    )(feat2, w1t, col(b1), w2t, col(b2), col(fq))


def _hy_filter_kernel(h_ref, w3_ref, dl_ref, tt_ref, mask_ref, o_ref):
    seq = h_ref.shape[1] // 2
    kf = jnp.dot(w3_ref[0], h_ref[:, :seq], preferred_element_type=F32, precision=HIGHEST)
    kb = jnp.dot(w3_ref[1], h_ref[:, seq:], preferred_element_type=F32, precision=HIGHEST)
    k = jnp.concatenate([kf, kb], axis=1) * (jnp.exp(-tt_ref[...] * dl_ref[...]) * mask_ref[...])
    o_ref[...] = (k * lax.rsqrt(jnp.sum(k * k, axis=1, keepdims=True) + EPS)).astype(o_ref.dtype)


def _hy_filter(h2, w3t, deltas, tt, mask, cb):
    depth, hid, n = h2.shape
    c = w3t.shape[3]
    return pl.pallas_call(
        _hy_filter_kernel,
        grid=(depth, HY_ORDER, c // cb),
        in_specs=[
            pl.BlockSpec((None, hid, n), lambda l, o, j: (l, 0, 0)),
            pl.BlockSpec((None, None, 2, cb, hid), lambda l, o, j: (l, o, 0, j, 0)),
            pl.BlockSpec((cb, 1), lambda l, o, j: (j, 0)),
            pl.BlockSpec((1, n), lambda l, o, j: (0, 0)),
            pl.BlockSpec((1, n), lambda l, o, j: (0, 0)),
        ],
        out_specs=pl.BlockSpec((None, None, cb, n), lambda l, o, j: (l, o, j, 0)),
        out_shape=jax.ShapeDtypeStruct((depth, HY_ORDER, c, n), BF16),
        compiler_params=_cparams("parallel", "parallel", "arbitrary"),
        name="hyena_filter",
    )(h2, w3t, deltas, tt, mask)


FFT_NO = 64
FFT_NI = 128


def _fft_tables():
    n = FFT_NO * FFT_NI
    a = np.arange(FFT_NO)
    fo = np.exp(-2j * np.pi * np.outer(a, a) / FFT_NO)
    i = np.arange(FFT_NI)
    ci = np.exp(-2j * np.pi * np.outer(i, i) / FFT_NI)
    tw = np.exp(-2j * np.pi * np.outer(a, i) / n)
    half = FFT_NO // 2
    f32 = lambda x: np.ascontiguousarray(x, dtype=np.float32)
    g1_real = f32(np.concatenate([fo.real, fo.imag], axis=0))
    g1 = f32(np.block([[fo.real[:, :half], -fo.imag[:, :half]], [fo.imag[:, :half], fo.real[:, :half]]]))
    w2 = f32(np.block([[ci.real, ci.imag], [-ci.imag, ci.real]]))
    w2i = f32(np.block([[ci.real, -ci.imag], [ci.imag, ci.real]]))
    g4 = f32(np.block([[fo.real[:half], fo.imag[:half]], [-fo.imag[:half], fo.real[:half]]]) / n)
    return dict(g1_real=g1_real, g1=g1, w2=w2, w2i=w2i, g4=g4, twr=f32(tw.real), twi=f32(tw.imag))


HY_CHUNK = 16
HY_SPEC_CHUNK = 32


def _fft_stage1(g1, y2, twr, twi):
    nch = y2.shape[1] // FFT_NI
    o1 = jnp.dot(g1, y2, preferred_element_type=F32)
    yr, yi = o1[:FFT_NO].astype(BF16), o1[FFT_NO:].astype(BF16)
    ar = yr * twr - yi * twi
    ai = yr * twi + yi * twr
    lanes = lambda ci: slice(ci * FFT_NI, (ci + 1) * FFT_NI)
    return jnp.concatenate([jnp.concatenate([ar[:, lanes(ci)], ai[:, lanes(ci)]], axis=1)
                            for ci in range(nch)], axis=0)


def _hy_spectrum_kernel(k_ref, g1_ref, w2_ref, twr_ref, twi_ref, o_ref):
    cb = k_ref.shape[0]
    ch = HY_SPEC_CHUNK
    g1 = g1_ref[...]
    w2 = w2_ref[...]
    twr = twr_ref[...]
    twi = twi_ref[...]

    def body(j, _):
        chans = pl.ds(pl.multiple_of(j * ch, ch), ch)
        ks = k_ref[chans]
        k2 = jnp.concatenate([ks[ci] for ci in range(ch)], axis=1).astype(BF16)
        z = jnp.dot(_fft_stage1(g1, k2, twr, twi), w2, preferred_element_type=F32)
        o_ref[chans] = z.reshape(ch, FFT_NO, 2 * FFT_NI).astype(o_ref.dtype)
        return 0

    lax.fori_loop(0, cb // ch, body, 0)


def _tiled_twiddles(tabs, nch):
    tile = lambda a: jnp.asarray(np.tile(a, (1, nch)), BF16)
    return tile(tabs["twr"]), tile(tabs["twi"])


def _hy_spectrum(kfilt, tabs, cb):
    g, c = kfilt.shape[:2]
    const = lambda a: pl.BlockSpec(a.shape, lambda i, j: (0,) * a.ndim)
    g1 = jnp.asarray(tabs["g1_real"], BF16)
    w2 = jnp.asarray(tabs["w2"], BF16)
    twr, twi = _tiled_twiddles(tabs, HY_SPEC_CHUNK)
    return pl.pallas_call(
        _hy_spectrum_kernel,
        grid=(g, c // cb),
        in_specs=[pl.BlockSpec((None, cb, FFT_NO, FFT_NI), lambda i, j: (i, j, 0, 0)),
                  const(g1), const(w2), const(twr), const(twi)],
        out_specs=pl.BlockSpec((None, cb, FFT_NO, 2 * FFT_NI), lambda i, j: (i, j, 0, 0)),
        out_shape=jax.ShapeDtypeStruct((g, c, FFT_NO, 2 * FFT_NI), BF16),
        compiler_params=_cparams("parallel", "arbitrary"),
        name="hyena_spectrum",
    )(kfilt, g1, w2, twr, twi)


def _shift_conv3(x, w0, w1, w2, b):
    nrow, nlane = x.shape
    row = lax.broadcasted_iota(jnp.int32, x.shape, 0)
    lane = lax.broadcasted_iota(jnp.int32, x.shape, 1)
    r = pltpu.roll(x, 1, axis=1)
    prev = jnp.where(lane == 0, pltpu.roll(r, 1, axis=0), r)
    prev = jnp.where((lane == 0) & (row == 0), 0.0, prev)
    r = pltpu.roll(x, nlane - 1, axis=1)
    nxt = jnp.where(lane == nlane - 1, pltpu.roll(r, nrow - 1, axis=0), r)
    nxt = jnp.where((lane == nlane - 1) & (row == nrow - 1), 0.0, nxt)
    return prev * w0 + x * w1 + nxt * w2 + b


def _hyena_kernel(x_ref, kf_ref, pm_ref, g1_ref, w2_ref, w2i_ref, g4_ref, twr_ref, twi_ref, o_ref):
    cb = x_ref.shape[2]
    half = FFT_NO // 2
    ch = HY_CHUNK
    g1 = g1_ref[...]
    g4 = g4_ref[...]
    w2 = w2_ref[...]
    w2i = w2i_ref[...]
    twr = twr_ref[...]
    twi = twi_ref[...]
    lanes = lambda ci: slice(ci * FFT_NI, (ci + 1) * FFT_NI)
    krows = lambda ci: slice(ci * FFT_NO, (ci + 1) * FFT_NO)

    def body(j, _):
        c0 = pl.multiple_of(j * ch, ch)
        pm = pm_ref[:, pl.ds(c0, ch)]
        conv = []
        for part in range(3):
            halves = []
            for bb in range(2):
                xs = x_ref[bb, part, pl.ds(c0, ch)].astype(F32)
                halves.append(jnp.concatenate(
                    [_shift_conv3(xs[ci], pm[part, ci], pm[3 + part, ci], pm[6 + part, ci], pm[9 + part, ci])
                     for ci in range(ch)], axis=1))
            conv.append(jnp.concatenate(halves, axis=0))
        y = conv[2]
        for order in range(HY_ORDER):
            z = jnp.dot(_fft_stage1(g1, y.astype(BF16), twr, twi), w2, preferred_element_type=F32)
            kf = kf_ref[order, pl.ds(c0, ch)].reshape(ch * FFT_NO, 2 * FFT_NI)
            zr, zi = z[:, :FFT_NI].astype(BF16), z[:, FFT_NI:].astype(BF16)
            kr, ki = kf[:, :FFT_NI], kf[:, FFT_NI:]
            p = jnp.concatenate([zr * kr - zi * ki, zr * ki + zi * kr], axis=1)
            q = jnp.dot(p, w2i, preferred_element_type=F32).astype(BF16)
            qr = jnp.concatenate([q[krows(ci), :FFT_NI] for ci in range(ch)], axis=1)
            qi = jnp.concatenate([q[krows(ci), FFT_NI:] for ci in range(ch)], axis=1)
            qs = jnp.concatenate([qr * twr + qi * twi, qi * twr - qr * twi], axis=0)
            cv = jnp.dot(g4, qs, preferred_element_type=F32)
            bias = jnp.concatenate([pm[12 + order, ci] for ci in range(ch)], axis=1)
            y = conv[order] * (cv + bias * y)
        for ci in range(ch):
            o_ref[0, c0 + ci] = y[:half, lanes(ci)].astype(o_ref.dtype)
            o_ref[1, c0 + ci] = y[half:, lanes(ci)].astype(o_ref.dtype)
        return 0

    lax.fori_loop(0, cb // ch, body, 0)


def _hyena(xt, kf, layer, pm, tabs, cb):
    nb, _, c, half, _ = xt.shape
    const = lambda a: pl.BlockSpec(a.shape, lambda j, p: (0,) * a.ndim)
    bf = lambda name: jnp.asarray(tabs[name], BF16)
    g1, w2, w2i, g4 = bf("g1"), bf("w2"), bf("w2i"), bf("g4")
    twr, twi = _tiled_twiddles(tabs, HY_CHUNK)
    return pl.pallas_call(
        _hyena_kernel,
        grid=(c // cb, nb // 2),
        in_specs=[
            pl.BlockSpec((2, 3, cb, half, FFT_NI), lambda j, p: (p, 0, j, 0, 0)),
            pl.BlockSpec((None, HY_ORDER, cb, FFT_NO, 2 * FFT_NI), lambda j, p: (layer, 0, j, 0, 0)),
            pl.BlockSpec((14, cb, 1, FFT_NI), lambda j, p: (0, j, 0, 0)),
            const(g1), const(w2), const(w2i), const(g4), const(twr), const(twi),
        ],
        out_specs=pl.BlockSpec((2, cb, half, FFT_NI), lambda j, p: (p, j, 0, 0)),
        out_shape=jax.ShapeDtypeStruct((nb, c, half, FFT_NI), BF16),
        compiler_params=_cparams("parallel", "arbitrary"),
        name="hyena",
    )(xt, kf, pm, g1, w2, w2i, g4, twr, twi)


def _dft_tables(seq):
    n = 2 * seq
    wmat = np.exp(-2j * np.pi * np.outer(np.arange(n), np.arange(n)) / n)
    f32 = lambda x: np.ascontiguousarray(x, dtype=np.float32)
    fwd_real = f32(np.concatenate([wmat.real, wmat.imag], axis=1))
    ws = wmat[:seq]
    fwd = f32(np.block([[ws.real, ws.imag], [-ws.imag, ws.real]]))
    wi = np.conj(wmat)[:, :seq] / n
    inv = f32(np.block([[wi.real, wi.imag], [-wi.imag, wi.real]]))
    return dict(fwd_real=fwd_real, fwd=fwd, inv=inv)


def _rowdft_kernel(x_ref, w_ref, o_ref):
    o_ref[...] = jnp.dot(x_ref[...].astype(BF16), w_ref[...], preferred_element_type=F32)


def _rowdft(x, w, tr):
    m, k = x.shape
    n = w.shape[1]
    return pl.pallas_call(
        _rowdft_kernel,
        grid=(m // tr,),
        in_specs=[pl.BlockSpec((tr, k), lambda i: (i, 0)), pl.BlockSpec((k, n), lambda i: (0, 0))],
        out_specs=pl.BlockSpec((tr, n), lambda i: (i, 0)),
        out_shape=jax.ShapeDtypeStruct((m, n), F32),
        compiler_params=_cparams("parallel"),
        name="row_dft",
    )(x, w)


def _lane_conv3(x, w0, w1, w2, b):
    seq = x.shape[1]
    lane = lax.broadcasted_iota(jnp.int32, x.shape, 1)
    prev = jnp.where(lane == 0, 0.0, pltpu.roll(x, 1, axis=1))
    nxt = jnp.where(lane == seq - 1, 0.0, pltpu.roll(x, seq - 1, axis=1))
    return prev * w0 + x * w1 + nxt * w2 + b


def _hyena_ctx_kernel(x_ref, kf_ref, pm_ref, fwd_ref, inv_ref, o_ref):
    seq = x_ref.shape[3]
    n = 2 * seq
    conv = [[_lane_conv3(x_ref[bb, part].astype(F32), pm_ref[part], pm_ref[3 + part], pm_ref[6 + part],
                         pm_ref[9 + part]) for bb in range(2)] for part in range(3)]
    ya, yb = conv[2]
    for order in range(HY_ORDER):
        z = jnp.dot(jnp.concatenate([ya, yb], axis=1).astype(BF16), fwd_ref[...], preferred_element_type=F32)
        kf = kf_ref[order]
        zr, zi, kr, ki = z[:, :n], z[:, n:], kf[:, :n], kf[:, n:]
        p = jnp.concatenate([zr * kr - zi * ki, zr * ki + zi * kr], axis=1).astype(BF16)
        cv = jnp.dot(p, inv_ref[...], preferred_element_type=F32)
        bias = pm_ref[12 + order]
        ya = conv[order][0] * (cv[:, :seq] + bias * ya)
        yb = conv[order][1] * (cv[:, seq:] + bias * yb)
    o_ref[0] = ya.astype(o_ref.dtype)
    o_ref[1] = yb.astype(o_ref.dtype)


def _hyena_ctx(xt, kf, pm, tabs, cb):
    nb, _, c, seq = xt.shape
    fwd, inv = jnp.asarray(tabs["fwd"], BF16), jnp.asarray(tabs["inv"], BF16)
    const = lambda a: pl.BlockSpec(a.shape, lambda j, p: (0,) * a.ndim)
    return pl.pallas_call(
        _hyena_ctx_kernel,
        grid=(c // cb, nb // 2),
        in_specs=[
            pl.BlockSpec((2, 3, cb, seq), lambda j, p: (p, 0, j, 0)),
            pl.BlockSpec((HY_ORDER, cb, 4 * seq), lambda j, p: (0, j, 0)),
            pl.BlockSpec((14, cb, 1), lambda j, p: (0, j, 0)),
            const(fwd), const(inv),
        ],
        out_specs=pl.BlockSpec((2, cb, seq), lambda j, p: (p, j, 0)),
        out_shape=jax.ShapeDtypeStruct((nb, c, seq), BF16),
        compiler_params=_cparams("parallel", "arbitrary"),
        name="hyena_ctx",
    )(xt, kf, pm, fwd, inv)


def _rope_tables(seq):
    t = np.arange(seq)
    pos = np.stack([t // GRID_W, t % GRID_W], axis=1).astype(np.float64)
    n = RET_DK // 4
    inv = ROPE_BASE ** (-np.arange(n, dtype=np.float64) / n)
    lane = np.arange(LANES) % RET_DK
    ang = pos[:, lane // (2 * n)] * inv[lane % n][None]
    sign = np.where(lane % (2 * n) < n, -1.0, 1.0)
    return np.cos(ang).astype(np.float32), (np.sin(ang) * sign[None]).astype(np.float32)


def _hy_filters(seq, hy_f_w1, hy_f_b1, hy_f_w2, hy_f_b2, hy_f_w3, hy_f_freq):
    depth, _, hid = hy_f_w1.shape
    c = hy_f_w3.shape[2] // (2 * HY_ORDER)
    feat2, tt, mask = _hy_features(seq)
    w1t = jnp.pad(hy_f_w1.transpose(0, 2, 1), ((0, 0), (0, 0), (0, HY_FEAT_ROWS - HY_EMB)))
    h2 = _hy_mlp(jnp.asarray(feat2), w1t, hy_f_b1, hy_f_w2.transpose(0, 2, 1), hy_f_b2, hy_f_freq)
    w3t = hy_f_w3.reshape(depth, hid, HY_ORDER, 2, c).transpose(0, 2, 3, 4, 1)
    deltas = np.abs(np.linspace(math.log(HY_TARGET) / HY_FAST, math.log(HY_TARGET) / HY_SLOW, c))
    deltas = jnp.asarray(deltas.astype(np.float32).reshape(c, 1))
    cb = min(c, (1024 * 1024) // (2 * seq))
    return _hy_filter(h2, w3t, deltas, jnp.asarray(tt), jnp.asarray(mask), cb)


def _hy_params(hy_conv_w, hy_conv_b, hy_bias):
    c = hy_bias.shape[1]
    return jnp.concatenate([hy_conv_w.reshape(9, c), hy_conv_b.reshape(3, c), hy_bias], axis=0)


def kernel(x, c, ctx, c_ctx, w_mod, b_mod, g_norm1, g_norm2, w_in, hy_conv_w, hy_conv_b, hy_f_w1, hy_f_b1, hy_f_w2, hy_f_b2, hy_f_w3, hy_f_freq, hy_bias, ret_log_decay, na_rpb, w_branch, w_out, ffn_w_in, ffn_conv_w, ffn_conv_b, ffn_w_out, g_final):
    nb, seq, d = x.shape
    cseq = ctx.shape[1]
    depth = w_mod.shape[0]
    hy_w = hy_bias.shape[2]
    t_lat, t_ctx = nb * seq, nb * cseq
    assert seq % TM == 0 and t_ctx % TM == 0 and TM % cseq == 0 and nb % 2 == 0 and nb < SUBLANES
    assert 2 * seq == FFT_NO * FFT_NI and seq % GRID_W == 0 and cseq & (cseq - 1) == 0
    blk_lat, blk_all = t_lat // TM, (t_lat + t_ctx) // TM
    mrow = jnp.asarray(np.concatenate([np.repeat(np.arange(nb), seq // TM), np.full(t_ctx // TM, nb)]), jnp.int32)
    seqlen = jnp.asarray(np.concatenate([np.full(blk_lat, seq), np.full(t_ctx // TM, cseq)]), jnp.int32)
    per = TM // TR
    mrow_r = jnp.repeat(mrow, per)
    tn, tn_ffn = 1536, 1024

    cond = jnp.zeros((SUBLANES, d), F32).at[:nb].set(c).at[nb].set(c_ctx)
    mods = _mods(cond, w_mod, b_mod).reshape(depth, SUBLANES, 6, 1, d)

    cos, sin = (jnp.asarray(a) for a in _rope_tables(seq))
    tabs = _fft_tables()
    ctabs = _dft_tables(cseq)
    filt = _hy_filters(seq, hy_f_w1, hy_f_b1, hy_f_w2, hy_f_b2, hy_f_w3, hy_f_freq)
    spec = _hy_spectrum(filt.reshape(depth * HY_ORDER, hy_w, FFT_NO, FFT_NI), tabs, 32)
    spec = spec.reshape(depth, HY_ORDER, hy_w, FFT_NO, 2 * FFT_NI)
    cfilt = _hy_filters(cseq, hy_f_w1, hy_f_b1, hy_f_w2, hy_f_b2, hy_f_w3, hy_f_freq)
    cspec = _rowdft(cfilt.reshape(depth * HY_ORDER * hy_w, 2 * cseq), jnp.asarray(ctabs["fwd_real"], BF16), 1024)
    cspec = cspec.reshape(depth, HY_ORDER, hy_w, 4 * cseq)

    xs = jnp.concatenate([x.reshape(t_lat, d), ctx.reshape(t_ctx, d)], axis=0)
    assert hy_w == HY_W and w_in.shape[2] == COL_GATE + N_BRANCH * d
    hy_cols = COL_GATE - COL_HY
    s_zero = jnp.zeros((nb, RET_HEADS, 2, LANES, RET_DV), F32)
    hn = _normmod(xs, g_norm1[0], mods[0], mrow, 0, 1, blk_all)
    for l in range(depth):
        last = l == depth - 1
        nblk = blk_lat if last else blk_all
        rows = nblk * TM
        z = _matmul(hn, w_in, l, blk_all, tn, w_in.shape[2] - hy_cols, skip=(COL_HY // tn, hy_cols // tn))
        hy_proj = functools.partial(_matmul_nt, hn, w_in, l, col0=COL_HY, n_out=hy_cols, nb=nb, tc=1024)
        zh = hy_proj(blk0=0, nblk=blk_lat, seq=seq)

        y_ret, s_ctx = _retention(z, ret_log_decay[l], cos, sin, s_zero, nb=nb, seq=cseq, row0=t_lat, rope=False,
                                  out_rows=blk_all * TM)
        y_ret, _ = _retention(z, ret_log_decay[l], cos, sin, s_ctx, nb=nb, seq=seq, row0=0, rope=True,
                              out_rows=blk_all * TM, y_prev=y_ret)
        y_na = _na_attention(z, na_rpb[l], nb=nb, seq=seq, cseq=cseq, crow0=t_lat, out_rows=rows)
        pm = _hy_params(hy_conv_w[l], hy_conv_b[l], hy_bias[l])
        pm_lat = jnp.broadcast_to(pm[:, :, None, None], (14, hy_w, 1, FFT_NI))
        y_hy = _hyena(zh.reshape(nb, 3, hy_w, FFT_NO // 2, FFT_NI), spec, l, pm_lat, tabs, 32)
        y_hy = y_hy.reshape(nb, hy_w, seq).transpose(0, 2, 1).reshape(t_lat, hy_w)
        if not last:
            y_na = _dense_attention(z, y_na, nb=nb, cseq=cseq, crow0=t_lat)
            zc = hy_proj(blk0=blk_lat, nblk=t_ctx // TM, seq=cseq)
            yc_hy = _hyena_ctx(zc.reshape(nb, 3, hy_w, cseq), cspec[l], pm[:, :, None], ctabs, 256)
            y_hy = jnp.concatenate([y_hy, yc_hy.transpose(0, 2, 1).reshape(t_ctx, hy_w)], axis=0)
        acc = _merge((y_hy, y_ret, y_na), z, COL_HY, w_branch[l].astype(BF16), nblk * per, TR, d)
        xs, hn = _out_proj(acc, w_out[l].astype(BF16), xs, mods[l], mrow_r, g_norm2[l], nblk * per)
        u = _matmul(hn, ffn_w_in, l, nblk, tn_ffn, ffn_w_in.shape[2])
        hg = _ffn_gate(u, ffn_conv_w[l], ffn_conv_b[l], seqlen, nblk, 512)
        nxt = l if last else l + 1
        w_ffo = ffn_w_out[l].astype(BF16)
        if last:
            res = _ffn_out(hg, w_ffo, xs, mods[l], mods[l], g_final, mrow_r, nblk * per, TR, 512, True)
        else:
            res = _ffn_out(hg, w_ffo, xs, mods[l], mods[nxt], g_norm1[nxt], mrow, nblk, TM, 256, False)
        if not last:
            xs, hn = res
    return res[0].reshape(nb, seq, d)
```

```python
import functools
import math

import numpy as np
import jax
import jax.numpy as jnp
from jax import lax
from jax.experimental import pallas as pl
from jax.experimental.pallas import tpu as pltpu

F32 = jnp.float32
BF16 = jnp.bfloat16

GRID_W = 64
N_BRANCH = 3
HY_ORDER = 2
HY_EMB = 33
HY_FAST = 0.3
HY_SLOW = 1.5
HY_TARGET = 1e-2
RET_HEADS = 8
RET_DK = 64
RET_DV = 128
RET_CHUNK = 128
NA_HEADS = 8
NA_DH = 128
NA_WR = 8
NA_WC = 16
ROPE_BASE = 10000.0
EPS = 1e-6
NEG = -1e30
HY_W = 1024

_RET_QK_W, _RET_V_W, _NA_W = RET_HEADS * RET_DK, RET_HEADS * RET_DV, NA_HEADS * NA_DH
COL_RET_K = 0
COL_RET_V = COL_RET_K + _RET_QK_W
COL_NA_K = COL_RET_V + _RET_V_W
COL_NA_V = COL_NA_K + _NA_W
COL_RET_Q = COL_NA_V + _NA_W
COL_RET_G = COL_RET_Q + _RET_QK_W
COL_NA_Q = COL_RET_G + _RET_V_W
COL_HY = COL_NA_Q + _NA_W
COL_GATE = COL_HY + (HY_ORDER + 1) * HY_W

LANES = 128
SUBLANES = 8
VMEM_LIMIT = 56 * 1024 * 1024

TM = 1024


def _cparams(*sem):
    return pltpu.CompilerParams(dimension_semantics=sem, vmem_limit_bytes=VMEM_LIMIT)


def _mods_kernel(a_ref, w_ref, b_ref, o_ref):
    k = pl.program_id(1)
    a = a_ref[...]
    a = a * jax.nn.sigmoid(a)
    part = jnp.dot(a.astype(BF16), w_ref[...].astype(BF16), preferred_element_type=F32)

    @pl.when(k == 0)
    def _():
        o_ref[...] = part + b_ref[...]

    @pl.when(k > 0)
    def _():
        o_ref[...] += part


def _mods(cond, w_mod, b_mod):
    depth, d, n = w_mod.shape
    tk = LANES
    return pl.pallas_call(
        _mods_kernel,
        grid=(depth, d // tk),
        in_specs=[
            pl.BlockSpec((SUBLANES, tk), lambda l, k: (0, k)),
            pl.BlockSpec((None, tk, n), lambda l, k: (l, k, 0)),
            pl.BlockSpec((None, 1, n), lambda l, k: (l, 0, 0)),
        ],
        out_specs=pl.BlockSpec((None, SUBLANES, n), lambda l, k: (l, 0, 0)),
        out_shape=jax.ShapeDtypeStruct((depth, SUBLANES, n), F32),
        compiler_params=_cparams("parallel", "arbitrary"),
        name="mods",
    )(cond, w_mod, b_mod.reshape(depth, 1, n))


def _normmod_kernel(mrow_ref, x_ref, g_ref, sh_ref, sc_ref, o_ref):
    x = x_ref[...]
    y = x * lax.rsqrt(jnp.mean(x * x, axis=-1, keepdims=True) + EPS)
    y = y * g_ref[...]
    o_ref[...] = (y * (1.0 + sc_ref[...]) + sh_ref[...]).astype(o_ref.dtype)


def _normmod(x, g, mods, mrow, shift_idx, scale_idx, nblk):
    m, d = x.shape
    grid_spec = pltpu.PrefetchScalarGridSpec(
        num_scalar_prefetch=1,
        grid=(nblk,),
        in_specs=[
            pl.BlockSpec((TM, d), lambda i, mr: (i, 0)),
            pl.BlockSpec((1, d), lambda i, mr: (0, 0)),
            pl.BlockSpec((None, None, 1, d), lambda i, mr: (mr[i], shift_idx, 0, 0)),
            pl.BlockSpec((None, None, 1, d), lambda i, mr: (mr[i], scale_idx, 0, 0)),
        ],
        out_specs=pl.BlockSpec((TM, d), lambda i, mr: (i, 0)),
    )
    return pl.pallas_call(
        _normmod_kernel,
        grid_spec=grid_spec,
        out_shape=jax.ShapeDtypeStruct((m, d), BF16),
        compiler_params=_cparams("parallel"),
        name="normmod",
    )(mrow, x, g.reshape(1, d), mods, mods)


def _mm_kernel(a_ref, w_ref, o_ref, w_scr):
    @pl.when(pl.program_id(1) == 0)
    def _():
        w_scr[...] = w_ref[...].astype(BF16)

    o_ref[...] = jnp.dot(a_ref[...], w_scr[...], preferred_element_type=F32).astype(o_ref.dtype)


def _matmul(a, w, layer, nblk, tn, n_out, skip=None):
    m, k = a.shape
    col = (lambda j: j) if skip is None else (lambda j: j + jnp.where(j >= skip[0], skip[1], 0))
    return pl.pallas_call(
        _mm_kernel,
        grid=(n_out // tn, nblk),
        in_specs=[pl.BlockSpec((TM, k), lambda j, i: (i, 0)),
                  pl.BlockSpec((None, k, tn), lambda j, i: (layer, 0, col(j)))],
        out_specs=pl.BlockSpec((TM, tn), lambda j, i: (i, j)),
        out_shape=jax.ShapeDtypeStruct((m, n_out), BF16),
        scratch_shapes=[pltpu.VMEM((k, tn), BF16)],
        compiler_params=_cparams("arbitrary", "arbitrary"),
        name="matmul",
    )(a, w)


def _mm_nt_kernel(w_ref, a_ref, o_ref, wt_scr):
    @pl.when(pl.program_id(1) == 0)
    def _():
        wt_scr[...] = w_ref[...].T.astype(BF16)

    r = lax.dot_general(wt_scr[...], a_ref[...], (((1,), (1,)), ((), ())), preferred_element_type=F32)
    per = o_ref.shape[0]
    width = r.shape[1] // per
    for s in range(per):
        o_ref[s] = r[:, s * width:(s + 1) * width].astype(o_ref.dtype)


def _matmul_nt(a, w, layer, *, col0, n_out, blk0, nblk, nb, seq, tc):
    k = a.shape[1]
    per, sblk = max(TM // seq, 1), max(seq // TM, 1)
    return pl.pallas_call(
        _mm_nt_kernel,
        grid=(n_out // tc, nblk),
        in_specs=[pl.BlockSpec((None, k, tc), lambda j, i: (layer, 0, col0 // tc + j)),
                  pl.BlockSpec((TM, k), lambda j, i: (blk0 + i, 0))],
        out_specs=pl.BlockSpec((per, tc, TM // per), lambda j, i: (i // sblk, j, i % sblk)),
        out_shape=jax.ShapeDtypeStruct((nb, n_out, seq), BF16),
        scratch_shapes=[pltpu.VMEM((tc, k), BF16)],
        compiler_params=_cparams("arbitrary", "arbitrary"),
        name="matmul_nt",
    )(w, a)


TR = 512


def _norm_modulate(x, g, shift, scale):
    y = x * lax.rsqrt(jnp.mean(x * x, axis=-1, keepdims=True) + EPS)
    return (y * g) * (1.0 + scale) + shift


def _outproj_kernel(mrow_ref, a_ref, w_ref, x_ref, gate_ref, g_ref, sh_ref, sc_ref, x_out, h_out):
    y = jnp.dot(a_ref[...], w_ref[...], preferred_element_type=F32)
    x = x_ref[...] + gate_ref[...] * y
    x_out[...] = x
    h_out[...] = _norm_modulate(x, g_ref[...], sh_ref[...], sc_ref[...]).astype(h_out.dtype)


def _out_proj(a, w, x, mods, mrow, g_norm, nblk):
    m, k = a.shape
    d = w.shape[1]
    mod = lambda idx: pl.BlockSpec((None, None, 1, d), lambda i, mr: (mr[i], idx, 0, 0))
    row = pl.BlockSpec((TR, d), lambda i, mr: (i, 0))
    grid_spec = pltpu.PrefetchScalarGridSpec(
        num_scalar_prefetch=1,
        grid=(nblk,),
        in_specs=[pl.BlockSpec((TR, k), lambda i, mr: (i, 0)), pl.BlockSpec((k, d), lambda i, mr: (0, 0)), row,
                  mod(2), pl.BlockSpec((1, d), lambda i, mr: (0, 0)), mod(3), mod(4)],
        out_specs=[row, row],
    )
    return pl.pallas_call(
        _outproj_kernel,
        grid_spec=grid_spec,
        out_shape=[jax.ShapeDtypeStruct((m, d), F32), jax.ShapeDtypeStruct((m, d), BF16)],
        compiler_params=_cparams("parallel"),
        name="out_proj",
    )(mrow, a, w, x, mods, g_norm.reshape(1, d), mods, mods)


def _merge_kernel(y0_ref, y1_ref, y2_ref, g0_ref, g1_ref, g2_ref, w_ref, o_ref):
    acc = None
    for i, (y_ref, g_ref) in enumerate(((y0_ref, g0_ref), (y1_ref, g1_ref), (y2_ref, g2_ref))):
        t = jnp.dot(y_ref[...], w_ref[i], preferred_element_type=F32)
        t = jax.nn.sigmoid(g_ref[...].astype(F32)) * t
        acc = t if acc is None else acc + t
    o_ref[...] = acc.astype(o_ref.dtype)


def _merge(ys, z, gate_col0, w_branch, nblk, tr, tn):
    m, bw = ys[0].shape
    d = w_branch.shape[2]
    assert gate_col0 % tn == 0 and d % tn == 0
    gspec = lambda i_br: pl.BlockSpec((tr, tn), lambda i, j: (i, (gate_col0 + i_br * d) // tn + j))
    yspec = pl.BlockSpec((tr, bw), lambda i, j: (i, 0))
    return pl.pallas_call(
        _merge_kernel,
        grid=(nblk, d // tn),
        in_specs=[yspec, yspec, yspec, gspec(0), gspec(1), gspec(2),
                  pl.BlockSpec((N_BRANCH, bw, tn), lambda i, j: (0, 0, j))],
        out_specs=pl.BlockSpec((tr, tn), lambda i, j: (i, j)),
        out_shape=jax.ShapeDtypeStruct((m, d), BF16),
        compiler_params=_cparams("parallel", "arbitrary"),
        name="merge",
    )(ys[0], ys[1], ys[2], z, z, z, w_branch)


def _ffn_gate_kernel(seq_ref, a_ref, ap_ref, an_ref, b_ref, cw_ref, cb_ref, o_ref):
    i = pl.program_id(0)
    seq_m1 = seq_ref[i] - 1
    tm = a_ref.shape[0]
    row = lax.broadcasted_iota(jnp.int32, (tm, 1), 0)
    halo_p = jnp.where(((i * tm) & seq_m1) == 0, 0.0, ap_ref[SUBLANES - 1:SUBLANES, :].astype(F32))
    halo_n = jnp.where((((i + 1) * tm) & seq_m1) == 0, 0.0, an_ref[0:1, :].astype(F32))

    def finish(inner_edges):
        a = a_ref[...].astype(F32)
        prev = jnp.where(row == 0, halo_p, pltpu.roll(a, 1, axis=0))
        nxt = jnp.where(row == tm - 1, halo_n, pltpu.roll(a, tm - 1, axis=0))
        if inner_edges:
            pos = row & seq_m1
            prev = jnp.where((pos == 0) & (row > 0), 0.0, prev)
            nxt = jnp.where((pos == seq_m1) & (row < tm - 1), 0.0, nxt)
        cw = cw_ref[...]
        conv = prev * cw[0:1, :] + a * cw[1:2, :] + nxt * cw[2:3, :] + cb_ref[...]
        o_ref[...] = jax.nn.gelu(conv.astype(BF16)) * b_ref[...]

    @pl.when(seq_m1 >= tm - 1)
    def _():
        finish(False)

    @pl.when(seq_m1 < tm - 1)
    def _():
        finish(True)


def _ffn_gate(u, conv_w, conv_b, seqlen, nblk, tc):
    m, ff2 = u.shape
    ff = ff2 // 2
    nrow8 = m // SUBLANES
    r8 = TM // SUBLANES
    grid_spec = pltpu.PrefetchScalarGridSpec(
        num_scalar_prefetch=1,
        grid=(nblk, ff // tc),
        in_specs=[
            pl.BlockSpec((TM, tc), lambda i, j, s: (i, j)),
            pl.BlockSpec((SUBLANES, tc), lambda i, j, s: (jnp.maximum(i * r8 - 1, 0), j)),
            pl.BlockSpec((SUBLANES, tc), lambda i, j, s: (jnp.minimum((i + 1) * r8, nrow8 - 1), j)),
            pl.BlockSpec((TM, tc), lambda i, j, s: (i, ff // tc + j)),
            pl.BlockSpec((3, tc), lambda i, j, s: (0, j)),
            pl.BlockSpec((1, tc), lambda i, j, s: (0, j)),
        ],
        out_specs=pl.BlockSpec((TM, tc), lambda i, j, s: (i, j)),
    )
    return pl.pallas_call(
        _ffn_gate_kernel,
        grid_spec=grid_spec,
        out_shape=jax.ShapeDtypeStruct((m, ff), BF16),
        compiler_params=_cparams("parallel", "arbitrary"),
        name="ffn_gate",
    )(seqlen, u, u, u, u, conv_w, conv_b.reshape(1, ff))


def _ffn_out_kernel(mrow_ref, h_ref, w_ref, x_ref, gate_ref, g_ref, sh_ref, sc_ref, *rest, final):
    outs, xrow = rest[:-1], rest[-1]
    j = pl.program_id(1)
    y = jnp.dot(h_ref[...], w_ref[...], preferred_element_type=F32)
    xt = x_ref[...] + gate_ref[...] * y
    xrow[j] = xt
    if not final:
        outs[0][...] = xt

    @pl.when(j == pl.num_programs(1) - 1)
    def _():
        x = jnp.concatenate([xrow[t] for t in range(xrow.shape[0])], axis=1)
        if final:
            y = x * lax.rsqrt(jnp.mean(x * x, axis=-1, keepdims=True) + EPS)
            outs[0][...] = y * g_ref[...]
        else:
            outs[1][...] = _norm_modulate(x, g_ref[...], sh_ref[...], sc_ref[...]).astype(outs[1].dtype)


def _ffn_out(h, w, x, mods, mods_next, g_next, mrow, nblk, tr, tn, final):
    m, ff = h.shape
    d = w.shape[1]
    mod = lambda arr_idx: pl.BlockSpec((None, None, 1, d), lambda i, j, mr: (mr[i], arr_idx, 0, 0))
    row = pl.BlockSpec((tr, d), lambda i, j, mr: (i, 0))
    tile = pl.BlockSpec((tr, tn), lambda i, j, mr: (i, j))
    grid_spec = pltpu.PrefetchScalarGridSpec(
        num_scalar_prefetch=1,
        grid=(nblk, d // tn),
        in_specs=[
            pl.BlockSpec((tr, ff), lambda i, j, mr: (i, 0)),
            pl.BlockSpec((ff, tn), lambda i, j, mr: (0, j)),
            tile,
            pl.BlockSpec((None, None, 1, tn), lambda i, j, mr: (mr[i], 5, 0, j)),
            pl.BlockSpec((1, d), lambda i, j, mr: (0, 0)), mod(0), mod(1),
        ],
        out_specs=[row] if final else [tile, row],
        scratch_shapes=[pltpu.VMEM((d // tn, tr, tn), F32)],
    )
    out_shape = [jax.ShapeDtypeStruct((m, d), F32)] + ([] if final else [jax.ShapeDtypeStruct((m, d), BF16)])
    return pl.pallas_call(
        functools.partial(_ffn_out_kernel, final=final),
        grid_spec=grid_spec,
        out_shape=out_shape,
        compiler_params=_cparams("parallel", "arbitrary"),
        name="ffn_out",
    )(mrow, h, w, x, mods, g_next.reshape(1, d), mods_next, mods_next)


def _rope_pair(x, cos, sin):
    lane = lax.broadcasted_iota(jnp.int32, x.shape, 1)
    first = lax.rem(lane, 32) < 16
    partner = jnp.where(first, pltpu.roll(x, LANES - 16, axis=1), pltpu.roll(x, 16, axis=1))
    return x * cos + partner * sin


def _ret_kernel(lg_ref, q_ref, k_ref, v_ref, g_ref, cos_ref, sin_ref, s0_ref, *rest, rope, has_prev):
    o_ref, sfin_ref, qr_scr, kr_scr, kv_scr = rest[1:] if has_prev else rest
    hp = pl.program_id(1)
    seq = q_ref.shape[0]
    c = RET_CHUNK
    nchunk = seq // c
    ks = RET_DK ** -0.5
    q = q_ref[...].astype(F32)
    k = k_ref[...].astype(F32)
    if rope:
        q = _rope_pair(q, cos_ref[...], sin_ref[...])
        k = _rope_pair(k, cos_ref[...], sin_ref[...])
    qr_scr[...] = q
    kr_scr[...] = k * ks

    pos_r = lax.broadcasted_iota(jnp.int32, (c, 1), 0).astype(F32)
    rel = (lax.broadcasted_iota(jnp.int32, (c, c), 0) - lax.broadcasted_iota(jnp.int32, (c, c), 1)).astype(F32)
    lane = lax.broadcasted_iota(jnp.int32, (1, LANES), 1)

    heads = []
    for hh in range(2):
        lgf = lg_ref[0, 2 * hp + hh]
        lgb = lg_ref[1, 2 * hp + hh]
        heads.append(dict(
            hm=(lane // RET_DK == hh).astype(F32),
            zeta_f=jnp.exp(lgf * (c - 1 - pos_r)), zeta_b=jnp.exp(lgb * pos_r),
            xi_f=jnp.exp(lgf * (pos_r + 1.0)), xi_b=jnp.exp(lgb * (c - pos_r)),
            dmat=jnp.where(rel >= 0, jnp.exp(lgf * jnp.maximum(rel, 0.0)), jnp.exp(lgb * jnp.maximum(-rel, 0.0))),
            gf=jnp.exp(lgf * c), gb=jnp.exp(lgb * c),
            vcols=slice(hh * RET_DV, (hh + 1) * RET_DV)))

    def kv_body(n, _):
        rows = pl.ds(pl.multiple_of(n * c, c), c)
        kr = kr_scr[rows, :]
        for hh, hd in enumerate(heads):
            kh = kr * hd["hm"]
            kz = jnp.concatenate([kh * hd["zeta_f"], kh * hd["zeta_b"]], axis=1).astype(BF16)
            kv_scr[hh, n] = lax.dot_general(kz, v_ref[rows, hd["vcols"]], (((0,), (0,)), ((), ())),
                                            preferred_element_type=F32)
        return 0

    lax.fori_loop(0, nchunk, kv_body, 0, unroll=min(8, nchunk))

    def scan(j, carry):
        nf, nbk = j, nchunk - 1 - j
        out = []
        for hh, hd in enumerate(heads):
            sf, sb = carry[2 * hh], carry[2 * hh + 1]
            tf = kv_scr[hh, nf, 0:LANES, :]
            kv_scr[hh, nf, 0:LANES, :] = sf
            tb = kv_scr[hh, nbk, LANES:2 * LANES, :]
            kv_scr[hh, nbk, LANES:2 * LANES, :] = sb
            out += [hd["gf"] * sf + tf, hd["gb"] * sb + tb]
        return tuple(out)

    fin = lax.fori_loop(0, nchunk, scan, (s0_ref[0, 0], s0_ref[0, 1], s0_ref[1, 0], s0_ref[1, 1]))
    for hh in range(2):
        sfin_ref[hh, 0] = fin[2 * hh]
        sfin_ref[hh, 1] = fin[2 * hh + 1]

    def out_body(n, _):
        rows = pl.ds(pl.multiple_of(n * c, c), c)
        qr = qr_scr[rows, :]
        kb = kr_scr[rows, :].astype(BF16)
        for hh, hd in enumerate(heads):
            qm = qr * hd["hm"]
            a = lax.dot_general(qm.astype(BF16), kb, (((1,), (1,)), ((), ())),
                                preferred_element_type=F32) * hd["dmat"]
            o = jnp.dot(a.astype(BF16), v_ref[rows, hd["vcols"]], preferred_element_type=F32)
            qx = jnp.concatenate([qm * hd["xi_f"], qm * hd["xi_b"]], axis=1).astype(BF16)
            o = o + jnp.dot(qx, kv_scr[hh, n].astype(BF16), preferred_element_type=F32)
            o = o * lax.rsqrt(jnp.mean(o * o, axis=-1, keepdims=True) + EPS)
            g = g_ref[rows, hd["vcols"]].astype(F32)
            o_ref[rows, hd["vcols"]] = (g * jax.nn.sigmoid(g) * o).astype(o_ref.dtype)
        return 0

    lax.fori_loop(0, nchunk, out_body, 0, unroll=min(8, nchunk))


def _retention(z, log_decay, cos, sin, s0, *, nb, seq, row0, rope, out_rows, y_prev=None):
    rb0 = row0 // seq
    hpairs = RET_HEADS // 2
    kcol, qcol = COL_RET_K // LANES, COL_RET_Q // LANES
    vcol, gcol = COL_RET_V // (2 * RET_DV), COL_RET_G // (2 * RET_DV)
    has_prev = y_prev is not None
    kernel = functools.partial(_ret_kernel, rope=rope, has_prev=has_prev)
    in_specs = [
        pl.BlockSpec(memory_space=pltpu.SMEM),
        pl.BlockSpec((seq, LANES), lambda b, p: (rb0 + b, qcol + p)),
        pl.BlockSpec((seq, LANES), lambda b, p: (rb0 + b, kcol + p)),
        pl.BlockSpec((seq, 2 * RET_DV), lambda b, p: (rb0 + b, vcol + p)),
        pl.BlockSpec((seq, 2 * RET_DV), lambda b, p: (rb0 + b, gcol + p)),
        pl.BlockSpec((seq, LANES), lambda b, p: (0, 0)),
        pl.BlockSpec((seq, LANES), lambda b, p: (0, 0)),
        pl.BlockSpec((None, 2, 2, LANES, RET_DV), lambda b, p: (b, p, 0, 0, 0)),
    ]
    args = [log_decay, z, z, z, z, cos, sin, s0]
    if has_prev:
        in_specs.append(pl.BlockSpec(memory_space=pl.ANY))
        args.append(y_prev)
    return pl.pallas_call(
        kernel,
        grid=(nb, hpairs),
        in_specs=in_specs,
        out_specs=[
            pl.BlockSpec((seq, 2 * RET_DV), lambda b, p: (rb0 + b, p)),
            pl.BlockSpec((None, 2, 2, LANES, RET_DV), lambda b, p: (b, p, 0, 0, 0)),
        ],
        out_shape=[
            jax.ShapeDtypeStruct((out_rows, RET_HEADS * RET_DV), BF16),
            jax.ShapeDtypeStruct((nb, RET_HEADS, 2, LANES, RET_DV), F32),
        ],
        scratch_shapes=[
            pltpu.VMEM((seq, LANES), F32),
            pltpu.VMEM((seq, LANES), F32),
            pltpu.VMEM((2, seq // RET_CHUNK, 2 * LANES, RET_DV), F32),
        ],
        input_output_aliases={len(args) - 1: 0} if has_prev else {},
        compiler_params=_cparams("parallel", "arbitrary"),
        name="retention_rope" if rope else "retention",
    )(*args)


NA_GROUP = 4
NA_KROWS = 12
NA_OVERLAP = 4


def _na_kernel(rpb_ref, q_ref, k_ref, v_ref, kc_ref, vc_ref, o_ref, bias_scr):
    seq = q_ref.shape[0]
    nrow = seq // GRID_W
    ndr, ndc = 2 * NA_WR - 1, 2 * NA_WC - 1
    scale = NA_DH ** -0.5
    nt = (((1,), (1,)), ((), ()))
    h = pl.program_id(0)
    lane = lax.broadcasted_iota(jnp.int32, (GRID_W, LANES), 1)

    @pl.when(pl.program_id(1) == 0)
    def _():
        qi = lax.broadcasted_iota(jnp.int32, (GRID_W, LANES), 0)
        ki = lane & (GRID_W - 1)
        dc = ki - qi + (NA_WC - 1)
        cs = jnp.clip(qi - NA_WC // 2, 0, GRID_W - NA_WC)
        colmask = (ki >= cs) & (ki < cs + NA_WC)

        def toeplitz(d):
            t = jnp.zeros((GRID_W, LANES), F32)
            for j in range(ndc):
                t = jnp.where(dc == j, rpb_ref[h * (ndr * ndc) + d * ndc + j], t)
            return t

        prev = jnp.zeros((GRID_W, LANES), F32)
        for i in range(ndr + 1):
            nxt = toeplitz(i) if i < ndr else jnp.zeros((GRID_W, LANES), F32)
            bias_scr[i] = jnp.where(colmask, jnp.where(lane < GRID_W, prev, nxt), NEG)
            prev = nxt

    kc = kc_ref[...]
    vc = vc_ref[...]
    kwin = NA_KROWS * GRID_W

    half = NA_WR // 2
    ntile = kwin // LANES

    def group(r0, interior):
        us = r0 - half if interior else min(max(r0 - half, 0), nrow - NA_KROWS)
        if isinstance(r0, int):
            qrows = pl.ds(r0 * GRID_W, NA_GROUP * GRID_W)
            krows = pl.ds(us * GRID_W, kwin)
        else:
            qrows = pl.ds(pl.multiple_of(r0 * GRID_W, NA_GROUP * GRID_W), NA_GROUP * GRID_W)
            krows = pl.ds(pl.multiple_of(us * GRID_W, GRID_W), kwin)
        q = q_ref[qrows, :]
        s = lax.dot_general(q, k_ref[krows, :], nt, preferred_element_type=F32)
        sc = lax.dot_general(q, kc, nt, preferred_element_type=F32) * scale
        e_rows, ec_rows, den_rows = [], [], []
        for u in range(NA_GROUP):
            qr = slice(u * GRID_W, (u + 1) * GRID_W)
            tiles = []
            for m in range(ntile):
                if interior:
                    off, idx = 2 * m - u, 2 * m - u + half
                else:
                    r = r0 + u
                    off = us + 2 * m - min(max(r - half, 0), nrow - NA_WR)
                    idx = min(max(us + 2 * m - r + NA_WR, 0), ndr)
                ok_lo, ok_hi = 0 <= off < NA_WR, 0 <= off + 1 < NA_WR
                if not (ok_lo or ok_hi):
                    tiles.append(None)
                    continue
                tab = bias_scr[idx]
                if not (ok_lo and ok_hi):
                    tab = tab + jnp.where(lane < GRID_W, 0.0 if ok_lo else NEG, 0.0 if ok_hi else NEG)
                st = s[qr, m * LANES:(m + 1) * LANES]
                tiles.append(jnp.where(tab > 0.5 * NEG, st * scale + tab, NEG))
            live = [t for t in tiles if t is not None]
            tmax = live[0]
            for t in live[1:]:
                tmax = jnp.maximum(tmax, t)
            scu = sc[qr, :]
            mx = jnp.maximum(jnp.max(tmax, axis=-1, keepdims=True), jnp.max(scu, axis=-1, keepdims=True))
            etiles = [None if t is None else jnp.exp(t - mx) for t in tiles]
            ecu = jnp.exp(scu - mx)
            esum = None
            for t in etiles:
                if t is not None:
                    esum = t if esum is None else esum + t
            den_rows.append(jnp.sum(esum, axis=-1, keepdims=True) + jnp.sum(ecu, axis=-1, keepdims=True))
            zero = jnp.zeros((GRID_W, LANES), BF16)
            e_rows.append(jnp.concatenate([zero if t is None else t.astype(BF16) for t in etiles], axis=1))
            ec_rows.append(ecu.astype(BF16))
        o = jnp.dot(jnp.concatenate(e_rows, axis=0), v_ref[krows, :], preferred_element_type=F32)
        o = o + jnp.dot(jnp.concatenate(ec_rows, axis=0), vc, preferred_element_type=F32)
        o_ref[qrows, :] = (o / jnp.concatenate(den_rows, axis=0)).astype(o_ref.dtype)

    ngroup = nrow // NA_GROUP
    g_lo = -(-half // NA_GROUP)
    g_hi = min((nrow - NA_KROWS + half) // NA_GROUP, (nrow - NA_WR + half - NA_GROUP + 1) // NA_GROUP)
    for g in range(g_lo):
        group(g * NA_GROUP, False)

    def body(i, _):
        for t in range(NA_OVERLAP):
            group((g_lo + NA_OVERLAP * i + t) * NA_GROUP, True)
        return 0

    nloop = (g_hi + 1 - g_lo) // NA_OVERLAP
    lax.fori_loop(0, nloop, body, 0)
    for g in range(g_lo + NA_OVERLAP * nloop, g_hi + 1):
        group(g * NA_GROUP, True)
    for g in range(g_hi + 1, ngroup):
        group(g * NA_GROUP, False)


def _na_attention(z, rpb, *, nb, seq, cseq, crow0, out_rows):
    kcol, vcol, qcol = COL_NA_K // NA_DH, COL_NA_V // NA_DH, COL_NA_Q // NA_DH
    crb0 = crow0 // cseq
    nrow = seq // GRID_W
    assert nrow >= NA_KROWS and nrow % NA_GROUP == 0 and GRID_W * 2 == LANES
    assert NA_KROWS % 2 == 0 and NA_KROWS >= NA_WR + NA_GROUP - 1
    return pl.pallas_call(
        _na_kernel,
        grid=(NA_HEADS, nb),
        in_specs=[
            pl.BlockSpec(memory_space=pltpu.SMEM),
            pl.BlockSpec((seq, NA_DH), lambda h, b: (b, qcol + h)),
            pl.BlockSpec((seq, NA_DH), lambda h, b: (b, kcol + h)),
            pl.BlockSpec((seq, NA_DH), lambda h, b: (b, vcol + h)),
            pl.BlockSpec((cseq, NA_DH), lambda h, b: (crb0 + b, kcol + h)),
            pl.BlockSpec((cseq, NA_DH), lambda h, b: (crb0 + b, vcol + h)),
        ],
        out_specs=pl.BlockSpec((seq, NA_DH), lambda h, b: (b, h)),
        out_shape=jax.ShapeDtypeStruct((out_rows, NA_HEADS * NA_DH), BF16),
        scratch_shapes=[pltpu.VMEM((2 * NA_WR, GRID_W, LANES), F32)],
        compiler_params=_cparams("arbitrary", "arbitrary"),
        name="na_attention",
    )(rpb.reshape(-1), z, z, z, z, z)


def _dense_attn_kernel(q_ref, k_ref, v_ref, yprev_ref, o_ref):
    s = lax.dot_general(q_ref[...], k_ref[...], (((1,), (1,)), ((), ())), preferred_element_type=F32) * NA_DH ** -0.5
    e = jnp.exp(s - jnp.max(s, axis=-1, keepdims=True))
    o = jnp.dot(e.astype(BF16), v_ref[...], preferred_element_type=F32)
    o_ref[...] = (o / jnp.sum(e, axis=-1, keepdims=True)).astype(o_ref.dtype)


def _dense_attention(z, y_prev, *, nb, cseq, crow0):
    kcol, vcol, qcol = COL_NA_K // NA_DH, COL_NA_V // NA_DH, COL_NA_Q // NA_DH
    crb0 = crow0 // cseq
    return pl.pallas_call(
        _dense_attn_kernel,
        grid=(nb, NA_HEADS),
        in_specs=[
            pl.BlockSpec((cseq, NA_DH), lambda b, h: (crb0 + b, qcol + h)),
            pl.BlockSpec((cseq, NA_DH), lambda b, h: (crb0 + b, kcol + h)),
            pl.BlockSpec((cseq, NA_DH), lambda b, h: (crb0 + b, vcol + h)),
            pl.BlockSpec(memory_space=pl.ANY),
        ],
        out_specs=pl.BlockSpec((cseq, NA_DH), lambda b, h: (crb0 + b, h)),
        out_shape=jax.ShapeDtypeStruct(y_prev.shape, BF16),
        input_output_aliases={3: 0},
        compiler_params=_cparams("parallel", "arbitrary"),
        name="dense_attention",
    )(z, z, z, y_prev)


HY_FEAT_ROWS = 64
HIGHEST = lax.Precision.HIGHEST


def _hy_features(seq):
    t = np.linspace(0.0, 1.0, seq)
    bands = (HY_EMB - 1) // 2
    w = 2.0 * math.pi * np.arange(seq) / seq
    fr = np.linspace(1e-4, bands - 1, bands)
    ang = fr[None] * w[:, None]
    feat = np.concatenate([t[:, None], np.cos(ang), -np.sin(ang)], axis=-1)
    src = np.concatenate([np.arange(seq), np.zeros(1, np.int64), np.arange(seq - 1, 0, -1)])
    feat2 = np.zeros((HY_FEAT_ROWS, 2 * seq), np.float32)
    feat2[:HY_EMB] = feat[src].T
    tt = t[src][None].astype(np.float32)
    mask = np.ones((1, 2 * seq), np.float32)
    mask[0, seq] = 0.0
    return feat2, tt, mask


def _hy_mlp_kernel(feat_ref, w1_ref, b1_ref, w2_ref, b2_ref, fq_ref, o_ref):
    fq = fq_ref[...]
    h = jnp.dot(w1_ref[...], feat_ref[...], preferred_element_type=F32, precision=HIGHEST)
    h = jnp.sin(fq * (h + b1_ref[...]))
    h = jnp.dot(w2_ref[...], h, preferred_element_type=F32, precision=HIGHEST)
    o_ref[...] = jnp.sin(fq * (h + b2_ref[...]))


def _hy_mlp(feat2, w1t, b1, w2t, b2, fq):
    depth, hid, _ = w1t.shape
    n = feat2.shape[1]
    col = lambda a: a.reshape(depth, hid, 1)
    wspec = lambda k: pl.BlockSpec((None, hid, k), lambda l: (l, 0, 0))
    return pl.pallas_call(
        _hy_mlp_kernel,
        grid=(depth,),
        in_specs=[pl.BlockSpec((HY_FEAT_ROWS, n), lambda l: (0, 0)), wspec(HY_FEAT_ROWS), wspec(1), wspec(hid),
                  wspec(1), wspec(1)],
        out_specs=pl.BlockSpec((None, hid, n), lambda l: (l, 0, 0)),
        out_shape=jax.ShapeDtypeStruct((depth, hid, n), F32),
        compiler_params=_cparams("parallel"),
        name="hyena_filter_mlp",
    )(feat2, w1t, col(b1), w2t, col(b2), col(fq))


def _hy_filter_kernel(h_ref, w3_ref, dl_ref, tt_ref, mask_ref, o_ref):
    seq = h_ref.shape[1] // 2
    kf = jnp.dot(w3_ref[0], h_ref[:, :seq], preferred_element_type=F32, precision=HIGHEST)
    kb = jnp.dot(w3_ref[1], h_ref[:, seq:], preferred_element_type=F32, precision=HIGHEST)
    k = jnp.concatenate([kf, kb], axis=1) * (jnp.exp(-tt_ref[...] * dl_ref[...]) * mask_ref[...])
    o_ref[...] = (k * lax.rsqrt(jnp.sum(k * k, axis=1, keepdims=True) + EPS)).astype(o_ref.dtype)


def _hy_filter(h2, w3t, deltas, tt, mask, cb):
    depth, hid, n = h2.shape
    c = w3t.shape[3]
    return pl.pallas_call(
        _hy_filter_kernel,
        grid=(depth, HY_ORDER, c // cb),
        in_specs=[
            pl.BlockSpec((None, hid, n), lambda l, o, j: (l, 0, 0)),
            pl.BlockSpec((None, None, 2, cb, hid), lambda l, o, j: (l, o, 0, j, 0)),
            pl.BlockSpec((cb, 1), lambda l, o, j: (j, 0)),
            pl.BlockSpec((1, n), lambda l, o, j: (0, 0)),
            pl.BlockSpec((1, n), lambda l, o, j: (0, 0)),
        ],
        out_specs=pl.BlockSpec((None, None, cb, n), lambda l, o, j: (l, o, j, 0)),
        out_shape=jax.ShapeDtypeStruct((depth, HY_ORDER, c, n), BF16),
        compiler_params=_cparams("parallel", "parallel", "arbitrary"),
        name="hyena_filter",
    )(h2, w3t, deltas, tt, mask)


FFT_NO = 64
FFT_NI = 128


def _fft_tables():
    n = FFT_NO * FFT_NI
    a = np.arange(FFT_NO)
    fo = np.exp(-2j * np.pi * np.outer(a, a) / FFT_NO)
    i = np.arange(FFT_NI)
    ci = np.exp(-2j * np.pi * np.outer(i, i) / FFT_NI)
    tw = np.exp(-2j * np.pi * np.outer(a, i) / n)
    half = FFT_NO // 2
    f32 = lambda x: np.ascontiguousarray(x, dtype=np.float32)
    g1_real = f32(np.concatenate([fo.real, fo.imag], axis=0))
    g1 = f32(np.block([[fo.real[:, :half], -fo.imag[:, :half]], [fo.imag[:, :half], fo.real[:, :half]]]))
    w2 = f32(np.block([[ci.real, ci.imag], [-ci.imag, ci.real]]))
    w2i = f32(np.block([[ci.real, -ci.imag], [ci.imag, ci.real]]))
    g4 = f32(np.block([[fo.real[:half], fo.imag[:half]], [-fo.imag[:half], fo.real[:half]]]) / n)
    return dict(g1_real=g1_real, g1=g1, w2=w2, w2i=w2i, g4=g4, twr=f32(tw.real), twi=f32(tw.imag))


HY_CHUNK = 16
HY_SPEC_CHUNK = 32


def _fft_stage1(g1, y2, twr, twi):
    nch = y2.shape[1] // FFT_NI
    o1 = jnp.dot(g1, y2, preferred_element_type=F32)
    yr, yi = o1[:FFT_NO].astype(BF16), o1[FFT_NO:].astype(BF16)
    ar = yr * twr - yi * twi
    ai = yr * twi + yi * twr
    lanes = lambda ci: slice(ci * FFT_NI, (ci + 1) * FFT_NI)
    return jnp.concatenate([jnp.concatenate([ar[:, lanes(ci)], ai[:, lanes(ci)]], axis=1)
                            for ci in range(nch)], axis=0)


def _hy_spectrum_kernel(k_ref, g1_ref, w2_ref, twr_ref, twi_ref, o_ref):
    cb = k_ref.shape[0]
    ch = HY_SPEC_CHUNK
    g1 = g1_ref[...]
    w2 = w2_ref[...]
    twr = twr_ref[...]
    twi = twi_ref[...]

    def body(j, _):
        chans = pl.ds(pl.multiple_of(j * ch, ch), ch)
        ks = k_ref[chans]
        k2 = jnp.concatenate([ks[ci] for ci in range(ch)], axis=1).astype(BF16)
        z = jnp.dot(_fft_stage1(g1, k2, twr, twi), w2, preferred_element_type=F32)
        o_ref[chans] = z.reshape(ch, FFT_NO, 2 * FFT_NI).astype(o_ref.dtype)
        return 0

    lax.fori_loop(0, cb // ch, body, 0)


def _tiled_twiddles(tabs, nch):
    tile = lambda a: jnp.asarray(np.tile(a, (1, nch)), BF16)
    return tile(tabs["twr"]), tile(tabs["twi"])


def _hy_spectrum(kfilt, tabs, cb):
    g, c = kfilt.shape[:2]
    const = lambda a: pl.BlockSpec(a.shape, lambda i, j: (0,) * a.ndim)
    g1 = jnp.asarray(tabs["g1_real"], BF16)
    w2 = jnp.asarray(tabs["w2"], BF16)
    twr, twi = _tiled_twiddles(tabs, HY_SPEC_CHUNK)
    return pl.pallas_call(
        _hy_spectrum_kernel,
        grid=(g, c // cb),
        in_specs=[pl.BlockSpec((None, cb, FFT_NO, FFT_NI), lambda i, j: (i, j, 0, 0)),
                  const(g1), const(w2), const(twr), const(twi)],
        out_specs=pl.BlockSpec((None, cb, FFT_NO, 2 * FFT_NI), lambda i, j: (i, j, 0, 0)),
        out_shape=jax.ShapeDtypeStruct((g, c, FFT_NO, 2 * FFT_NI), BF16),
        compiler_params=_cparams("parallel", "arbitrary"),
        name="hyena_spectrum",
    )(kfilt, g1, w2, twr, twi)


def _shift_conv3(x, w0, w1, w2, b):
    nrow, nlane = x.shape
    row = lax.broadcasted_iota(jnp.int32, x.shape, 0)
    lane = lax.broadcasted_iota(jnp.int32, x.shape, 1)
    r = pltpu.roll(x, 1, axis=1)
    prev = jnp.where(lane == 0, pltpu.roll(r, 1, axis=0), r)
    prev = jnp.where((lane == 0) & (row == 0), 0.0, prev)
    r = pltpu.roll(x, nlane - 1, axis=1)
    nxt = jnp.where(lane == nlane - 1, pltpu.roll(r, nrow - 1, axis=0), r)
    nxt = jnp.where((lane == nlane - 1) & (row == nrow - 1), 0.0, nxt)
    return prev * w0 + x * w1 + nxt * w2 + b


def _hyena_kernel(x_ref, kf_ref, pm_ref, g1_ref, w2_ref, w2i_ref, g4_ref, twr_ref, twi_ref, o_ref):
    cb = x_ref.shape[2]
    half = FFT_NO // 2
    ch = HY_CHUNK
    g1 = g1_ref[...]
    g4 = g4_ref[...]
    w2 = w2_ref[...]
    w2i = w2i_ref[...]
    twr = twr_ref[...]
    twi = twi_ref[...]
    lanes = lambda ci: slice(ci * FFT_NI, (ci + 1) * FFT_NI)
    krows = lambda ci: slice(ci * FFT_NO, (ci + 1) * FFT_NO)

    def body(j, _):
        c0 = pl.multiple_of(j * ch, ch)
        pm = pm_ref[:, pl.ds(c0, ch)]
        conv = []
        for part in range(3):
            halves = []
            for bb in range(2):
                xs = x_ref[bb, part, pl.ds(c0, ch)].astype(F32)
                halves.append(jnp.concatenate(
                    [_shift_conv3(xs[ci], pm[part, ci], pm[3 + part, ci], pm[6 + part, ci], pm[9 + part, ci])
                     for ci in range(ch)], axis=1))
            conv.append(jnp.concatenate(halves, axis=0))
        y = conv[2]
        for order in range(HY_ORDER):
            z = jnp.dot(_fft_stage1(g1, y.astype(BF16), twr, twi), w2, preferred_element_type=F32)
            kf = kf_ref[order, pl.ds(c0, ch)].reshape(ch * FFT_NO, 2 * FFT_NI)
            zr, zi = z[:, :FFT_NI].astype(BF16), z[:, FFT_NI:].astype(BF16)
            kr, ki = kf[:, :FFT_NI], kf[:, FFT_NI:]
            p = jnp.concatenate([zr * kr - zi * ki, zr * ki + zi * kr], axis=1)
            q = jnp.dot(p, w2i, preferred_element_type=F32).astype(BF16)
            qr = jnp.concatenate([q[krows(ci), :FFT_NI] for ci in range(ch)], axis=1)
            qi = jnp.concatenate([q[krows(ci), FFT_NI:] for ci in range(ch)], axis=1)
            qs = jnp.concatenate([qr * twr + qi * twi, qi * twr - qr * twi], axis=0)
            cv = jnp.dot(g4, qs, preferred_element_type=F32)
            bias = jnp.concatenate([pm[12 + order, ci] for ci in range(ch)], axis=1)
            y = conv[order] * (cv + bias * y)
        for ci in range(ch):
            o_ref[0, c0 + ci] = y[:half, lanes(ci)].astype(o_ref.dtype)
            o_ref[1, c0 + ci] = y[half:, lanes(ci)].astype(o_ref.dtype)
        return 0

    lax.fori_loop(0, cb // ch, body, 0)


def _hyena(xt, kf, layer, pm, tabs, cb):
    nb, _, c, half, _ = xt.shape
    const = lambda a: pl.BlockSpec(a.shape, lambda j, p: (0,) * a.ndim)
    bf = lambda name: jnp.asarray(tabs[name], BF16)
    g1, w2, w2i, g4 = bf("g1"), bf("w2"), bf("w2i"), bf("g4")
    twr, twi = _tiled_twiddles(tabs, HY_CHUNK)
    return pl.pallas_call(
        _hyena_kernel,
        grid=(c // cb, nb // 2),
        in_specs=[
            pl.BlockSpec((2, 3, cb, half, FFT_NI), lambda j, p: (p, 0, j, 0, 0)),
            pl.BlockSpec((None, HY_ORDER, cb, FFT_NO, 2 * FFT_NI), lambda j, p: (layer, 0, j, 0, 0)),
            pl.BlockSpec((14, cb, 1, FFT_NI), lambda j, p: (0, j, 0, 0)),
            const(g1), const(w2), const(w2i), const(g4), const(twr), const(twi),
        ],
        out_specs=pl.BlockSpec((2, cb, half, FFT_NI), lambda j, p: (p, j, 0, 0)),
        out_shape=jax.ShapeDtypeStruct((nb, c, half, FFT_NI), BF16),
        compiler_params=_cparams("parallel", "arbitrary"),
        name="hyena",
    )(xt, kf, pm, g1, w2, w2i, g4, twr, twi)


def _dft_tables(seq):
    n = 2 * seq
    wmat = np.exp(-2j * np.pi * np.outer(np.arange(n), np.arange(n)) / n)
    f32 = lambda x: np.ascontiguousarray(x, dtype=np.float32)
    fwd_real = f32(np.concatenate([wmat.real, wmat.imag], axis=1))
    ws = wmat[:seq]
    fwd = f32(np.block([[ws.real, ws.imag], [-ws.imag, ws.real]]))
    wi = np.conj(wmat)[:, :seq] / n
    inv = f32(np.block([[wi.real, wi.imag], [-wi.imag, wi.real]]))
    return dict(fwd_real=fwd_real, fwd=fwd, inv=inv)


def _rowdft_kernel(x_ref, w_ref, o_ref):
    o_ref[...] = jnp.dot(x_ref[...].astype(BF16), w_ref[...], preferred_element_type=F32)


def _rowdft(x, w, tr):
    m, k = x.shape
    n = w.shape[1]
    return pl.pallas_call(
        _rowdft_kernel,
        grid=(m // tr,),
        in_specs=[pl.BlockSpec((tr, k), lambda i: (i, 0)), pl.BlockSpec((k, n), lambda i: (0, 0))],
        out_specs=pl.BlockSpec((tr, n), lambda i: (i, 0)),
        out_shape=jax.ShapeDtypeStruct((m, n), F32),
        compiler_params=_cparams("parallel"),
        name="row_dft",
    )(x, w)


def _lane_conv3(x, w0, w1, w2, b):
    seq = x.shape[1]
    lane = lax.broadcasted_iota(jnp.int32, x.shape, 1)
    prev = jnp.where(lane == 0, 0.0, pltpu.roll(x, 1, axis=1))
    nxt = jnp.where(lane == seq - 1, 0.0, pltpu.roll(x, seq - 1, axis=1))
    return prev * w0 + x * w1 + nxt * w2 + b


def _hyena_ctx_kernel(x_ref, kf_ref, pm_ref, fwd_ref, inv_ref, o_ref):
    seq = x_ref.shape[3]
    n = 2 * seq
    conv = [[_lane_conv3(x_ref[bb, part].astype(F32), pm_ref[part], pm_ref[3 + part], pm_ref[6 + part],
                         pm_ref[9 + part]) for bb in range(2)] for part in range(3)]
    ya, yb = conv[2]
    for order in range(HY_ORDER):
        z = jnp.dot(jnp.concatenate([ya, yb], axis=1).astype(BF16), fwd_ref[...], preferred_element_type=F32)
        kf = kf_ref[order]
        zr, zi, kr, ki = z[:, :n], z[:, n:], kf[:, :n], kf[:, n:]
        p = jnp.concatenate([zr * kr - zi * ki, zr * ki + zi * kr], axis=1).astype(BF16)
        cv = jnp.dot(p, inv_ref[...], preferred_element_type=F32)
        bias = pm_ref[12 + order]
        ya = conv[order][0] * (cv[:, :seq] + bias * ya)
        yb = conv[order][1] * (cv[:, seq:] + bias * yb)
    o_ref[0] = ya.astype(o_ref.dtype)
    o_ref[1] = yb.astype(o_ref.dtype)


def _hyena_ctx(xt, kf, pm, tabs, cb):
    nb, _, c, seq = xt.shape
    fwd, inv = jnp.asarray(tabs["fwd"], BF16), jnp.asarray(tabs["inv"], BF16)
    const = lambda a: pl.BlockSpec(a.shape, lambda j, p: (0,) * a.ndim)
    return pl.pallas_call(
        _hyena_ctx_kernel,
        grid=(c // cb, nb // 2),
        in_specs=[
            pl.BlockSpec((2, 3, cb, seq), lambda j, p: (p, 0, j, 0)),
            pl.BlockSpec((HY_ORDER, cb, 4 * seq), lambda j, p: (0, j, 0)),
            pl.BlockSpec((14, cb, 1), lambda j, p: (0, j, 0)),
            const(fwd), const(inv),
        ],
        out_specs=pl.BlockSpec((2, cb, seq), lambda j, p: (p, j, 0)),
        out_shape=jax.ShapeDtypeStruct((nb, c, seq), BF16),
        compiler_params=_cparams("parallel", "arbitrary"),
        name="hyena_ctx",
    )(xt, kf, pm, fwd, inv)


def _rope_tables(seq):
    t = np.arange(seq)
    pos = np.stack([t // GRID_W, t % GRID_W], axis=1).astype(np.float64)
    n = RET_DK // 4
    inv = ROPE_BASE ** (-np.arange(n, dtype=np.float64) / n)
    lane = np.arange(LANES) % RET_DK
    ang = pos[:, lane // (2 * n)] * inv[lane % n][None]
    sign = np.where(lane % (2 * n) < n, -1.0, 1.0)
    return np.cos(ang).astype(np.float32), (np.sin(ang) * sign[None]).astype(np.float32)


def _hy_filters(seq, hy_f_w1, hy_f_b1, hy_f_w2, hy_f_b2, hy_f_w3, hy_f_freq):
    depth, _, hid = hy_f_w1.shape
    c = hy_f_w3.shape[2] // (2 * HY_ORDER)
    feat2, tt, mask = _hy_features(seq)
    w1t = jnp.pad(hy_f_w1.transpose(0, 2, 1), ((0, 0), (0, 0), (0, HY_FEAT_ROWS - HY_EMB)))
    h2 = _hy_mlp(jnp.asarray(feat2), w1t, hy_f_b1, hy_f_w2.transpose(0, 2, 1), hy_f_b2, hy_f_freq)
    w3t = hy_f_w3.reshape(depth, hid, HY_ORDER, 2, c).transpose(0, 2, 3, 4, 1)
    deltas = np.abs(np.linspace(math.log(HY_TARGET) / HY_FAST, math.log(HY_TARGET) / HY_SLOW, c))
    deltas = jnp.asarray(deltas.astype(np.float32).reshape(c, 1))
    cb = min(c, (1024 * 1024) // (2 * seq))
    return _hy_filter(h2, w3t, deltas, jnp.asarray(tt), jnp.asarray(mask), cb)


def _hy_params(hy_conv_w, hy_conv_b, hy_bias):
    c = hy_bias.shape[1]
    return jnp.concatenate([hy_conv_w.reshape(9, c), hy_conv_b.reshape(3, c), hy_bias], axis=0)


def kernel(x, c, ctx, c_ctx, w_mod, b_mod, g_norm1, g_norm2, w_in, hy_conv_w, hy_conv_b, hy_f_w1, hy_f_b1, hy_f_w2, hy_f_b2, hy_f_w3, hy_f_freq, hy_bias, ret_log_decay, na_rpb, w_branch, w_out, ffn_w_in, ffn_conv_w, ffn_conv_b, ffn_w_out, g_final):
    nb, seq, d = x.shape
    cseq = ctx.shape[1]
    depth = w_mod.shape[0]
    hy_w = hy_bias.shape[2]
    t_lat, t_ctx = nb * seq, nb * cseq
    assert seq % TM == 0 and t_ctx % TM == 0 and TM % cseq == 0 and nb % 2 == 0 and nb < SUBLANES
    assert 2 * seq == FFT_NO * FFT_NI and seq % GRID_W == 0 and cseq & (cseq - 1) == 0
    blk_lat, blk_all = t_lat // TM, (t_lat + t_ctx) // TM
    mrow = jnp.asarray(np.concatenate([np.repeat(np.arange(nb), seq // TM), np.full(t_ctx // TM, nb)]), jnp.int32)
    seqlen = jnp.asarray(np.concatenate([np.full(blk_lat, seq), np.full(t_ctx // TM, cseq)]), jnp.int32)
    per = TM // TR
    mrow_r = jnp.repeat(mrow, per)
    tn, tn_ffn = 1536, 1024

    cond = jnp.zeros((SUBLANES, d), F32).at[:nb].set(c).at[nb].set(c_ctx)
    mods = _mods(cond, w_mod, b_mod).reshape(depth, SUBLANES, 6, 1, d)

    cos, sin = (jnp.asarray(a) for a in _rope_tables(seq))
    tabs = _fft_tables()
    ctabs = _dft_tables(cseq)
    filt = _hy_filters(seq, hy_f_w1, hy_f_b1, hy_f_w2, hy_f_b2, hy_f_w3, hy_f_freq)
    spec = _hy_spectrum(filt.reshape(depth * HY_ORDER, hy_w, FFT_NO, FFT_NI), tabs, 32)
    spec = spec.reshape(depth, HY_ORDER, hy_w, FFT_NO, 2 * FFT_NI)
    cfilt = _hy_filters(cseq, hy_f_w1, hy_f_b1, hy_f_w2, hy_f_b2, hy_f_w3, hy_f_freq)
    cspec = _rowdft(cfilt.reshape(depth * HY_ORDER * hy_w, 2 * cseq), jnp.asarray(ctabs["fwd_real"], BF16), 1024)
    cspec = cspec.reshape(depth, HY_ORDER, hy_w, 4 * cseq)

    xs = jnp.concatenate([x.reshape(t_lat, d), ctx.reshape(t_ctx, d)], axis=0)
    assert hy_w == HY_W and w_in.shape[2] == COL_GATE + N_BRANCH * d
    hy_cols = COL_GATE - COL_HY
    s_zero = jnp.zeros((nb, RET_HEADS, 2, LANES, RET_DV), F32)
    hn = _normmod(xs, g_norm1[0], mods[0], mrow, 0, 1, blk_all)
    for l in range(depth):
        last = l == depth - 1
        nblk = blk_lat if last else blk_all
        rows = nblk * TM
        z = _matmul(hn, w_in, l, blk_all, tn, w_in.shape[2] - hy_cols, skip=(COL_HY // tn, hy_cols // tn))
        hy_proj = functools.partial(_matmul_nt, hn, w_in, l, col0=COL_HY, n_out=hy_cols, nb=nb, tc=tn)
        zh = hy_proj(blk0=0, nblk=blk_lat, seq=seq)

        y_ret, s_ctx = _retention(z, ret_log_decay[l], cos, sin, s_zero, nb=nb, seq=cseq, row0=t_lat, rope=False,
                                  out_rows=blk_all * TM)
        y_ret, _ = _retention(z, ret_log_decay[l], cos, sin, s_ctx, nb=nb, seq=seq, row0=0, rope=True,
                              out_rows=blk_all * TM, y_prev=y_ret)
        y_na = _na_attention(z, na_rpb[l], nb=nb, seq=seq, cseq=cseq, crow0=t_lat, out_rows=rows)
        pm = _hy_params(hy_conv_w[l], hy_conv_b[l], hy_bias[l])
        pm_lat = jnp.broadcast_to(pm[:, :, None, None], (14, hy_w, 1, FFT_NI))
        y_hy = _hyena(zh.reshape(nb, 3, hy_w, FFT_NO // 2, FFT_NI), spec, l, pm_lat, tabs, 32)
        y_hy = y_hy.reshape(nb, hy_w, seq).transpose(0, 2, 1).reshape(t_lat, hy_w)
        if not last:
            y_na = _dense_attention(z, y_na, nb=nb, cseq=cseq, crow0=t_lat)
            zc = hy_proj(blk0=blk_lat, nblk=t_ctx // TM, seq=cseq)
            yc_hy = _hyena_ctx(zc.reshape(nb, 3, hy_w, cseq), cspec[l], pm[:, :, None], ctabs, 256)
            y_hy = jnp.concatenate([y_hy, yc_hy.transpose(0, 2, 1).reshape(t_ctx, hy_w)], axis=0)
        acc = _merge((y_hy, y_ret, y_na), z, COL_HY, w_branch[l].astype(BF16), nblk * per, TR, d)
        xs, hn = _out_proj(acc, w_out[l].astype(BF16), xs, mods[l], mrow_r, g_norm2[l], nblk * per)
        u = _matmul(hn, ffn_w_in, l, nblk, tn_ffn, ffn_w_in.shape[2])
        hg = _ffn_gate(u, ffn_conv_w[l], ffn_conv_b[l], seqlen, nblk, 512)
        nxt = l if last else l + 1
        w_ffo = ffn_w_out[l].astype(BF16)
        if last:
            res = _ffn_out(hg, w_ffo, xs, mods[l], mods[l], g_final, mrow_r, nblk * per, TR, 512, True)
        else:
            res = _ffn_out(hg, w_ffo, xs, mods[l], mods[nxt], g_norm1[nxt], mrow, nblk, TM, 256, False)
        if not last:
            xs, hn = res
    return res[0].reshape(nb, seq, d)
```

```python
import functools
import math

import numpy as np
import jax
import jax.numpy as jnp
from jax import lax
from jax.experimental import pallas as pl
from jax.experimental.pallas import tpu as pltpu

F32 = jnp.float32
BF16 = jnp.bfloat16

GRID_W = 64
N_BRANCH = 3
HY_ORDER = 2
HY_EMB = 33
HY_FAST = 0.3
HY_SLOW = 1.5
HY_TARGET = 1e-2
RET_HEADS = 8
RET_DK = 64
RET_DV = 128
RET_CHUNK = 128
NA_HEADS = 8
NA_DH = 128
NA_WR = 8
NA_WC = 16
ROPE_BASE = 10000.0
EPS = 1e-6
NEG = -1e30
HY_W = 1024

_RET_QK_W, _RET_V_W, _NA_W = RET_HEADS * RET_DK, RET_HEADS * RET_DV, NA_HEADS * NA_DH
COL_RET_K = 0
COL_RET_V = COL_RET_K + _RET_QK_W
COL_NA_K = COL_RET_V + _RET_V_W
COL_NA_V = COL_NA_K + _NA_W
COL_RET_Q = COL_NA_V + _NA_W
COL_RET_G = COL_RET_Q + _RET_QK_W
COL_NA_Q = COL_RET_G + _RET_V_W
COL_HY = COL_NA_Q + _NA_W
COL_GATE = COL_HY + (HY_ORDER + 1) * HY_W

LANES = 128
SUBLANES = 8
VMEM_LIMIT = 56 * 1024 * 1024

TM = 1024


def _cparams(*sem):
    return pltpu.CompilerParams(dimension_semantics=sem, vmem_limit_bytes=VMEM_LIMIT)


def _mods_kernel(a_ref, w_ref, b_ref, o_ref):
    k = pl.program_id(1)
    a = a_ref[...]
    a = a * jax.nn.sigmoid(a)
    part = jnp.dot(a.astype(BF16), w_ref[...].astype(BF16), preferred_element_type=F32)

    @pl.when(k == 0)
    def _():
        o_ref[...] = part + b_ref[...]

    @pl.when(k > 0)
    def _():
        o_ref[...] += part


def _mods(cond, w_mod, b_mod):
    depth, d, n = w_mod.shape
    tk = LANES
    return pl.pallas_call(
        _mods_kernel,
        grid=(depth, d // tk),
        in_specs=[
            pl.BlockSpec((SUBLANES, tk), lambda l, k: (0, k)),
            pl.BlockSpec((None, tk, n), lambda l, k: (l, k, 0)),
            pl.BlockSpec((None, 1, n), lambda l, k: (l, 0, 0)),
        ],
        out_specs=pl.BlockSpec((None, SUBLANES, n), lambda l, k: (l, 0, 0)),
        out_shape=jax.ShapeDtypeStruct((depth, SUBLANES, n), F32),
        compiler_params=_cparams("parallel", "arbitrary"),
        name="mods",
    )(cond, w_mod, b_mod.reshape(depth, 1, n))


def _normmod_kernel(mrow_ref, x_ref, g_ref, sh_ref, sc_ref, o_ref):
    x = x_ref[...]
    y = x * lax.rsqrt(jnp.mean(x * x, axis=-1, keepdims=True) + EPS)
    y = y * g_ref[...]
    o_ref[...] = (y * (1.0 + sc_ref[...]) + sh_ref[...]).astype(o_ref.dtype)


def _normmod(x, g, mods, mrow, shift_idx, scale_idx, nblk):
    m, d = x.shape
    grid_spec = pltpu.PrefetchScalarGridSpec(
        num_scalar_prefetch=1,
        grid=(nblk,),
        in_specs=[
            pl.BlockSpec((TM, d), lambda i, mr: (i, 0)),
            pl.BlockSpec((1, d), lambda i, mr: (0, 0)),
            pl.BlockSpec((None, None, 1, d), lambda i, mr: (mr[i], shift_idx, 0, 0)),
            pl.BlockSpec((None, None, 1, d), lambda i, mr: (mr[i], scale_idx, 0, 0)),
        ],
        out_specs=pl.BlockSpec((TM, d), lambda i, mr: (i, 0)),
    )
    return pl.pallas_call(
        _normmod_kernel,
        grid_spec=grid_spec,
        out_shape=jax.ShapeDtypeStruct((m, d), BF16),
        compiler_params=_cparams("parallel"),
        name="normmod",
    )(mrow, x, g.reshape(1, d), mods, mods)


def _mm_kernel(a_ref, w_ref, o_ref, w_scr):
    @pl.when(pl.program_id(1) == 0)
    def _():
        w_scr[...] = w_ref[...].astype(BF16)

    o_ref[...] = jnp.dot(a_ref[...], w_scr[...], preferred_element_type=F32).astype(o_ref.dtype)


def _matmul(a, w, layer, nblk, tn, n_out, skip=None):
    m, k = a.shape
    col = (lambda j: j) if skip is None else (lambda j: j + jnp.where(j >= skip[0], skip[1], 0))
    return pl.pallas_call(
        _mm_kernel,
        grid=(n_out // tn, nblk),
        in_specs=[pl.BlockSpec((TM, k), lambda j, i: (i, 0)),
                  pl.BlockSpec((None, k, tn), lambda j, i: (layer, 0, col(j)))],
        out_specs=pl.BlockSpec((TM, tn), lambda j, i: (i, j)),
        out_shape=jax.ShapeDtypeStruct((m, n_out), BF16),
        scratch_shapes=[pltpu.VMEM((k, tn), BF16)],
        compiler_params=_cparams("arbitrary", "arbitrary"),
        name="matmul",
    )(a, w)


def _mm_nt_kernel(w_ref, a_ref, o_ref, wt_scr):
    @pl.when(pl.program_id(1) == 0)
    def _():
        wt_scr[...] = w_ref[...].T.astype(BF16)

    r = lax.dot_general(wt_scr[...], a_ref[...], (((1,), (1,)), ((), ())), preferred_element_type=F32)
    per = o_ref.shape[0]
    width = r.shape[1] // per
    for s in range(per):
        o_ref[s] = r[:, s * width:(s + 1) * width].astype(o_ref.dtype)


def _matmul_nt(a, w, layer, *, col0, n_out, blk0, nblk, nb, seq, tc):
    k = a.shape[1]
    per, sblk = max(TM // seq, 1), max(seq // TM, 1)
    return pl.pallas_call(
        _mm_nt_kernel,
        grid=(n_out // tc, nblk),
        in_specs=[pl.BlockSpec((None, k, tc), lambda j, i: (layer, 0, col0 // tc + j)),
                  pl.BlockSpec((TM, k), lambda j, i: (blk0 + i, 0))],
        out_specs=pl.BlockSpec((per, tc, TM // per), lambda j, i: (i // sblk, j, i % sblk)),
        out_shape=jax.ShapeDtypeStruct((nb, n_out, seq), BF16),
        scratch_shapes=[pltpu.VMEM((tc, k), BF16)],
        compiler_params=_cparams("arbitrary", "arbitrary"),
        name="matmul_nt",
    )(w, a)


TR = 512


def _norm_modulate(x, g, shift, scale):
    y = x * lax.rsqrt(jnp.mean(x * x, axis=-1, keepdims=True) + EPS)
    return (y * g) * (1.0 + scale) + shift


def _outproj_kernel(mrow_ref, a_ref, w_ref, x_ref, gate_ref, g_ref, sh_ref, sc_ref, x_out, h_out):
    y = jnp.dot(a_ref[...], w_ref[...], preferred_element_type=F32)
    x = x_ref[...] + gate_ref[...] * y
    x_out[...] = x
    h_out[...] = _norm_modulate(x, g_ref[...], sh_ref[...], sc_ref[...]).astype(h_out.dtype)


def _out_proj(a, w, x, mods, mrow, g_norm, nblk):
    m, k = a.shape
    d = w.shape[1]
    mod = lambda idx: pl.BlockSpec((None, None, 1, d), lambda i, mr: (mr[i], idx, 0, 0))
    row = pl.BlockSpec((TR, d), lambda i, mr: (i, 0))
    grid_spec = pltpu.PrefetchScalarGridSpec(
        num_scalar_prefetch=1,
        grid=(nblk,),
        in_specs=[pl.BlockSpec((TR, k), lambda i, mr: (i, 0)), pl.BlockSpec((k, d), lambda i, mr: (0, 0)), row,
                  mod(2), pl.BlockSpec((1, d), lambda i, mr: (0, 0)), mod(3), mod(4)],
        out_specs=[row, row],
    )
    return pl.pallas_call(
        _outproj_kernel,
        grid_spec=grid_spec,
        out_shape=[jax.ShapeDtypeStruct((m, d), F32), jax.ShapeDtypeStruct((m, d), BF16)],
        compiler_params=_cparams("parallel"),
        name="out_proj",
    )(mrow, a, w, x, mods, g_norm.reshape(1, d), mods, mods)


def _merge_kernel(y0_ref, y1_ref, y2_ref, g0_ref, g1_ref, g2_ref, w_ref, o_ref):
    acc = None
    for i, (y_ref, g_ref) in enumerate(((y0_ref, g0_ref), (y1_ref, g1_ref), (y2_ref, g2_ref))):
        t = jnp.dot(y_ref[...], w_ref[i], preferred_element_type=F32)
        t = jax.nn.sigmoid(g_ref[...].astype(F32)) * t
        acc = t if acc is None else acc + t
    o_ref[...] = acc.astype(o_ref.dtype)


def _merge(ys, z, gate_col0, w_branch, nblk, tr, tn):
    m, bw = ys[0].shape
    d = w_branch.shape[2]
    assert gate_col0 % tn == 0 and d % tn == 0
    gspec = lambda i_br: pl.BlockSpec((tr, tn), lambda i, j: (i, (gate_col0 + i_br * d) // tn + j))
    yspec = pl.BlockSpec((tr, bw), lambda i, j: (i, 0))
    return pl.pallas_call(
        _merge_kernel,
        grid=(nblk, d // tn),
        in_specs=[yspec, yspec, yspec, gspec(0), gspec(1), gspec(2),
                  pl.BlockSpec((N_BRANCH, bw, tn), lambda i, j: (0, 0, j))],
        out_specs=pl.BlockSpec((tr, tn), lambda i, j: (i, j)),
        out_shape=jax.ShapeDtypeStruct((m, d), BF16),
        compiler_params=_cparams("parallel", "arbitrary"),
        name="merge",
    )(ys[0], ys[1], ys[2], z, z, z, w_branch)


def _ffn_gate_kernel(seq_ref, a_ref, ap_ref, an_ref, b_ref, cw_ref, cb_ref, o_ref):
    i = pl.program_id(0)
    seq_m1 = seq_ref[i] - 1
    tm = a_ref.shape[0]
    row = lax.broadcasted_iota(jnp.int32, (tm, 1), 0)
    halo_p = jnp.where(((i * tm) & seq_m1) == 0, 0.0, ap_ref[SUBLANES - 1:SUBLANES, :].astype(F32))
    halo_n = jnp.where((((i + 1) * tm) & seq_m1) == 0, 0.0, an_ref[0:1, :].astype(F32))

    def finish(inner_edges):
        a = a_ref[...].astype(F32)
        prev = jnp.where(row == 0, halo_p, pltpu.roll(a, 1, axis=0))
        nxt = jnp.where(row == tm - 1, halo_n, pltpu.roll(a, tm - 1, axis=0))
        if inner_edges:
            pos = row & seq_m1
            prev = jnp.where((pos == 0) & (row > 0), 0.0, prev)
            nxt = jnp.where((pos == seq_m1) & (row < tm - 1), 0.0, nxt)
        cw = cw_ref[...]
        conv = prev * cw[0:1, :] + a * cw[1:2, :] + nxt * cw[2:3, :] + cb_ref[...]
        o_ref[...] = jax.nn.gelu(conv.astype(BF16)) * b_ref[...]

    @pl.when(seq_m1 >= tm - 1)
    def _():
        finish(False)

    @pl.when(seq_m1 < tm - 1)
    def _():
        finish(True)


def _ffn_gate(u, conv_w, conv_b, seqlen, nblk, tc):
    m, ff2 = u.shape
    ff = ff2 // 2
    nrow8 = m // SUBLANES
    r8 = TM // SUBLANES
    grid_spec = pltpu.PrefetchScalarGridSpec(
        num_scalar_prefetch=1,
        grid=(nblk, ff // tc),
        in_specs=[
            pl.BlockSpec((TM, tc), lambda i, j, s: (i, j)),
            pl.BlockSpec((SUBLANES, tc), lambda i, j, s: (jnp.maximum(i * r8 - 1, 0), j)),
            pl.BlockSpec((SUBLANES, tc), lambda i, j, s: (jnp.minimum((i + 1) * r8, nrow8 - 1), j)),
            pl.BlockSpec((TM, tc), lambda i, j, s: (i, ff // tc + j)),
            pl.BlockSpec((3, tc), lambda i, j, s: (0, j)),
            pl.BlockSpec((1, tc), lambda i, j, s: (0, j)),
        ],
        out_specs=pl.BlockSpec((TM, tc), lambda i, j, s: (i, j)),
    )
    return pl.pallas_call(
        _ffn_gate_kernel,
        grid_spec=grid_spec,
        out_shape=jax.ShapeDtypeStruct((m, ff), BF16),
        compiler_params=_cparams("parallel", "arbitrary"),
        name="ffn_gate",
    )(seqlen, u, u, u, u, conv_w, conv_b.reshape(1, ff))


def _ffn_out_kernel(mrow_ref, h_ref, w_ref, x_ref, gate_ref, g_ref, sh_ref, sc_ref, *rest, final):
    outs, xrow = rest[:-1], rest[-1]
    j = pl.program_id(1)
    y = jnp.dot(h_ref[...], w_ref[...], preferred_element_type=F32)
    xt = x_ref[...] + gate_ref[...] * y
    xrow[j] = xt
    if not final:
        outs[0][...] = xt

    @pl.when(j == pl.num_programs(1) - 1)
    def _():
        x = jnp.concatenate([xrow[t] for t in range(xrow.shape[0])], axis=1)
        if final:
            y = x * lax.rsqrt(jnp.mean(x * x, axis=-1, keepdims=True) + EPS)
            outs[0][...] = y * g_ref[...]
        else:
            outs[1][...] = _norm_modulate(x, g_ref[...], sh_ref[...], sc_ref[...]).astype(outs[1].dtype)


def _ffn_out(h, w, x, mods, mods_next, g_next, mrow, nblk, tr, tn, final):
    m, ff = h.shape
    d = w.shape[1]
    mod = lambda arr_idx: pl.BlockSpec((None, None, 1, d), lambda i, j, mr: (mr[i], arr_idx, 0, 0))
    row = pl.BlockSpec((tr, d), lambda i, j, mr: (i, 0))
    tile = pl.BlockSpec((tr, tn), lambda i, j, mr: (i, j))
    grid_spec = pltpu.PrefetchScalarGridSpec(
        num_scalar_prefetch=1,
        grid=(nblk, d // tn),
        in_specs=[
            pl.BlockSpec((tr, ff), lambda i, j, mr: (i, 0)),
            pl.BlockSpec((ff, tn), lambda i, j, mr: (0, j)),
            tile,
            pl.BlockSpec((None, None, 1, tn), lambda i, j, mr: (mr[i], 5, 0, j)),
            pl.BlockSpec((1, d), lambda i, j, mr: (0, 0)), mod(0), mod(1),
        ],
        out_specs=[row] if final else [tile, row],
        scratch_shapes=[pltpu.VMEM((d // tn, tr, tn), F32)],
    )
    out_shape = [jax.ShapeDtypeStruct((m, d), F32)] + ([] if final else [jax.ShapeDtypeStruct((m, d), BF16)])
    return pl.pallas_call(
        functools.partial(_ffn_out_kernel, final=final),
        grid_spec=grid_spec,
        out_shape=out_shape,
        compiler_params=_cparams("parallel", "arbitrary"),
        name="ffn_out",
    )(mrow, h, w, x, mods, g_next.reshape(1, d), mods_next, mods_next)


def _rope_pair(x, cos, sin):
    src = lax.broadcasted_iota(jnp.int32, (LANES, LANES), 0)
    dst = lax.broadcasted_iota(jnp.int32, (LANES, LANES), 1)
    perm = (src == jnp.where((dst & 31) < 16, dst + 16, dst - 16)).astype(BF16)
    partner = jnp.dot(x, perm, preferred_element_type=F32)
    return x.astype(F32) * cos + partner * sin


def _ret_kernel(lg_ref, q_ref, k_ref, v_ref, g_ref, cos_ref, sin_ref, s0_ref, *rest, rope, has_prev):
    o_ref, sfin_ref, qr_scr, kr_scr, kv_scr = rest[1:] if has_prev else rest
    hp = pl.program_id(1)
    seq = q_ref.shape[0]
    c = RET_CHUNK
    nchunk = seq // c
    ks = RET_DK ** -0.5
    if rope:
        q = _rope_pair(q_ref[...], cos_ref[...], sin_ref[...])
        k = _rope_pair(k_ref[...], cos_ref[...], sin_ref[...])
    else:
        q = q_ref[...].astype(F32)
        k = k_ref[...].astype(F32)
    qr_scr[...] = q
    kr_scr[...] = k * ks

    pos_r = lax.broadcasted_iota(jnp.int32, (c, 1), 0).astype(F32)
    rel = (lax.broadcasted_iota(jnp.int32, (c, c), 0) - lax.broadcasted_iota(jnp.int32, (c, c), 1)).astype(F32)
    lane = lax.broadcasted_iota(jnp.int32, (1, LANES), 1)

    heads = []
    for hh in range(2):
        lgf = lg_ref[0, 2 * hp + hh]
        lgb = lg_ref[1, 2 * hp + hh]
        heads.append(dict(
            hm=(lane // RET_DK == hh).astype(F32),
            zeta_f=jnp.exp(lgf * (c - 1 - pos_r)), zeta_b=jnp.exp(lgb * pos_r),
            xi_f=jnp.exp(lgf * (pos_r + 1.0)), xi_b=jnp.exp(lgb * (c - pos_r)),
            dmat=jnp.where(rel >= 0, jnp.exp(lgf * jnp.maximum(rel, 0.0)), jnp.exp(lgb * jnp.maximum(-rel, 0.0))),
            gf=jnp.exp(lgf * c), gb=jnp.exp(lgb * c),
            vcols=slice(hh * RET_DV, (hh + 1) * RET_DV)))

    def kv_body(n, _):
        rows = pl.ds(pl.multiple_of(n * c, c), c)
        kr = kr_scr[rows, :]
        for hh, hd in enumerate(heads):
            kh = kr * hd["hm"]
            kz = jnp.concatenate([kh * hd["zeta_f"], kh * hd["zeta_b"]], axis=1).astype(BF16)
            kv_scr[hh, n] = lax.dot_general(kz, v_ref[rows, hd["vcols"]], (((0,), (0,)), ((), ())),
                                            preferred_element_type=F32)
        return 0

    lax.fori_loop(0, nchunk, kv_body, 0, unroll=min(8, nchunk))

    def scan(j, carry):
        nf, nbk = j, nchunk - 1 - j
        out = []
        for hh, hd in enumerate(heads):
            sf, sb = carry[2 * hh], carry[2 * hh + 1]
            tf = kv_scr[hh, nf, 0:LANES, :]
            kv_scr[hh, nf, 0:LANES, :] = sf
            tb = kv_scr[hh, nbk, LANES:2 * LANES, :]
            kv_scr[hh, nbk, LANES:2 * LANES, :] = sb
            out += [hd["gf"] * sf + tf, hd["gb"] * sb + tb]
        return tuple(out)

    fin = lax.fori_loop(0, nchunk, scan, (s0_ref[0, 0], s0_ref[0, 1], s0_ref[1, 0], s0_ref[1, 1]))
    for hh in range(2):
        sfin_ref[hh, 0] = fin[2 * hh]
        sfin_ref[hh, 1] = fin[2 * hh + 1]

    def out_body(n, _):
        rows = pl.ds(pl.multiple_of(n * c, c), c)
        qr = qr_scr[rows, :]
        kb = kr_scr[rows, :].astype(BF16)
        for hh, hd in enumerate(heads):
            qm = qr * hd["hm"]
            a = lax.dot_general(qm.astype(BF16), kb, (((1,), (1,)), ((), ())),
                                preferred_element_type=F32) * hd["dmat"]
            o = jnp.dot(a.astype(BF16), v_ref[rows, hd["vcols"]], preferred_element_type=F32)
            qx = jnp.concatenate([qm * hd["xi_f"], qm * hd["xi_b"]], axis=1).astype(BF16)
            o = o + jnp.dot(qx, kv_scr[hh, n].astype(BF16), preferred_element_type=F32)
            o = o * lax.rsqrt(jnp.mean(o * o, axis=-1, keepdims=True) + EPS)
            g = g_ref[rows, hd["vcols"]].astype(F32)
            o_ref[rows, hd["vcols"]] = (g * jax.nn.sigmoid(g) * o).astype(o_ref.dtype)
        return 0

    lax.fori_loop(0, nchunk, out_body, 0, unroll=min(8, nchunk))


def _retention(z, log_decay, cos, sin, s0, *, nb, seq, row0, rope, out_rows, y_prev=None):
    rb0 = row0 // seq
    hpairs = RET_HEADS // 2
    kcol, qcol = COL_RET_K // LANES, COL_RET_Q // LANES
    vcol, gcol = COL_RET_V // (2 * RET_DV), COL_RET_G // (2 * RET_DV)
    has_prev = y_prev is not None
    kernel = functools.partial(_ret_kernel, rope=rope, has_prev=has_prev)
    in_specs = [
        pl.BlockSpec(memory_space=pltpu.SMEM),
        pl.BlockSpec((seq, LANES), lambda b, p: (rb0 + b, qcol + p)),
        pl.BlockSpec((seq, LANES), lambda b, p: (rb0 + b, kcol + p)),
        pl.BlockSpec((seq, 2 * RET_DV), lambda b, p: (rb0 + b, vcol + p)),
        pl.BlockSpec((seq, 2 * RET_DV), lambda b, p: (rb0 + b, gcol + p)),
        pl.BlockSpec((seq, LANES), lambda b, p: (0, 0)),
        pl.BlockSpec((seq, LANES), lambda b, p: (0, 0)),
        pl.BlockSpec((None, 2, 2, LANES, RET_DV), lambda b, p: (b, p, 0, 0, 0)),
    ]
    args = [log_decay, z, z, z, z, cos, sin, s0]
    if has_prev:
        in_specs.append(pl.BlockSpec(memory_space=pl.ANY))
        args.append(y_prev)
    return pl.pallas_call(
        kernel,
        grid=(nb, hpairs),
        in_specs=in_specs,
        out_specs=[
            pl.BlockSpec((seq, 2 * RET_DV), lambda b, p: (rb0 + b, p)),
            pl.BlockSpec((None, 2, 2, LANES, RET_DV), lambda b, p: (b, p, 0, 0, 0)),
        ],
        out_shape=[
            jax.ShapeDtypeStruct((out_rows, RET_HEADS * RET_DV), BF16),
            jax.ShapeDtypeStruct((nb, RET_HEADS, 2, LANES, RET_DV), F32),
        ],
        scratch_shapes=[
            pltpu.VMEM((seq, LANES), F32),
            pltpu.VMEM((seq, LANES), F32),
            pltpu.VMEM((2, seq // RET_CHUNK, 2 * LANES, RET_DV), F32),
        ],
        input_output_aliases={len(args) - 1: 0} if has_prev else {},
        compiler_params=_cparams("parallel", "arbitrary"),
        name="retention_rope" if rope else "retention",
    )(*args)


NA_GROUP = 4
NA_KROWS = 12
NA_OVERLAP = 4


def _na_kernel(rpb_ref, q_ref, k_ref, v_ref, kc_ref, vc_ref, o_ref, bias_scr):
    seq = q_ref.shape[0]
    nrow = seq // GRID_W
    ndr, ndc = 2 * NA_WR - 1, 2 * NA_WC - 1
    scale = NA_DH ** -0.5
    nt = (((1,), (1,)), ((), ()))
    h = pl.program_id(0)
    lane = lax.broadcasted_iota(jnp.int32, (GRID_W, LANES), 1)

    @pl.when(pl.program_id(1) == 0)
    def _():
        qi = lax.broadcasted_iota(jnp.int32, (GRID_W, LANES), 0)
        ki = lane & (GRID_W - 1)
        dc = ki - qi + (NA_WC - 1)
        cs = jnp.clip(qi - NA_WC // 2, 0, GRID_W - NA_WC)
        colmask = (ki >= cs) & (ki < cs + NA_WC)

        def toeplitz(d):
            t = jnp.zeros((GRID_W, LANES), F32)
            for j in range(ndc):
                t = jnp.where(dc == j, rpb_ref[h * (ndr * ndc) + d * ndc + j], t)
            return t

        prev = jnp.zeros((GRID_W, LANES), F32)
        for i in range(ndr + 1):
            nxt = toeplitz(i) if i < ndr else jnp.zeros((GRID_W, LANES), F32)
            bias_scr[i] = jnp.where(colmask, jnp.where(lane < GRID_W, prev, nxt), NEG)
            prev = nxt

    kc = kc_ref[...]
    vc = vc_ref[...]
    kwin = NA_KROWS * GRID_W

    half = NA_WR // 2
    ntile = kwin // LANES

    def group(r0, interior):
        us = r0 - half if interior else min(max(r0 - half, 0), nrow - NA_KROWS)
        if isinstance(r0, int):
            qrows = pl.ds(r0 * GRID_W, NA_GROUP * GRID_W)
            krows = pl.ds(us * GRID_W, kwin)
        else:
            qrows = pl.ds(pl.multiple_of(r0 * GRID_W, NA_GROUP * GRID_W), NA_GROUP * GRID_W)
            krows = pl.ds(pl.multiple_of(us * GRID_W, GRID_W), kwin)
        q = q_ref[qrows, :]
        s = lax.dot_general(q, k_ref[krows, :], nt, preferred_element_type=F32)
        sc = lax.dot_general(q, kc, nt, preferred_element_type=F32) * scale
        e_rows, ec_rows, den_rows = [], [], []
        for u in range(NA_GROUP):
            qr = slice(u * GRID_W, (u + 1) * GRID_W)
            tiles = []
            for m in range(ntile):
                if interior:
                    off, idx = 2 * m - u, 2 * m - u + half
                else:
                    r = r0 + u
                    off = us + 2 * m - min(max(r - half, 0), nrow - NA_WR)
                    idx = min(max(us + 2 * m - r + NA_WR, 0), ndr)
                ok_lo, ok_hi = 0 <= off < NA_WR, 0 <= off + 1 < NA_WR
                if not (ok_lo or ok_hi):
                    tiles.append(None)
                    continue
                tab = bias_scr[idx]
                if not (ok_lo and ok_hi):
                    tab = tab + jnp.where(lane < GRID_W, 0.0 if ok_lo else NEG, 0.0 if ok_hi else NEG)
                st = s[qr, m * LANES:(m + 1) * LANES]
                tiles.append(jnp.where(tab > 0.5 * NEG, st * scale + tab, NEG))
            live = [t for t in tiles if t is not None]
            tmax = live[0]
            for t in live[1:]:
                tmax = jnp.maximum(tmax, t)
            scu = sc[qr, :]
            mx = jnp.maximum(jnp.max(tmax, axis=-1, keepdims=True), jnp.max(scu, axis=-1, keepdims=True))
            etiles = [None if t is None else jnp.exp(t - mx) for t in tiles]
            ecu = jnp.exp(scu - mx)
            esum = None
            for t in etiles:
                if t is not None:
                    esum = t if esum is None else esum + t
            den_rows.append(jnp.sum(esum, axis=-1, keepdims=True) + jnp.sum(ecu, axis=-1, keepdims=True))
            zero = jnp.zeros((GRID_W, LANES), BF16)
            e_rows.append(jnp.concatenate([zero if t is None else t.astype(BF16) for t in etiles], axis=1))
            ec_rows.append(ecu.astype(BF16))
        o = jnp.dot(jnp.concatenate(e_rows, axis=0), v_ref[krows, :], preferred_element_type=F32)
        o = o + jnp.dot(jnp.concatenate(ec_rows, axis=0), vc, preferred_element_type=F32)
        o_ref[qrows, :] = (o / jnp.concatenate(den_rows, axis=0)).astype(o_ref.dtype)

    ngroup = nrow // NA_GROUP
    g_lo = -(-half // NA_GROUP)
    g_hi = min((nrow - NA_KROWS + half) // NA_GROUP, (nrow - NA_WR + half - NA_GROUP + 1) // NA_GROUP)
    for g in range(g_lo):
        group(g * NA_GROUP, False)

    def body(i, _):
        for t in range(NA_OVERLAP):
            group((g_lo + NA_OVERLAP * i + t) * NA_GROUP, True)
        return 0

    nloop = (g_hi + 1 - g_lo) // NA_OVERLAP
    lax.fori_loop(0, nloop, body, 0)
    for g in range(g_lo + NA_OVERLAP * nloop, g_hi + 1):
        group(g * NA_GROUP, True)
    for g in range(g_hi + 1, ngroup):
        group(g * NA_GROUP, False)


def _na_attention(z, rpb, *, nb, seq, cseq, crow0, out_rows):
    kcol, vcol, qcol = COL_NA_K // NA_DH, COL_NA_V // NA_DH, COL_NA_Q // NA_DH
    crb0 = crow0 // cseq
    nrow = seq // GRID_W
    assert nrow >= NA_KROWS and nrow % NA_GROUP == 0 and GRID_W * 2 == LANES
    assert NA_KROWS % 2 == 0 and NA_KROWS >= NA_WR + NA_GROUP - 1
    return pl.pallas_call(
        _na_kernel,
        grid=(NA_HEADS, nb),
        in_specs=[
            pl.BlockSpec(memory_space=pltpu.SMEM),
            pl.BlockSpec((seq, NA_DH), lambda h, b: (b, qcol + h)),
            pl.BlockSpec((seq, NA_DH), lambda h, b: (b, kcol + h)),
            pl.BlockSpec((seq, NA_DH), lambda h, b: (b, vcol + h)),
            pl.BlockSpec((cseq, NA_DH), lambda h, b: (crb0 + b, kcol + h)),
            pl.BlockSpec((cseq, NA_DH), lambda h, b: (crb0 + b, vcol + h)),
        ],
        out_specs=pl.BlockSpec((seq, NA_DH), lambda h, b: (b, h)),
        out_shape=jax.ShapeDtypeStruct((out_rows, NA_HEADS * NA_DH), BF16),
        scratch_shapes=[pltpu.VMEM((2 * NA_WR, GRID_W, LANES), F32)],
        compiler_params=_cparams("arbitrary", "arbitrary"),
        name="na_attention",
    )(rpb.reshape(-1), z, z, z, z, z)


def _dense_attn_kernel(q_ref, k_ref, v_ref, yprev_ref, o_ref):
    s = lax.dot_general(q_ref[...], k_ref[...], (((1,), (1,)), ((), ())), preferred_element_type=F32) * NA_DH ** -0.5
    e = jnp.exp(s - jnp.max(s, axis=-1, keepdims=True))
    o = jnp.dot(e.astype(BF16), v_ref[...], preferred_element_type=F32)
    o_ref[...] = (o / jnp.sum(e, axis=-1, keepdims=True)).astype(o_ref.dtype)


def _dense_attention(z, y_prev, *, nb, cseq, crow0):
    kcol, vcol, qcol = COL_NA_K // NA_DH, COL_NA_V // NA_DH, COL_NA_Q // NA_DH
    crb0 = crow0 // cseq
    return pl.pallas_call(
        _dense_attn_kernel,
        grid=(nb, NA_HEADS),
        in_specs=[
            pl.BlockSpec((cseq, NA_DH), lambda b, h: (crb0 + b, qcol + h)),
            pl.BlockSpec((cseq, NA_DH), lambda b, h: (crb0 + b, kcol + h)),
            pl.BlockSpec((cseq, NA_DH), lambda b, h: (crb0 + b, vcol + h)),
            pl.BlockSpec(memory_space=pl.ANY),
        ],
        out_specs=pl.BlockSpec((cseq, NA_DH), lambda b, h: (crb0 + b, h)),
        out_shape=jax.ShapeDtypeStruct(y_prev.shape, BF16),
        input_output_aliases={3: 0},
        compiler_params=_cparams("parallel", "arbitrary"),
        name="dense_attention",
    )(z, z, z, y_prev)


HY_FEAT_ROWS = 64
HIGHEST = lax.Precision.HIGHEST


def _hy_features(seq):
    t = np.linspace(0.0, 1.0, seq)
    bands = (HY_EMB - 1) // 2
    w = 2.0 * math.pi * np.arange(seq) / seq
    fr = np.linspace(1e-4, bands - 1, bands)
    ang = fr[None] * w[:, None]
    feat = np.concatenate([t[:, None], np.cos(ang), -np.sin(ang)], axis=-1)
    src = np.concatenate([np.arange(seq), np.zeros(1, np.int64), np.arange(seq - 1, 0, -1)])
    feat2 = np.zeros((HY_FEAT_ROWS, 2 * seq), np.float32)
    feat2[:HY_EMB] = feat[src].T
    tt = t[src][None].astype(np.float32)
    mask = np.ones((1, 2 * seq), np.float32)
    mask[0, seq] = 0.0
    return feat2, tt, mask


def _hy_mlp_kernel(feat_ref, w1_ref, b1_ref, w2_ref, b2_ref, fq_ref, o_ref):
    fq = fq_ref[...]
    h = jnp.dot(w1_ref[...], feat_ref[...], preferred_element_type=F32, precision=HIGHEST)
    h = jnp.sin(fq * (h + b1_ref[...]))
    h = jnp.dot(w2_ref[...], h, preferred_element_type=F32, precision=HIGHEST)
    o_ref[...] = jnp.sin(fq * (h + b2_ref[...]))


def _hy_mlp(feat2, w1t, b1, w2t, b2, fq):
    depth, hid, _ = w1t.shape
    n = feat2.shape[1]
    col = lambda a: a.reshape(depth, hid, 1)
    wspec = lambda k: pl.BlockSpec((None, hid, k), lambda l: (l, 0, 0))
    return pl.pallas_call(
        _hy_mlp_kernel,
        grid=(depth,),
        in_specs=[pl.BlockSpec((HY_FEAT_ROWS, n), lambda l: (0, 0)), wspec(HY_FEAT_ROWS), wspec(1), wspec(hid),
                  wspec(1), wspec(1)],
        out_specs=pl.BlockSpec((None, hid, n), lambda l: (l, 0, 0)),
        out_shape=jax.ShapeDtypeStruct((depth, hid, n), F32),
        compiler_params=_cparams("parallel"),
        name="hyena_filter_mlp",
    )(feat2, w1t, col(b1), w2t, col(b2), col(fq))


def _hy_filter_kernel(h_ref, w3_ref, dl_ref, tt_ref, mask_ref, o_ref):
    seq = h_ref.shape[1] // 2
    kf = jnp.dot(w3_ref[0], h_ref[:, :seq], preferred_element_type=F32, precision=HIGHEST)
    kb = jnp.dot(w3_ref[1], h_ref[:, seq:], preferred_element_type=F32, precision=HIGHEST)
    k = jnp.concatenate([kf, kb], axis=1) * (jnp.exp(-tt_ref[...] * dl_ref[...]) * mask_ref[...])
    o_ref[...] = (k * lax.rsqrt(jnp.sum(k * k, axis=1, keepdims=True) + EPS)).astype(o_ref.dtype)


def _hy_filter(h2, w3t, deltas, tt, mask, cb):
    depth, hid, n = h2.shape
    c = w3t.shape[3]
    return pl.pallas_call(
        _hy_filter_kernel,
        grid=(depth, HY_ORDER, c // cb),
        in_specs=[
            pl.BlockSpec((None, hid, n), lambda l, o, j: (l, 0, 0)),
            pl.BlockSpec((None, None, 2, cb, hid), lambda l, o, j: (l, o, 0, j, 0)),
            pl.BlockSpec((cb, 1), lambda l, o, j: (j, 0)),
            pl.BlockSpec((1, n), lambda l, o, j: (0, 0)),
            pl.BlockSpec((1, n), lambda l, o, j: (0, 0)),
        ],
        out_specs=pl.BlockSpec((None, None, cb, n), lambda l, o, j: (l, o, j, 0)),
        out_shape=jax.ShapeDtypeStruct((depth, HY_ORDER, c, n), BF16),
        compiler_params=_cparams("parallel", "parallel", "arbitrary"),
        name="hyena_filter",
    )(h2, w3t, deltas, tt, mask)


FFT_NO = 64
FFT_NI = 128


def _fft_tables():
    n = FFT_NO * FFT_NI
    a = np.arange(FFT_NO)
    fo = np.exp(-2j * np.pi * np.outer(a, a) / FFT_NO)
    i = np.arange(FFT_NI)
    ci = np.exp(-2j * np.pi * np.outer(i, i) / FFT_NI)
    tw = np.exp(-2j * np.pi * np.outer(a, i) / n)
    half = FFT_NO // 2
    f32 = lambda x: np.ascontiguousarray(x, dtype=np.float32)
    g1_real = f32(np.concatenate([fo.real, fo.imag], axis=0))
    g1 = f32(np.block([[fo.real[:, :half], -fo.imag[:, :half]], [fo.imag[:, :half], fo.real[:, :half]]]))
    w2 = f32(np.block([[ci.real, ci.imag], [-ci.imag, ci.real]]))
    w2i = f32(np.block([[ci.real, -ci.imag], [ci.imag, ci.real]]))
    g4 = f32(np.block([[fo.real[:half], fo.imag[:half]], [-fo.imag[:half], fo.real[:half]]]) / n)
    return dict(g1_real=g1_real, g1=g1, w2=w2, w2i=w2i, g4=g4, twr=f32(tw.real), twi=f32(tw.imag))


HY_CHUNK = 16
HY_SPEC_CHUNK = 32


def _fft_stage1(g1, y2, twr, twi):
    nch = y2.shape[1] // FFT_NI
    o1 = jnp.dot(g1, y2, preferred_element_type=F32)
    yr, yi = o1[:FFT_NO].astype(BF16), o1[FFT_NO:].astype(BF16)
    ar = yr * twr - yi * twi
    ai = yr * twi + yi * twr
    lanes = lambda ci: slice(ci * FFT_NI, (ci + 1) * FFT_NI)
    return jnp.concatenate([jnp.concatenate([ar[:, lanes(ci)], ai[:, lanes(ci)]], axis=1)
                            for ci in range(nch)], axis=0)


def _hy_spectrum_kernel(k_ref, g1_ref, w2_ref, twr_ref, twi_ref, o_ref):
    cb = k_ref.shape[0]
    ch = HY_SPEC_CHUNK
    g1 = g1_ref[...]
    w2 = w2_ref[...]
    twr = twr_ref[...]
    twi = twi_ref[...]

    def body(j, _):
        chans = pl.ds(pl.multiple_of(j * ch, ch), ch)
        ks = k_ref[chans]
        k2 = jnp.concatenate([ks[ci] for ci in range(ch)], axis=1).astype(BF16)
        z = jnp.dot(_fft_stage1(g1, k2, twr, twi), w2, preferred_element_type=F32)
        o_ref[chans] = z.reshape(ch, FFT_NO, 2 * FFT_NI).astype(o_ref.dtype)
        return 0

    lax.fori_loop(0, cb // ch, body, 0)


def _tiled_twiddles(tabs, nch):
    tile = lambda a: jnp.asarray(np.tile(a, (1, nch)), BF16)
    return tile(tabs["twr"]), tile(tabs["twi"])


def _hy_spectrum(kfilt, tabs, cb):
    g, c = kfilt.shape[:2]
    const = lambda a: pl.BlockSpec(a.shape, lambda i, j: (0,) * a.ndim)
    g1 = jnp.asarray(tabs["g1_real"], BF16)
    w2 = jnp.asarray(tabs["w2"], BF16)
    twr, twi = _tiled_twiddles(tabs, HY_SPEC_CHUNK)
    return pl.pallas_call(
        _hy_spectrum_kernel,
        grid=(g, c // cb),
        in_specs=[pl.BlockSpec((None, cb, FFT_NO, FFT_NI), lambda i, j: (i, j, 0, 0)),
                  const(g1), const(w2), const(twr), const(twi)],
        out_specs=pl.BlockSpec((None, cb, FFT_NO, 2 * FFT_NI), lambda i, j: (i, j, 0, 0)),
        out_shape=jax.ShapeDtypeStruct((g, c, FFT_NO, 2 * FFT_NI), BF16),
        compiler_params=_cparams("parallel", "arbitrary"),
        name="hyena_spectrum",
    )(kfilt, g1, w2, twr, twi)


def _shift_conv3(x, w0, w1, w2, b):
    nrow, nlane = x.shape
    row = lax.broadcasted_iota(jnp.int32, x.shape, 0)
    lane = lax.broadcasted_iota(jnp.int32, x.shape, 1)
    r = pltpu.roll(x, 1, axis=1)
    prev = jnp.where(lane == 0, pltpu.roll(r, 1, axis=0), r)
    prev = jnp.where((lane == 0) & (row == 0), 0.0, prev)
    r = pltpu.roll(x, nlane - 1, axis=1)
    nxt = jnp.where(lane == nlane - 1, pltpu.roll(r, nrow - 1, axis=0), r)
    nxt = jnp.where((lane == nlane - 1) & (row == nrow - 1), 0.0, nxt)
    return prev * w0 + x * w1 + nxt * w2 + b


def _hyena_kernel(x_ref, kf_ref, pm_ref, g1_ref, w2_ref, w2i_ref, g4_ref, twr_ref, twi_ref, o_ref):
    cb = x_ref.shape[2]
    half = FFT_NO // 2
    ch = HY_CHUNK
    g1 = g1_ref[...]
    g4 = g4_ref[...]
    w2 = w2_ref[...]
    w2i = w2i_ref[...]
    twr = twr_ref[...]
    twi = twi_ref[...]
    lanes = lambda ci: slice(ci * FFT_NI, (ci + 1) * FFT_NI)
    krows = lambda ci: slice(ci * FFT_NO, (ci + 1) * FFT_NO)

    def body(j, _):
        c0 = pl.multiple_of(j * ch, ch)
        pm = pm_ref[:, pl.ds(c0, ch)]
        conv = []
        for part in range(3):
            halves = []
            for bb in range(2):
                xs = x_ref[bb, part, pl.ds(c0, ch)].astype(F32)
                halves.append(jnp.concatenate(
                    [_shift_conv3(xs[ci], pm[part, ci], pm[3 + part, ci], pm[6 + part, ci], pm[9 + part, ci])
                     for ci in range(ch)], axis=1))
            conv.append(jnp.concatenate(halves, axis=0))
        y = conv[2]
        for order in range(HY_ORDER):
            z = jnp.dot(_fft_stage1(g1, y.astype(BF16), twr, twi), w2, preferred_element_type=F32)
            kf = kf_ref[order, pl.ds(c0, ch)].reshape(ch * FFT_NO, 2 * FFT_NI)
            zr, zi = z[:, :FFT_NI].astype(BF16), z[:, FFT_NI:].astype(BF16)
            kr, ki = kf[:, :FFT_NI], kf[:, FFT_NI:]
            p = jnp.concatenate([zr * kr - zi * ki, zr * ki + zi * kr], axis=1)
            q = jnp.dot(p, w2i, preferred_element_type=F32).astype(BF16)
            qr = jnp.concatenate([q[krows(ci), :FFT_NI] for ci in range(ch)], axis=1)
            qi = jnp.concatenate([q[krows(ci), FFT_NI:] for ci in range(ch)], axis=1)
            qs = jnp.concatenate([qr * twr + qi * twi, qi * twr - qr * twi], axis=0)
            cv = jnp.dot(g4, qs, preferred_element_type=F32)
            bias = jnp.concatenate([pm[12 + order, ci] for ci in range(ch)], axis=1)
            y = conv[order] * (cv + bias * y)
        for ci in range(ch):
            o_ref[0, c0 + ci] = y[:half, lanes(ci)].astype(o_ref.dtype)
            o_ref[1, c0 + ci] = y[half:, lanes(ci)].astype(o_ref.dtype)
        return 0

    lax.fori_loop(0, cb // ch, body, 0)


def _hyena(xt, kf, layer, pm, tabs, cb):
    nb, _, c, half, _ = xt.shape
    const = lambda a: pl.BlockSpec(a.shape, lambda j, p: (0,) * a.ndim)
    bf = lambda name: jnp.asarray(tabs[name], BF16)
    g1, w2, w2i, g4 = bf("g1"), bf("w2"), bf("w2i"), bf("g4")
    twr, twi = _tiled_twiddles(tabs, HY_CHUNK)
    return pl.pallas_call(
        _hyena_kernel,
        grid=(c // cb, nb // 2),
        in_specs=[
            pl.BlockSpec((2, 3, cb, half, FFT_NI), lambda j, p: (p, 0, j, 0, 0)),
            pl.BlockSpec((None, HY_ORDER, cb, FFT_NO, 2 * FFT_NI), lambda j, p: (layer, 0, j, 0, 0)),
            pl.BlockSpec((14, cb, 1, FFT_NI), lambda j, p: (0, j, 0, 0)),
            const(g1), const(w2), const(w2i), const(g4), const(twr), const(twi),
        ],
        out_specs=pl.BlockSpec((2, cb, half, FFT_NI), lambda j, p: (p, j, 0, 0)),
        out_shape=jax.ShapeDtypeStruct((nb, c, half, FFT_NI), BF16),
        compiler_params=_cparams("parallel", "arbitrary"),
        name="hyena",
    )(xt, kf, pm, g1, w2, w2i, g4, twr, twi)


def _dft_tables(seq):
    n = 2 * seq
    wmat = np.exp(-2j * np.pi * np.outer(np.arange(n), np.arange(n)) / n)
    f32 = lambda x: np.ascontiguousarray(x, dtype=np.float32)
    fwd_real = f32(np.concatenate([wmat.real, wmat.imag], axis=1))
    ws = wmat[:seq]
    fwd = f32(np.block([[ws.real, ws.imag], [-ws.imag, ws.real]]))
    wi = np.conj(wmat)[:, :seq] / n
    inv = f32(np.block([[wi.real, wi.imag], [-wi.imag, wi.real]]))
    return dict(fwd_real=fwd_real, fwd=fwd, inv=inv)


def _rowdft_kernel(x_ref, w_ref, o_ref):
    o_ref[...] = jnp.dot(x_ref[...].astype(BF16), w_ref[...], preferred_element_type=F32)


def _rowdft(x, w, tr):
    m, k = x.shape
    n = w.shape[1]
    return pl.pallas_call(
        _rowdft_kernel,
        grid=(m // tr,),
        in_specs=[pl.BlockSpec((tr, k), lambda i: (i, 0)), pl.BlockSpec((k, n), lambda i: (0, 0))],
        out_specs=pl.BlockSpec((tr, n), lambda i: (i, 0)),
        out_shape=jax.ShapeDtypeStruct((m, n), F32),
        compiler_params=_cparams("parallel"),
        name="row_dft",
    )(x, w)


def _lane_conv3(x, w0, w1, w2, b):
    seq = x.shape[1]
    lane = lax.broadcasted_iota(jnp.int32, x.shape, 1)
    prev = jnp.where(lane == 0, 0.0, pltpu.roll(x, 1, axis=1))
    nxt = jnp.where(lane == seq - 1, 0.0, pltpu.roll(x, seq - 1, axis=1))
    return prev * w0 + x * w1 + nxt * w2 + b


def _hyena_ctx_kernel(x_ref, kf_ref, pm_ref, fwd_ref, inv_ref, o_ref):
    seq = x_ref.shape[3]
    n = 2 * seq
    conv = [[_lane_conv3(x_ref[bb, part].astype(F32), pm_ref[part], pm_ref[3 + part], pm_ref[6 + part],
                         pm_ref[9 + part]) for bb in range(2)] for part in range(3)]
    ya, yb = conv[2]
    for order in range(HY_ORDER):
        z = jnp.dot(jnp.concatenate([ya, yb], axis=1).astype(BF16), fwd_ref[...], preferred_element_type=F32)
        kf = kf_ref[order]
        zr, zi, kr, ki = z[:, :n], z[:, n:], kf[:, :n], kf[:, n:]
        p = jnp.concatenate([zr * kr - zi * ki, zr * ki + zi * kr], axis=1).astype(BF16)
        cv = jnp.dot(p, inv_ref[...], preferred_element_type=F32)
        bias = pm_ref[12 + order]
        ya = conv[order][0] * (cv[:, :seq] + bias * ya)
        yb = conv[order][1] * (cv[:, seq:] + bias * yb)
    o_ref[0] = ya.astype(o_ref.dtype)
    o_ref[1] = yb.astype(o_ref.dtype)


def _hyena_ctx(xt, kf, pm, tabs, cb):
    nb, _, c, seq = xt.shape
    fwd, inv = jnp.asarray(tabs["fwd"], BF16), jnp.asarray(tabs["inv"], BF16)
    const = lambda a: pl.BlockSpec(a.shape, lambda j, p: (0,) * a.ndim)
    return pl.pallas_call(
        _hyena_ctx_kernel,
        grid=(c // cb, nb // 2),
        in_specs=[
            pl.BlockSpec((2, 3, cb, seq), lambda j, p: (p, 0, j, 0)),
            pl.BlockSpec((HY_ORDER, cb, 4 * seq), lambda j, p: (0, j, 0)),
            pl.BlockSpec((14, cb, 1), lambda j, p: (0, j, 0)),
            const(fwd), const(inv),
        ],
        out_specs=pl.BlockSpec((2, cb, seq), lambda j, p: (p, j, 0)),
        out_shape=jax.ShapeDtypeStruct((nb, c, seq), BF16),
        compiler_params=_cparams("parallel", "arbitrary"),
        name="hyena_ctx",
    )(xt, kf, pm, fwd, inv)


def _rope_tables(seq):
    t = np.arange(seq)
    pos = np.stack([t // GRID_W, t % GRID_W], axis=1).astype(np.float64)
    n = RET_DK // 4
    inv = ROPE_BASE ** (-np.arange(n, dtype=np.float64) / n)
    lane = np.arange(LANES) % RET_DK
    ang = pos[:, lane // (2 * n)] * inv[lane % n][None]
    sign = np.where(lane % (2 * n) < n, -1.0, 1.0)
    return np.cos(ang).astype(np.float32), (np.sin(ang) * sign[None]).astype(np.float32)


def _hy_filters(seq, hy_f_w1, hy_f_b1, hy_f_w2, hy_f_b2, hy_f_w3, hy_f_freq):
    depth, _, hid = hy_f_w1.shape
    c = hy_f_w3.shape[2] // (2 * HY_ORDER)
    feat2, tt, mask = _hy_features(seq)
    w1t = jnp.pad(hy_f_w1.transpose(0, 2, 1), ((0, 0), (0, 0), (0, HY_FEAT_ROWS - HY_EMB)))
    h2 = _hy_mlp(jnp.asarray(feat2), w1t, hy_f_b1, hy_f_w2.transpose(0, 2, 1), hy_f_b2, hy_f_freq)
    w3t = hy_f_w3.reshape(depth, hid, HY_ORDER, 2, c).transpose(0, 2, 3, 4, 1)
    deltas = np.abs(np.linspace(math.log(HY_TARGET) / HY_FAST, math.log(HY_TARGET) / HY_SLOW, c))
    deltas = jnp.asarray(deltas.astype(np.float32).reshape(c, 1))
    cb = min(c, (1024 * 1024) // (2 * seq))
    return _hy_filter(h2, w3t, deltas, jnp.asarray(tt), jnp.asarray(mask), cb)


def _hy_params(hy_conv_w, hy_conv_b, hy_bias):
    c = hy_bias.shape[1]
    return jnp.concatenate([hy_conv_w.reshape(9, c), hy_conv_b.reshape(3, c), hy_bias], axis=0)


def kernel(x, c, ctx, c_ctx, w_mod, b_mod, g_norm1, g_norm2, w_in, hy_conv_w, hy_conv_b, hy_f_w1, hy_f_b1, hy_f_w2, hy_f_b2, hy_f_w3, hy_f_freq, hy_bias, ret_log_decay, na_rpb, w_branch, w_out, ffn_w_in, ffn_conv_w, ffn_conv_b, ffn_w_out, g_final):
    nb, seq, d = x.shape
    cseq = ctx.shape[1]
    depth = w_mod.shape[0]
    hy_w = hy_bias.shape[2]
    t_lat, t_ctx = nb * seq, nb * cseq
    assert seq % TM == 0 and t_ctx % TM == 0 and TM % cseq == 0 and nb % 2 == 0 and nb < SUBLANES
    assert 2 * seq == FFT_NO * FFT_NI and seq % GRID_W == 0 and cseq & (cseq - 1) == 0
    blk_lat, blk_all = t_lat // TM, (t_lat + t_ctx) // TM
    mrow = jnp.asarray(np.concatenate([np.repeat(np.arange(nb), seq // TM), np.full(t_ctx // TM, nb)]), jnp.int32)
    seqlen = jnp.asarray(np.concatenate([np.full(blk_lat, seq), np.full(t_ctx // TM, cseq)]), jnp.int32)
    per = TM // TR
    mrow_r = jnp.repeat(mrow, per)
    tn, tn_ffn = 1536, 1024

    cond = jnp.zeros((SUBLANES, d), F32).at[:nb].set(c).at[nb].set(c_ctx)
    mods = _mods(cond, w_mod, b_mod).reshape(depth, SUBLANES, 6, 1, d)

    cos, sin = (jnp.asarray(a) for a in _rope_tables(seq))
    tabs = _fft_tables()
    ctabs = _dft_tables(cseq)
    filt = _hy_filters(seq, hy_f_w1, hy_f_b1, hy_f_w2, hy_f_b2, hy_f_w3, hy_f_freq)
    spec = _hy_spectrum(filt.reshape(depth * HY_ORDER, hy_w, FFT_NO, FFT_NI), tabs, 32)
    spec = spec.reshape(depth, HY_ORDER, hy_w, FFT_NO, 2 * FFT_NI)
    cfilt = _hy_filters(cseq, hy_f_w1, hy_f_b1, hy_f_w2, hy_f_b2, hy_f_w3, hy_f_freq)
    cspec = _rowdft(cfilt.reshape(depth * HY_ORDER * hy_w, 2 * cseq), jnp.asarray(ctabs["fwd_real"], BF16), 1024)
    cspec = cspec.reshape(depth, HY_ORDER, hy_w, 4 * cseq)

    xs = jnp.concatenate([x.reshape(t_lat, d), ctx.reshape(t_ctx, d)], axis=0)
    assert hy_w == HY_W and w_in.shape[2] == COL_GATE + N_BRANCH * d
    hy_cols = COL_GATE - COL_HY
    s_zero = jnp.zeros((nb, RET_HEADS, 2, LANES, RET_DV), F32)
    hn = _normmod(xs, g_norm1[0], mods[0], mrow, 0, 1, blk_all)
    for l in range(depth):
        last = l == depth - 1
        nblk = blk_lat if last else blk_all
        rows = nblk * TM
        z = _matmul(hn, w_in, l, blk_all, tn, w_in.shape[2] - hy_cols, skip=(COL_HY // tn, hy_cols // tn))
        hy_proj = functools.partial(_matmul_nt, hn, w_in, l, col0=COL_HY, n_out=hy_cols, nb=nb, tc=tn)
        zh = hy_proj(blk0=0, nblk=blk_lat, seq=seq)

        y_ret, s_ctx = _retention(z, ret_log_decay[l], cos, sin, s_zero, nb=nb, seq=cseq, row0=t_lat, rope=False,
                                  out_rows=blk_all * TM)
        y_ret, _ = _retention(z, ret_log_decay[l], cos, sin, s_ctx, nb=nb, seq=seq, row0=0, rope=True,
                              out_rows=blk_all * TM, y_prev=y_ret)
        y_na = _na_attention(z, na_rpb[l], nb=nb, seq=seq, cseq=cseq, crow0=t_lat, out_rows=rows)
        pm = _hy_params(hy_conv_w[l], hy_conv_b[l], hy_bias[l])
        pm_lat = jnp.broadcast_to(pm[:, :, None, None], (14, hy_w, 1, FFT_NI))
        y_hy = _hyena(zh.reshape(nb, 3, hy_w, FFT_NO // 2, FFT_NI), spec, l, pm_lat, tabs, 32)
        y_hy = y_hy.reshape(nb, hy_w, seq).transpose(0, 2, 1).reshape(t_lat, hy_w)
        if not last:
            y_na = _dense_attention(z, y_na, nb=nb, cseq=cseq, crow0=t_lat)
            zc = hy_proj(blk0=blk_lat, nblk=t_ctx // TM, seq=cseq)
            yc_hy = _hyena_ctx(zc.reshape(nb, 3, hy_w, cseq), cspec[l], pm[:, :, None], ctabs, 256)
            y_hy = jnp.concatenate([y_hy, yc_hy.transpose(0, 2, 1).reshape(t_ctx, hy_w)], axis=0)
        acc = _merge((y_hy, y_ret, y_na), z, COL_HY, w_branch[l].astype(BF16), nblk * per, TR, d)
        xs, hn = _out_proj(acc, w_out[l].astype(BF16), xs, mods[l], mrow_r, g_norm2[l], nblk * per)
        u = _matmul(hn, ffn_w_in, l, nblk, tn_ffn, ffn_w_in.shape[2])
        hg = _ffn_gate(u, ffn_conv_w[l], ffn_conv_b[l], seqlen, nblk, 512)
        nxt = l if last else l + 1
        w_ffo = ffn_w_out[l].astype(BF16)
        if last:
            res = _ffn_out(hg, w_ffo, xs, mods[l], mods[l], g_final, mrow_r, nblk * per, TR, 512, True)
        else:
            res = _ffn_out(hg, w_ffo, xs, mods[l], mods[nxt], g_norm1[nxt], mrow, nblk, TM, 256, False)
        if not last:
            xs, hn = res
    return res[0].reshape(nb, seq, d)
```
